```python
import math
import jax, jax.numpy as jnp
from jax import lax
import numpy as np

D_MODEL = 2048
BATCH = 2
SEQ = 8192
DEPTH = 1

PLE_DIM = 256
D_MIX = D_MODEL
POOL_WIDTH = D_MIX // 2
POOL_WINDOWS = (2, 4, 8, 16)
POOL_GROUPS = len(POOL_WINDOWS)
POOL_GC = POOL_WIDTH // POOL_GROUPS
GLA_WIDTH = D_MIX - POOL_WIDTH
GLA_HEADS = 4
GLA_DV = GLA_WIDTH // GLA_HEADS
GLA_DK = GLA_DV // 2
GLA_KEY_WIDTH = GLA_HEADS * GLA_DK
GLA_GATE_RANK = 16
GLA_GATE_TEMP = 16.0
GLA_CHUNK = 64
PEER_HEADS = 8
PEER_NKEYS = 128
PEER_EXPERTS = PEER_NKEYS * PEER_NKEYS
PEER_DQ = 256
PEER_HALF = PEER_DQ // 2
PEER_TOPK = 16
PEER_TOKEN_BLOCK = 128
ALPHA = float((2 * DEPTH) ** 0.25)
BETA = float((8 * DEPTH) ** -0.25)
LN_EPS = 1e-5
RMS_EPS = 1e-6
D_IN = POOL_WIDTH + 2 * GLA_KEY_WIDTH + GLA_WIDTH + GLA_GATE_RANK + GLA_WIDTH

kernel_name = "hybrid_pool_gla_peer_deepnorm"


def layer_norm(x, w, b):
    xf = x.astype(jnp.float32)
    mu = jnp.mean(xf, axis=-1, keepdims=True)
    var = jnp.mean(jnp.square(xf - mu), axis=-1, keepdims=True)
    y = (xf - mu) * lax.rsqrt(var + LN_EPS)
    return (y * w.astype(jnp.float32) + b.astype(jnp.float32)).astype(x.dtype)


def pool_mixer(u, w_pool, scale):
    B, S, _ = u.shape
    uf = u.astype(jnp.float32).reshape(B, S, POOL_GROUPS, POOL_GC)
    cs = jnp.cumsum(uf, axis=1)
    pos = jnp.arange(1, S + 1, dtype=jnp.float32)
    outs = []
    for g, w in enumerate(POOL_WINDOWS):
        c = cs[:, :, g]
        lag = jnp.pad(c, ((0, 0), (w, 0), (0, 0)))[:, :S]
        cnt = jnp.minimum(pos, float(w))[None, :, None]
        outs.append((c - lag) / cnt - uf[:, :, g])
    d = jnp.stack(outs, axis=2).astype(u.dtype)
    y = jnp.einsum('bsgc,gcd->bsgd', d, w_pool)
    return y.reshape(B, S, POOL_WIDTH) * scale


def gla_chunked(q, k, v, g):
    B, S, H, DK = q.shape
    DV = v.shape[-1]
    nc = S // GLA_CHUNK

    def to_chunks(t):
        return t.reshape(B, nc, GLA_CHUNK, H, t.shape[-1]).transpose(1, 0, 3, 2, 4)

    qc, kc, vc, gc = (to_chunks(t.astype(jnp.float32)) for t in (q, k, v, g))
    causal = jnp.tril(jnp.ones((GLA_CHUNK, GLA_CHUNK), dtype=bool))[:, :, None]

    def step(state, inp):
        qb, kb, vb, gb = inp
        b = jnp.cumsum(gb, axis=2)
        diff = b[:, :, :, None, :] - b[:, :, None, :, :]
        decay = jnp.exp(jnp.where(causal, diff, -jnp.inf))
        attn = jnp.einsum('bhid,bhjd,bhijd->bhij', qb, kb, decay)
        o = (jnp.einsum('bhij,bhje->bhie', attn, vb)
             + jnp.einsum('bhid,bhde->bhie', qb * jnp.exp(b), state))
        b_last = b[:, :, -1:, :]
        new_state = (jnp.exp(b_last)[:, :, 0, :, None] * state
                     + jnp.einsum('bhjd,bhje->bhde', kb * jnp.exp(b_last - b), vb))
        return new_state, o

    state0 = jnp.zeros((B, H, DK, DV), jnp.float32)
    _, oc = lax.scan(step, state0, (qc, kc, vc, gc))
    return oc.transpose(1, 0, 3, 2, 4).reshape(B, S, H, DV)


def gla_mixer(q, k, v, glr, r, w_gate_up, b_gate, norm_w):
    B, S, _ = q.shape
    log_a = jax.nn.log_sigmoid((glr @ w_gate_up + b_gate).astype(jnp.float32)) / GLA_GATE_TEMP
    qh = q.reshape(B, S, GLA_HEADS, GLA_DK) * (GLA_DK ** -0.5)
    kh = k.reshape(B, S, GLA_HEADS, GLA_DK)
    vh = v.reshape(B, S, GLA_HEADS, GLA_DV)
    gh = log_a.reshape(B, S, GLA_HEADS, GLA_DK)
    o = gla_chunked(qh, kh, vh, gh)
    o = o * lax.rsqrt(jnp.mean(jnp.square(o), axis=-1, keepdims=True) + RMS_EPS)
    o = o * norm_w.astype(jnp.float32).reshape(GLA_HEADS, GLA_DV)
    o = o.reshape(B, S, GLA_WIDTH).astype(q.dtype)
    return o * jax.nn.silu(r)


def peer(x, w_query, sub_keys, u_tab, v_tab):
    B, S, D = x.shape
    T = B * S
    xt = x.reshape(T, D)
    q = (xt @ w_query).reshape(T, PEER_HEADS, 2, PEER_HALF).astype(jnp.float32)
    scores = jnp.einsum('thpc,hpnc->thpn', q, sub_keys.astype(jnp.float32))
    s, idx = lax.top_k(scores, PEER_TOPK)
    cand = s[:, :, 0, :, None] + s[:, :, 1, None, :]
    cand_idx = idx[:, :, 0, :, None] * PEER_NKEYS + idx[:, :, 1, None, :]
    cand = cand.reshape(T, PEER_HEADS, PEER_TOPK * PEER_TOPK)
    cand_idx = cand_idx.reshape(T, PEER_HEADS, PEER_TOPK * PEER_TOPK)
    top_s, sel = lax.top_k(cand, PEER_TOPK)
    expert = jnp.take_along_axis(cand_idx, sel, axis=-1)
    gate = jax.nn.softmax(top_s, axis=-1).astype(x.dtype)
    hk = PEER_HEADS * PEER_TOPK
    nb = T // PEER_TOKEN_BLOCK

    def block(args):
        xb, eb, gb = args
        u = jnp.take(u_tab, eb, axis=0)
        h = jax.nn.gelu(jnp.einsum('tkd,td->tk', u, xb), approximate=False)
        vv = jnp.take(v_tab, eb, axis=0)
        return jnp.einsum('tk,tkd->td', gb * h, vv)

    y = lax.map(block, (xt.reshape(nb, PEER_TOKEN_BLOCK, D),
                        expert.reshape(nb, PEER_TOKEN_BLOCK, hk),
                        gate.reshape(nb, PEER_TOKEN_BLOCK, hk)))
    return y.reshape(B, S, D)


def setup_inputs(seed: int = 0) -> dict:
    key = jax.random.key(seed)
    ks = jax.random.split(key, 20)
    n = jax.random.normal
    f32 = jnp.float32
    return {
        "x": n(ks[0], (BATCH, SEQ, D_MODEL), f32),
        "p": n(ks[1], (DEPTH, BATCH, SEQ, PLE_DIM), f32),
        "w_in": n(ks[2], (DEPTH, D_MODEL, D_IN), f32) * D_MODEL ** -0.5,
        "gla_w_gate_up": n(ks[3], (DEPTH, GLA_GATE_RANK, GLA_KEY_WIDTH), f32) * GLA_GATE_RANK ** -0.5,
        "gla_b_gate": 1.0 + 0.1 * n(ks[4], (DEPTH, GLA_KEY_WIDTH), f32),
        "gla_norm_w": 1.0 + 0.02 * n(ks[5], (DEPTH, GLA_WIDTH), f32),
        "pool_w": n(ks[6], (DEPTH, POOL_GROUPS, POOL_GC, POOL_GC), f32) * POOL_GC ** -0.5,
        "pool_scale": 1.0 + 0.02 * n(ks[7], (DEPTH, POOL_WIDTH), f32),
        "w_out": n(ks[8], (DEPTH, D_MIX, D_MODEL), f32) * (BETA * D_MIX ** -0.5),
        "ln1_w": 1.0 + 0.02 * n(ks[9], (DEPTH, D_MODEL), f32),
        "ln1_b": 0.02 * n(ks[10], (DEPTH, D_MODEL), f32),
        "peer_w_query": n(ks[11], (DEPTH, D_MODEL, PEER_HEADS * PEER_DQ), f32) * D_MODEL ** -0.5,
        "peer_sub_keys": n(ks[12], (DEPTH, PEER_HEADS, 2, PEER_NKEYS, PEER_HALF), f32) * PEER_HALF ** -0.5,
        "peer_u": n(ks[13], (DEPTH, PEER_EXPERTS, D_MODEL), f32) * D_MODEL ** -0.5,
        "peer_v": n(ks[14], (DEPTH, PEER_EXPERTS, D_MODEL), f32) * (BETA * PEER_HEADS ** -0.5),
        "ple_w_gate": n(ks[15], (DEPTH, D_MODEL, D_MODEL), f32) * D_MODEL ** -0.5,
        "ple_w_proj": n(ks[16], (DEPTH, PLE_DIM, D_MODEL), f32) * (BETA * PLE_DIM ** -0.5),
        "ln2_w": 1.0 + 0.02 * n(ks[17], (DEPTH, D_MODEL), f32),
        "ln2_b": 0.02 * n(ks[18], (DEPTH, D_MODEL), f32),
    }


def reference(x, p, w_in, gla_w_gate_up, gla_b_gate, gla_norm_w, pool_w, pool_scale,
              w_out, ln1_w, ln1_b, peer_w_query, peer_sub_keys, peer_u, peer_v,
              ple_w_gate, ple_w_proj, ln2_w, ln2_b):
    splits = np.cumsum([POOL_WIDTH, GLA_KEY_WIDTH, GLA_KEY_WIDTH, GLA_WIDTH, GLA_GATE_RANK]).tolist()
    for i in range(DEPTH):
        proj = x @ w_in[i]
        u_pool, q, k, v, glr, r = jnp.split(proj, splits, axis=-1)
        y_pool = pool_mixer(u_pool, pool_w[i], pool_scale[i])
        y_gla = gla_mixer(q, k, v, glr, r, gla_w_gate_up[i], gla_b_gate[i], gla_norm_w[i])
        mix = jnp.concatenate([y_pool, y_gla], axis=-1) @ w_out[i]
        x1 = layer_norm(ALPHA * x + mix, ln1_w[i], ln1_b[i])
        y_ffn = peer(x1, peer_w_query[i], peer_sub_keys[i], peer_u[i], peer_v[i])
        ple = jax.nn.sigmoid(x1 @ ple_w_gate[i]) * (p[i] @ ple_w_proj[i])
        x = layer_norm(ALPHA * x1 + y_ffn + ple, ln2_w[i], ln2_b[i])
    return x
```

```python
import functools
import math

import jax
import jax.numpy as jnp
from jax import lax
from jax.experimental import pallas as pl
from jax.experimental.pallas import tpu as pltpu

F32 = jnp.float32
BF16 = jnp.bfloat16

D_MODEL = 2048
PLE_DIM = 256
POOL_WIDTH = 1024
POOL_WINDOWS = (2, 4, 8, 16)
POOL_GC = 256
POOL_HALO = 16
GLA_WIDTH = 1024
GLA_HEADS = 4
GLA_DV = 256
GLA_DK = 128
GLA_KEY_WIDTH = 512
GLA_GATE_RANK = 16
GLA_GATE_TEMP = 16.0
GLA_CHUNK = 64
PEER_HEADS = 8
PEER_NKEYS = 128
PEER_HALF = 128
PEER_TOPK = 16
DEPTH = 1
ALPHA = float((2 * DEPTH) ** 0.25)
LN_EPS = 1e-5
RMS_EPS = 1e-6
LANES = 128
NEG_INF = float("-inf")

COL_Q = POOL_WIDTH
COL_K = COL_Q + GLA_KEY_WIDTH
COL_V = COL_K + GLA_KEY_WIDTH
COL_R = COL_V + GLA_WIDTH
PROJ_COLS = COL_R + GLA_WIDTH


def _params(sem, vmem_mib):
    return pltpu.CompilerParams(dimension_semantics=sem, vmem_limit_bytes=vmem_mib * 1024 * 1024)


def _proj_kernel(x_ref, w_ref, wg_ref, o_ref, glr_ref, xb_ref):
    @pl.when(pl.program_id(1) == 0)
    def _():
        xb = x_ref[...].astype(BF16)
        xb_ref[...] = xb
        glr_ref[...] = jnp.dot(xb, wg_ref[...], preferred_element_type=F32)

    o_ref[...] = jnp.dot(xb_ref[...], w_ref[...], preferred_element_type=F32).astype(o_ref.dtype)


def _proj(x2, w_main, w_glr, tm=512, tn=1024):
    T = x2.shape[0]
    return pl.pallas_call(
        _proj_kernel,
        grid=(T // tm, PROJ_COLS // tn),
        in_specs=[
            pl.BlockSpec((tm, D_MODEL), lambda i, n: (i, 0)),
            pl.BlockSpec((D_MODEL, tn), lambda i, n: (0, n)),
            pl.BlockSpec((D_MODEL, LANES), lambda i, n: (0, 0)),
        ],
        out_specs=[
            pl.BlockSpec((tm, tn), lambda i, n: (i, n)),
            pl.BlockSpec((tm, LANES), lambda i, n: (i, 0)),
        ],
        out_shape=[
            jax.ShapeDtypeStruct((T, PROJ_COLS), BF16),
            jax.ShapeDtypeStruct((T, LANES), F32),
        ],
        scratch_shapes=[pltpu.VMEM((tm, D_MODEL), BF16)],
        compiler_params=_params(("parallel", "arbitrary"), 40),
        name="proj",
    )(x2, w_main, w_glr)


def _pool_kernel(u_ref, halo_ref, w_ref, sc_ref, o_ref, ext_ref, *, tiles_per_seq, tm):
    t = pl.program_id(0) % tiles_per_seq
    halo = jnp.where(t == 0, 0.0, halo_ref[...].astype(F32))
    ext_ref[0:POOL_HALO, :] = halo
    ext_ref[POOL_HALO:, :] = u_ref[...].astype(F32)
    pos = t * tm + lax.broadcasted_iota(jnp.int32, (tm, 1), 0)
    for g, w in enumerate(POOL_WINDOWS):
        cols = slice(g * POOL_GC, (g + 1) * POOL_GC)
        u = ext_ref[POOL_HALO:, cols]
        acc = u
        for j in range(1, w):
            acc = acc + ext_ref[POOL_HALO - j:POOL_HALO - j + tm, cols]
        cnt = jnp.minimum(pos + 1, w).astype(F32)
        d = acc / cnt - u
        y = jnp.dot(d.astype(BF16), w_ref[g], preferred_element_type=F32)
        o_ref[:, cols] = (y * sc_ref[:, cols]).astype(o_ref.dtype)


def _pool(proj, pool_w, pool_scale, S, tm=512):
    T = proj.shape[0]
    hb = tm // POOL_HALO
    return pl.pallas_call(
        functools.partial(_pool_kernel, tiles_per_seq=S // tm, tm=tm),
        grid=(T // tm,),
        in_specs=[
            pl.BlockSpec((tm, POOL_WIDTH), lambda i: (i, 0)),
            pl.BlockSpec((POOL_HALO, POOL_WIDTH), lambda i: (jnp.maximum(i * hb - 1, 0), 0)),
            pl.BlockSpec((len(POOL_WINDOWS), POOL_GC, POOL_GC), lambda i: (0, 0, 0)),
            pl.BlockSpec((1, POOL_WIDTH), lambda i: (0, 0)),
        ],
        out_specs=pl.BlockSpec((tm, POOL_WIDTH), lambda i: (i, 0)),
        out_shape=jax.ShapeDtypeStruct((T, POOL_WIDTH), BF16),
        scratch_shapes=[pltpu.VMEM((POOL_HALO + tm, POOL_WIDTH), F32)],
        compiler_params=_params(("parallel",), 32),
        name="pool",
    )(proj, proj, pool_w, pool_scale)


def _gla_kernel(q_ref, k_ref, v_ref, r_ref, glr_ref, wg_ref, bg_ref, nw_ref, o_ref, s_ref, *, n_chunks):
    @pl.when(pl.program_id(2) == 0)
    def _():
        s_ref[...] = jnp.zeros_like(s_ref)

    C = GLA_CHUNK
    row = lax.broadcasted_iota(jnp.int32, (C, C), 0)
    col = lax.broadcasted_iota(jnp.int32, (C, C), 1)
    causal = col <= row
    tril = causal.astype(F32)
    wg = wg_ref[...]
    bg = bg_ref[...]
    nw = nw_ref[...]
    nt = (((1,), (1,)), ((), ()))
    for c in range(n_chunks):
        rows = slice(c * C, (c + 1) * C)
        z = jnp.dot(glr_ref[rows, :], wg, preferred_element_type=F32, precision=lax.Precision.HIGHEST) + bg
        g = jax.nn.log_sigmoid(z) / GLA_GATE_TEMP
        b = jnp.dot(tril, g, preferred_element_type=F32, precision=lax.Precision.HIGHEST)
        b_last = b[C - 1:C, :]
        b_mid = b[C // 2 - 1:C // 2, :]
        q = q_ref[rows, :].astype(F32) * (GLA_DK ** -0.5)
        k = k_ref[rows, :].astype(F32)
        v = v_ref[rows, :]
        q_state = (q * jnp.exp(b)).astype(BF16)
        q_in = (q * jnp.exp(b - b_mid)).astype(BF16)
        k_in = (k * jnp.exp(b_mid - b)).astype(BF16)
        k_out = k * jnp.exp(b_last - b)
        attn = lax.dot_general(q_in, k_in, nt, preferred_element_type=F32)
        attn = jnp.where(causal, attn, 0.0).astype(BF16)
        s = s_ref[...]
        o = (jnp.dot(attn, v, preferred_element_type=F32)
             + jnp.dot(q_state, s.astype(BF16), preferred_element_type=F32))
        decay = jnp.transpose(jnp.broadcast_to(jnp.exp(b_last), (C, GLA_DK)))[:, 0:1]
        s_ref[...] = decay * s + jnp.dot(jnp.transpose(k_out).astype(BF16), v, preferred_element_type=F32)
        o = o * lax.rsqrt(jnp.mean(jnp.square(o), axis=-1, keepdims=True) + RMS_EPS)
        o = o * nw
        r = r_ref[rows, :].astype(F32)
        o_ref[rows, :] = (o * (r * jax.nn.sigmoid(r))).astype(o_ref.dtype)


def _gla(proj, glr, wg, bg, nw, B, S, L=256):
    T = proj.shape[0]
    nl = S // L
    rb = lambda b, l: b * nl + l
    return pl.pallas_call(
        functools.partial(_gla_kernel, n_chunks=L // GLA_CHUNK),
        grid=(B, GLA_HEADS, nl),
        in_specs=[
            pl.BlockSpec((L, GLA_DK), lambda b, h, l: (rb(b, l), COL_Q // GLA_DK + h)),
            pl.BlockSpec((L, GLA_DK), lambda b, h, l: (rb(b, l), COL_K // GLA_DK + h)),
            pl.BlockSpec((L, GLA_DV), lambda b, h, l: (rb(b, l), COL_V // GLA_DV + h)),
            pl.BlockSpec((L, GLA_DV), lambda b, h, l: (rb(b, l), COL_R // GLA_DV + h)),
            pl.BlockSpec((L, LANES), lambda b, h, l: (rb(b, l), 0)),
            pl.BlockSpec((None, LANES, GLA_DK), lambda b, h, l: (h, 0, 0)),
            pl.BlockSpec((None, 1, GLA_DK), lambda b, h, l: (h, 0, 0)),
            pl.BlockSpec((None, 1, GLA_DV), lambda b, h, l: (h, 0, 0)),
        ],
        out_specs=pl.BlockSpec((L, GLA_DV), lambda b, h, l: (rb(b, l), h)),
        out_shape=jax.ShapeDtypeStruct((T, GLA_WIDTH), BF16),
        scratch_shapes=[pltpu.VMEM((GLA_DK, GLA_DV), F32)],
        compiler_params=_params(("parallel", "parallel", "arbitrary"), 32),
        name="gla",
    )(proj, proj, proj, proj, glr, wg, bg, nw)


def _layer_norm(h, w, b):
    mu = jnp.mean(h, axis=-1, keepdims=True)
    hc = h - mu
    var = jnp.mean(jnp.square(hc), axis=-1, keepdims=True)
    return hc * lax.rsqrt(var + LN_EPS) * w + b


def _outproj_kernel(yp_ref, yg_ref, x_ref, w_ref, lw_ref, lb_ref, x1_ref, x1t_ref):
    mix = (jnp.dot(yp_ref[...], w_ref[0:POOL_WIDTH, :], preferred_element_type=F32)
           + jnp.dot(yg_ref[...], w_ref[POOL_WIDTH:, :], preferred_element_type=F32))
    x1 = _layer_norm(ALPHA * x_ref[...] + mix, lw_ref[...], lb_ref[...])
    x1_ref[...] = x1
    x1t_ref[...] = jnp.transpose(x1).astype(BF16)


def _outproj(y_pool, y_gla, x2, w_out, ln_w, ln_b, tm=256):
    T = x2.shape[0]
    return pl.pallas_call(
        _outproj_kernel,
        grid=(T // tm,),
        in_specs=[
            pl.BlockSpec((tm, POOL_WIDTH), lambda i: (i, 0)),
            pl.BlockSpec((tm, GLA_WIDTH), lambda i: (i, 0)),
            pl.BlockSpec((tm, D_MODEL), lambda i: (i, 0)),
            pl.BlockSpec((D_MODEL, D_MODEL), lambda i: (0, 0)),
            pl.BlockSpec((1, D_MODEL), lambda i: (0, 0)),
            pl.BlockSpec((1, D_MODEL), lambda i: (0, 0)),
        ],
        out_specs=[
            pl.BlockSpec((tm, D_MODEL), lambda i: (i, 0)),
            pl.BlockSpec((D_MODEL, tm), lambda i: (0, i)),
        ],
        out_shape=[
            jax.ShapeDtypeStruct((T, D_MODEL), F32),
            jax.ShapeDtypeStruct((D_MODEL, T), BF16),
        ],
        compiler_params=_params(("parallel",), 48),
        name="outproj",
    )(y_pool, y_gla, x2, w_out, ln_w, ln_b)


N_SORT = PEER_TOPK + 1


def _sorted_top(arr):
    rid = lax.broadcasted_iota(jnp.int32, (24, LANES), 0)

    def body(kk, carry):
        a, top = carry
        m = jnp.max(a, axis=0, keepdims=True)
        top = jnp.where(rid == kk, m, top)
        a = jnp.where(a == m, NEG_INF, a)
        return a, top

    _, top = lax.fori_loop(0, N_SORT, body, (arr, jnp.full((24, LANES), NEG_INF, F32)))
    return top


def _pair_stats(ta, tb):
    r8 = lax.broadcasted_iota(jnp.int32, (8, LANES), 0)
    b_lo, b_hi = tb[0:8], tb[8:16]
    a = [ta[i:i + 1] for i in range(8)]
    a16, b16, b0 = ta[16:17], tb[16:17], tb[0:1]
    p2 = jnp.where(r8 < 5, a[2] + b_lo, jnp.where(r8 == 5, a16 + b0, jnp.where(r8 == 6, a[0] + b16, NEG_INF)))
    pieces = [
        a[0] + b_lo, a[0] + b_hi, a[1] + b_lo, p2,
        jnp.where(r8 < 4, a[3] + b_lo, NEG_INF),
        jnp.where(r8 < 3, a[4] + b_lo, NEG_INF),
        jnp.where(r8 < 2, a[5] + b_lo, NEG_INF),
        jnp.where(r8 < 2, a[6] + b_lo, NEG_INF),
        jnp.where(r8 < 2, a[7] + b_lo, NEG_INF),
        ta[8:16] + b0,
    ]
    cand = jnp.concatenate(pieces, axis=0)
    top_sum = a[0] + b0

    def body(kk, carry):
        c, z, v16, v17 = carry
        m = jnp.max(c, axis=0, keepdims=True)
        z = z + jnp.where(kk < PEER_TOPK, jnp.exp(m - top_sum), 0.0)
        v16 = jnp.where(kk == PEER_TOPK - 1, m, v16)
        v17 = jnp.where(kk == PEER_TOPK, m, v17)
        c = jnp.where(c == m, NEG_INF, c)
        return c, z, v16, v17

    zero = jnp.zeros((1, LANES), F32)
    _, z, v16, v17 = lax.fori_loop(0, N_SORT, body, (cand, zero, zero, zero))
    return 0.5 * (v16 + v17), a[0], b0, 1.0 / z


def _query_kernel(x1t_ref, wq_ref, keys_ref, s1_ref, s2_ref, st_ref, q_ref, *, tm):
    q_ref[...] = jnp.dot(wq_ref[...], x1t_ref[...], preferred_element_type=F32)
    pad = jnp.zeros((4, LANES), F32)
    for h in range(PEER_HEADS):
        for p, s_ref in enumerate((s1_ref, s2_ref)):
            hp = 2 * h + p
            s_ref[h] = jnp.dot(keys_ref[hp], q_ref[hp * PEER_HALF:(hp + 1) * PEER_HALF, :],
                               preferred_element_type=F32, precision=lax.Precision.HIGHEST)
        for tc in range(tm // LANES):
            lanes = slice(tc * LANES, (tc + 1) * LANES)
            ta = _sorted_top(s1_ref[h, :, lanes])
            tb = _sorted_top(s2_ref[h, :, lanes])
            tau, m1, m2, rz = _pair_stats(ta, tb)
            st_ref[h, :, lanes] = jnp.concatenate([tau, m1, m2, rz, pad], axis=0)


def _query(x1t, wq_t, keys, tm=256):
    T = x1t.shape[1]
    sc_spec = pl.BlockSpec((PEER_HEADS, PEER_NKEYS, tm), lambda i: (0, 0, i))
    return pl.pallas_call(
        functools.partial(_query_kernel, tm=tm),
        grid=(T // tm,),
        in_specs=[
            pl.BlockSpec((D_MODEL, tm), lambda i: (0, i)),
            pl.BlockSpec((D_MODEL, D_MODEL), lambda i: (0, 0)),
            pl.BlockSpec((2 * PEER_HEADS, PEER_NKEYS, PEER_HALF), lambda i: (0, 0, 0)),
        ],
        out_specs=[sc_spec, sc_spec, pl.BlockSpec((PEER_HEADS, 8, tm), lambda i: (0, 0, i))],
        out_shape=[
            jax.ShapeDtypeStruct((PEER_HEADS, PEER_NKEYS, T), F32),
            jax.ShapeDtypeStruct((PEER_HEADS, PEER_NKEYS, T), F32),
            jax.ShapeDtypeStruct((PEER_HEADS, 8, T), F32),
        ],
        scratch_shapes=[pltpu.VMEM((D_MODEL, tm), F32)],
        compiler_params=_params(("parallel",), 48),
        name="query",
    )(x1t, wq_t, keys)


PEER_SB = 32


def _gelu(x):
    return 0.5 * x * (1.0 + lax.erf(x * (1.0 / math.sqrt(2.0))))


def _peer_kernel(x1t_ref, u_ref, vt_ref, s1_ref, s2_ref, st_ref, y_ref,
                 acc_ref, c1_ref, phi_ref, e2_ref, st_scr, ht_scr, *, tm, te):
    j = pl.program_id(1)
    n1 = te // PEER_NKEYS

    @pl.when(j == 0)
    def _():
        acc_ref[...] = jnp.zeros_like(acc_ref)
        for h in range(PEER_HEADS):
            tau, m1, m2, rz = (st_ref[h, r:r + 1, :] for r in range(4))
            s1 = s1_ref[h]
            c1 = jnp.exp(s1 - m1)
            phi = tau - s1
            for tc in range(tm // LANES):
                c1_ref[h, tc] = c1[:, tc * LANES:(tc + 1) * LANES]
                phi_ref[h, tc] = phi[:, tc * LANES:(tc + 1) * LANES]
            e2_ref[h] = jnp.exp(s2_ref[h] - m2) * rz

    st_scr[...] = jnp.dot(u_ref[...], x1t_ref[...], preferred_element_type=F32)

    def per_i1(i1l, carry):
        i1 = j * n1 + i1l
        base = pl.multiple_of(i1l * PEER_NKEYS, PEER_NKEYS)
        for tc in range(tm // LANES):
            lanes = slice(tc * LANES, (tc + 1) * LANES)
            phis = [phi_ref[h, tc, pl.ds(i1, PEER_SB, stride=0), :] for h in range(PEER_HEADS)]
            cs = [c1_ref[h, tc, pl.ds(i1, PEER_SB, stride=0), :] for h in range(PEER_HEADS)]
            for sb in range(PEER_NKEYS // PEER_SB):
                rows = slice(sb * PEER_SB, (sb + 1) * PEER_SB)
                g = jnp.zeros((PEER_SB, LANES), F32)
                for h in range(PEER_HEADS):
                    sel = jnp.where(s2_ref[h, rows, lanes] >= phis[h], e2_ref[h, rows, lanes], 0.0)
                    g = g + cs[h] * sel
                srows = pl.ds(base + sb * PEER_SB, PEER_SB)
                ht_scr[srows, lanes] = (g * _gelu(st_scr[srows, lanes])).astype(BF16)
        return carry

    lax.fori_loop(0, n1, per_i1, 0)
    acc_ref[...] += jnp.dot(vt_ref[...], ht_scr[...], preferred_element_type=F32)

    @pl.when(j == pl.num_programs(1) - 1)
    def _():
        y_ref[...] = jnp.transpose(acc_ref[...])


def _peer(x1t, u_tab, vt_tab, s1t, s2t, stats, tm=512, te=512):
    T = x1t.shape[1]
    E = u_tab.shape[0]
    sc_spec = pl.BlockSpec((PEER_HEADS, PEER_NKEYS, tm), lambda i, j: (0, 0, i))
    return pl.pallas_call(
        functools.partial(_peer_kernel, tm=tm, te=te),
        grid=(T // tm, E // te),
        in_specs=[
            pl.BlockSpec((D_MODEL, tm), lambda i, j: (0, i)),
            pl.BlockSpec((te, D_MODEL), lambda i, j: (j, 0)),
            pl.BlockSpec((D_MODEL, te), lambda i, j: (0, j)),
            sc_spec, sc_spec,
            pl.BlockSpec((PEER_HEADS, 8, tm), lambda i, j: (0, 0, i)),
        ],
        out_specs=pl.BlockSpec((tm, D_MODEL), lambda i, j: (i, 0)),
        out_shape=jax.ShapeDtypeStruct((T, D_MODEL), F32),
        scratch_shapes=[
            pltpu.VMEM((D_MODEL, tm), F32),
            pltpu.VMEM((PEER_HEADS, tm // LANES, PEER_NKEYS, LANES), F32),
            pltpu.VMEM((PEER_HEADS, tm // LANES, PEER_NKEYS, LANES), F32),
            pltpu.VMEM((PEER_HEADS, PEER_NKEYS, tm), F32),
            pltpu.VMEM((te, tm), F32),
            pltpu.VMEM((te, tm), BF16),
        ],
        compiler_params=_params(("parallel", "arbitrary"), 56),
        name="peer",
    )(x1t, u_tab, vt_tab, s1t, s2t, stats)


def _final_kernel(x1_ref, y_ref, p_ref, wg_ref, wp_ref, lw_ref, lb_ref, o_ref):
    x1 = x1_ref[...]
    gate = jax.nn.sigmoid(jnp.dot(x1.astype(BF16), wg_ref[...], preferred_element_type=F32))
    emb = jnp.dot(p_ref[...].astype(BF16), wp_ref[...], preferred_element_type=F32)
    o_ref[...] = _layer_norm(ALPHA * x1 + y_ref[...] + gate * emb, lw_ref[...], lb_ref[...])


def _final(x1, y_ffn, p2, w_gate, w_proj, ln_w, ln_b, tm=256):
    T = x1.shape[0]
    return pl.pallas_call(
        _final_kernel,
        grid=(T // tm,),
        in_specs=[
            pl.BlockSpec((tm, D_MODEL), lambda i: (i, 0)),
            pl.BlockSpec((tm, D_MODEL), lambda i: (i, 0)),
            pl.BlockSpec((tm, PLE_DIM), lambda i: (i, 0)),
            pl.BlockSpec((D_MODEL, D_MODEL), lambda i: (0, 0)),
            pl.BlockSpec((PLE_DIM, D_MODEL), lambda i: (0, 0)),
            pl.BlockSpec((1, D_MODEL), lambda i: (0, 0)),
            pl.BlockSpec((1, D_MODEL), lambda i: (0, 0)),
        ],
        out_specs=pl.BlockSpec((tm, D_MODEL), lambda i: (i, 0)),
        out_shape=jax.ShapeDtypeStruct((T, D_MODEL), F32),
        compiler_params=_params(("parallel",), 48),
        name="final",
    )(x1, y_ffn, p2, w_gate, w_proj, ln_w, ln_b)


def _layer(x2, p2, B, S, w_in, gla_w_gate_up, gla_b_gate, gla_norm_w, pool_w, pool_scale, w_out,
           ln1_w, ln1_b, peer_w_query, peer_sub_keys, peer_u, peer_v, ple_w_gate, ple_w_proj, ln2_w, ln2_b):
    glr0 = COL_R
    w_main = jnp.concatenate([w_in[:, :glr0], w_in[:, glr0 + GLA_GATE_RANK:]], axis=1).astype(BF16)
    w_glr = jnp.pad(w_in[:, glr0:glr0 + GLA_GATE_RANK], ((0, 0), (0, LANES - GLA_GATE_RANK))).astype(BF16)
    proj, glr = _proj(x2, w_main, w_glr)

    y_pool = _pool(proj, pool_w.astype(BF16), pool_scale.reshape(1, POOL_WIDTH), S)

    wg = jnp.pad(gla_w_gate_up, ((0, LANES - GLA_GATE_RANK), (0, 0)))
    wg = wg.reshape(LANES, GLA_HEADS, GLA_DK).transpose(1, 0, 2)
    y_gla = _gla(proj, glr, wg, gla_b_gate.reshape(GLA_HEADS, 1, GLA_DK),
                 gla_norm_w.reshape(GLA_HEADS, 1, GLA_DV), B, S)

    x1, x1t = _outproj(y_pool, y_gla, x2, w_out.astype(BF16),
                       ln1_w.reshape(1, D_MODEL), ln1_b.reshape(1, D_MODEL))

    keys = peer_sub_keys.reshape(2 * PEER_HEADS, PEER_NKEYS, PEER_HALF)
    s1t, s2t, stats = _query(x1t, peer_w_query.T.astype(BF16), keys)
    y_ffn = _peer(x1t, peer_u.astype(BF16), peer_v.T.astype(BF16), s1t, s2t, stats)

    return _final(x1, y_ffn, p2, ple_w_gate.astype(BF16), ple_w_proj.astype(BF16),
                  ln2_w.reshape(1, D_MODEL), ln2_b.reshape(1, D_MODEL))


def kernel(x, p, w_in, gla_w_gate_up, gla_b_gate, gla_norm_w, pool_w, pool_scale, w_out, ln1_w, ln1_b,
           peer_w_query, peer_sub_keys, peer_u, peer_v, ple_w_gate, ple_w_proj, ln2_w, ln2_b):
    B, S, D = x.shape
    x2 = x.reshape(B * S, D)
    for i in range(w_in.shape[0]):
        x2 = _layer(x2, p[i].reshape(B * S, PLE_DIM), B, S, w_in[i], gla_w_gate_up[i], gla_b_gate[i],
                    gla_norm_w[i], pool_w[i], pool_scale[i], w_out[i], ln1_w[i], ln1_b[i],
                    peer_w_query[i], peer_sub_keys[i], peer_u[i], peer_v[i], ple_w_gate[i],
                    ple_w_proj[i], ln2_w[i], ln2_b[i])
    return x2.reshape(B, S, D)
```

```python
import functools
import math

import jax
import jax.numpy as jnp
from jax import lax
from jax.experimental import pallas as pl
from jax.experimental.pallas import tpu as pltpu

F32 = jnp.float32
BF16 = jnp.bfloat16

D_MODEL = 2048
PLE_DIM = 256
POOL_WIDTH = 1024
POOL_WINDOWS = (2, 4, 8, 16)
POOL_GC = 256
POOL_HALO = 16
GLA_WIDTH = 1024
GLA_HEADS = 4
GLA_DV = 256
GLA_DK = 128
GLA_KEY_WIDTH = 512
GLA_GATE_RANK = 16
GLA_GATE_TEMP = 16.0
GLA_CHUNK = 64
PEER_HEADS = 8
PEER_NKEYS = 128
PEER_HALF = 128
PEER_TOPK = 16
DEPTH = 1
ALPHA = float((2 * DEPTH) ** 0.25)
LN_EPS = 1e-5
RMS_EPS = 1e-6
LANES = 128
NEG_INF = float("-inf")

COL_Q = POOL_WIDTH
COL_K = COL_Q + GLA_KEY_WIDTH
COL_V = COL_K + GLA_KEY_WIDTH
COL_R = COL_V + GLA_WIDTH
PROJ_COLS = COL_R + GLA_WIDTH


def _params(sem, vmem_mib):
    return pltpu.CompilerParams(dimension_semantics=sem, vmem_limit_bytes=vmem_mib * 1024 * 1024)


def _proj_kernel(x_ref, w_ref, wg_ref, o_ref, glr_ref, xb_ref):
    @pl.when(pl.program_id(1) == 0)
    def _():
        xb = x_ref[...].astype(BF16)
        xb_ref[...] = xb
        glr_ref[...] = jnp.dot(xb, wg_ref[...], preferred_element_type=F32)

    o_ref[...] = jnp.dot(xb_ref[...], w_ref[...], preferred_element_type=F32).astype(o_ref.dtype)


def _proj(x2, w_main, w_glr, tm=512, tn=1024):
    T = x2.shape[0]
    return pl.pallas_call(
        _proj_kernel,
        grid=(T // tm, PROJ_COLS // tn),
        in_specs=[
            pl.BlockSpec((tm, D_MODEL), lambda i, n: (i, 0)),
            pl.BlockSpec((D_MODEL, tn), lambda i, n: (0, n)),
            pl.BlockSpec((D_MODEL, LANES), lambda i, n: (0, 0)),
        ],
        out_specs=[
            pl.BlockSpec((tm, tn), lambda i, n: (i, n)),
            pl.BlockSpec((tm, LANES), lambda i, n: (i, 0)),
        ],
        out_shape=[
            jax.ShapeDtypeStruct((T, PROJ_COLS), BF16),
            jax.ShapeDtypeStruct((T, LANES), F32),
        ],
        scratch_shapes=[pltpu.VMEM((tm, D_MODEL), BF16)],
        compiler_params=_params(("parallel", "arbitrary"), 40),
        name="proj",
    )(x2, w_main, w_glr)


def _pool_kernel(u_ref, halo_ref, w_ref, sc_ref, o_ref, ext_ref, *, tiles_per_seq, tm):
    t = pl.program_id(0) % tiles_per_seq
    halo = jnp.where(t == 0, 0.0, halo_ref[...].astype(F32))
    ext_ref[0:POOL_HALO, :] = halo
    ext_ref[POOL_HALO:, :] = u_ref[...].astype(F32)
    pos = t * tm + lax.broadcasted_iota(jnp.int32, (tm, 1), 0)
    for g, w in enumerate(POOL_WINDOWS):
        cols = slice(g * POOL_GC, (g + 1) * POOL_GC)
        u = ext_ref[POOL_HALO:, cols]
        acc = u
        for j in range(1, w):
            acc = acc + ext_ref[POOL_HALO - j:POOL_HALO - j + tm, cols]
        cnt = jnp.minimum(pos + 1, w).astype(F32)
        d = acc / cnt - u
        y = jnp.dot(d.astype(BF16), w_ref[g], preferred_element_type=F32)
        o_ref[:, cols] = (y * sc_ref[:, cols]).astype(o_ref.dtype)


def _pool(proj, pool_w, pool_scale, S, tm=512):
    T = proj.shape[0]
    hb = tm // POOL_HALO
    return pl.pallas_call(
        functools.partial(_pool_kernel, tiles_per_seq=S // tm, tm=tm),
        grid=(T // tm,),
        in_specs=[
            pl.BlockSpec((tm, POOL_WIDTH), lambda i: (i, 0)),
            pl.BlockSpec((POOL_HALO, POOL_WIDTH), lambda i: (jnp.maximum(i * hb - 1, 0), 0)),
            pl.BlockSpec((len(POOL_WINDOWS), POOL_GC, POOL_GC), lambda i: (0, 0, 0)),
            pl.BlockSpec((1, POOL_WIDTH), lambda i: (0, 0)),
        ],
        out_specs=pl.BlockSpec((tm, POOL_WIDTH), lambda i: (i, 0)),
        out_shape=jax.ShapeDtypeStruct((T, POOL_WIDTH), BF16),
        scratch_shapes=[pltpu.VMEM((POOL_HALO + tm, POOL_WIDTH), F32)],
        compiler_params=_params(("parallel",), 32),
        name="pool",
    )(proj, proj, pool_w, pool_scale)


def _gla_kernel(q_ref, k_ref, v_ref, r_ref, glr_ref, wg_ref, bg_ref, nw_ref, o_ref, s_ref, *, n_chunks):
    @pl.when(pl.program_id(2) == 0)
    def _():
        s_ref[...] = jnp.zeros_like(s_ref)

    C = GLA_CHUNK
    row = lax.broadcasted_iota(jnp.int32, (C, C), 0)
    col = lax.broadcasted_iota(jnp.int32, (C, C), 1)
    causal = col <= row
    tril = causal.astype(F32)
    wg = wg_ref[...]
    bg = bg_ref[...]
    nw = nw_ref[...]
    nt = (((1,), (1,)), ((), ()))
    for c in range(n_chunks):
        rows = slice(c * C, (c + 1) * C)
        z = jnp.dot(glr_ref[rows, :], wg, preferred_element_type=F32, precision=lax.Precision.HIGHEST) + bg
        g = jax.nn.log_sigmoid(z) / GLA_GATE_TEMP
        b = jnp.dot(tril, g, preferred_element_type=F32, precision=lax.Precision.HIGHEST)
        b_last = b[C - 1:C, :]
        b_mid = b[C // 2 - 1:C // 2, :]
        q = q_ref[rows, :].astype(F32) * (GLA_DK ** -0.5)
        k = k_ref[rows, :].astype(F32)
        v = v_ref[rows, :]
        q_state = (q * jnp.exp(b)).astype(BF16)
        q_in = (q * jnp.exp(b - b_mid)).astype(BF16)
        k_in = (k * jnp.exp(b_mid - b)).astype(BF16)
        k_out = k * jnp.exp(b_last - b)
        attn = lax.dot_general(q_in, k_in, nt, preferred_element_type=F32)
        attn = jnp.where(causal, attn, 0.0).astype(BF16)
        s = s_ref[...]
        o = (jnp.dot(attn, v, preferred_element_type=F32)
             + jnp.dot(q_state, s.astype(BF16), preferred_element_type=F32))
        decay = jnp.transpose(jnp.broadcast_to(jnp.exp(b_last), (C, GLA_DK)))[:, 0:1]
        s_ref[...] = decay * s + jnp.dot(jnp.transpose(k_out).astype(BF16), v, preferred_element_type=F32)
        o = o * lax.rsqrt(jnp.mean(jnp.square(o), axis=-1, keepdims=True) + RMS_EPS)
        o = o * nw
        r = r_ref[rows, :].astype(F32)
        o_ref[rows, :] = (o * (r * jax.nn.sigmoid(r))).astype(o_ref.dtype)


def _gla(proj, glr, wg, bg, nw, B, S, L=256):
    T = proj.shape[0]
    nl = S // L
    rb = lambda b, l: b * nl + l
    return pl.pallas_call(
        functools.partial(_gla_kernel, n_chunks=L // GLA_CHUNK),
        grid=(B, GLA_HEADS, nl),
        in_specs=[
            pl.BlockSpec((L, GLA_DK), lambda b, h, l: (rb(b, l), COL_Q // GLA_DK + h)),
            pl.BlockSpec((L, GLA_DK), lambda b, h, l: (rb(b, l), COL_K // GLA_DK + h)),
            pl.BlockSpec((L, GLA_DV), lambda b, h, l: (rb(b, l), COL_V // GLA_DV + h)),
            pl.BlockSpec((L, GLA_DV), lambda b, h, l: (rb(b, l), COL_R // GLA_DV + h)),
            pl.BlockSpec((L, LANES), lambda b, h, l: (rb(b, l), 0)),
            pl.BlockSpec((None, LANES, GLA_DK), lambda b, h, l: (h, 0, 0)),
            pl.BlockSpec((None, 1, GLA_DK), lambda b, h, l: (h, 0, 0)),
            pl.BlockSpec((None, 1, GLA_DV), lambda b, h, l: (h, 0, 0)),
        ],
        out_specs=pl.BlockSpec((L, GLA_DV), lambda b, h, l: (rb(b, l), h)),
        out_shape=jax.ShapeDtypeStruct((T, GLA_WIDTH), BF16),
        scratch_shapes=[pltpu.VMEM((GLA_DK, GLA_DV), F32)],
        compiler_params=_params(("parallel", "parallel", "arbitrary"), 32),
        name="gla",
    )(proj, proj, proj, proj, glr, wg, bg, nw)


def _layer_norm(h, w, b):
    mu = jnp.mean(h, axis=-1, keepdims=True)
    hc = h - mu
    var = jnp.mean(jnp.square(hc), axis=-1, keepdims=True)
    return hc * lax.rsqrt(var + LN_EPS) * w + b


def _outproj_kernel(yp_ref, yg_ref, x_ref, w_ref, lw_ref, lb_ref, x1_ref, x1t_ref):
    mix = (jnp.dot(yp_ref[...], w_ref[0:POOL_WIDTH, :], preferred_element_type=F32)
           + jnp.dot(yg_ref[...], w_ref[POOL_WIDTH:, :], preferred_element_type=F32))
    x1 = _layer_norm(ALPHA * x_ref[...] + mix, lw_ref[...], lb_ref[...])
    x1_ref[...] = x1
    x1t_ref[...] = jnp.transpose(x1).astype(BF16)


def _outproj(y_pool, y_gla, x2, w_out, ln_w, ln_b, tm=256):
    T = x2.shape[0]
    return pl.pallas_call(
        _outproj_kernel,
        grid=(T // tm,),
        in_specs=[
            pl.BlockSpec((tm, POOL_WIDTH), lambda i: (i, 0)),
            pl.BlockSpec((tm, GLA_WIDTH), lambda i: (i, 0)),
            pl.BlockSpec((tm, D_MODEL), lambda i: (i, 0)),
            pl.BlockSpec((D_MODEL, D_MODEL), lambda i: (0, 0)),
            pl.BlockSpec((1, D_MODEL), lambda i: (0, 0)),
            pl.BlockSpec((1, D_MODEL), lambda i: (0, 0)),
        ],
        out_specs=[
            pl.BlockSpec((tm, D_MODEL), lambda i: (i, 0)),
            pl.BlockSpec((D_MODEL, tm), lambda i: (0, i)),
        ],
        out_shape=[
            jax.ShapeDtypeStruct((T, D_MODEL), F32),
            jax.ShapeDtypeStruct((D_MODEL, T), BF16),
        ],
        compiler_params=_params(("parallel",), 48),
        name="outproj",
    )(y_pool, y_gla, x2, w_out, ln_w, ln_b)


N_SORT = PEER_TOPK + 1


def _sorted_top2(arr_a, arr_b):
    rid = lax.broadcasted_iota(jnp.int32, (24, LANES), 0)

    def body(kk, carry):
        out = []
        for a, top in carry:
            m = jnp.max(a, axis=0, keepdims=True)
            out.append((jnp.where(a == m, NEG_INF, a), jnp.where(rid == kk, m, top)))
        return tuple(out)

    init = jnp.full((24, LANES), NEG_INF, F32)
    (_, top_a), (_, top_b) = lax.fori_loop(0, N_SORT, body, ((arr_a, init), (arr_b, init)))
    return top_a, top_b


def _pair_stats(ta, tb):
    r8 = lax.broadcasted_iota(jnp.int32, (8, LANES), 0)
    b_lo, b_hi = tb[0:8], tb[8:16]
    a = [ta[i:i + 1] for i in range(8)]
    a16, b16, b0 = ta[16:17], tb[16:17], tb[0:1]
    p2 = jnp.where(r8 < 5, a[2] + b_lo, jnp.where(r8 == 5, a16 + b0, jnp.where(r8 == 6, a[0] + b16, NEG_INF)))
    pieces = [
        a[0] + b_lo, a[0] + b_hi, a[1] + b_lo, p2,
        jnp.where(r8 < 4, a[3] + b_lo, NEG_INF),
        jnp.where(r8 < 3, a[4] + b_lo, NEG_INF),
        jnp.where(r8 < 2, a[5] + b_lo, NEG_INF),
        jnp.where(r8 < 2, a[6] + b_lo, NEG_INF),
        jnp.where(r8 < 2, a[7] + b_lo, NEG_INF),
        ta[8:16] + b0,
    ]
    cand = jnp.concatenate(pieces, axis=0)
    top_sum = a[0] + b0

    def body(kk, carry):
        c, z, v16, v17 = carry
        m = jnp.max(c, axis=0, keepdims=True)
        z = z + jnp.where(kk < PEER_TOPK, jnp.exp(m - top_sum), 0.0)
        v16 = jnp.where(kk == PEER_TOPK - 1, m, v16)
        v17 = jnp.where(kk == PEER_TOPK, m, v17)
        c = jnp.where(c == m, NEG_INF, c)
        return c, z, v16, v17

    zero = jnp.zeros((1, LANES), F32)
    _, z, v16, v17 = lax.fori_loop(0, N_SORT, body, (cand, zero, zero, zero))
    return 0.5 * (v16 + v17), a[0], b0, 1.0 / z


def _split_bf16(x):
    hi = x.astype(BF16)
    return hi, (x - hi.astype(F32)).astype(BF16)


def _query_kernel(x1t_ref, wq_ref, khi_ref, klo_ref, c1_ref, phi_ref, e2_ref, s2_ref, q_ref, s1_scr, *, tm):
    q_ref[...] = jnp.dot(wq_ref[...], x1t_ref[...], preferred_element_type=F32)
    for h in range(PEER_HEADS):
        for p in range(2):
            hp = 2 * h + p
            q_hi, q_lo = _split_bf16(q_ref[hp * PEER_HALF:(hp + 1) * PEER_HALF, :])
            k_hi, k_lo = khi_ref[hp], klo_ref[hp]
            sc = (jnp.dot(k_hi, q_hi, preferred_element_type=F32)
                  + jnp.dot(k_hi, q_lo, preferred_element_type=F32)
                  + jnp.dot(k_lo, q_hi, preferred_element_type=F32))
            if p == 0:
                s1_scr[...] = sc
            else:
                s2_ref[h] = sc
        for tc in range(tm // LANES):
            lanes = slice(tc * LANES, (tc + 1) * LANES)
            s1 = s1_scr[:, lanes]
            s2 = s2_ref[h, :, lanes]
            ta, tb = _sorted_top2(s1, s2)
            tau, m1, m2, rz = _pair_stats(ta, tb)
            c1_ref[h, tc] = jnp.exp(s1 - m1)
            phi_ref[h, tc] = tau - s1
            e2_ref[h, :, lanes] = jnp.exp(s2 - m2) * rz


def _query(x1t, wq_t, keys_hi, keys_lo, tm=256):
    T = x1t.shape[1]
    nc = tm // LANES
    row_spec = pl.BlockSpec((PEER_HEADS, nc, PEER_NKEYS, LANES), lambda i: (0, i, 0, 0))
    col_spec = pl.BlockSpec((PEER_HEADS, PEER_NKEYS, tm), lambda i: (0, 0, i))
    key_spec = pl.BlockSpec((2 * PEER_HEADS, PEER_NKEYS, PEER_HALF), lambda i: (0, 0, 0))
    row_shape = jax.ShapeDtypeStruct((PEER_HEADS, T // LANES, PEER_NKEYS, LANES), F32)
    col_shape = jax.ShapeDtypeStruct((PEER_HEADS, PEER_NKEYS, T), F32)
    return pl.pallas_call(
        functools.partial(_query_kernel, tm=tm),
        grid=(T // tm,),
        in_specs=[
            pl.BlockSpec((D_MODEL, tm), lambda i: (0, i)),
            pl.BlockSpec((D_MODEL, D_MODEL), lambda i: (0, 0)),
            key_spec, key_spec,
        ],
        out_specs=[row_spec, row_spec, col_spec, col_spec],
        out_shape=[row_shape, row_shape, col_shape, col_shape],
        scratch_shapes=[pltpu.VMEM((D_MODEL, tm), F32), pltpu.VMEM((PEER_NKEYS, tm), F32)],
        compiler_params=_params(("parallel",), 48),
        name="query",
    )(x1t, wq_t, keys_hi, keys_lo)


PEER_SB = 32
PEER_RG = 2
SUBLANES = 8


def _gelu(x):
    return 0.5 * x * (1.0 + lax.erf(x * (1.0 / math.sqrt(2.0))))


def _peer_kernel(x1t_ref, u_ref, vt_ref, c1_ref, phi_ref, e2_ref, s2_ref, y_ref,
                 acc_ref, st_scr, ht_scr, *, tm, te):
    j = pl.program_id(1)
    n1 = te // PEER_NKEYS

    @pl.when(j == 0)
    def _():
        acc_ref[...] = jnp.zeros_like(acc_ref)

    st_scr[...] = jnp.dot(u_ref[...], x1t_ref[...], preferred_element_type=F32)

    reps = PEER_SB // SUBLANES
    for r0 in range(0, n1, PEER_RG):
        for tc in range(tm // LANES):
            lanes = slice(tc * LANES, (tc + 1) * LANES)
            for sb in range(PEER_NKEYS // PEER_SB):
                rows = slice(sb * PEER_SB, (sb + 1) * PEER_SB)
                g = [jnp.zeros((PEER_SB, LANES), F32) for _ in range(PEER_RG)]
                for h in range(PEER_HEADS):
                    s2c = s2_ref[h, rows, lanes]
                    e2c = e2_ref[h, rows, lanes]
                    for r in range(PEER_RG):
                        row = pl.ds(j * n1 + r0 + r, SUBLANES, stride=0)
                        phi = jnp.concatenate([phi_ref[h, tc, row, :]] * reps, axis=0)
                        c1 = jnp.concatenate([c1_ref[h, tc, row, :]] * reps, axis=0)
                        g[r] = g[r] + c1 * jnp.where(s2c >= phi, e2c, 0.0)
                for r in range(PEER_RG):
                    base = (r0 + r) * PEER_NKEYS + sb * PEER_SB
                    srows = slice(base, base + PEER_SB)
                    ht_scr[srows, lanes] = (g[r] * _gelu(st_scr[srows, lanes])).astype(BF16)

    acc_ref[...] += jnp.dot(vt_ref[...], ht_scr[...], preferred_element_type=F32)

    @pl.when(j == pl.num_programs(1) - 1)
    def _():
        y_ref[...] = jnp.transpose(acc_ref[...])


def _peer(x1t, u_tab, vt_tab, c1, phi, e2, s2t, tm=512, te=512):
    T = x1t.shape[1]
    E = u_tab.shape[0]
    row_spec = pl.BlockSpec((PEER_HEADS, tm // LANES, PEER_NKEYS, LANES), lambda i, j: (0, i, 0, 0))
    col_spec = pl.BlockSpec((PEER_HEADS, PEER_NKEYS, tm), lambda i, j: (0, 0, i))
    return pl.pallas_call(
        functools.partial(_peer_kernel, tm=tm, te=te),
        grid=(T // tm, E // te),
        in_specs=[
            pl.BlockSpec((D_MODEL, tm), lambda i, j: (0, i)),
            pl.BlockSpec((te, D_MODEL), lambda i, j: (j, 0)),
            pl.BlockSpec((D_MODEL, te), lambda i, j: (0, j)),
            row_spec, row_spec, col_spec, col_spec,
        ],
        out_specs=pl.BlockSpec((tm, D_MODEL), lambda i, j: (i, 0)),
        out_shape=jax.ShapeDtypeStruct((T, D_MODEL), F32),
        scratch_shapes=[
            pltpu.VMEM((D_MODEL, tm), F32),
            pltpu.VMEM((te, tm), F32),
            pltpu.VMEM((te, tm), BF16),
        ],
        compiler_params=_params(("parallel", "arbitrary"), 56),
        name="peer",
    )(x1t, u_tab, vt_tab, c1, phi, e2, s2t)


def _final_kernel(x1_ref, y_ref, p_ref, wg_ref, wp_ref, lw_ref, lb_ref, o_ref):
    x1 = x1_ref[...]
    gate = jax.nn.sigmoid(jnp.dot(x1.astype(BF16), wg_ref[...], preferred_element_type=F32))
    emb = jnp.dot(p_ref[...].astype(BF16), wp_ref[...], preferred_element_type=F32)
    o_ref[...] = _layer_norm(ALPHA * x1 + y_ref[...] + gate * emb, lw_ref[...], lb_ref[...])


def _final(x1, y_ffn, p2, w_gate, w_proj, ln_w, ln_b, tm=256):
    T = x1.shape[0]
    return pl.pallas_call(
        _final_kernel,
        grid=(T // tm,),
        in_specs=[
            pl.BlockSpec((tm, D_MODEL), lambda i: (i, 0)),
            pl.BlockSpec((tm, D_MODEL), lambda i: (i, 0)),
            pl.BlockSpec((tm, PLE_DIM), lambda i: (i, 0)),
            pl.BlockSpec((D_MODEL, D_MODEL), lambda i: (0, 0)),
            pl.BlockSpec((PLE_DIM, D_MODEL), lambda i: (0, 0)),
            pl.BlockSpec((1, D_MODEL), lambda i: (0, 0)),
            pl.BlockSpec((1, D_MODEL), lambda i: (0, 0)),
        ],
        out_specs=pl.BlockSpec((tm, D_MODEL), lambda i: (i, 0)),
        out_shape=jax.ShapeDtypeStruct((T, D_MODEL), F32),
        compiler_params=_params(("parallel",), 48),
        name="final",
    )(x1, y_ffn, p2, w_gate, w_proj, ln_w, ln_b)


def _layer(x2, p2, B, S, w_in, gla_w_gate_up, gla_b_gate, gla_norm_w, pool_w, pool_scale, w_out,
           ln1_w, ln1_b, peer_w_query, peer_sub_keys, peer_u, peer_v, ple_w_gate, ple_w_proj, ln2_w, ln2_b):
    glr0 = COL_R
    w_main = jnp.concatenate([w_in[:, :glr0], w_in[:, glr0 + GLA_GATE_RANK:]], axis=1).astype(BF16)
    w_glr = jnp.pad(w_in[:, glr0:glr0 + GLA_GATE_RANK], ((0, 0), (0, LANES - GLA_GATE_RANK))).astype(BF16)
    proj, glr = _proj(x2, w_main, w_glr)

    y_pool = _pool(proj, pool_w.astype(BF16), pool_scale.reshape(1, POOL_WIDTH), S)

    wg = jnp.pad(gla_w_gate_up, ((0, LANES - GLA_GATE_RANK), (0, 0)))
    wg = wg.reshape(LANES, GLA_HEADS, GLA_DK).transpose(1, 0, 2)
    y_gla = _gla(proj, glr, wg, gla_b_gate.reshape(GLA_HEADS, 1, GLA_DK),
                 gla_norm_w.reshape(GLA_HEADS, 1, GLA_DV), B, S)

    x1, x1t = _outproj(y_pool, y_gla, x2, w_out.astype(BF16),
                       ln1_w.reshape(1, D_MODEL), ln1_b.reshape(1, D_MODEL))

    keys = peer_sub_keys.reshape(2 * PEER_HEADS, PEER_NKEYS, PEER_HALF)
    keys_hi, keys_lo = _split_bf16(keys)
    c1, phi, e2, s2t = _query(x1t, peer_w_query.T.astype(BF16), keys_hi, keys_lo)
    y_ffn = _peer(x1t, peer_u.astype(BF16), peer_v.T.astype(BF16), c1, phi, e2, s2t)

    return _final(x1, y_ffn, p2, ple_w_gate.astype(BF16), ple_w_proj.astype(BF16),
                  ln2_w.reshape(1, D_MODEL), ln2_b.reshape(1, D_MODEL))


def kernel(x, p, w_in, gla_w_gate_up, gla_b_gate, gla_norm_w, pool_w, pool_scale, w_out, ln1_w, ln1_b,
           peer_w_query, peer_sub_keys, peer_u, peer_v, ple_w_gate, ple_w_proj, ln2_w, ln2_b):
    B, S, D = x.shape
    x2 = x.reshape(B * S, D)
    for i in range(w_in.shape[0]):
        x2 = _layer(x2, p[i].reshape(B * S, PLE_DIM), B, S, w_in[i], gla_w_gate_up[i], gla_b_gate[i],
                    gla_norm_w[i], pool_w[i], pool_scale[i], w_out[i], ln1_w[i], ln1_b[i],
                    peer_w_query[i], peer_sub_keys[i], peer_u[i], peer_v[i], ple_w_gate[i],
                    ple_w_proj[i], ln2_w[i], ln2_b[i])
    return x2.reshape(B, S, D)
```

```python
import functools
import math

import jax
import jax.numpy as jnp
from jax import lax
from jax.experimental import pallas as pl
from jax.experimental.pallas import tpu as pltpu

F32 = jnp.float32
BF16 = jnp.bfloat16

D_MODEL = 2048
PLE_DIM = 256
POOL_WIDTH = 1024
POOL_WINDOWS = (2, 4, 8, 16)
POOL_GC = 256
POOL_HALO = 16
GLA_WIDTH = 1024
GLA_HEADS = 4
GLA_DV = 256
GLA_DK = 128
GLA_KEY_WIDTH = 512
GLA_GATE_RANK = 16
GLA_GATE_TEMP = 16.0
GLA_CHUNK = 64
PEER_HEADS = 8
PEER_NKEYS = 128
PEER_HALF = 128
PEER_TOPK = 16
DEPTH = 1
ALPHA = float((2 * DEPTH) ** 0.25)
LN_EPS = 1e-5
RMS_EPS = 1e-6
LANES = 128
NEG_INF = float("-inf")

COL_Q = POOL_WIDTH
COL_K = COL_Q + GLA_KEY_WIDTH
COL_V = COL_K + GLA_KEY_WIDTH
COL_R = COL_V + GLA_WIDTH
PROJ_COLS = COL_R + GLA_WIDTH


def _params(sem, vmem_mib):
    return pltpu.CompilerParams(dimension_semantics=sem, vmem_limit_bytes=vmem_mib * 1024 * 1024)


def _proj_kernel(x_ref, w_ref, wg_ref, o_ref, glr_ref, xb_ref):
    @pl.when(pl.program_id(1) == 0)
    def _():
        xb = x_ref[...].astype(BF16)
        xb_ref[...] = xb
        glr_ref[...] = jnp.dot(xb, wg_ref[...], preferred_element_type=F32)

    o_ref[...] = jnp.dot(xb_ref[...], w_ref[...], preferred_element_type=F32).astype(o_ref.dtype)


def _proj(x2, w_main, w_glr, tm=512, tn=1024):
    T = x2.shape[0]
    return pl.pallas_call(
        _proj_kernel,
        grid=(T // tm, PROJ_COLS // tn),
        in_specs=[
            pl.BlockSpec((tm, D_MODEL), lambda i, n: (i, 0)),
            pl.BlockSpec((D_MODEL, tn), lambda i, n: (0, n)),
            pl.BlockSpec((D_MODEL, LANES), lambda i, n: (0, 0)),
        ],
        out_specs=[
            pl.BlockSpec((tm, tn), lambda i, n: (i, n)),
            pl.BlockSpec((tm, LANES), lambda i, n: (i, 0)),
        ],
        out_shape=[
            jax.ShapeDtypeStruct((T, PROJ_COLS), BF16),
            jax.ShapeDtypeStruct((T, LANES), F32),
        ],
        scratch_shapes=[pltpu.VMEM((tm, D_MODEL), BF16)],
        compiler_params=_params(("parallel", "arbitrary"), 40),
        name="proj",
    )(x2, w_main, w_glr)


def _pool_kernel(u_ref, halo_ref, w_ref, sc_ref, o_ref, ext_ref, *, tiles_per_seq, tm):
    t = pl.program_id(0) % tiles_per_seq
    halo = jnp.where(t == 0, 0.0, halo_ref[...].astype(F32))
    ext_ref[0:POOL_HALO, :] = halo
    ext_ref[POOL_HALO:, :] = u_ref[...].astype(F32)
    pos = t * tm + lax.broadcasted_iota(jnp.int32, (tm, 1), 0)
    for g, w in enumerate(POOL_WINDOWS):
        cols = slice(g * POOL_GC, (g + 1) * POOL_GC)
        u = ext_ref[POOL_HALO:, cols]
        acc = u
        for j in range(1, w):
            acc = acc + ext_ref[POOL_HALO - j:POOL_HALO - j + tm, cols]
        cnt = jnp.minimum(pos + 1, w).astype(F32)
        d = acc / cnt - u
        y = jnp.dot(d.astype(BF16), w_ref[g], preferred_element_type=F32)
        o_ref[:, cols] = (y * sc_ref[:, cols]).astype(o_ref.dtype)


def _pool(proj, pool_w, pool_scale, S, tm=512):
    T = proj.shape[0]
    hb = tm // POOL_HALO
    return pl.pallas_call(
        functools.partial(_pool_kernel, tiles_per_seq=S // tm, tm=tm),
        grid=(T // tm,),
        in_specs=[
            pl.BlockSpec((tm, POOL_WIDTH), lambda i: (i, 0)),
            pl.BlockSpec((POOL_HALO, POOL_WIDTH), lambda i: (jnp.maximum(i * hb - 1, 0), 0)),
            pl.BlockSpec((len(POOL_WINDOWS), POOL_GC, POOL_GC), lambda i: (0, 0, 0)),
            pl.BlockSpec((1, POOL_WIDTH), lambda i: (0, 0)),
        ],
        out_specs=pl.BlockSpec((tm, POOL_WIDTH), lambda i: (i, 0)),
        out_shape=jax.ShapeDtypeStruct((T, POOL_WIDTH), BF16),
        scratch_shapes=[pltpu.VMEM((POOL_HALO + tm, POOL_WIDTH), F32)],
        compiler_params=_params(("parallel",), 32),
        name="pool",
    )(proj, proj, pool_w, pool_scale)


def _gla_kernel(q_ref, k_ref, v_ref, r_ref, glr_ref, wg_ref, bg_ref, nw_ref, o_ref, s_ref, *, n_chunks):
    @pl.when(pl.program_id(2) == 0)
    def _():
        s_ref[...] = jnp.zeros_like(s_ref)

    C = GLA_CHUNK
    row = lax.broadcasted_iota(jnp.int32, (C, C), 0)
    col = lax.broadcasted_iota(jnp.int32, (C, C), 1)
    causal = col <= row
    tril = causal.astype(F32)
    wg = wg_ref[...]
    bg = bg_ref[...]
    nw = nw_ref[...]
    nt = (((1,), (1,)), ((), ()))
    for c in range(n_chunks):
        rows = slice(c * C, (c + 1) * C)
        z = jnp.dot(glr_ref[rows, :], wg, preferred_element_type=F32, precision=lax.Precision.HIGHEST) + bg
        g = jax.nn.log_sigmoid(z) / GLA_GATE_TEMP
        b = jnp.dot(tril, g, preferred_element_type=F32, precision=lax.Precision.HIGHEST)
        b_last = b[C - 1:C, :]
        b_mid = b[C // 2 - 1:C // 2, :]
        q = q_ref[rows, :].astype(F32) * (GLA_DK ** -0.5)
        k = k_ref[rows, :].astype(F32)
        v = v_ref[rows, :]
        q_state = (q * jnp.exp(b)).astype(BF16)
        q_in = (q * jnp.exp(b - b_mid)).astype(BF16)
        k_in = (k * jnp.exp(b_mid - b)).astype(BF16)
        k_out = k * jnp.exp(b_last - b)
        attn = lax.dot_general(q_in, k_in, nt, preferred_element_type=F32)
        attn = jnp.where(causal, attn, 0.0).astype(BF16)
        s = s_ref[...]
        o = (jnp.dot(attn, v, preferred_element_type=F32)
             + jnp.dot(q_state, s.astype(BF16), preferred_element_type=F32))
        decay = jnp.transpose(jnp.broadcast_to(jnp.exp(b_last), (C, GLA_DK)))[:, 0:1]
        s_ref[...] = decay * s + jnp.dot(jnp.transpose(k_out).astype(BF16), v, preferred_element_type=F32)
        o = o * lax.rsqrt(jnp.mean(jnp.square(o), axis=-1, keepdims=True) + RMS_EPS)
        o = o * nw
        r = r_ref[rows, :].astype(F32)
        o_ref[rows, :] = (o * (r * jax.nn.sigmoid(r))).astype(o_ref.dtype)


def _gla(proj, glr, wg, bg, nw, B, S, L=256):
    T = proj.shape[0]
    nl = S // L
    rb = lambda b, l: b * nl + l
    return pl.pallas_call(
        functools.partial(_gla_kernel, n_chunks=L // GLA_CHUNK),
        grid=(B, GLA_HEADS, nl),
        in_specs=[
            pl.BlockSpec((L, GLA_DK), lambda b, h, l: (rb(b, l), COL_Q // GLA_DK + h)),
            pl.BlockSpec((L, GLA_DK), lambda b, h, l: (rb(b, l), COL_K // GLA_DK + h)),
            pl.BlockSpec((L, GLA_DV), lambda b, h, l: (rb(b, l), COL_V // GLA_DV + h)),
            pl.BlockSpec((L, GLA_DV), lambda b, h, l: (rb(b, l), COL_R // GLA_DV + h)),
            pl.BlockSpec((L, LANES), lambda b, h, l: (rb(b, l), 0)),
            pl.BlockSpec((None, LANES, GLA_DK), lambda b, h, l: (h, 0, 0)),
            pl.BlockSpec((None, 1, GLA_DK), lambda b, h, l: (h, 0, 0)),
            pl.BlockSpec((None, 1, GLA_DV), lambda b, h, l: (h, 0, 0)),
        ],
        out_specs=pl.BlockSpec((L, GLA_DV), lambda b, h, l: (rb(b, l), h)),
        out_shape=jax.ShapeDtypeStruct((T, GLA_WIDTH), BF16),
        scratch_shapes=[pltpu.VMEM((GLA_DK, GLA_DV), F32)],
        compiler_params=_params(("parallel", "parallel", "arbitrary"), 32),
        name="gla",
    )(proj, proj, proj, proj, glr, wg, bg, nw)


def _layer_norm(h, w, b):
    mu = jnp.mean(h, axis=-1, keepdims=True)
    hc = h - mu
    var = jnp.mean(jnp.square(hc), axis=-1, keepdims=True)
    return hc * lax.rsqrt(var + LN_EPS) * w + b


def _outproj_kernel(yp_ref, yg_ref, x_ref, w_ref, lw_ref, lb_ref, x1_ref, x1t_ref):
    mix = (jnp.dot(yp_ref[...], w_ref[0:POOL_WIDTH, :], preferred_element_type=F32)
           + jnp.dot(yg_ref[...], w_ref[POOL_WIDTH:, :], preferred_element_type=F32))
    x1 = _layer_norm(ALPHA * x_ref[...] + mix, lw_ref[...], lb_ref[...])
    x1_ref[...] = x1
    x1t_ref[...] = jnp.transpose(x1).astype(BF16)


def _outproj(y_pool, y_gla, x2, w_out, ln_w, ln_b, tm=256):
    T = x2.shape[0]
    return pl.pallas_call(
        _outproj_kernel,
        grid=(T // tm,),
        in_specs=[
            pl.BlockSpec((tm, POOL_WIDTH), lambda i: (i, 0)),
            pl.BlockSpec((tm, GLA_WIDTH), lambda i: (i, 0)),
            pl.BlockSpec((tm, D_MODEL), lambda i: (i, 0)),
            pl.BlockSpec((D_MODEL, D_MODEL), lambda i: (0, 0)),
            pl.BlockSpec((1, D_MODEL), lambda i: (0, 0)),
            pl.BlockSpec((1, D_MODEL), lambda i: (0, 0)),
        ],
        out_specs=[
            pl.BlockSpec((tm, D_MODEL), lambda i: (i, 0)),
            pl.BlockSpec((D_MODEL, tm), lambda i: (0, i)),
        ],
        out_shape=[
            jax.ShapeDtypeStruct((T, D_MODEL), F32),
            jax.ShapeDtypeStruct((D_MODEL, T), BF16),
        ],
        compiler_params=_params(("parallel",), 48),
        name="outproj",
    )(y_pool, y_gla, x2, w_out, ln_w, ln_b)


N_SORT = PEER_TOPK + 1


def _sorted_top2(arr_a, arr_b):
    rid = lax.broadcasted_iota(jnp.int32, (24, LANES), 0)

    def body(kk, carry):
        out = []
        for a, top in carry:
            m = jnp.max(a, axis=0, keepdims=True)
            out.append((jnp.where(a == m, NEG_INF, a), jnp.where(rid == kk, m, top)))
        return tuple(out)

    init = jnp.full((24, LANES), NEG_INF, F32)
    (_, top_a), (_, top_b) = lax.fori_loop(0, N_SORT, body, ((arr_a, init), (arr_b, init)))
    return top_a, top_b


def _pair_stats(ta, tb):
    r8 = lax.broadcasted_iota(jnp.int32, (8, LANES), 0)
    b_lo, b_hi = tb[0:8], tb[8:16]
    a = [ta[i:i + 1] for i in range(8)]
    a16, b16, b0 = ta[16:17], tb[16:17], tb[0:1]
    p2 = jnp.where(r8 < 5, a[2] + b_lo, jnp.where(r8 == 5, a16 + b0, jnp.where(r8 == 6, a[0] + b16, NEG_INF)))
    pieces = [
        a[0] + b_lo, a[0] + b_hi, a[1] + b_lo, p2,
        jnp.where(r8 < 4, a[3] + b_lo, NEG_INF),
        jnp.where(r8 < 3, a[4] + b_lo, NEG_INF),
        jnp.where(r8 < 2, a[5] + b_lo, NEG_INF),
        jnp.where(r8 < 2, a[6] + b_lo, NEG_INF),
        jnp.where(r8 < 2, a[7] + b_lo, NEG_INF),
        ta[8:16] + b0,
    ]
    cand = jnp.concatenate(pieces, axis=0)
    top_sum = a[0] + b0

    def body(kk, carry):
        c, z, v16, v17 = carry
        m = jnp.max(c, axis=0, keepdims=True)
        z = z + jnp.where(kk < PEER_TOPK, jnp.exp(m - top_sum), 0.0)
        v16 = jnp.where(kk == PEER_TOPK - 1, m, v16)
        v17 = jnp.where(kk == PEER_TOPK, m, v17)
        c = jnp.where(c == m, NEG_INF, c)
        return c, z, v16, v17

    zero = jnp.zeros((1, LANES), F32)
    _, z, v16, v17 = lax.fori_loop(0, N_SORT, body, (cand, zero, zero, zero))
    return 0.5 * (v16 + v17), a[0], b0, 1.0 / z


def _split_bf16(x):
    hi = x.astype(BF16)
    return hi, (x - hi.astype(F32)).astype(BF16)


def _query_kernel(x1t_ref, wq_ref, khi_ref, klo_ref, c1_ref, phi_ref, e2_ref, s2_ref, q_ref, s1_scr, *, tm):
    q_ref[...] = jnp.dot(wq_ref[...], x1t_ref[...], preferred_element_type=F32)
    for h in range(PEER_HEADS):
        for p in range(2):
            hp = 2 * h + p
            q_hi, q_lo = _split_bf16(q_ref[hp * PEER_HALF:(hp + 1) * PEER_HALF, :])
            k_hi, k_lo = khi_ref[hp], klo_ref[hp]
            sc = (jnp.dot(k_hi, q_hi, preferred_element_type=F32)
                  + jnp.dot(k_hi, q_lo, preferred_element_type=F32)
                  + jnp.dot(k_lo, q_hi, preferred_element_type=F32))
            if p == 0:
                s1_scr[...] = sc
            else:
                s2_ref[h] = sc
        for tc in range(tm // LANES):
            lanes = slice(tc * LANES, (tc + 1) * LANES)
            s1 = s1_scr[:, lanes]
            s2 = s2_ref[h, :, lanes]
            ta, tb = _sorted_top2(s1, s2)
            tau, m1, m2, rz = _pair_stats(ta, tb)
            c1_ref[h, tc] = jnp.exp(s1 - m1)
            phi_ref[h, tc] = tau - s1
            e2_ref[h, :, lanes] = jnp.exp(s2 - m2) * rz


def _query(x1t, wq_t, keys_hi, keys_lo, tm=256):
    T = x1t.shape[1]
    nc = tm // LANES
    row_spec = pl.BlockSpec((PEER_HEADS, nc, PEER_NKEYS, LANES), lambda i: (0, i, 0, 0))
    col_spec = pl.BlockSpec((PEER_HEADS, PEER_NKEYS, tm), lambda i: (0, 0, i))
    key_spec = pl.BlockSpec((2 * PEER_HEADS, PEER_NKEYS, PEER_HALF), lambda i: (0, 0, 0))
    row_shape = jax.ShapeDtypeStruct((PEER_HEADS, T // LANES, PEER_NKEYS, LANES), F32)
    col_shape = jax.ShapeDtypeStruct((PEER_HEADS, PEER_NKEYS, T), F32)
    return pl.pallas_call(
        functools.partial(_query_kernel, tm=tm),
        grid=(T // tm,),
        in_specs=[
            pl.BlockSpec((D_MODEL, tm), lambda i: (0, i)),
            pl.BlockSpec((D_MODEL, D_MODEL), lambda i: (0, 0)),
            key_spec, key_spec,
        ],
        out_specs=[row_spec, row_spec, col_spec, col_spec],
        out_shape=[row_shape, row_shape, col_shape, col_shape],
        scratch_shapes=[pltpu.VMEM((D_MODEL, tm), F32), pltpu.VMEM((PEER_NKEYS, tm), F32)],
        compiler_params=_params(("parallel",), 48),
        name="query",
    )(x1t, wq_t, keys_hi, keys_lo)


PEER_SB = 32
PEER_RG = 2
SUBLANES = 8
PEER_TE = 1024


def _gelu(x):
    return 0.5 * x * (1.0 + lax.erf(x * (1.0 / math.sqrt(2.0))))


def _peer_kernel(x1t_ref, u_ref, vt_ref, c1_ref, phi_ref, e2_ref, s2_ref, y_ref,
                 acc_ref, st_scr, ht_scr, *, tm, te):
    j = pl.program_id(1)
    n1 = te // PEER_NKEYS
    grows = PEER_RG * PEER_NKEYS

    @pl.when(j == 0)
    def _():
        acc_ref[...] = jnp.zeros_like(acc_ref)

    for gi in range(n1 // PEER_RG):
        crows = slice(gi * grows, (gi + 1) * grows)
        st_scr[crows, :] = jnp.dot(u_ref[crows, :], x1t_ref[...], preferred_element_type=F32)

    reps = PEER_SB // SUBLANES
    for gi in range(n1 // PEER_RG):
        crows = slice(gi * grows, (gi + 1) * grows)
        for tc in range(tm // LANES):
            lanes = slice(tc * LANES, (tc + 1) * LANES)
            for sb in range(PEER_NKEYS // PEER_SB):
                rows = slice(sb * PEER_SB, (sb + 1) * PEER_SB)
                g = [jnp.zeros((PEER_SB, LANES), F32) for _ in range(PEER_RG)]
                for h in range(PEER_HEADS):
                    s2c = s2_ref[h, rows, lanes]
                    e2c = e2_ref[h, rows, lanes]
                    for r in range(PEER_RG):
                        row = pl.ds(j * n1 + gi * PEER_RG + r, SUBLANES, stride=0)
                        phi = jnp.concatenate([phi_ref[h, tc, row, :]] * reps, axis=0)
                        c1 = jnp.concatenate([c1_ref[h, tc, row, :]] * reps, axis=0)
                        g[r] = g[r] + c1 * jnp.where(s2c >= phi, e2c, 0.0)
                for r in range(PEER_RG):
                    base = gi * grows + r * PEER_NKEYS + sb * PEER_SB
                    srows = slice(base, base + PEER_SB)
                    ht_scr[srows, lanes] = (g[r] * _gelu(st_scr[srows, lanes])).astype(BF16)
        acc_ref[...] += jnp.dot(vt_ref[:, crows], ht_scr[crows, :], preferred_element_type=F32)

    @pl.when(j == pl.num_programs(1) - 1)
    def _():
        y_ref[...] = jnp.transpose(acc_ref[...])


def _peer(x1t, u_tab, vt_tiles, c1, phi, e2, s2t, tm=512):
    T = x1t.shape[1]
    nj, _, te = vt_tiles.shape
    once = pl.Buffered(1)
    row_spec = pl.BlockSpec((PEER_HEADS, tm // LANES, PEER_NKEYS, LANES), lambda i, j: (0, i, 0, 0),
                            pipeline_mode=once)
    col_spec = pl.BlockSpec((PEER_HEADS, PEER_NKEYS, tm), lambda i, j: (0, 0, i), pipeline_mode=once)
    return pl.pallas_call(
        functools.partial(_peer_kernel, tm=tm, te=te),
        grid=(T // tm, nj),
        in_specs=[
            pl.BlockSpec((D_MODEL, tm), lambda i, j: (0, i), pipeline_mode=once),
            pl.BlockSpec((te, D_MODEL), lambda i, j: (j, 0)),
            pl.BlockSpec((None, D_MODEL, te), lambda i, j: (j, 0, 0)),
            row_spec, row_spec, col_spec, col_spec,
        ],
        out_specs=pl.BlockSpec((tm, D_MODEL), lambda i, j: (i, 0)),
        out_shape=jax.ShapeDtypeStruct((T, D_MODEL), F32),
        scratch_shapes=[
            pltpu.VMEM((D_MODEL, tm), F32),
            pltpu.VMEM((te, tm), F32),
            pltpu.VMEM((te, tm), BF16),
        ],
        compiler_params=_params(("parallel", "arbitrary"), 56),
        name="peer",
    )(x1t, u_tab, vt_tiles, c1, phi, e2, s2t)


def _final_kernel(x1_ref, y_ref, p_ref, wg_ref, wp_ref, lw_ref, lb_ref, o_ref):
    x1 = x1_ref[...]
    gate = jax.nn.sigmoid(jnp.dot(x1.astype(BF16), wg_ref[...], preferred_element_type=F32))
    emb = jnp.dot(p_ref[...].astype(BF16), wp_ref[...], preferred_element_type=F32)
    o_ref[...] = _layer_norm(ALPHA * x1 + y_ref[...] + gate * emb, lw_ref[...], lb_ref[...])


def _final(x1, y_ffn, p2, w_gate, w_proj, ln_w, ln_b, tm=256):
    T = x1.shape[0]
    return pl.pallas_call(
        _final_kernel,
        grid=(T // tm,),
        in_specs=[
            pl.BlockSpec((tm, D_MODEL), lambda i: (i, 0)),
            pl.BlockSpec((tm, D_MODEL), lambda i: (i, 0)),
            pl.BlockSpec((tm, PLE_DIM), lambda i: (i, 0)),
            pl.BlockSpec((D_MODEL, D_MODEL), lambda i: (0, 0)),
            pl.BlockSpec((PLE_DIM, D_MODEL), lambda i: (0, 0)),
            pl.BlockSpec((1, D_MODEL), lambda i: (0, 0)),
            pl.BlockSpec((1, D_MODEL), lambda i: (0, 0)),
        ],
        out_specs=pl.BlockSpec((tm, D_MODEL), lambda i: (i, 0)),
        out_shape=jax.ShapeDtypeStruct((T, D_MODEL), F32),
        compiler_params=_params(("parallel",), 48),
        name="final",
    )(x1, y_ffn, p2, w_gate, w_proj, ln_w, ln_b)


def _layer(x2, p2, B, S, w_in, gla_w_gate_up, gla_b_gate, gla_norm_w, pool_w, pool_scale, w_out,
           ln1_w, ln1_b, peer_w_query, peer_sub_keys, peer_u, peer_v, ple_w_gate, ple_w_proj, ln2_w, ln2_b):
    glr0 = COL_R
    w_main = jnp.concatenate([w_in[:, :glr0], w_in[:, glr0 + GLA_GATE_RANK:]], axis=1).astype(BF16)
    w_glr = jnp.pad(w_in[:, glr0:glr0 + GLA_GATE_RANK], ((0, 0), (0, LANES - GLA_GATE_RANK))).astype(BF16)
    proj, glr = _proj(x2, w_main, w_glr)

    y_pool = _pool(proj, pool_w.astype(BF16), pool_scale.reshape(1, POOL_WIDTH), S)

    wg = jnp.pad(gla_w_gate_up, ((0, LANES - GLA_GATE_RANK), (0, 0)))
    wg = wg.reshape(LANES, GLA_HEADS, GLA_DK).transpose(1, 0, 2)
    y_gla = _gla(proj, glr, wg, gla_b_gate.reshape(GLA_HEADS, 1, GLA_DK),
                 gla_norm_w.reshape(GLA_HEADS, 1, GLA_DV), B, S)

    x1, x1t = _outproj(y_pool, y_gla, x2, w_out.astype(BF16),
                       ln1_w.reshape(1, D_MODEL), ln1_b.reshape(1, D_MODEL))

    keys = peer_sub_keys.reshape(2 * PEER_HEADS, PEER_NKEYS, PEER_HALF)
    keys_hi, keys_lo = _split_bf16(keys)
    c1, phi, e2, s2t = _query(x1t, peer_w_query.T.astype(BF16), keys_hi, keys_lo)
    n_exp = peer_v.shape[0]
    vt_tiles = peer_v.astype(BF16).reshape(n_exp // PEER_TE, PEER_TE, D_MODEL).transpose(0, 2, 1)
    y_ffn = _peer(x1t, peer_u.astype(BF16), vt_tiles, c1, phi, e2, s2t)

    return _final(x1, y_ffn, p2, ple_w_gate.astype(BF16), ple_w_proj.astype(BF16),
                  ln2_w.reshape(1, D_MODEL), ln2_b.reshape(1, D_MODEL))


def kernel(x, p, w_in, gla_w_gate_up, gla_b_gate, gla_norm_w, pool_w, pool_scale, w_out, ln1_w, ln1_b,
           peer_w_query, peer_sub_keys, peer_u, peer_v, ple_w_gate, ple_w_proj, ln2_w, ln2_b):
    B, S, D = x.shape
    x2 = x.reshape(B * S, D)
    for i in range(w_in.shape[0]):
        x2 = _layer(x2, p[i].reshape(B * S, PLE_DIM), B, S, w_in[i], gla_w_gate_up[i], gla_b_gate[i],
                    gla_norm_w[i], pool_w[i], pool_scale[i], w_out[i], ln1_w[i], ln1_b[i],
                    peer_w_query[i], peer_sub_keys[i], peer_u[i], peer_v[i], ple_w_gate[i],
                    ple_w_proj[i], ln2_w[i], ln2_b[i])
    return x2.reshape(B, S, D)
```

```python
import functools
import math

import jax
import jax.numpy as jnp
from jax import lax
from jax.experimental import pallas as pl
from jax.experimental.pallas import tpu as pltpu

F32 = jnp.float32
BF16 = jnp.bfloat16

D_MODEL = 2048
PLE_DIM = 256
POOL_WIDTH = 1024
POOL_WINDOWS = (2, 4, 8, 16)
POOL_GC = 256
POOL_HALO = 16
GLA_WIDTH = 1024
GLA_HEADS = 4
GLA_DV = 256
GLA_DK = 128
GLA_KEY_WIDTH = 512
GLA_GATE_RANK = 16
GLA_GATE_TEMP = 16.0
GLA_CHUNK = 64
PEER_HEADS = 8
PEER_NKEYS = 128
PEER_HALF = 128
PEER_TOPK = 16
DEPTH = 1
ALPHA = float((2 * DEPTH) ** 0.25)
LN_EPS = 1e-5
RMS_EPS = 1e-6
LANES = 128
NEG_INF = float("-inf")

COL_Q = POOL_WIDTH
COL_K = COL_Q + GLA_KEY_WIDTH
COL_V = COL_K + GLA_KEY_WIDTH
COL_R = COL_V + GLA_WIDTH
PROJ_COLS = COL_R + GLA_WIDTH


def _params(sem, vmem_mib):
    return pltpu.CompilerParams(dimension_semantics=sem, vmem_limit_bytes=vmem_mib * 1024 * 1024)


def _proj_kernel(x_ref, w_ref, wg_ref, o_ref, glr_ref, xb_ref):
    @pl.when(pl.program_id(1) == 0)
    def _():
        xb = x_ref[...].astype(BF16)
        xb_ref[...] = xb
        glr_ref[...] = jnp.dot(xb, wg_ref[...], preferred_element_type=F32)

    o_ref[...] = jnp.dot(xb_ref[...], w_ref[...], preferred_element_type=F32).astype(o_ref.dtype)


def _proj(x2, w_main, w_glr, tm=512, tn=1024):
    T = x2.shape[0]
    return pl.pallas_call(
        _proj_kernel,
        grid=(T // tm, PROJ_COLS // tn),
        in_specs=[
            pl.BlockSpec((tm, D_MODEL), lambda i, n: (i, 0)),
            pl.BlockSpec((D_MODEL, tn), lambda i, n: (0, n)),
            pl.BlockSpec((D_MODEL, LANES), lambda i, n: (0, 0)),
        ],
        out_specs=[
            pl.BlockSpec((tm, tn), lambda i, n: (i, n)),
            pl.BlockSpec((tm, LANES), lambda i, n: (i, 0)),
        ],
        out_shape=[
            jax.ShapeDtypeStruct((T, PROJ_COLS), BF16),
            jax.ShapeDtypeStruct((T, LANES), F32),
        ],
        scratch_shapes=[pltpu.VMEM((tm, D_MODEL), BF16)],
        compiler_params=_params(("parallel", "arbitrary"), 40),
        name="proj",
    )(x2, w_main, w_glr)


def _pool_kernel(u_ref, halo_ref, w_ref, sc_ref, o_ref, ext_ref, *, tiles_per_seq, tm):
    t = pl.program_id(0) % tiles_per_seq
    halo = jnp.where(t == 0, 0.0, halo_ref[...].astype(F32))
    ext_ref[0:POOL_HALO, :] = halo
    ext_ref[POOL_HALO:, :] = u_ref[...].astype(F32)
    pos = t * tm + lax.broadcasted_iota(jnp.int32, (tm, 1), 0)
    for g, w in enumerate(POOL_WINDOWS):
        cols = slice(g * POOL_GC, (g + 1) * POOL_GC)
        u = ext_ref[POOL_HALO:, cols]
        acc = u
        for j in range(1, w):
            acc = acc + ext_ref[POOL_HALO - j:POOL_HALO - j + tm, cols]
        cnt = jnp.minimum(pos + 1, w).astype(F32)
        d = acc / cnt - u
        y = jnp.dot(d.astype(BF16), w_ref[g], preferred_element_type=F32)
        o_ref[:, cols] = (y * sc_ref[:, cols]).astype(o_ref.dtype)


def _pool(proj, pool_w, pool_scale, S, tm=512):
    T = proj.shape[0]
    hb = tm // POOL_HALO
    return pl.pallas_call(
        functools.partial(_pool_kernel, tiles_per_seq=S // tm, tm=tm),
        grid=(T // tm,),
        in_specs=[
            pl.BlockSpec((tm, POOL_WIDTH), lambda i: (i, 0)),
            pl.BlockSpec((POOL_HALO, POOL_WIDTH), lambda i: (jnp.maximum(i * hb - 1, 0), 0)),
            pl.BlockSpec((len(POOL_WINDOWS), POOL_GC, POOL_GC), lambda i: (0, 0, 0)),
            pl.BlockSpec((1, POOL_WIDTH), lambda i: (0, 0)),
        ],
        out_specs=pl.BlockSpec((tm, POOL_WIDTH), lambda i: (i, 0)),
        out_shape=jax.ShapeDtypeStruct((T, POOL_WIDTH), BF16),
        scratch_shapes=[pltpu.VMEM((POOL_HALO + tm, POOL_WIDTH), F32)],
        compiler_params=_params(("parallel",), 32),
        name="pool",
    )(proj, proj, pool_w, pool_scale)


def _gla_kernel(q_ref, k_ref, v_ref, r_ref, glr_ref, wg_ref, bg_ref, nw_ref, o_ref, s_ref, *, n_chunks):
    @pl.when(pl.program_id(2) == 0)
    def _():
        s_ref[...] = jnp.zeros_like(s_ref)

    C = GLA_CHUNK
    row = lax.broadcasted_iota(jnp.int32, (C, C), 0)
    col = lax.broadcasted_iota(jnp.int32, (C, C), 1)
    causal = col <= row
    tril = causal.astype(F32)
    wg = wg_ref[...]
    bg = bg_ref[...]
    nw = nw_ref[...]
    nt = (((1,), (1,)), ((), ()))
    for c in range(n_chunks):
        rows = slice(c * C, (c + 1) * C)
        z = jnp.dot(glr_ref[rows, :], wg, preferred_element_type=F32, precision=lax.Precision.HIGHEST) + bg
        g = jax.nn.log_sigmoid(z) / GLA_GATE_TEMP
        b = jnp.dot(tril, g, preferred_element_type=F32, precision=lax.Precision.HIGHEST)
        b_last = b[C - 1:C, :]
        b_mid = b[C // 2 - 1:C // 2, :]
        q = q_ref[rows, :].astype(F32) * (GLA_DK ** -0.5)
        k = k_ref[rows, :].astype(F32)
        v = v_ref[rows, :]
        q_state = (q * jnp.exp(b)).astype(BF16)
        q_in = (q * jnp.exp(b - b_mid)).astype(BF16)
        k_in = (k * jnp.exp(b_mid - b)).astype(BF16)
        k_out = k * jnp.exp(b_last - b)
        attn = lax.dot_general(q_in, k_in, nt, preferred_element_type=F32)
        attn = jnp.where(causal, attn, 0.0).astype(BF16)
        s = s_ref[...]
        o = (jnp.dot(attn, v, preferred_element_type=F32)
             + jnp.dot(q_state, s.astype(BF16), preferred_element_type=F32))
        decay = jnp.transpose(jnp.broadcast_to(jnp.exp(b_last), (C, GLA_DK)))[:, 0:1]
        s_ref[...] = decay * s + jnp.dot(jnp.transpose(k_out).astype(BF16), v, preferred_element_type=F32)
        o = o * lax.rsqrt(jnp.mean(jnp.square(o), axis=-1, keepdims=True) + RMS_EPS)
        o = o * nw
        r = r_ref[rows, :].astype(F32)
        o_ref[rows, :] = (o * (r * jax.nn.sigmoid(r))).astype(o_ref.dtype)


def _gla(proj, glr, wg, bg, nw, B, S, L=256):
    T = proj.shape[0]
    nl = S // L
    rb = lambda b, l: b * nl + l
    return pl.pallas_call(
        functools.partial(_gla_kernel, n_chunks=L // GLA_CHUNK),
        grid=(B, GLA_HEADS, nl),
        in_specs=[
            pl.BlockSpec((L, GLA_DK), lambda b, h, l: (rb(b, l), COL_Q // GLA_DK + h)),
            pl.BlockSpec((L, GLA_DK), lambda b, h, l: (rb(b, l), COL_K // GLA_DK + h)),
            pl.BlockSpec((L, GLA_DV), lambda b, h, l: (rb(b, l), COL_V // GLA_DV + h)),
            pl.BlockSpec((L, GLA_DV), lambda b, h, l: (rb(b, l), COL_R // GLA_DV + h)),
            pl.BlockSpec((L, LANES), lambda b, h, l: (rb(b, l), 0)),
            pl.BlockSpec((None, LANES, GLA_DK), lambda b, h, l: (h, 0, 0)),
            pl.BlockSpec((None, 1, GLA_DK), lambda b, h, l: (h, 0, 0)),
            pl.BlockSpec((None, 1, GLA_DV), lambda b, h, l: (h, 0, 0)),
        ],
        out_specs=pl.BlockSpec((L, GLA_DV), lambda b, h, l: (rb(b, l), h)),
        out_shape=jax.ShapeDtypeStruct((T, GLA_WIDTH), BF16),
        scratch_shapes=[pltpu.VMEM((GLA_DK, GLA_DV), F32)],
        compiler_params=_params(("parallel", "parallel", "arbitrary"), 32),
        name="gla",
    )(proj, proj, proj, proj, glr, wg, bg, nw)


def _layer_norm(h, w, b):
    mu = jnp.mean(h, axis=-1, keepdims=True)
    hc = h - mu
    var = jnp.mean(jnp.square(hc), axis=-1, keepdims=True)
    return hc * lax.rsqrt(var + LN_EPS) * w + b


def _outproj_kernel(yp_ref, yg_ref, x_ref, w_ref, lw_ref, lb_ref, x1_ref, x1t_ref):
    mix = (jnp.dot(yp_ref[...], w_ref[0:POOL_WIDTH, :], preferred_element_type=F32)
           + jnp.dot(yg_ref[...], w_ref[POOL_WIDTH:, :], preferred_element_type=F32))
    x1 = _layer_norm(ALPHA * x_ref[...] + mix, lw_ref[...], lb_ref[...])
    x1_ref[...] = x1
    x1t_ref[...] = jnp.transpose(x1).astype(BF16)


def _outproj(y_pool, y_gla, x2, w_out, ln_w, ln_b, tm=256):
    T = x2.shape[0]
    return pl.pallas_call(
        _outproj_kernel,
        grid=(T // tm,),
        in_specs=[
            pl.BlockSpec((tm, POOL_WIDTH), lambda i: (i, 0)),
            pl.BlockSpec((tm, GLA_WIDTH), lambda i: (i, 0)),
            pl.BlockSpec((tm, D_MODEL), lambda i: (i, 0)),
            pl.BlockSpec((D_MODEL, D_MODEL), lambda i: (0, 0)),
            pl.BlockSpec((1, D_MODEL), lambda i: (0, 0)),
            pl.BlockSpec((1, D_MODEL), lambda i: (0, 0)),
        ],
        out_specs=[
            pl.BlockSpec((tm, D_MODEL), lambda i: (i, 0)),
            pl.BlockSpec((D_MODEL, tm), lambda i: (0, i)),
        ],
        out_shape=[
            jax.ShapeDtypeStruct((T, D_MODEL), F32),
            jax.ShapeDtypeStruct((D_MODEL, T), BF16),
        ],
        compiler_params=_params(("parallel",), 48),
        name="outproj",
    )(y_pool, y_gla, x2, w_out, ln_w, ln_b)


N_SORT = PEER_TOPK + 1


def _sorted_top2(arr_a, arr_b):
    rid = lax.broadcasted_iota(jnp.int32, (24, LANES), 0)

    def body(kk, carry):
        out = []
        for a, top in carry:
            m = jnp.max(a, axis=0, keepdims=True)
            out.append((jnp.where(a == m, NEG_INF, a), jnp.where(rid == kk, m, top)))
        return tuple(out)

    init = jnp.full((24, LANES), NEG_INF, F32)
    (_, top_a), (_, top_b) = lax.fori_loop(0, N_SORT, body, ((arr_a, init), (arr_b, init)))
    return top_a, top_b


def _pair_stats(ta, tb):
    r8 = lax.broadcasted_iota(jnp.int32, (8, LANES), 0)
    b_lo, b_hi = tb[0:8], tb[8:16]
    a = [ta[i:i + 1] for i in range(8)]
    a16, b16, b0 = ta[16:17], tb[16:17], tb[0:1]
    p2 = jnp.where(r8 < 5, a[2] + b_lo, jnp.where(r8 == 5, a16 + b0, jnp.where(r8 == 6, a[0] + b16, NEG_INF)))
    pieces = [
        a[0] + b_lo, a[0] + b_hi, a[1] + b_lo, p2,
        jnp.where(r8 < 4, a[3] + b_lo, NEG_INF),
        jnp.where(r8 < 3, a[4] + b_lo, NEG_INF),
        jnp.where(r8 < 2, a[5] + b_lo, NEG_INF),
        jnp.where(r8 < 2, a[6] + b_lo, NEG_INF),
        jnp.where(r8 < 2, a[7] + b_lo, NEG_INF),
        ta[8:16] + b0,
    ]
    cand = jnp.concatenate(pieces, axis=0)
    top_sum = a[0] + b0

    def body(kk, carry):
        c, z, v16, v17 = carry
        m = jnp.max(c, axis=0, keepdims=True)
        z = z + jnp.where(kk < PEER_TOPK, jnp.exp(m - top_sum), 0.0)
        v16 = jnp.where(kk == PEER_TOPK - 1, m, v16)
        v17 = jnp.where(kk == PEER_TOPK, m, v17)
        c = jnp.where(c == m, NEG_INF, c)
        return c, z, v16, v17

    zero = jnp.zeros((1, LANES), F32)
    _, z, v16, v17 = lax.fori_loop(0, N_SORT, body, (cand, zero, zero, zero))
    return 0.5 * (v16 + v17), a[0], b0, 1.0 / z


def _split_bf16(x):
    hi = x.astype(BF16)
    return hi, (x - hi.astype(F32)).astype(BF16)


def _query_kernel(x1t_ref, wq_ref, khi_ref, klo_ref, c1_ref, phi_ref, e2_ref, s2_ref, q_ref, s1_scr, *, tm):
    q_ref[...] = jnp.dot(wq_ref[...], x1t_ref[...], preferred_element_type=F32)
    for h in range(PEER_HEADS):
        for p in range(2):
            hp = 2 * h + p
            q_hi, q_lo = _split_bf16(q_ref[hp * PEER_HALF:(hp + 1) * PEER_HALF, :])
            k_hi, k_lo = khi_ref[hp], klo_ref[hp]
            sc = (jnp.dot(k_hi, q_hi, preferred_element_type=F32)
                  + jnp.dot(k_hi, q_lo, preferred_element_type=F32)
                  + jnp.dot(k_lo, q_hi, preferred_element_type=F32))
            if p == 0:
                s1_scr[...] = sc
            else:
                s2_ref[h] = sc
        for tc in range(tm // LANES):
            lanes = slice(tc * LANES, (tc + 1) * LANES)
            s1 = s1_scr[:, lanes]
            s2 = s2_ref[h, :, lanes]
            ta, tb = _sorted_top2(s1, s2)
            tau, m1, m2, rz = _pair_stats(ta, tb)
            c1_ref[h, tc] = jnp.exp(s1 - m1)
            phi_ref[h, tc] = tau - s1
            e2_ref[h, :, lanes] = jnp.exp(s2 - m2) * rz


def _query(x1t, wq_t, keys_hi, keys_lo, tm=256):
    T = x1t.shape[1]
    nc = tm // LANES
    row_spec = pl.BlockSpec((PEER_HEADS, nc, PEER_NKEYS, LANES), lambda i: (0, i, 0, 0))
    col_spec = pl.BlockSpec((PEER_HEADS, PEER_NKEYS, tm), lambda i: (0, 0, i))
    key_spec = pl.BlockSpec((2 * PEER_HEADS, PEER_NKEYS, PEER_HALF), lambda i: (0, 0, 0))
    row_shape = jax.ShapeDtypeStruct((PEER_HEADS, T // LANES, PEER_NKEYS, LANES), F32)
    col_shape = jax.ShapeDtypeStruct((PEER_HEADS, PEER_NKEYS, T), F32)
    return pl.pallas_call(
        functools.partial(_query_kernel, tm=tm),
        grid=(T // tm,),
        in_specs=[
            pl.BlockSpec((D_MODEL, tm), lambda i: (0, i)),
            pl.BlockSpec((D_MODEL, D_MODEL), lambda i: (0, 0)),
            key_spec, key_spec,
        ],
        out_specs=[row_spec, row_spec, col_spec, col_spec],
        out_shape=[row_shape, row_shape, col_shape, col_shape],
        scratch_shapes=[pltpu.VMEM((D_MODEL, tm), F32), pltpu.VMEM((PEER_NKEYS, tm), F32)],
        compiler_params=_params(("parallel",), 48),
        name="query",
    )(x1t, wq_t, keys_hi, keys_lo)


PEER_SB = 32
PEER_RG = 2
SUBLANES = 8
PEER_TE = 1024


def _gelu(x):
    return 0.5 * x * (1.0 + lax.erf(x * (1.0 / math.sqrt(2.0))))


def _peer_kernel(x1t_ref, u_ref, vt_ref, c1_ref, phi_ref, e2_ref, s2_ref, y_ref,
                 acc_ref, st_scr, ht_scr, *, tm, te):
    j = pl.program_id(1)
    n1 = te // PEER_NKEYS
    grows = PEER_RG * PEER_NKEYS

    @pl.when(j == 0)
    def _():
        acc_ref[...] = jnp.zeros_like(acc_ref)

    for gi in range(n1 // PEER_RG):
        crows = slice(gi * grows, (gi + 1) * grows)
        st_scr[crows, :] = jnp.dot(u_ref[crows, :], x1t_ref[...], preferred_element_type=F32)

    reps = PEER_SB // SUBLANES
    for gi in range(n1 // PEER_RG):
        crows = slice(gi * grows, (gi + 1) * grows)
        for tc in range(tm // LANES):
            lanes = slice(tc * LANES, (tc + 1) * LANES)
            for sb in range(PEER_NKEYS // PEER_SB):
                rows = slice(sb * PEER_SB, (sb + 1) * PEER_SB)
                g = [jnp.zeros((PEER_SB, LANES), F32) for _ in range(PEER_RG)]
                for h in range(PEER_HEADS):
                    s2c = s2_ref[h, rows, lanes]
                    e2c = e2_ref[h, rows, lanes]
                    tile_rows = pl.ds(pl.multiple_of(j * n1, SUBLANES), n1)
                    phi_t = phi_ref[h, tc, tile_rows, :]
                    c1_t = c1_ref[h, tc, tile_rows, :]
                    for r in range(PEER_RG):
                        ri = gi * PEER_RG + r
                        phi = jnp.broadcast_to(phi_t[ri:ri + 1, :], (PEER_SB, LANES))
                        c1 = jnp.broadcast_to(c1_t[ri:ri + 1, :], (PEER_SB, LANES))
                        g[r] = g[r] + c1 * jnp.where(s2c >= phi, e2c, 0.0)
                for r in range(PEER_RG):
                    base = gi * grows + r * PEER_NKEYS + sb * PEER_SB
                    srows = slice(base, base + PEER_SB)
                    ht_scr[srows, lanes] = (g[r] * _gelu(st_scr[srows, lanes])).astype(BF16)
        acc_ref[...] += jnp.dot(vt_ref[:, crows], ht_scr[crows, :], preferred_element_type=F32)

    @pl.when(j == pl.num_programs(1) - 1)
    def _():
        y_ref[...] = jnp.transpose(acc_ref[...])


def _peer(x1t, u_tab, vt_tiles, c1, phi, e2, s2t, tm=512):
    T = x1t.shape[1]
    nj, _, te = vt_tiles.shape
    once = pl.Buffered(1)
    row_spec = pl.BlockSpec((PEER_HEADS, tm // LANES, PEER_NKEYS, LANES), lambda i, j: (0, i, 0, 0),
                            pipeline_mode=once)
    col_spec = pl.BlockSpec((PEER_HEADS, PEER_NKEYS, tm), lambda i, j: (0, 0, i), pipeline_mode=once)
    return pl.pallas_call(
        functools.partial(_peer_kernel, tm=tm, te=te),
        grid=(T // tm, nj),
        in_specs=[
            pl.BlockSpec((D_MODEL, tm), lambda i, j: (0, i), pipeline_mode=once),
            pl.BlockSpec((te, D_MODEL), lambda i, j: (j, 0)),
            pl.BlockSpec((None, D_MODEL, te), lambda i, j: (j, 0, 0)),
            row_spec, row_spec, col_spec, col_spec,
        ],
        out_specs=pl.BlockSpec((tm, D_MODEL), lambda i, j: (i, 0)),
        out_shape=jax.ShapeDtypeStruct((T, D_MODEL), F32),
        scratch_shapes=[
            pltpu.VMEM((D_MODEL, tm), F32),
            pltpu.VMEM((te, tm), F32),
            pltpu.VMEM((te, tm), BF16),
        ],
        compiler_params=_params(("parallel", "arbitrary"), 56),
        name="peer",
    )(x1t, u_tab, vt_tiles, c1, phi, e2, s2t)


def _final_kernel(x1_ref, y_ref, p_ref, wg_ref, wp_ref, lw_ref, lb_ref, o_ref):
    x1 = x1_ref[...]
    gate = jax.nn.sigmoid(jnp.dot(x1.astype(BF16), wg_ref[...], preferred_element_type=F32))
    emb = jnp.dot(p_ref[...].astype(BF16), wp_ref[...], preferred_element_type=F32)
    o_ref[...] = _layer_norm(ALPHA * x1 + y_ref[...] + gate * emb, lw_ref[...], lb_ref[...])


def _final(x1, y_ffn, p2, w_gate, w_proj, ln_w, ln_b, tm=256):
    T = x1.shape[0]
    return pl.pallas_call(
        _final_kernel,
        grid=(T // tm,),
        in_specs=[
            pl.BlockSpec((tm, D_MODEL), lambda i: (i, 0)),
            pl.BlockSpec((tm, D_MODEL), lambda i: (i, 0)),
            pl.BlockSpec((tm, PLE_DIM), lambda i: (i, 0)),
            pl.BlockSpec((D_MODEL, D_MODEL), lambda i: (0, 0)),
            pl.BlockSpec((PLE_DIM, D_MODEL), lambda i: (0, 0)),
            pl.BlockSpec((1, D_MODEL), lambda i: (0, 0)),
            pl.BlockSpec((1, D_MODEL), lambda i: (0, 0)),
        ],
        out_specs=pl.BlockSpec((tm, D_MODEL), lambda i: (i, 0)),
        out_shape=jax.ShapeDtypeStruct((T, D_MODEL), F32),
        compiler_params=_params(("parallel",), 48),
        name="final",
    )(x1, y_ffn, p2, w_gate, w_proj, ln_w, ln_b)


def _layer(x2, p2, B, S, w_in, gla_w_gate_up, gla_b_gate, gla_norm_w, pool_w, pool_scale, w_out,
           ln1_w, ln1_b, peer_w_query, peer_sub_keys, peer_u, peer_v, ple_w_gate, ple_w_proj, ln2_w, ln2_b):
    glr0 = COL_R
    w_main = jnp.concatenate([w_in[:, :glr0], w_in[:, glr0 + GLA_GATE_RANK:]], axis=1).astype(BF16)
    w_glr = jnp.pad(w_in[:, glr0:glr0 + GLA_GATE_RANK], ((0, 0), (0, LANES - GLA_GATE_RANK))).astype(BF16)
    proj, glr = _proj(x2, w_main, w_glr)

    y_pool = _pool(proj, pool_w.astype(BF16), pool_scale.reshape(1, POOL_WIDTH), S)

    wg = jnp.pad(gla_w_gate_up, ((0, LANES - GLA_GATE_RANK), (0, 0)))
    wg = wg.reshape(LANES, GLA_HEADS, GLA_DK).transpose(1, 0, 2)
    y_gla = _gla(proj, glr, wg, gla_b_gate.reshape(GLA_HEADS, 1, GLA_DK),
                 gla_norm_w.reshape(GLA_HEADS, 1, GLA_DV), B, S)

    x1, x1t = _outproj(y_pool, y_gla, x2, w_out.astype(BF16),
                       ln1_w.reshape(1, D_MODEL), ln1_b.reshape(1, D_MODEL))

    keys = peer_sub_keys.reshape(2 * PEER_HEADS, PEER_NKEYS, PEER_HALF)
    keys_hi, keys_lo = _split_bf16(keys)
    c1, phi, e2, s2t = _query(x1t, peer_w_query.T.astype(BF16), keys_hi, keys_lo)
    n_exp = peer_v.shape[0]
    vt_tiles = peer_v.astype(BF16).reshape(n_exp // PEER_TE, PEER_TE, D_MODEL).transpose(0, 2, 1)
    y_ffn = _peer(x1t, peer_u.astype(BF16), vt_tiles, c1, phi, e2, s2t)

    return _final(x1, y_ffn, p2, ple_w_gate.astype(BF16), ple_w_proj.astype(BF16),
                  ln2_w.reshape(1, D_MODEL), ln2_b.reshape(1, D_MODEL))


def kernel(x, p, w_in, gla_w_gate_up, gla_b_gate, gla_norm_w, pool_w, pool_scale, w_out, ln1_w, ln1_b,
           peer_w_query, peer_sub_keys, peer_u, peer_v, ple_w_gate, ple_w_proj, ln2_w, ln2_b):
    B, S, D = x.shape
    x2 = x.reshape(B * S, D)
    for i in range(w_in.shape[0]):
        x2 = _layer(x2, p[i].reshape(B * S, PLE_DIM), B, S, w_in[i], gla_w_gate_up[i], gla_b_gate[i],
                    gla_norm_w[i], pool_w[i], pool_scale[i], w_out[i], ln1_w[i], ln1_b[i],
                    peer_w_query[i], peer_sub_keys[i], peer_u[i], peer_v[i], ple_w_gate[i],
                    ple_w_proj[i], ln2_w[i], ln2_b[i])
    return x2.reshape(B, S, D)
```

```python
import functools
import math

import jax
import jax.numpy as jnp
from jax import lax
from jax.experimental import pallas as pl
from jax.experimental.pallas import tpu as pltpu

F32 = jnp.float32
BF16 = jnp.bfloat16

D_MODEL = 2048
PLE_DIM = 256
POOL_WIDTH = 1024
POOL_WINDOWS = (2, 4, 8, 16)
POOL_GC = 256
POOL_HALO = 16
GLA_WIDTH = 1024
GLA_HEADS = 4
GLA_DV = 256
GLA_DK = 128
GLA_KEY_WIDTH = 512
GLA_GATE_RANK = 16
GLA_GATE_TEMP = 16.0
GLA_CHUNK = 64
PEER_HEADS = 8
PEER_NKEYS = 128
PEER_HALF = 128
PEER_TOPK = 16
DEPTH = 1
ALPHA = float((2 * DEPTH) ** 0.25)
LN_EPS = 1e-5
RMS_EPS = 1e-6
LANES = 128
NEG_INF = float("-inf")

COL_Q = POOL_WIDTH
COL_K = COL_Q + GLA_KEY_WIDTH
COL_V = COL_K + GLA_KEY_WIDTH
COL_R = COL_V + GLA_WIDTH
PROJ_COLS = COL_R + GLA_WIDTH


def _params(sem, vmem_mib):
    return pltpu.CompilerParams(dimension_semantics=sem, vmem_limit_bytes=vmem_mib * 1024 * 1024)


def _proj_kernel(x_ref, w_ref, wg_ref, o_ref, glr_ref, xb_ref):
    @pl.when(pl.program_id(1) == 0)
    def _():
        xb = x_ref[...].astype(BF16)
        xb_ref[...] = xb
        glr_ref[...] = jnp.dot(xb, wg_ref[...], preferred_element_type=F32)

    o_ref[...] = jnp.dot(xb_ref[...], w_ref[...], preferred_element_type=F32).astype(o_ref.dtype)


def _proj(x2, w_main, w_glr, tm=512, tn=1024):
    T = x2.shape[0]
    return pl.pallas_call(
        _proj_kernel,
        grid=(T // tm, PROJ_COLS // tn),
        in_specs=[
            pl.BlockSpec((tm, D_MODEL), lambda i, n: (i, 0)),
            pl.BlockSpec((D_MODEL, tn), lambda i, n: (0, n)),
            pl.BlockSpec((D_MODEL, LANES), lambda i, n: (0, 0)),
        ],
        out_specs=[
            pl.BlockSpec((tm, tn), lambda i, n: (i, n)),
            pl.BlockSpec((tm, LANES), lambda i, n: (i, 0)),
        ],
        out_shape=[
            jax.ShapeDtypeStruct((T, PROJ_COLS), BF16),
            jax.ShapeDtypeStruct((T, LANES), F32),
        ],
        scratch_shapes=[pltpu.VMEM((tm, D_MODEL), BF16)],
        compiler_params=_params(("parallel", "arbitrary"), 40),
        name="proj",
    )(x2, w_main, w_glr)


def _pool_kernel(u_ref, halo_ref, w_ref, sc_ref, o_ref, ext_ref, *, tiles_per_seq, tm):
    t = pl.program_id(0) % tiles_per_seq
    halo = jnp.where(t == 0, 0.0, halo_ref[...].astype(F32))
    ext_ref[0:POOL_HALO, :] = halo
    ext_ref[POOL_HALO:, :] = u_ref[...].astype(F32)
    pos = t * tm + lax.broadcasted_iota(jnp.int32, (tm, 1), 0)
    for g, w in enumerate(POOL_WINDOWS):
        cols = slice(g * POOL_GC, (g + 1) * POOL_GC)
        u = ext_ref[POOL_HALO:, cols]
        acc = u
        for j in range(1, w):
            acc = acc + ext_ref[POOL_HALO - j:POOL_HALO - j + tm, cols]
        cnt = jnp.minimum(pos + 1, w).astype(F32)
        d = acc / cnt - u
        y = jnp.dot(d.astype(BF16), w_ref[g], preferred_element_type=F32)
        o_ref[:, cols] = (y * sc_ref[:, cols]).astype(o_ref.dtype)


def _pool(proj, pool_w, pool_scale, S, tm=512):
    T = proj.shape[0]
    hb = tm // POOL_HALO
    return pl.pallas_call(
        functools.partial(_pool_kernel, tiles_per_seq=S // tm, tm=tm),
        grid=(T // tm,),
        in_specs=[
            pl.BlockSpec((tm, POOL_WIDTH), lambda i: (i, 0)),
            pl.BlockSpec((POOL_HALO, POOL_WIDTH), lambda i: (jnp.maximum(i * hb - 1, 0), 0)),
            pl.BlockSpec((len(POOL_WINDOWS), POOL_GC, POOL_GC), lambda i: (0, 0, 0)),
            pl.BlockSpec((1, POOL_WIDTH), lambda i: (0, 0)),
        ],
        out_specs=pl.BlockSpec((tm, POOL_WIDTH), lambda i: (i, 0)),
        out_shape=jax.ShapeDtypeStruct((T, POOL_WIDTH), BF16),
        scratch_shapes=[pltpu.VMEM((POOL_HALO + tm, POOL_WIDTH), F32)],
        compiler_params=_params(("parallel",), 32),
        name="pool",
    )(proj, proj, pool_w, pool_scale)


def _gla_kernel(q_ref, k_ref, v_ref, r_ref, glr_ref, wg_ref, bg_ref, nw_ref, o_ref, s_ref, *, n_chunks):
    @pl.when(pl.program_id(2) == 0)
    def _():
        s_ref[...] = jnp.zeros_like(s_ref)

    C = GLA_CHUNK
    row = lax.broadcasted_iota(jnp.int32, (C, C), 0)
    col = lax.broadcasted_iota(jnp.int32, (C, C), 1)
    causal = col <= row
    tril = causal.astype(F32)
    wg = wg_ref[...]
    bg = bg_ref[...]
    nw = nw_ref[...]
    nt = (((1,), (1,)), ((), ()))
    for c in range(n_chunks):
        rows = slice(c * C, (c + 1) * C)
        z = jnp.dot(glr_ref[rows, :], wg, preferred_element_type=F32, precision=lax.Precision.HIGHEST) + bg
        g = jax.nn.log_sigmoid(z) / GLA_GATE_TEMP
        b = jnp.dot(tril, g, preferred_element_type=F32, precision=lax.Precision.HIGHEST)
        b_last = b[C - 1:C, :]
        b_mid = b[C // 2 - 1:C // 2, :]
        q = q_ref[rows, :].astype(F32) * (GLA_DK ** -0.5)
        k = k_ref[rows, :].astype(F32)
        v = v_ref[rows, :]
        q_state = (q * jnp.exp(b)).astype(BF16)
        q_in = (q * jnp.exp(b - b_mid)).astype(BF16)
        k_in = (k * jnp.exp(b_mid - b)).astype(BF16)
        k_out = k * jnp.exp(b_last - b)
        attn = lax.dot_general(q_in, k_in, nt, preferred_element_type=F32)
        attn = jnp.where(causal, attn, 0.0).astype(BF16)
        s = s_ref[...]
        o = (jnp.dot(attn, v, preferred_element_type=F32)
             + jnp.dot(q_state, s.astype(BF16), preferred_element_type=F32))
        decay = jnp.transpose(jnp.broadcast_to(jnp.exp(b_last), (C, GLA_DK)))[:, 0:1]
        s_ref[...] = decay * s + jnp.dot(jnp.transpose(k_out).astype(BF16), v, preferred_element_type=F32)
        o = o * lax.rsqrt(jnp.mean(jnp.square(o), axis=-1, keepdims=True) + RMS_EPS)
        o = o * nw
        r = r_ref[rows, :].astype(F32)
        o_ref[rows, :] = (o * (r * jax.nn.sigmoid(r))).astype(o_ref.dtype)


def _gla(proj, glr, wg, bg, nw, B, S, L=256):
    T = proj.shape[0]
    nl = S // L
    rb = lambda b, l: b * nl + l
    return pl.pallas_call(
        functools.partial(_gla_kernel, n_chunks=L // GLA_CHUNK),
        grid=(B, GLA_HEADS, nl),
        in_specs=[
            pl.BlockSpec((L, GLA_DK), lambda b, h, l: (rb(b, l), COL_Q // GLA_DK + h)),
            pl.BlockSpec((L, GLA_DK), lambda b, h, l: (rb(b, l), COL_K // GLA_DK + h)),
            pl.BlockSpec((L, GLA_DV), lambda b, h, l: (rb(b, l), COL_V // GLA_DV + h)),
            pl.BlockSpec((L, GLA_DV), lambda b, h, l: (rb(b, l), COL_R // GLA_DV + h)),
            pl.BlockSpec((L, LANES), lambda b, h, l: (rb(b, l), 0)),
            pl.BlockSpec((None, LANES, GLA_DK), lambda b, h, l: (h, 0, 0)),
            pl.BlockSpec((None, 1, GLA_DK), lambda b, h, l: (h, 0, 0)),
            pl.BlockSpec((None, 1, GLA_DV), lambda b, h, l: (h, 0, 0)),
        ],
        out_specs=pl.BlockSpec((L, GLA_DV), lambda b, h, l: (rb(b, l), h)),
        out_shape=jax.ShapeDtypeStruct((T, GLA_WIDTH), BF16),
        scratch_shapes=[pltpu.VMEM((GLA_DK, GLA_DV), F32)],
        compiler_params=_params(("parallel", "parallel", "arbitrary"), 32),
        name="gla",
    )(proj, proj, proj, proj, glr, wg, bg, nw)


def _layer_norm(h, w, b):
    mu = jnp.mean(h, axis=-1, keepdims=True)
    hc = h - mu
    var = jnp.mean(jnp.square(hc), axis=-1, keepdims=True)
    return hc * lax.rsqrt(var + LN_EPS) * w + b


def _outproj_kernel(yp_ref, yg_ref, x_ref, w_ref, lw_ref, lb_ref, x1_ref, x1t_ref):
    mix = (jnp.dot(yp_ref[...], w_ref[0:POOL_WIDTH, :], preferred_element_type=F32)
           + jnp.dot(yg_ref[...], w_ref[POOL_WIDTH:, :], preferred_element_type=F32))
    x1 = _layer_norm(ALPHA * x_ref[...] + mix, lw_ref[...], lb_ref[...])
    x1_ref[...] = x1
    x1t_ref[...] = jnp.transpose(x1).astype(BF16)


def _outproj(y_pool, y_gla, x2, w_out, ln_w, ln_b, tm=256):
    T = x2.shape[0]
    return pl.pallas_call(
        _outproj_kernel,
        grid=(T // tm,),
        in_specs=[
            pl.BlockSpec((tm, POOL_WIDTH), lambda i: (i, 0)),
            pl.BlockSpec((tm, GLA_WIDTH), lambda i: (i, 0)),
            pl.BlockSpec((tm, D_MODEL), lambda i: (i, 0)),
            pl.BlockSpec((D_MODEL, D_MODEL), lambda i: (0, 0)),
            pl.BlockSpec((1, D_MODEL), lambda i: (0, 0)),
            pl.BlockSpec((1, D_MODEL), lambda i: (0, 0)),
        ],
        out_specs=[
            pl.BlockSpec((tm, D_MODEL), lambda i: (i, 0)),
            pl.BlockSpec((D_MODEL, tm), lambda i: (0, i)),
        ],
        out_shape=[
            jax.ShapeDtypeStruct((T, D_MODEL), F32),
            jax.ShapeDtypeStruct((D_MODEL, T), BF16),
        ],
        compiler_params=_params(("parallel",), 48),
        name="outproj",
    )(y_pool, y_gla, x2, w_out, ln_w, ln_b)


N_SORT = PEER_TOPK + 1
UNRANKED = 127.0


def _sorted_top2(arr_a, arr_b):
    rid = lax.broadcasted_iota(jnp.int32, (24, LANES), 0)

    def body(kk, carry):
        a, top_a, b, top_b, rank_b = carry
        ma = jnp.max(a, axis=0, keepdims=True)
        mb = jnp.max(b, axis=0, keepdims=True)
        hit_b = b == mb
        return (jnp.where(a == ma, NEG_INF, a), jnp.where(rid == kk, ma, top_a),
                jnp.where(hit_b, NEG_INF, b), jnp.where(rid == kk, mb, top_b),
                jnp.where(hit_b, kk.astype(F32), rank_b))

    top0 = jnp.full((24, LANES), NEG_INF, F32)
    rank0 = jnp.full(arr_b.shape, UNRANKED, F32)
    _, top_a, _, top_b, rank_b = lax.fori_loop(0, N_SORT, body, (arr_a, top0, arr_b, top0, rank0))
    return top_a, top_b, rank_b


def _pair_stats(ta, tb):
    r8 = lax.broadcasted_iota(jnp.int32, (8, LANES), 0)
    b_lo, b_hi = tb[0:8], tb[8:16]
    a = [ta[i:i + 1] for i in range(8)]
    a16, b16, b0 = ta[16:17], tb[16:17], tb[0:1]
    p2 = jnp.where(r8 < 5, a[2] + b_lo, jnp.where(r8 == 5, a16 + b0, jnp.where(r8 == 6, a[0] + b16, NEG_INF)))
    pieces = [
        a[0] + b_lo, a[0] + b_hi, a[1] + b_lo, p2,
        jnp.where(r8 < 4, a[3] + b_lo, NEG_INF),
        jnp.where(r8 < 3, a[4] + b_lo, NEG_INF),
        jnp.where(r8 < 2, a[5] + b_lo, NEG_INF),
        jnp.where(r8 < 2, a[6] + b_lo, NEG_INF),
        jnp.where(r8 < 2, a[7] + b_lo, NEG_INF),
        ta[8:16] + b0,
    ]
    cand = jnp.concatenate(pieces, axis=0)
    top_sum = a[0] + b0

    def body(kk, carry):
        c, z, v16, v17 = carry
        m = jnp.max(c, axis=0, keepdims=True)
        z = z + jnp.where(kk < PEER_TOPK, jnp.exp(m - top_sum), 0.0)
        v16 = jnp.where(kk == PEER_TOPK - 1, m, v16)
        v17 = jnp.where(kk == PEER_TOPK, m, v17)
        c = jnp.where(c == m, NEG_INF, c)
        return c, z, v16, v17

    zero = jnp.zeros((1, LANES), F32)
    _, z, v16, v17 = lax.fori_loop(0, N_SORT, body, (cand, zero, zero, zero))
    return 0.5 * (v16 + v17), a[0], b0, 1.0 / z


def _split_bf16(x):
    hi = x.astype(BF16)
    return hi, (x - hi.astype(F32)).astype(BF16)


def _query_kernel(x1t_ref, wq_ref, khi_ref, klo_ref, c1_ref, n1_ref, e2_ref, r2_ref, q_ref, s1_scr, s2_scr, *, tm):
    q_ref[...] = jnp.dot(wq_ref[...], x1t_ref[...], preferred_element_type=F32)
    for h in range(PEER_HEADS):
        for p in range(2):
            hp = 2 * h + p
            q_hi, q_lo = _split_bf16(q_ref[hp * PEER_HALF:(hp + 1) * PEER_HALF, :])
            k_hi, k_lo = khi_ref[hp], klo_ref[hp]
            sc = (jnp.dot(k_hi, q_hi, preferred_element_type=F32)
                  + jnp.dot(k_hi, q_lo, preferred_element_type=F32)
                  + jnp.dot(k_lo, q_hi, preferred_element_type=F32))
            if p == 0:
                s1_scr[...] = sc
            else:
                s2_scr[...] = sc
        for tc in range(tm // LANES):
            lanes = slice(tc * LANES, (tc + 1) * LANES)
            s1 = s1_scr[:, lanes]
            s2 = s2_scr[:, lanes]
            ta, tb, rank2 = _sorted_top2(s1, s2)
            tau, m1, m2, rz = _pair_stats(ta, tb)
            n1 = jnp.zeros_like(s1)
            for jj in range(PEER_TOPK):
                n1 = n1 + jnp.where(s1 + tb[jj:jj + 1] >= tau, 1.0, 0.0)
            c1_ref[h, tc] = jnp.exp(s1 - m1)
            n1_ref[h, tc] = n1
            e2_ref[h, :, lanes] = (jnp.exp(s2 - m2) * rz).astype(BF16)
            r2_ref[h, :, lanes] = rank2.astype(BF16)


def _query(x1t, wq_t, keys_hi, keys_lo, tm=256):
    T = x1t.shape[1]
    nc = tm // LANES
    row_spec = pl.BlockSpec((PEER_HEADS, nc, PEER_NKEYS, LANES), lambda i: (0, i, 0, 0))
    col_spec = pl.BlockSpec((PEER_HEADS, PEER_NKEYS, tm), lambda i: (0, 0, i))
    key_spec = pl.BlockSpec((2 * PEER_HEADS, PEER_NKEYS, PEER_HALF), lambda i: (0, 0, 0))
    row_shape = jax.ShapeDtypeStruct((PEER_HEADS, T // LANES, PEER_NKEYS, LANES), F32)
    col_shape = jax.ShapeDtypeStruct((PEER_HEADS, PEER_NKEYS, T), BF16)
    return pl.pallas_call(
        functools.partial(_query_kernel, tm=tm),
        grid=(T // tm,),
        in_specs=[
            pl.BlockSpec((D_MODEL, tm), lambda i: (0, i)),
            pl.BlockSpec((D_MODEL, D_MODEL), lambda i: (0, 0)),
            key_spec, key_spec,
        ],
        out_specs=[row_spec, row_spec, col_spec, col_spec],
        out_shape=[row_shape, row_shape, col_shape, col_shape],
        scratch_shapes=[pltpu.VMEM((D_MODEL, tm), F32), pltpu.VMEM((PEER_NKEYS, tm), F32),
                        pltpu.VMEM((PEER_NKEYS, tm), F32)],
        compiler_params=_params(("parallel",), 48),
        name="query",
    )(x1t, wq_t, keys_hi, keys_lo)


PEER_SB = 64
PEER_RG = 2
SUBLANES = 8
PEER_TE = 1024


def _gelu(x):
    return 0.5 * x * (1.0 + lax.erf(x * (1.0 / math.sqrt(2.0))))


def _bcast_row_bf16(tile, ri, rows):
    packed = jnp.broadcast_to(tile[ri:ri + 1, :], (2 * SUBLANES, LANES)).astype(BF16)
    return jnp.concatenate([packed] * (rows // (2 * SUBLANES)), axis=0)


def _peer_kernel(x1t_ref, u_ref, vt_ref, c1_ref, n1_ref, e2_ref, r2_ref, y_ref,
                 acc_ref, st_scr, ht_scr, *, tm, te):
    j = pl.program_id(1)
    n1 = te // PEER_NKEYS
    grows = PEER_RG * PEER_NKEYS

    @pl.when(j == 0)
    def _():
        acc_ref[...] = jnp.zeros_like(acc_ref)

    for gi in range(n1 // PEER_RG):
        crows = slice(gi * grows, (gi + 1) * grows)
        st_scr[crows, :] = jnp.dot(u_ref[crows, :], x1t_ref[...], preferred_element_type=F32)

    nsb = PEER_NKEYS // PEER_SB
    tile_rows = pl.ds(pl.multiple_of(j * n1, SUBLANES), n1)
    for gi in range(n1 // PEER_RG):
        crows = slice(gi * grows, (gi + 1) * grows)
        for tc in range(tm // LANES):
            lanes = slice(tc * LANES, (tc + 1) * LANES)
            g = [[jnp.zeros((PEER_SB, LANES), BF16) for _ in range(nsb)] for _ in range(PEER_RG)]
            for h in range(PEER_HEADS):
                c1_t = c1_ref[h, tc, tile_rows, :]
                n1_t = n1_ref[h, tc, tile_rows, :]
                c1b = [_bcast_row_bf16(c1_t, gi * PEER_RG + r, PEER_SB) for r in range(PEER_RG)]
                n1b = [_bcast_row_bf16(n1_t, gi * PEER_RG + r, PEER_SB) for r in range(PEER_RG)]
                for sb in range(nsb):
                    rows = slice(sb * PEER_SB, (sb + 1) * PEER_SB)
                    r2c = r2_ref[h, rows, lanes]
                    e2c = e2_ref[h, rows, lanes]
                    for r in range(PEER_RG):
                        g[r][sb] = g[r][sb] + c1b[r] * jnp.where(r2c < n1b[r], e2c, jnp.zeros_like(e2c))
            for r in range(PEER_RG):
                for sb in range(nsb):
                    base = gi * grows + r * PEER_NKEYS + sb * PEER_SB
                    srows = slice(base, base + PEER_SB)
                    ht_scr[srows, lanes] = g[r][sb] * _gelu(st_scr[srows, lanes]).astype(BF16)
        acc_ref[...] += jnp.dot(vt_ref[:, crows], ht_scr[crows, :], preferred_element_type=F32)

    @pl.when(j == pl.num_programs(1) - 1)
    def _():
        y_ref[...] = jnp.transpose(acc_ref[...])


def _peer(x1t, u_tab, vt_tiles, c1, n1, e2, r2, tm=512):
    T = x1t.shape[1]
    nj, _, te = vt_tiles.shape
    once = pl.Buffered(1)
    row_spec = pl.BlockSpec((PEER_HEADS, tm // LANES, PEER_NKEYS, LANES), lambda i, j: (0, i, 0, 0),
                            pipeline_mode=once)
    col_spec = pl.BlockSpec((PEER_HEADS, PEER_NKEYS, tm), lambda i, j: (0, 0, i), pipeline_mode=once)
    return pl.pallas_call(
        functools.partial(_peer_kernel, tm=tm, te=te),
        grid=(T // tm, nj),
        in_specs=[
            pl.BlockSpec((D_MODEL, tm), lambda i, j: (0, i), pipeline_mode=once),
            pl.BlockSpec((te, D_MODEL), lambda i, j: (j, 0)),
            pl.BlockSpec((None, D_MODEL, te), lambda i, j: (j, 0, 0)),
            row_spec, row_spec, col_spec, col_spec,
        ],
        out_specs=pl.BlockSpec((tm, D_MODEL), lambda i, j: (i, 0)),
        out_shape=jax.ShapeDtypeStruct((T, D_MODEL), F32),
        scratch_shapes=[
            pltpu.VMEM((D_MODEL, tm), F32),
            pltpu.VMEM((te, tm), F32),
            pltpu.VMEM((te, tm), BF16),
        ],
        compiler_params=_params(("parallel", "arbitrary"), 56),
        name="peer",
    )(x1t, u_tab, vt_tiles, c1, n1, e2, r2)


def _final_kernel(x1_ref, y_ref, p_ref, wg_ref, wp_ref, lw_ref, lb_ref, o_ref):
    x1 = x1_ref[...]
    gate = jax.nn.sigmoid(jnp.dot(x1.astype(BF16), wg_ref[...], preferred_element_type=F32))
    emb = jnp.dot(p_ref[...].astype(BF16), wp_ref[...], preferred_element_type=F32)
    o_ref[...] = _layer_norm(ALPHA * x1 + y_ref[...] + gate * emb, lw_ref[...], lb_ref[...])


def _final(x1, y_ffn, p2, w_gate, w_proj, ln_w, ln_b, tm=256):
    T = x1.shape[0]
    return pl.pallas_call(
        _final_kernel,
        grid=(T // tm,),
        in_specs=[
            pl.BlockSpec((tm, D_MODEL), lambda i: (i, 0)),
            pl.BlockSpec((tm, D_MODEL), lambda i: (i, 0)),
            pl.BlockSpec((tm, PLE_DIM), lambda i: (i, 0)),
            pl.BlockSpec((D_MODEL, D_MODEL), lambda i: (0, 0)),
            pl.BlockSpec((PLE_DIM, D_MODEL), lambda i: (0, 0)),
            pl.BlockSpec((1, D_MODEL), lambda i: (0, 0)),
            pl.BlockSpec((1, D_MODEL), lambda i: (0, 0)),
        ],
        out_specs=pl.BlockSpec((tm, D_MODEL), lambda i: (i, 0)),
        out_shape=jax.ShapeDtypeStruct((T, D_MODEL), F32),
        compiler_params=_params(("parallel",), 48),
        name="final",
    )(x1, y_ffn, p2, w_gate, w_proj, ln_w, ln_b)


def _layer(x2, p2, B, S, w_in, gla_w_gate_up, gla_b_gate, gla_norm_w, pool_w, pool_scale, w_out,
           ln1_w, ln1_b, peer_w_query, peer_sub_keys, peer_u, peer_v, ple_w_gate, ple_w_proj, ln2_w, ln2_b):
    glr0 = COL_R
    w_main = jnp.concatenate([w_in[:, :glr0], w_in[:, glr0 + GLA_GATE_RANK:]], axis=1).astype(BF16)
    w_glr = jnp.pad(w_in[:, glr0:glr0 + GLA_GATE_RANK], ((0, 0), (0, LANES - GLA_GATE_RANK))).astype(BF16)
    proj, glr = _proj(x2, w_main, w_glr)

    y_pool = _pool(proj, pool_w.astype(BF16), pool_scale.reshape(1, POOL_WIDTH), S)

    wg = jnp.pad(gla_w_gate_up, ((0, LANES - GLA_GATE_RANK), (0, 0)))
    wg = wg.reshape(LANES, GLA_HEADS, GLA_DK).transpose(1, 0, 2)
    y_gla = _gla(proj, glr, wg, gla_b_gate.reshape(GLA_HEADS, 1, GLA_DK),
                 gla_norm_w.reshape(GLA_HEADS, 1, GLA_DV), B, S)

    x1, x1t = _outproj(y_pool, y_gla, x2, w_out.astype(BF16),
                       ln1_w.reshape(1, D_MODEL), ln1_b.reshape(1, D_MODEL))

    keys = peer_sub_keys.reshape(2 * PEER_HEADS, PEER_NKEYS, PEER_HALF)
    keys_hi, keys_lo = _split_bf16(keys)
    c1, n1, e2, r2 = _query(x1t, peer_w_query.T.astype(BF16), keys_hi, keys_lo)
    n_exp = peer_v.shape[0]
    vt_tiles = peer_v.astype(BF16).reshape(n_exp // PEER_TE, PEER_TE, D_MODEL).transpose(0, 2, 1)
    y_ffn = _peer(x1t, peer_u.astype(BF16), vt_tiles, c1, n1, e2, r2)

    return _final(x1, y_ffn, p2, ple_w_gate.astype(BF16), ple_w_proj.astype(BF16),
                  ln2_w.reshape(1, D_MODEL), ln2_b.reshape(1, D_MODEL))


def kernel(x, p, w_in, gla_w_gate_up, gla_b_gate, gla_norm_w, pool_w, pool_scale, w_out, ln1_w, ln1_b,
           peer_w_query, peer_sub_keys, peer_u, peer_v, ple_w_gate, ple_w_proj, ln2_w, ln2_b):
    B, S, D = x.shape
    x2 = x.reshape(B * S, D)
    for i in range(w_in.shape[0]):
        x2 = _layer(x2, p[i].reshape(B * S, PLE_DIM), B, S, w_in[i], gla_w_gate_up[i], gla_b_gate[i],
                    gla_norm_w[i], pool_w[i], pool_scale[i], w_out[i], ln1_w[i], ln1_b[i],
                    peer_w_query[i], peer_sub_keys[i], peer_u[i], peer_v[i], ple_w_gate[i],
                    ple_w_proj[i], ln2_w[i], ln2_b[i])
    return x2.reshape(B, S, D)
```

```python
import functools
import math

import jax
import jax.numpy as jnp
from jax import lax
from jax.experimental import pallas as pl
from jax.experimental.pallas import tpu as pltpu

F32 = jnp.float32
BF16 = jnp.bfloat16

D_MODEL = 2048
PLE_DIM = 256
POOL_WIDTH = 1024
POOL_WINDOWS = (2, 4, 8, 16)
POOL_GC = 256
POOL_HALO = 16
GLA_WIDTH = 1024
GLA_HEADS = 4
GLA_DV = 256
GLA_DK = 128
GLA_KEY_WIDTH = 512
GLA_GATE_RANK = 16
GLA_GATE_TEMP = 16.0
GLA_CHUNK = 64
PEER_HEADS = 8
PEER_NKEYS = 128
PEER_HALF = 128
PEER_TOPK = 16
DEPTH = 1
ALPHA = float((2 * DEPTH) ** 0.25)
LN_EPS = 1e-5
RMS_EPS = 1e-6
LANES = 128
NEG_INF = float("-inf")

COL_Q = POOL_WIDTH
COL_K = COL_Q + GLA_KEY_WIDTH
COL_V = COL_K + GLA_KEY_WIDTH
COL_R = COL_V + GLA_WIDTH
PROJ_COLS = COL_R + GLA_WIDTH


def _params(sem, vmem_mib):
    return pltpu.CompilerParams(dimension_semantics=sem, vmem_limit_bytes=vmem_mib * 1024 * 1024)


def _proj_kernel(x_ref, w_ref, wg_ref, o_ref, glr_ref, xb_ref):
    @pl.when(pl.program_id(1) == 0)
    def _():
        xb = x_ref[...].astype(BF16)
        xb_ref[...] = xb
        glr_ref[...] = jnp.dot(xb, wg_ref[...], preferred_element_type=F32)

    o_ref[...] = jnp.dot(xb_ref[...], w_ref[...], preferred_element_type=F32).astype(o_ref.dtype)


def _proj(x2, w_main, w_glr, tm=512, tn=1024):
    T = x2.shape[0]
    return pl.pallas_call(
        _proj_kernel,
        grid=(T // tm, PROJ_COLS // tn),
        in_specs=[
            pl.BlockSpec((tm, D_MODEL), lambda i, n: (i, 0)),
            pl.BlockSpec((D_MODEL, tn), lambda i, n: (0, n)),
            pl.BlockSpec((D_MODEL, LANES), lambda i, n: (0, 0)),
        ],
        out_specs=[
            pl.BlockSpec((tm, tn), lambda i, n: (i, n)),
            pl.BlockSpec((tm, LANES), lambda i, n: (i, 0)),
        ],
        out_shape=[
            jax.ShapeDtypeStruct((T, PROJ_COLS), BF16),
            jax.ShapeDtypeStruct((T, LANES), F32),
        ],
        scratch_shapes=[pltpu.VMEM((tm, D_MODEL), BF16)],
        compiler_params=_params(("parallel", "arbitrary"), 40),
        name="proj",
    )(x2, w_main, w_glr)


def _pool_kernel(u_ref, halo_ref, w_ref, sc_ref, o_ref, ext_ref, *, tiles_per_seq, tm):
    t = pl.program_id(0) % tiles_per_seq
    halo = jnp.where(t == 0, 0.0, halo_ref[...].astype(F32))
    ext_ref[0:POOL_HALO, :] = halo
    ext_ref[POOL_HALO:, :] = u_ref[...].astype(F32)
    pos = t * tm + lax.broadcasted_iota(jnp.int32, (tm, 1), 0)
    for g, w in enumerate(POOL_WINDOWS):
        cols = slice(g * POOL_GC, (g + 1) * POOL_GC)
        u = ext_ref[POOL_HALO:, cols]
        acc = u
        for j in range(1, w):
            acc = acc + ext_ref[POOL_HALO - j:POOL_HALO - j + tm, cols]
        cnt = jnp.minimum(pos + 1, w).astype(F32)
        d = acc / cnt - u
        y = jnp.dot(d.astype(BF16), w_ref[g], preferred_element_type=F32)
        o_ref[:, cols] = (y * sc_ref[:, cols]).astype(o_ref.dtype)


def _pool(proj, pool_w, pool_scale, S, tm=512):
    T = proj.shape[0]
    hb = tm // POOL_HALO
    return pl.pallas_call(
        functools.partial(_pool_kernel, tiles_per_seq=S // tm, tm=tm),
        grid=(T // tm,),
        in_specs=[
            pl.BlockSpec((tm, POOL_WIDTH), lambda i: (i, 0)),
            pl.BlockSpec((POOL_HALO, POOL_WIDTH), lambda i: (jnp.maximum(i * hb - 1, 0), 0)),
            pl.BlockSpec((len(POOL_WINDOWS), POOL_GC, POOL_GC), lambda i: (0, 0, 0)),
            pl.BlockSpec((1, POOL_WIDTH), lambda i: (0, 0)),
        ],
        out_specs=pl.BlockSpec((tm, POOL_WIDTH), lambda i: (i, 0)),
        out_shape=jax.ShapeDtypeStruct((T, POOL_WIDTH), BF16),
        scratch_shapes=[pltpu.VMEM((POOL_HALO + tm, POOL_WIDTH), F32)],
        compiler_params=_params(("parallel",), 32),
        name="pool",
    )(proj, proj, pool_w, pool_scale)


def _gla_kernel(q_ref, k_ref, v_ref, r_ref, glr_ref, wg_ref, bg_ref, nw_ref, o_ref, s_ref, *, n_chunks):
    @pl.when(pl.program_id(1) == 0)
    def _():
        s_ref[...] = jnp.zeros_like(s_ref)

    C = GLA_CHUNK
    row = lax.broadcasted_iota(jnp.int32, (C, C), 0)
    col = lax.broadcasted_iota(jnp.int32, (C, C), 1)
    causal = col <= row
    tril = causal.astype(BF16)
    wg = wg_ref[...]
    bg = bg_ref[...]
    nt = (((1,), (1,)), ((), ()))
    for c in range(n_chunks):
        rows = slice(c * C, (c + 1) * C)
        z = jnp.dot(glr_ref[rows, :].astype(BF16), wg, preferred_element_type=F32) + bg
        g = jax.nn.log_sigmoid(z) / GLA_GATE_TEMP
        g_hi, g_lo = _split_bf16(g)
        b_all = (jnp.dot(tril, g_hi, preferred_element_type=F32)
                 + jnp.dot(tril, g_lo, preferred_element_type=F32))
        for h in range(GLA_HEADS):
            kc = slice(h * GLA_DK, (h + 1) * GLA_DK)
            vc = slice(h * GLA_DV, (h + 1) * GLA_DV)
            b = b_all[:, kc]
            b_last = b[C - 1:C, :]
            b_mid = b[C // 2 - 1:C // 2, :]
            q = q_ref[rows, kc].astype(F32) * (GLA_DK ** -0.5)
            k = k_ref[rows, kc].astype(F32)
            v = v_ref[rows, vc]
            q_state = (q * jnp.exp(b)).astype(BF16)
            q_in = (q * jnp.exp(b - b_mid)).astype(BF16)
            k_in = (k * jnp.exp(b_mid - b)).astype(BF16)
            k_out = k * jnp.exp(b_last - b)
            attn = lax.dot_general(q_in, k_in, nt, preferred_element_type=F32)
            attn = jnp.where(causal, attn, 0.0).astype(BF16)
            s = s_ref[h]
            o = (jnp.dot(attn, v, preferred_element_type=F32)
                 + jnp.dot(q_state, s.astype(BF16), preferred_element_type=F32))
            decay = jnp.transpose(jnp.broadcast_to(jnp.exp(b_last), (C, GLA_DK)))[:, 0:1]
            s_ref[h] = decay * s + jnp.dot(jnp.transpose(k_out).astype(BF16), v, preferred_element_type=F32)
            o = o * lax.rsqrt(jnp.mean(jnp.square(o), axis=-1, keepdims=True) + RMS_EPS)
            o = o * nw_ref[:, vc]
            r = r_ref[rows, vc].astype(F32)
            o_ref[rows, vc] = (o * (r * jax.nn.sigmoid(r))).astype(o_ref.dtype)


def _gla(proj, glr, wg, bg, nw, B, S, L=256):
    T = proj.shape[0]
    nl = S // L
    rb = lambda b, l: b * nl + l
    return pl.pallas_call(
        functools.partial(_gla_kernel, n_chunks=L // GLA_CHUNK),
        grid=(B, nl),
        in_specs=[
            pl.BlockSpec((L, GLA_KEY_WIDTH), lambda b, l: (rb(b, l), COL_Q // GLA_KEY_WIDTH)),
            pl.BlockSpec((L, GLA_KEY_WIDTH), lambda b, l: (rb(b, l), COL_K // GLA_KEY_WIDTH)),
            pl.BlockSpec((L, GLA_WIDTH), lambda b, l: (rb(b, l), COL_V // GLA_WIDTH)),
            pl.BlockSpec((L, GLA_WIDTH), lambda b, l: (rb(b, l), COL_R // GLA_WIDTH)),
            pl.BlockSpec((L, LANES), lambda b, l: (rb(b, l), 0)),
            pl.BlockSpec((LANES, GLA_KEY_WIDTH), lambda b, l: (0, 0)),
            pl.BlockSpec((1, GLA_KEY_WIDTH), lambda b, l: (0, 0)),
            pl.BlockSpec((1, GLA_WIDTH), lambda b, l: (0, 0)),
        ],
        out_specs=pl.BlockSpec((L, GLA_WIDTH), lambda b, l: (rb(b, l), 0)),
        out_shape=jax.ShapeDtypeStruct((T, GLA_WIDTH), BF16),
        scratch_shapes=[pltpu.VMEM((GLA_HEADS, GLA_DK, GLA_DV), F32)],
        compiler_params=_params(("parallel", "arbitrary"), 32),
        name="gla",
    )(proj, proj, proj, proj, glr, wg, bg, nw)


def _layer_norm(h, w, b):
    mu = jnp.mean(h, axis=-1, keepdims=True)
    hc = h - mu
    var = jnp.mean(jnp.square(hc), axis=-1, keepdims=True)
    return hc * lax.rsqrt(var + LN_EPS) * w + b


def _outproj_kernel(yp_ref, yg_ref, x_ref, w_ref, lw_ref, lb_ref, x1_ref, x1t_ref):
    mix = (jnp.dot(yp_ref[...], w_ref[0:POOL_WIDTH, :], preferred_element_type=F32)
           + jnp.dot(yg_ref[...], w_ref[POOL_WIDTH:, :], preferred_element_type=F32))
    x1 = _layer_norm(ALPHA * x_ref[...] + mix, lw_ref[...], lb_ref[...])
    x1_ref[...] = x1
    x1t_ref[...] = jnp.transpose(x1).astype(BF16)


def _outproj(y_pool, y_gla, x2, w_out, ln_w, ln_b, tm=256):
    T = x2.shape[0]
    return pl.pallas_call(
        _outproj_kernel,
        grid=(T // tm,),
        in_specs=[
            pl.BlockSpec((tm, POOL_WIDTH), lambda i: (i, 0)),
            pl.BlockSpec((tm, GLA_WIDTH), lambda i: (i, 0)),
            pl.BlockSpec((tm, D_MODEL), lambda i: (i, 0)),
            pl.BlockSpec((D_MODEL, D_MODEL), lambda i: (0, 0)),
            pl.BlockSpec((1, D_MODEL), lambda i: (0, 0)),
            pl.BlockSpec((1, D_MODEL), lambda i: (0, 0)),
        ],
        out_specs=[
            pl.BlockSpec((tm, D_MODEL), lambda i: (i, 0)),
            pl.BlockSpec((D_MODEL, tm), lambda i: (0, i)),
        ],
        out_shape=[
            jax.ShapeDtypeStruct((T, D_MODEL), F32),
            jax.ShapeDtypeStruct((D_MODEL, T), BF16),
        ],
        compiler_params=_params(("parallel",), 48),
        name="outproj",
    )(y_pool, y_gla, x2, w_out, ln_w, ln_b)


N_SORT = PEER_TOPK + 1
UNRANKED = 127.0


def _sorted_top2(arr_a, arr_b):
    rid = lax.broadcasted_iota(jnp.int32, (24, LANES), 0)

    def body(kk, carry):
        a, top_a, b, top_b, rank_b = carry
        ma = jnp.max(a, axis=0, keepdims=True)
        mb = jnp.max(b, axis=0, keepdims=True)
        hit_b = b == mb
        return (jnp.where(a == ma, NEG_INF, a), jnp.where(rid == kk, ma, top_a),
                jnp.where(hit_b, NEG_INF, b), jnp.where(rid == kk, mb, top_b),
                jnp.where(hit_b, kk.astype(F32), rank_b))

    top0 = jnp.full((24, LANES), NEG_INF, F32)
    rank0 = jnp.full(arr_b.shape, UNRANKED, F32)
    _, top_a, _, top_b, rank_b = lax.fori_loop(0, N_SORT, body, (arr_a, top0, arr_b, top0, rank0))
    return top_a, top_b, rank_b


def _pair_stats(ta, tb):
    r8 = lax.broadcasted_iota(jnp.int32, (8, LANES), 0)
    b_lo, b_hi = tb[0:8], tb[8:16]
    a = [ta[i:i + 1] for i in range(8)]
    a16, b16, b0 = ta[16:17], tb[16:17], tb[0:1]
    p2 = jnp.where(r8 < 5, a[2] + b_lo, jnp.where(r8 == 5, a16 + b0, jnp.where(r8 == 6, a[0] + b16, NEG_INF)))
    pieces = [
        a[0] + b_lo, a[0] + b_hi, a[1] + b_lo, p2,
        jnp.where(r8 < 4, a[3] + b_lo, NEG_INF),
        jnp.where(r8 < 3, a[4] + b_lo, NEG_INF),
        jnp.where(r8 < 2, a[5] + b_lo, NEG_INF),
        jnp.where(r8 < 2, a[6] + b_lo, NEG_INF),
        jnp.where(r8 < 2, a[7] + b_lo, NEG_INF),
        ta[8:16] + b0,
    ]
    cand = jnp.concatenate(pieces, axis=0)
    top_sum = a[0] + b0

    def body(kk, carry):
        c, z, v16, v17 = carry
        m = jnp.max(c, axis=0, keepdims=True)
        z = z + jnp.where(kk < PEER_TOPK, jnp.exp(m - top_sum), 0.0)
        v16 = jnp.where(kk == PEER_TOPK - 1, m, v16)
        v17 = jnp.where(kk == PEER_TOPK, m, v17)
        c = jnp.where(c == m, NEG_INF, c)
        return c, z, v16, v17

    zero = jnp.zeros((1, LANES), F32)
    _, z, v16, v17 = lax.fori_loop(0, N_SORT, body, (cand, zero, zero, zero))
    return 0.5 * (v16 + v17), a[0], b0, 1.0 / z


def _split_bf16(x):
    hi = x.astype(BF16)
    return hi, (x - hi.astype(F32)).astype(BF16)


def _query_kernel(x1t_ref, wq_ref, khi_ref, klo_ref, c1_ref, n1_ref, e2_ref, r2_ref, q_ref, s1_scr, s2_scr, *, tm):
    q_ref[...] = jnp.dot(wq_ref[...], x1t_ref[...], preferred_element_type=F32)
    for h in range(PEER_HEADS):
        for p in range(2):
            hp = 2 * h + p
            q_hi, q_lo = _split_bf16(q_ref[hp * PEER_HALF:(hp + 1) * PEER_HALF, :])
            k_hi, k_lo = khi_ref[hp], klo_ref[hp]
            sc = (jnp.dot(k_hi, q_hi, preferred_element_type=F32)
                  + jnp.dot(k_hi, q_lo, preferred_element_type=F32)
                  + jnp.dot(k_lo, q_hi, preferred_element_type=F32))
            if p == 0:
                s1_scr[...] = sc
            else:
                s2_scr[...] = sc
        for tc in range(tm // LANES):
            lanes = slice(tc * LANES, (tc + 1) * LANES)
            s1 = s1_scr[:, lanes]
            s2 = s2_scr[:, lanes]
            ta, tb, rank2 = _sorted_top2(s1, s2)
            tau, m1, m2, rz = _pair_stats(ta, tb)
            n1 = jnp.zeros_like(s1)
            for jj in range(PEER_TOPK):
                n1 = jnp.where(s1 >= tau - tb[jj:jj + 1], float(jj + 1), n1)
            c1_ref[h, tc] = jnp.exp(s1 - m1)
            n1_ref[h, tc] = n1
            e2_ref[h, :, lanes] = (jnp.exp(s2 - m2) * rz).astype(BF16)
            r2_ref[h, :, lanes] = rank2.astype(BF16)


def _query(x1t, wq_t, keys_hi, keys_lo, tm=256):
    T = x1t.shape[1]
    nc = tm // LANES
    row_spec = pl.BlockSpec((PEER_HEADS, nc, PEER_NKEYS, LANES), lambda i: (0, i, 0, 0))
    col_spec = pl.BlockSpec((PEER_HEADS, PEER_NKEYS, tm), lambda i: (0, 0, i))
    key_spec = pl.BlockSpec((2 * PEER_HEADS, PEER_NKEYS, PEER_HALF), lambda i: (0, 0, 0))
    row_shape = jax.ShapeDtypeStruct((PEER_HEADS, T // LANES, PEER_NKEYS, LANES), F32)
    col_shape = jax.ShapeDtypeStruct((PEER_HEADS, PEER_NKEYS, T), BF16)
    return pl.pallas_call(
        functools.partial(_query_kernel, tm=tm),
        grid=(T // tm,),
        in_specs=[
            pl.BlockSpec((D_MODEL, tm), lambda i: (0, i)),
            pl.BlockSpec((D_MODEL, D_MODEL), lambda i: (0, 0)),
            key_spec, key_spec,
        ],
        out_specs=[row_spec, row_spec, col_spec, col_spec],
        out_shape=[row_shape, row_shape, col_shape, col_shape],
        scratch_shapes=[pltpu.VMEM((D_MODEL, tm), F32), pltpu.VMEM((PEER_NKEYS, tm), F32),
                        pltpu.VMEM((PEER_NKEYS, tm), F32)],
        compiler_params=_params(("parallel",), 48),
        name="query",
    )(x1t, wq_t, keys_hi, keys_lo)


PEER_SB = 64
PEER_RG = 2
SUBLANES = 8
PEER_TE = 1024


def _gelu(x):
    return 0.5 * x * (1.0 + lax.erf(x * (1.0 / math.sqrt(2.0))))


def _bcast_row_bf16(tile, ri, rows):
    packed = jnp.broadcast_to(tile[ri:ri + 1, :], (2 * SUBLANES, LANES)).astype(BF16)
    return jnp.concatenate([packed] * (rows // (2 * SUBLANES)), axis=0)


def _peer_kernel(x1t_ref, u_ref, vt_ref, c1_ref, n1_ref, e2_ref, r2_ref, y_ref,
                 acc_ref, st_scr, ht_scr, *, tm, te):
    j = pl.program_id(1)
    n1 = te // PEER_NKEYS
    grows = PEER_RG * PEER_NKEYS

    @pl.when(j == 0)
    def _():
        acc_ref[...] = jnp.zeros_like(acc_ref)

    for gi in range(n1 // PEER_RG):
        crows = slice(gi * grows, (gi + 1) * grows)
        st_scr[crows, :] = jnp.dot(u_ref[crows, :], x1t_ref[...], preferred_element_type=F32)

    nsb = PEER_NKEYS // PEER_SB
    tile_rows = pl.ds(pl.multiple_of(j * n1, SUBLANES), n1)
    for gi in range(n1 // PEER_RG):
        crows = slice(gi * grows, (gi + 1) * grows)
        for tc in range(tm // LANES):
            lanes = slice(tc * LANES, (tc + 1) * LANES)
            g = [[jnp.zeros((PEER_SB, LANES), BF16) for _ in range(nsb)] for _ in range(PEER_RG)]
            for h in range(PEER_HEADS):
                c1_t = c1_ref[h, tc, tile_rows, :]
                n1_t = n1_ref[h, tc, tile_rows, :]
                c1b = [_bcast_row_bf16(c1_t, gi * PEER_RG + r, PEER_SB) for r in range(PEER_RG)]
                n1b = [_bcast_row_bf16(n1_t, gi * PEER_RG + r, PEER_SB) for r in range(PEER_RG)]
                for sb in range(nsb):
                    rows = slice(sb * PEER_SB, (sb + 1) * PEER_SB)
                    r2c = r2_ref[h, rows, lanes]
                    e2c = e2_ref[h, rows, lanes]
                    for r in range(PEER_RG):
                        g[r][sb] = g[r][sb] + c1b[r] * jnp.where(r2c < n1b[r], e2c, jnp.zeros_like(e2c))
            for r in range(PEER_RG):
                for sb in range(nsb):
                    base = gi * grows + r * PEER_NKEYS + sb * PEER_SB
                    srows = slice(base, base + PEER_SB)
                    ht_scr[srows, lanes] = g[r][sb] * _gelu(st_scr[srows, lanes]).astype(BF16)
        acc_ref[...] += jnp.dot(vt_ref[:, crows], ht_scr[crows, :], preferred_element_type=F32)

    @pl.when(j == pl.num_programs(1) - 1)
    def _():
        y_ref[...] = jnp.transpose(acc_ref[...])


def _peer(x1t, u_tab, vt_tiles, c1, n1, e2, r2, tm=512):
    T = x1t.shape[1]
    nj, _, te = vt_tiles.shape
    once = pl.Buffered(1)
    row_spec = pl.BlockSpec((PEER_HEADS, tm // LANES, PEER_NKEYS, LANES), lambda i, j: (0, i, 0, 0),
                            pipeline_mode=once)
    col_spec = pl.BlockSpec((PEER_HEADS, PEER_NKEYS, tm), lambda i, j: (0, 0, i), pipeline_mode=once)
    return pl.pallas_call(
        functools.partial(_peer_kernel, tm=tm, te=te),
        grid=(T // tm, nj),
        in_specs=[
            pl.BlockSpec((D_MODEL, tm), lambda i, j: (0, i), pipeline_mode=once),
            pl.BlockSpec((te, D_MODEL), lambda i, j: (j, 0)),
            pl.BlockSpec((None, D_MODEL, te), lambda i, j: (j, 0, 0)),
            row_spec, row_spec, col_spec, col_spec,
        ],
        out_specs=pl.BlockSpec((tm, D_MODEL), lambda i, j: (i, 0)),
        out_shape=jax.ShapeDtypeStruct((T, D_MODEL), F32),
        scratch_shapes=[
            pltpu.VMEM((D_MODEL, tm), F32),
            pltpu.VMEM((te, tm), F32),
            pltpu.VMEM((te, tm), BF16),
        ],
        compiler_params=_params(("parallel", "arbitrary"), 56),
        name="peer",
    )(x1t, u_tab, vt_tiles, c1, n1, e2, r2)


def _final_kernel(x1_ref, y_ref, p_ref, wg_ref, wp_ref, lw_ref, lb_ref, o_ref):
    x1 = x1_ref[...]
    gate = jax.nn.sigmoid(jnp.dot(x1.astype(BF16), wg_ref[...], preferred_element_type=F32))
    emb = jnp.dot(p_ref[...].astype(BF16), wp_ref[...], preferred_element_type=F32)
    o_ref[...] = _layer_norm(ALPHA * x1 + y_ref[...] + gate * emb, lw_ref[...], lb_ref[...])


def _final(x1, y_ffn, p2, w_gate, w_proj, ln_w, ln_b, tm=256):
    T = x1.shape[0]
    return pl.pallas_call(
        _final_kernel,
        grid=(T // tm,),
        in_specs=[
            pl.BlockSpec((tm, D_MODEL), lambda i: (i, 0)),
            pl.BlockSpec((tm, D_MODEL), lambda i: (i, 0)),
            pl.BlockSpec((tm, PLE_DIM), lambda i: (i, 0)),
            pl.BlockSpec((D_MODEL, D_MODEL), lambda i: (0, 0)),
            pl.BlockSpec((PLE_DIM, D_MODEL), lambda i: (0, 0)),
            pl.BlockSpec((1, D_MODEL), lambda i: (0, 0)),
            pl.BlockSpec((1, D_MODEL), lambda i: (0, 0)),
        ],
        out_specs=pl.BlockSpec((tm, D_MODEL), lambda i: (i, 0)),
        out_shape=jax.ShapeDtypeStruct((T, D_MODEL), F32),
        compiler_params=_params(("parallel",), 48),
        name="final",
    )(x1, y_ffn, p2, w_gate, w_proj, ln_w, ln_b)


def _layer(x2, p2, B, S, w_in, gla_w_gate_up, gla_b_gate, gla_norm_w, pool_w, pool_scale, w_out,
           ln1_w, ln1_b, peer_w_query, peer_sub_keys, peer_u, peer_v, ple_w_gate, ple_w_proj, ln2_w, ln2_b):
    glr0 = COL_R
    w_main = jnp.concatenate([w_in[:, :glr0], w_in[:, glr0 + GLA_GATE_RANK:]], axis=1).astype(BF16)
    w_glr = jnp.pad(w_in[:, glr0:glr0 + GLA_GATE_RANK], ((0, 0), (0, LANES - GLA_GATE_RANK))).astype(BF16)
    proj, glr = _proj(x2, w_main, w_glr)

    y_pool = _pool(proj, pool_w.astype(BF16), pool_scale.reshape(1, POOL_WIDTH), S)

    wg = jnp.pad(gla_w_gate_up, ((0, LANES - GLA_GATE_RANK), (0, 0))).astype(BF16)
    y_gla = _gla(proj, glr, wg, gla_b_gate.reshape(1, GLA_KEY_WIDTH),
                 gla_norm_w.reshape(1, GLA_WIDTH), B, S)

    x1, x1t = _outproj(y_pool, y_gla, x2, w_out.astype(BF16),
                       ln1_w.reshape(1, D_MODEL), ln1_b.reshape(1, D_MODEL))

    keys = peer_sub_keys.reshape(2 * PEER_HEADS, PEER_NKEYS, PEER_HALF)
    keys_hi, keys_lo = _split_bf16(keys)
    c1, n1, e2, r2 = _query(x1t, peer_w_query.T.astype(BF16), keys_hi, keys_lo)
    n_exp = peer_v.shape[0]
    vt_tiles = peer_v.astype(BF16).reshape(n_exp // PEER_TE, PEER_TE, D_MODEL).transpose(0, 2, 1)
    y_ffn = _peer(x1t, peer_u.astype(BF16), vt_tiles, c1, n1, e2, r2)

    return _final(x1, y_ffn, p2, ple_w_gate.astype(BF16), ple_w_proj.astype(BF16),
                  ln2_w.reshape(1, D_MODEL), ln2_b.reshape(1, D_MODEL))


def kernel(x, p, w_in, gla_w_gate_up, gla_b_gate, gla_norm_w, pool_w, pool_scale, w_out, ln1_w, ln1_b,
           peer_w_query, peer_sub_keys, peer_u, peer_v, ple_w_gate, ple_w_proj, ln2_w, ln2_b):
    B, S, D = x.shape
    x2 = x.reshape(B * S, D)
    for i in range(w_in.shape[0]):
        x2 = _layer(x2, p[i].reshape(B * S, PLE_DIM), B, S, w_in[i], gla_w_gate_up[i], gla_b_gate[i],
                    gla_norm_w[i], pool_w[i], pool_scale[i], w_out[i], ln1_w[i], ln1_b[i],
                    peer_w_query[i], peer_sub_keys[i], peer_u[i], peer_v[i], ple_w_gate[i],
                    ple_w_proj[i], ln2_w[i], ln2_b[i])
    return x2.reshape(B, S, D)
```

```python
import functools
import math

import jax
import jax.numpy as jnp
from jax import lax
from jax.experimental import pallas as pl
from jax.experimental.pallas import tpu as pltpu

F32 = jnp.float32
BF16 = jnp.bfloat16

D_MODEL = 2048
PLE_DIM = 256
POOL_WIDTH = 1024
POOL_WINDOWS = (2, 4, 8, 16)
POOL_GC = 256
POOL_HALO = 16
GLA_WIDTH = 1024
GLA_HEADS = 4
GLA_DV = 256
GLA_DK = 128
GLA_KEY_WIDTH = 512
GLA_GATE_RANK = 16
GLA_GATE_TEMP = 16.0
GLA_CHUNK = 64
PEER_HEADS = 8
PEER_NKEYS = 128
PEER_HALF = 128
PEER_TOPK = 16
DEPTH = 1
ALPHA = float((2 * DEPTH) ** 0.25)
LN_EPS = 1e-5
RMS_EPS = 1e-6
LANES = 128
NEG_INF = float("-inf")

COL_Q = POOL_WIDTH
COL_K = COL_Q + GLA_KEY_WIDTH
COL_V = COL_K + GLA_KEY_WIDTH
COL_R = COL_V + GLA_WIDTH
PROJ_COLS = COL_R + GLA_WIDTH


def _params(sem, vmem_mib):
    return pltpu.CompilerParams(dimension_semantics=sem, vmem_limit_bytes=vmem_mib * 1024 * 1024)


def _proj_kernel(x_ref, w_ref, wg_ref, o_ref, glr_ref, xb_ref):
    @pl.when(pl.program_id(1) == 0)
    def _():
        xb = x_ref[...].astype(BF16)
        xb_ref[...] = xb
        glr_ref[...] = jnp.dot(xb, wg_ref[...], preferred_element_type=F32)

    o_ref[...] = jnp.dot(xb_ref[...], w_ref[...], preferred_element_type=F32).astype(o_ref.dtype)


def _proj(x2, w_main, w_glr, tm=512, tn=1024):
    T = x2.shape[0]
    return pl.pallas_call(
        _proj_kernel,
        grid=(T // tm, PROJ_COLS // tn),
        in_specs=[
            pl.BlockSpec((tm, D_MODEL), lambda i, n: (i, 0)),
            pl.BlockSpec((D_MODEL, tn), lambda i, n: (0, n)),
            pl.BlockSpec((D_MODEL, LANES), lambda i, n: (0, 0)),
        ],
        out_specs=[
            pl.BlockSpec((tm, tn), lambda i, n: (i, n)),
            pl.BlockSpec((tm, LANES), lambda i, n: (i, 0)),
        ],
        out_shape=[
            jax.ShapeDtypeStruct((T, PROJ_COLS), BF16),
            jax.ShapeDtypeStruct((T, LANES), F32),
        ],
        scratch_shapes=[pltpu.VMEM((tm, D_MODEL), BF16)],
        compiler_params=_params(("parallel", "arbitrary"), 40),
        name="proj",
    )(x2, w_main, w_glr)


def _pool_kernel(u_ref, halo_ref, w_ref, sc_ref, o_ref, ext_ref, *, tiles_per_seq, tm):
    t = pl.program_id(0) % tiles_per_seq
    halo = jnp.where(t == 0, 0.0, halo_ref[...].astype(F32))
    ext_ref[0:POOL_HALO, :] = halo
    ext_ref[POOL_HALO:, :] = u_ref[...].astype(F32)
    pos = t * tm + lax.broadcasted_iota(jnp.int32, (tm, 1), 0)
    for g, w in enumerate(POOL_WINDOWS):
        cols = slice(g * POOL_GC, (g + 1) * POOL_GC)
        u = ext_ref[POOL_HALO:, cols]
        acc = u
        for j in range(1, w):
            acc = acc + ext_ref[POOL_HALO - j:POOL_HALO - j + tm, cols]
        cnt = jnp.minimum(pos + 1, w).astype(F32)
        d = acc / cnt - u
        y = jnp.dot(d.astype(BF16), w_ref[g], preferred_element_type=F32)
        o_ref[:, cols] = (y * sc_ref[:, cols]).astype(o_ref.dtype)


def _pool(proj, pool_w, pool_scale, S, tm=512):
    T = proj.shape[0]
    hb = tm // POOL_HALO
    return pl.pallas_call(
        functools.partial(_pool_kernel, tiles_per_seq=S // tm, tm=tm),
        grid=(T // tm,),
        in_specs=[
            pl.BlockSpec((tm, POOL_WIDTH), lambda i: (i, 0)),
            pl.BlockSpec((POOL_HALO, POOL_WIDTH), lambda i: (jnp.maximum(i * hb - 1, 0), 0)),
            pl.BlockSpec((len(POOL_WINDOWS), POOL_GC, POOL_GC), lambda i: (0, 0, 0)),
            pl.BlockSpec((1, POOL_WIDTH), lambda i: (0, 0)),
        ],
        out_specs=pl.BlockSpec((tm, POOL_WIDTH), lambda i: (i, 0)),
        out_shape=jax.ShapeDtypeStruct((T, POOL_WIDTH), BF16),
        scratch_shapes=[pltpu.VMEM((POOL_HALO + tm, POOL_WIDTH), F32)],
        compiler_params=_params(("parallel",), 32),
        name="pool",
    )(proj, proj, pool_w, pool_scale)


def _gla_kernel(q_ref, k_ref, v_ref, r_ref, glr_ref, wg_ref, bg_ref, nw_ref, o_ref, s_ref, *, n_chunks):
    @pl.when(pl.program_id(1) == 0)
    def _():
        s_ref[...] = jnp.zeros_like(s_ref)

    C = GLA_CHUNK
    row = lax.broadcasted_iota(jnp.int32, (C, C), 0)
    col = lax.broadcasted_iota(jnp.int32, (C, C), 1)
    causal = col <= row
    tril = causal.astype(BF16)
    wg = wg_ref[...]
    bg = bg_ref[...]
    nt = (((1,), (1,)), ((), ()))
    for c in range(n_chunks):
        rows = slice(c * C, (c + 1) * C)
        z = jnp.dot(glr_ref[rows, :].astype(BF16), wg, preferred_element_type=F32) + bg
        g = jax.nn.log_sigmoid(z) / GLA_GATE_TEMP
        g_hi, g_lo = _split_bf16(g)
        b_all = (jnp.dot(tril, g_hi, preferred_element_type=F32)
                 + jnp.dot(tril, g_lo, preferred_element_type=F32))
        for h in range(GLA_HEADS):
            kc = slice(h * GLA_DK, (h + 1) * GLA_DK)
            vc = slice(h * GLA_DV, (h + 1) * GLA_DV)
            b = b_all[:, kc]
            b_last = b[C - 1:C, :]
            b_mid = b[C // 2 - 1:C // 2, :]
            q = q_ref[rows, kc].astype(F32) * (GLA_DK ** -0.5)
            k = k_ref[rows, kc].astype(F32)
            v = v_ref[rows, vc]
            q_state = (q * jnp.exp(b)).astype(BF16)
            q_in = (q * jnp.exp(b - b_mid)).astype(BF16)
            k_in = (k * jnp.exp(b_mid - b)).astype(BF16)
            k_out = k * jnp.exp(b_last - b)
            attn = lax.dot_general(q_in, k_in, nt, preferred_element_type=F32)
            attn = jnp.where(causal, attn, 0.0).astype(BF16)
            s = s_ref[h]
            o = (jnp.dot(attn, v, preferred_element_type=F32)
                 + jnp.dot(q_state, s.astype(BF16), preferred_element_type=F32))
            decay = jnp.transpose(jnp.broadcast_to(jnp.exp(b_last), (C, GLA_DK)))[:, 0:1]
            s_ref[h] = decay * s + jnp.dot(jnp.transpose(k_out).astype(BF16), v, preferred_element_type=F32)
            o = o * lax.rsqrt(jnp.mean(jnp.square(o), axis=-1, keepdims=True) + RMS_EPS)
            o = o * nw_ref[:, vc]
            r = r_ref[rows, vc].astype(F32)
            o_ref[rows, vc] = (o * (r * jax.nn.sigmoid(r))).astype(o_ref.dtype)


def _gla(proj, glr, wg, bg, nw, B, S, L=256):
    T = proj.shape[0]
    nl = S // L
    rb = lambda b, l: b * nl + l
    return pl.pallas_call(
        functools.partial(_gla_kernel, n_chunks=L // GLA_CHUNK),
        grid=(B, nl),
        in_specs=[
            pl.BlockSpec((L, GLA_KEY_WIDTH), lambda b, l: (rb(b, l), COL_Q // GLA_KEY_WIDTH)),
            pl.BlockSpec((L, GLA_KEY_WIDTH), lambda b, l: (rb(b, l), COL_K // GLA_KEY_WIDTH)),
            pl.BlockSpec((L, GLA_WIDTH), lambda b, l: (rb(b, l), COL_V // GLA_WIDTH)),
            pl.BlockSpec((L, GLA_WIDTH), lambda b, l: (rb(b, l), COL_R // GLA_WIDTH)),
            pl.BlockSpec((L, LANES), lambda b, l: (rb(b, l), 0)),
            pl.BlockSpec((LANES, GLA_KEY_WIDTH), lambda b, l: (0, 0)),
            pl.BlockSpec((1, GLA_KEY_WIDTH), lambda b, l: (0, 0)),
            pl.BlockSpec((1, GLA_WIDTH), lambda b, l: (0, 0)),
        ],
        out_specs=pl.BlockSpec((L, GLA_WIDTH), lambda b, l: (rb(b, l), 0)),
        out_shape=jax.ShapeDtypeStruct((T, GLA_WIDTH), BF16),
        scratch_shapes=[pltpu.VMEM((GLA_HEADS, GLA_DK, GLA_DV), F32)],
        compiler_params=_params(("parallel", "arbitrary"), 32),
        name="gla",
    )(proj, proj, proj, proj, glr, wg, bg, nw)


def _layer_norm(h, w, b):
    mu = jnp.mean(h, axis=-1, keepdims=True)
    hc = h - mu
    var = jnp.mean(jnp.square(hc), axis=-1, keepdims=True)
    return hc * lax.rsqrt(var + LN_EPS) * w + b


def _outproj_kernel(yp_ref, yg_ref, x_ref, w_ref, lw_ref, lb_ref, x1_ref, x1t_ref):
    mix = (jnp.dot(yp_ref[...], w_ref[0:POOL_WIDTH, :], preferred_element_type=F32)
           + jnp.dot(yg_ref[...], w_ref[POOL_WIDTH:, :], preferred_element_type=F32))
    x1 = _layer_norm(ALPHA * x_ref[...] + mix, lw_ref[...], lb_ref[...])
    x1_ref[...] = x1
    x1t_ref[...] = jnp.transpose(x1).astype(BF16)


def _outproj(y_pool, y_gla, x2, w_out, ln_w, ln_b, tm=256):
    T = x2.shape[0]
    return pl.pallas_call(
        _outproj_kernel,
        grid=(T // tm,),
        in_specs=[
            pl.BlockSpec((tm, POOL_WIDTH), lambda i: (i, 0)),
            pl.BlockSpec((tm, GLA_WIDTH), lambda i: (i, 0)),
            pl.BlockSpec((tm, D_MODEL), lambda i: (i, 0)),
            pl.BlockSpec((D_MODEL, D_MODEL), lambda i: (0, 0)),
            pl.BlockSpec((1, D_MODEL), lambda i: (0, 0)),
            pl.BlockSpec((1, D_MODEL), lambda i: (0, 0)),
        ],
        out_specs=[
            pl.BlockSpec((tm, D_MODEL), lambda i: (i, 0)),
            pl.BlockSpec((D_MODEL, tm), lambda i: (0, i)),
        ],
        out_shape=[
            jax.ShapeDtypeStruct((T, D_MODEL), F32),
            jax.ShapeDtypeStruct((D_MODEL, T), BF16),
        ],
        compiler_params=_params(("parallel",), 48),
        name="outproj",
    )(y_pool, y_gla, x2, w_out, ln_w, ln_b)


N_SORT = PEER_TOPK + 1


def _sort_network(n):
    pairs = []

    def merge(lo, m, r):
        step = 2 * r
        if step < m:
            merge(lo, m, step)
            merge(lo + r, m, step)
            pairs.extend((i, i + r) for i in range(lo + r, lo + m - r, step))
        else:
            pairs.append((lo, lo + r))

    def sort(lo, m):
        if m > 1:
            sort(lo, m // 2)
            sort(lo + m // 2, m // 2)
            merge(lo, m, 1)

    sort(0, n)
    return tuple(pairs)


def _sorted_top(arr):
    nv = arr.shape[0] // SUBLANES
    v = [arr[k * SUBLANES:(k + 1) * SUBLANES] for k in range(nv)]
    for i, j in _sort_network(nv):
        v[i], v[j] = jnp.maximum(v[i], v[j]), jnp.minimum(v[i], v[j])
    tops = []
    for kk in range(N_SORT):
        m = jnp.max(v[0], axis=0, keepdims=True)
        tops.append(m)
        hit = v[0] == m
        for k in range(N_SORT - 1 - kk):
            v[k] = jnp.where(hit, v[k + 1] if k + 1 < nv else NEG_INF, v[k])
    return tops


def _rows_to_tile(rows):
    rid = lax.broadcasted_iota(jnp.int32, (SUBLANES, LANES), 0)
    tile = jnp.full((SUBLANES, LANES), NEG_INF, F32)
    for k, r in enumerate(rows):
        tile = jnp.where(rid == k, r, tile)
    return tile


def _pair_stats(a, b):
    r8 = lax.broadcasted_iota(jnp.int32, (SUBLANES, LANES), 0)
    b_lo, b_hi, a_hi = _rows_to_tile(b[0:8]), _rows_to_tile(b[8:16]), _rows_to_tile(a[8:16])
    p2 = jnp.where(r8 < 5, a[2] + b_lo, jnp.where(r8 == 5, a[16] + b[0], jnp.where(r8 == 6, a[0] + b[16], NEG_INF)))
    pieces = [
        a[0] + b_lo, a[0] + b_hi, a[1] + b_lo, p2,
        jnp.where(r8 < 4, a[3] + b_lo, NEG_INF),
        jnp.where(r8 < 3, a[4] + b_lo, NEG_INF),
        jnp.where(r8 < 2, a[5] + b_lo, NEG_INF),
        jnp.where(r8 < 2, a[6] + b_lo, NEG_INF),
        jnp.where(r8 < 2, a[7] + b_lo, NEG_INF),
        a_hi + b[0],
    ]
    cand = jnp.concatenate(pieces, axis=0)
    top_sum = a[0] + b[0]

    def body(kk, carry):
        c, z, v16, v17 = carry
        m = jnp.max(c, axis=0, keepdims=True)
        z = z + jnp.where(kk < PEER_TOPK, jnp.exp(m - top_sum), 0.0)
        v16 = jnp.where(kk == PEER_TOPK - 1, m, v16)
        v17 = jnp.where(kk == PEER_TOPK, m, v17)
        c = jnp.where(c == m, NEG_INF, c)
        return c, z, v16, v17

    zero = jnp.zeros((1, LANES), F32)
    _, z, v16, v17 = lax.fori_loop(0, N_SORT, body, (cand, zero, zero, zero))
    return 0.5 * (v16 + v17), 1.0 / z


def _split_bf16(x):
    hi = x.astype(BF16)
    return hi, (x - hi.astype(F32)).astype(BF16)


def _query_kernel(x1t_ref, wq_ref, khi_ref, klo_ref, c1_ref, n1_ref, e2_ref, r2_ref, q_ref, s1_scr, s2_scr, *, tm):
    q_ref[...] = jnp.dot(wq_ref[...], x1t_ref[...], preferred_element_type=F32)
    for h in range(PEER_HEADS):
        for p in range(2):
            hp = 2 * h + p
            q_hi, q_lo = _split_bf16(q_ref[hp * PEER_HALF:(hp + 1) * PEER_HALF, :])
            k_hi, k_lo = khi_ref[hp], klo_ref[hp]
            sc = (jnp.dot(k_hi, q_hi, preferred_element_type=F32)
                  + jnp.dot(k_hi, q_lo, preferred_element_type=F32)
                  + jnp.dot(k_lo, q_hi, preferred_element_type=F32))
            if p == 0:
                s1_scr[...] = sc
            else:
                s2_scr[...] = sc
        for tc in range(tm // LANES):
            lanes = slice(tc * LANES, (tc + 1) * LANES)
            s1 = s1_scr[:, lanes]
            s2 = s2_scr[:, lanes]
            ta, tb = _sorted_top(s1), _sorted_top(s2)
            tau, rz = _pair_stats(ta, tb)
            n1 = jnp.zeros_like(s1)
            r2 = jnp.zeros_like(s2)
            for jj in range(PEER_TOPK):
                n1 = jnp.where(s1 >= tau - tb[jj], float(jj + 1), n1)
                r2 = jnp.where(s2 < tb[jj], float(jj + 1), r2)
            c1_ref[h, tc] = jnp.exp(s1 - ta[0])
            n1_ref[h, tc] = n1
            e2_ref[h, :, lanes] = (jnp.exp(s2 - tb[0]) * rz).astype(BF16)
            r2_ref[h, :, lanes] = r2.astype(BF16)


def _query(x1t, wq_t, keys_hi, keys_lo, tm=256):
    T = x1t.shape[1]
    nc = tm // LANES
    row_spec = pl.BlockSpec((PEER_HEADS, nc, PEER_NKEYS, LANES), lambda i: (0, i, 0, 0))
    col_spec = pl.BlockSpec((PEER_HEADS, PEER_NKEYS, tm), lambda i: (0, 0, i))
    key_spec = pl.BlockSpec((2 * PEER_HEADS, PEER_NKEYS, PEER_HALF), lambda i: (0, 0, 0))
    row_shape = jax.ShapeDtypeStruct((PEER_HEADS, T // LANES, PEER_NKEYS, LANES), F32)
    col_shape = jax.ShapeDtypeStruct((PEER_HEADS, PEER_NKEYS, T), BF16)
    return pl.pallas_call(
        functools.partial(_query_kernel, tm=tm),
        grid=(T // tm,),
        in_specs=[
            pl.BlockSpec((D_MODEL, tm), lambda i: (0, i)),
            pl.BlockSpec((D_MODEL, D_MODEL), lambda i: (0, 0)),
            key_spec, key_spec,
        ],
        out_specs=[row_spec, row_spec, col_spec, col_spec],
        out_shape=[row_shape, row_shape, col_shape, col_shape],
        scratch_shapes=[pltpu.VMEM((D_MODEL, tm), F32), pltpu.VMEM((PEER_NKEYS, tm), F32),
                        pltpu.VMEM((PEER_NKEYS, tm), F32)],
        compiler_params=_params(("parallel",), 48),
        name="query",
    )(x1t, wq_t, keys_hi, keys_lo)


PEER_SB = 64
PEER_RG = 2
SUBLANES = 8
PEER_TE = 1024


def _gelu(x):
    return 0.5 * x * (1.0 + lax.erf(x * (1.0 / math.sqrt(2.0))))


def _bcast_row_bf16(tile, ri, rows):
    packed = jnp.broadcast_to(tile[ri:ri + 1, :], (2 * SUBLANES, LANES)).astype(BF16)
    return jnp.concatenate([packed] * (rows // (2 * SUBLANES)), axis=0)


def _peer_kernel(x1t_ref, u_ref, vt_ref, c1_ref, n1_ref, e2_ref, r2_ref, y_ref,
                 acc_ref, st_scr, ht_scr, *, tm, te):
    j = pl.program_id(1)
    n1 = te // PEER_NKEYS
    grows = PEER_RG * PEER_NKEYS

    @pl.when(j == 0)
    def _():
        acc_ref[...] = jnp.zeros_like(acc_ref)

    for gi in range(n1 // PEER_RG):
        crows = slice(gi * grows, (gi + 1) * grows)
        st_scr[crows, :] = jnp.dot(u_ref[crows, :], x1t_ref[...], preferred_element_type=F32)

    nsb = PEER_NKEYS // PEER_SB
    tile_rows = pl.ds(pl.multiple_of(j * n1, SUBLANES), n1)
    for gi in range(n1 // PEER_RG):
        crows = slice(gi * grows, (gi + 1) * grows)
        for tc in range(tm // LANES):
            lanes = slice(tc * LANES, (tc + 1) * LANES)
            g = [[jnp.zeros((PEER_SB, LANES), BF16) for _ in range(nsb)] for _ in range(PEER_RG)]
            for h in range(PEER_HEADS):
                c1_t = c1_ref[h, tc, tile_rows, :]
                n1_t = n1_ref[h, tc, tile_rows, :]
                c1b = [_bcast_row_bf16(c1_t, gi * PEER_RG + r, PEER_SB) for r in range(PEER_RG)]
                n1b = [_bcast_row_bf16(n1_t, gi * PEER_RG + r, PEER_SB) for r in range(PEER_RG)]
                for sb in range(nsb):
                    rows = slice(sb * PEER_SB, (sb + 1) * PEER_SB)
                    r2c = r2_ref[h, rows, lanes]
                    e2c = e2_ref[h, rows, lanes]
                    for r in range(PEER_RG):
                        g[r][sb] = g[r][sb] + c1b[r] * jnp.where(r2c < n1b[r], e2c, jnp.zeros_like(e2c))
            for r in range(PEER_RG):
                for sb in range(nsb):
                    base = gi * grows + r * PEER_NKEYS + sb * PEER_SB
                    srows = slice(base, base + PEER_SB)
                    ht_scr[srows, lanes] = g[r][sb] * _gelu(st_scr[srows, lanes]).astype(BF16)
        acc_ref[...] += jnp.dot(vt_ref[:, crows], ht_scr[crows, :], preferred_element_type=F32)

    @pl.when(j == pl.num_programs(1) - 1)
    def _():
        y_ref[...] = jnp.transpose(acc_ref[...])


def _peer(x1t, u_tab, vt_tiles, c1, n1, e2, r2, tm=512):
    T = x1t.shape[1]
    nj, _, te = vt_tiles.shape
    once = pl.Buffered(1)
    row_spec = pl.BlockSpec((PEER_HEADS, tm // LANES, PEER_NKEYS, LANES), lambda i, j: (0, i, 0, 0),
                            pipeline_mode=once)
    col_spec = pl.BlockSpec((PEER_HEADS, PEER_NKEYS, tm), lambda i, j: (0, 0, i), pipeline_mode=once)
    return pl.pallas_call(
        functools.partial(_peer_kernel, tm=tm, te=te),
        grid=(T // tm, nj),
        in_specs=[
            pl.BlockSpec((D_MODEL, tm), lambda i, j: (0, i), pipeline_mode=once),
            pl.BlockSpec((te, D_MODEL), lambda i, j: (j, 0)),
            pl.BlockSpec((None, D_MODEL, te), lambda i, j: (j, 0, 0)),
            row_spec, row_spec, col_spec, col_spec,
        ],
        out_specs=pl.BlockSpec((tm, D_MODEL), lambda i, j: (i, 0)),
        out_shape=jax.ShapeDtypeStruct((T, D_MODEL), F32),
        scratch_shapes=[
            pltpu.VMEM((D_MODEL, tm), F32),
            pltpu.VMEM((te, tm), F32),
            pltpu.VMEM((te, tm), BF16),
        ],
        compiler_params=_params(("parallel", "arbitrary"), 56),
        name="peer",
    )(x1t, u_tab, vt_tiles, c1, n1, e2, r2)


def _final_kernel(x1_ref, y_ref, p_ref, wg_ref, wp_ref, lw_ref, lb_ref, o_ref):
    x1 = x1_ref[...]
    gate = jax.nn.sigmoid(jnp.dot(x1.astype(BF16), wg_ref[...], preferred_element_type=F32))
    emb = jnp.dot(p_ref[...].astype(BF16), wp_ref[...], preferred_element_type=F32)
    o_ref[...] = _layer_norm(ALPHA * x1 + y_ref[...] + gate * emb, lw_ref[...], lb_ref[...])


def _final(x1, y_ffn, p2, w_gate, w_proj, ln_w, ln_b, tm=256):
    T = x1.shape[0]
    return pl.pallas_call(
        _final_kernel,
        grid=(T // tm,),
        in_specs=[
            pl.BlockSpec((tm, D_MODEL), lambda i: (i, 0)),
            pl.BlockSpec((tm, D_MODEL), lambda i: (i, 0)),
            pl.BlockSpec((tm, PLE_DIM), lambda i: (i, 0)),
            pl.BlockSpec((D_MODEL, D_MODEL), lambda i: (0, 0)),
            pl.BlockSpec((PLE_DIM, D_MODEL), lambda i: (0, 0)),
            pl.BlockSpec((1, D_MODEL), lambda i: (0, 0)),
            pl.BlockSpec((1, D_MODEL), lambda i: (0, 0)),
        ],
        out_specs=pl.BlockSpec((tm, D_MODEL), lambda i: (i, 0)),
        out_shape=jax.ShapeDtypeStruct((T, D_MODEL), F32),
        compiler_params=_params(("parallel",), 48),
        name="final",
    )(x1, y_ffn, p2, w_gate, w_proj, ln_w, ln_b)


def _layer(x2, p2, B, S, w_in, gla_w_gate_up, gla_b_gate, gla_norm_w, pool_w, pool_scale, w_out,
           ln1_w, ln1_b, peer_w_query, peer_sub_keys, peer_u, peer_v, ple_w_gate, ple_w_proj, ln2_w, ln2_b):
    glr0 = COL_R
    w_main = jnp.concatenate([w_in[:, :glr0], w_in[:, glr0 + GLA_GATE_RANK:]], axis=1).astype(BF16)
    w_glr = jnp.pad(w_in[:, glr0:glr0 + GLA_GATE_RANK], ((0, 0), (0, LANES - GLA_GATE_RANK))).astype(BF16)
    proj, glr = _proj(x2, w_main, w_glr)

    y_pool = _pool(proj, pool_w.astype(BF16), pool_scale.reshape(1, POOL_WIDTH), S)

    wg = jnp.pad(gla_w_gate_up, ((0, LANES - GLA_GATE_RANK), (0, 0))).astype(BF16)
    y_gla = _gla(proj, glr, wg, gla_b_gate.reshape(1, GLA_KEY_WIDTH),
                 gla_norm_w.reshape(1, GLA_WIDTH), B, S)

    x1, x1t = _outproj(y_pool, y_gla, x2, w_out.astype(BF16),
                       ln1_w.reshape(1, D_MODEL), ln1_b.reshape(1, D_MODEL))

    keys = peer_sub_keys.reshape(2 * PEER_HEADS, PEER_NKEYS, PEER_HALF)
    keys_hi, keys_lo = _split_bf16(keys)
    c1, n1, e2, r2 = _query(x1t, peer_w_query.T.astype(BF16), keys_hi, keys_lo)
    n_exp = peer_v.shape[0]
    vt_tiles = peer_v.astype(BF16).reshape(n_exp // PEER_TE, PEER_TE, D_MODEL).transpose(0, 2, 1)
    y_ffn = _peer(x1t, peer_u.astype(BF16), vt_tiles, c1, n1, e2, r2)

    return _final(x1, y_ffn, p2, ple_w_gate.astype(BF16), ple_w_proj.astype(BF16),
                  ln2_w.reshape(1, D_MODEL), ln2_b.reshape(1, D_MODEL))


def kernel(x, p, w_in, gla_w_gate_up, gla_b_gate, gla_norm_w, pool_w, pool_scale, w_out, ln1_w, ln1_b,
           peer_w_query, peer_sub_keys, peer_u, peer_v, ple_w_gate, ple_w_proj, ln2_w, ln2_b):
    B, S, D = x.shape
    x2 = x.reshape(B * S, D)
    for i in range(w_in.shape[0]):
        x2 = _layer(x2, p[i].reshape(B * S, PLE_DIM), B, S, w_in[i], gla_w_gate_up[i], gla_b_gate[i],
                    gla_norm_w[i], pool_w[i], pool_scale[i], w_out[i], ln1_w[i], ln1_b[i],
                    peer_w_query[i], peer_sub_keys[i], peer_u[i], peer_v[i], ple_w_gate[i],
                    ple_w_proj[i], ln2_w[i], ln2_b[i])
    return x2.reshape(B, S, D)
```

```python
import functools
import math

import jax
import jax.numpy as jnp
from jax import lax
from jax.experimental import pallas as pl
from jax.experimental.pallas import tpu as pltpu

F32 = jnp.float32
BF16 = jnp.bfloat16

D_MODEL = 2048
PLE_DIM = 256
POOL_WIDTH = 1024
POOL_WINDOWS = (2, 4, 8, 16)
POOL_GC = 256
POOL_HALO = 16
GLA_WIDTH = 1024
GLA_HEADS = 4
GLA_DV = 256
GLA_DK = 128
GLA_KEY_WIDTH = 512
GLA_GATE_RANK = 16
GLA_GATE_TEMP = 16.0
GLA_CHUNK = 64
PEER_HEADS = 8
PEER_NKEYS = 128
PEER_HALF = 128
PEER_TOPK = 16
DEPTH = 1
ALPHA = float((2 * DEPTH) ** 0.25)
LN_EPS = 1e-5
RMS_EPS = 1e-6
LANES = 128
NEG_INF = float("-inf")

COL_Q = POOL_WIDTH
COL_K = COL_Q + GLA_KEY_WIDTH
COL_V = COL_K + GLA_KEY_WIDTH
COL_R = COL_V + GLA_WIDTH
PROJ_COLS = COL_R + GLA_WIDTH


def _params(sem, vmem_mib):
    return pltpu.CompilerParams(dimension_semantics=sem, vmem_limit_bytes=vmem_mib * 1024 * 1024)


def _proj_kernel(x_ref, w_ref, wg_ref, o_ref, glr_ref, xb_ref):
    @pl.when(pl.program_id(1) == 0)
    def _():
        xb = x_ref[...].astype(BF16)
        xb_ref[...] = xb
        glr_ref[...] = jnp.dot(xb, wg_ref[...], preferred_element_type=F32)

    o_ref[...] = jnp.dot(xb_ref[...], w_ref[...], preferred_element_type=F32).astype(o_ref.dtype)


def _proj(x2, w_main, w_glr, tm=512, tn=2048):
    T = x2.shape[0]
    return pl.pallas_call(
        _proj_kernel,
        grid=(T // tm, PROJ_COLS // tn),
        in_specs=[
            pl.BlockSpec((tm, D_MODEL), lambda i, n: (i, 0)),
            pl.BlockSpec((D_MODEL, tn), lambda i, n: (0, n)),
            pl.BlockSpec((D_MODEL, LANES), lambda i, n: (0, 0)),
        ],
        out_specs=[
            pl.BlockSpec((tm, tn), lambda i, n: (i, n)),
            pl.BlockSpec((tm, LANES), lambda i, n: (i, 0)),
        ],
        out_shape=[
            jax.ShapeDtypeStruct((T, PROJ_COLS), BF16),
            jax.ShapeDtypeStruct((T, LANES), F32),
        ],
        scratch_shapes=[pltpu.VMEM((tm, D_MODEL), BF16)],
        compiler_params=_params(("parallel", "arbitrary"), 40),
        name="proj",
    )(x2, w_main, w_glr)


def _pool_kernel(u_ref, halo_ref, w_ref, sc_ref, o_ref, ext_ref, *, tiles_per_seq, tm):
    t = pl.program_id(0) % tiles_per_seq
    halo = jnp.where(t == 0, 0.0, halo_ref[...].astype(F32))
    ext_ref[0:POOL_HALO, :] = halo
    ext_ref[POOL_HALO:, :] = u_ref[...].astype(F32)
    pos = t * tm + lax.broadcasted_iota(jnp.int32, (tm, 1), 0)
    for g, w in enumerate(POOL_WINDOWS):
        cols = slice(g * POOL_GC, (g + 1) * POOL_GC)
        u = ext_ref[POOL_HALO:, cols]
        acc = u
        for j in range(1, w):
            acc = acc + ext_ref[POOL_HALO - j:POOL_HALO - j + tm, cols]
        cnt = jnp.minimum(pos + 1, w).astype(F32)
        d = acc / cnt - u
        y = jnp.dot(d.astype(BF16), w_ref[g], preferred_element_type=F32)
        o_ref[:, cols] = (y * sc_ref[:, cols]).astype(o_ref.dtype)


def _pool(proj, pool_w, pool_scale, S, tm=512):
    T = proj.shape[0]
    hb = tm // POOL_HALO
    return pl.pallas_call(
        functools.partial(_pool_kernel, tiles_per_seq=S // tm, tm=tm),
        grid=(T // tm,),
        in_specs=[
            pl.BlockSpec((tm, POOL_WIDTH), lambda i: (i, 0)),
            pl.BlockSpec((POOL_HALO, POOL_WIDTH), lambda i: (jnp.maximum(i * hb - 1, 0), 0)),
            pl.BlockSpec((len(POOL_WINDOWS), POOL_GC, POOL_GC), lambda i: (0, 0, 0)),
            pl.BlockSpec((1, POOL_WIDTH), lambda i: (0, 0)),
        ],
        out_specs=pl.BlockSpec((tm, POOL_WIDTH), lambda i: (i, 0)),
        out_shape=jax.ShapeDtypeStruct((T, POOL_WIDTH), BF16),
        scratch_shapes=[pltpu.VMEM((POOL_HALO + tm, POOL_WIDTH), F32)],
        compiler_params=_params(("parallel",), 32),
        name="pool",
    )(proj, proj, pool_w, pool_scale)


def _gla_kernel(q_ref, k_ref, v_ref, r_ref, glr_ref, wg_ref, bg_ref, nw_ref, o_ref, s_ref, *, n_chunks):
    @pl.when(pl.program_id(1) == 0)
    def _():
        s_ref[...] = jnp.zeros_like(s_ref)

    C = GLA_CHUNK
    row = lax.broadcasted_iota(jnp.int32, (C, C), 0)
    col = lax.broadcasted_iota(jnp.int32, (C, C), 1)
    causal = col <= row
    tril = causal.astype(BF16)
    wg = wg_ref[...]
    bg = bg_ref[...]
    nt = (((1,), (1,)), ((), ()))
    for c in range(n_chunks):
        rows = slice(c * C, (c + 1) * C)
        z = jnp.dot(glr_ref[rows, :].astype(BF16), wg, preferred_element_type=F32) + bg
        g = jax.nn.log_sigmoid(z) / GLA_GATE_TEMP
        g_hi, g_lo = _split_bf16(g)
        b_all = (jnp.dot(tril, g_hi, preferred_element_type=F32)
                 + jnp.dot(tril, g_lo, preferred_element_type=F32))
        for h in range(GLA_HEADS):
            kc = slice(h * GLA_DK, (h + 1) * GLA_DK)
            vc = slice(h * GLA_DV, (h + 1) * GLA_DV)
            b = b_all[:, kc]
            b_last = b[C - 1:C, :]
            b_mid = b[C // 2 - 1:C // 2, :]
            q = q_ref[rows, kc].astype(F32) * (GLA_DK ** -0.5)
            k = k_ref[rows, kc].astype(F32)
            v = v_ref[rows, vc]
            q_state = (q * jnp.exp(b)).astype(BF16)
            q_in = (q * jnp.exp(b - b_mid)).astype(BF16)
            k_in = (k * jnp.exp(b_mid - b)).astype(BF16)
            k_out = k * jnp.exp(b_last - b)
            attn = lax.dot_general(q_in, k_in, nt, preferred_element_type=F32)
            attn = jnp.where(causal, attn, 0.0).astype(BF16)
            s = s_ref[h]
            o = (jnp.dot(attn, v, preferred_element_type=F32)
                 + jnp.dot(q_state, s.astype(BF16), preferred_element_type=F32))
            decay = jnp.transpose(jnp.broadcast_to(jnp.exp(b_last), (C, GLA_DK)))[:, 0:1]
            s_ref[h] = decay * s + jnp.dot(jnp.transpose(k_out).astype(BF16), v, preferred_element_type=F32)
            o = o * lax.rsqrt(jnp.mean(jnp.square(o), axis=-1, keepdims=True) + RMS_EPS)
            o = o * nw_ref[:, vc]
            r = r_ref[rows, vc].astype(F32)
            o_ref[rows, vc] = (o * (r * jax.nn.sigmoid(r))).astype(o_ref.dtype)


def _gla(proj, glr, wg, bg, nw, B, S, L=256):
    T = proj.shape[0]
    nl = S // L
    rb = lambda b, l: b * nl + l
    return pl.pallas_call(
        functools.partial(_gla_kernel, n_chunks=L // GLA_CHUNK),
        grid=(B, nl),
        in_specs=[
            pl.BlockSpec((L, GLA_KEY_WIDTH), lambda b, l: (rb(b, l), COL_Q // GLA_KEY_WIDTH)),
            pl.BlockSpec((L, GLA_KEY_WIDTH), lambda b, l: (rb(b, l), COL_K // GLA_KEY_WIDTH)),
            pl.BlockSpec((L, GLA_WIDTH), lambda b, l: (rb(b, l), COL_V // GLA_WIDTH)),
            pl.BlockSpec((L, GLA_WIDTH), lambda b, l: (rb(b, l), COL_R // GLA_WIDTH)),
            pl.BlockSpec((L, LANES), lambda b, l: (rb(b, l), 0)),
            pl.BlockSpec((LANES, GLA_KEY_WIDTH), lambda b, l: (0, 0)),
            pl.BlockSpec((1, GLA_KEY_WIDTH), lambda b, l: (0, 0)),
            pl.BlockSpec((1, GLA_WIDTH), lambda b, l: (0, 0)),
        ],
        out_specs=pl.BlockSpec((L, GLA_WIDTH), lambda b, l: (rb(b, l), 0)),
        out_shape=jax.ShapeDtypeStruct((T, GLA_WIDTH), BF16),
        scratch_shapes=[pltpu.VMEM((GLA_HEADS, GLA_DK, GLA_DV), F32)],
        compiler_params=_params(("parallel", "arbitrary"), 32),
        name="gla",
    )(proj, proj, proj, proj, glr, wg, bg, nw)


def _layer_norm(h, w, b):
    mu = jnp.mean(h, axis=-1, keepdims=True)
    hc = h - mu
    var = jnp.mean(jnp.square(hc), axis=-1, keepdims=True)
    return hc * lax.rsqrt(var + LN_EPS) * w + b


def _outproj_kernel(yp_ref, yg_ref, x_ref, w_ref, lw_ref, lb_ref, x1_ref, x1t_ref):
    mix = (jnp.dot(yp_ref[...], w_ref[0:POOL_WIDTH, :], preferred_element_type=F32)
           + jnp.dot(yg_ref[...], w_ref[POOL_WIDTH:, :], preferred_element_type=F32))
    x1 = _layer_norm(ALPHA * x_ref[...] + mix, lw_ref[...], lb_ref[...])
    x1_ref[...] = x1
    x1t_ref[...] = jnp.transpose(x1).astype(BF16)


def _outproj(y_pool, y_gla, x2, w_out, ln_w, ln_b, tm=256):
    T = x2.shape[0]
    return pl.pallas_call(
        _outproj_kernel,
        grid=(T // tm,),
        in_specs=[
            pl.BlockSpec((tm, POOL_WIDTH), lambda i: (i, 0)),
            pl.BlockSpec((tm, GLA_WIDTH), lambda i: (i, 0)),
            pl.BlockSpec((tm, D_MODEL), lambda i: (i, 0)),
            pl.BlockSpec((D_MODEL, D_MODEL), lambda i: (0, 0)),
            pl.BlockSpec((1, D_MODEL), lambda i: (0, 0)),
            pl.BlockSpec((1, D_MODEL), lambda i: (0, 0)),
        ],
        out_specs=[
            pl.BlockSpec((tm, D_MODEL), lambda i: (i, 0)),
            pl.BlockSpec((D_MODEL, tm), lambda i: (0, i)),
        ],
        out_shape=[
            jax.ShapeDtypeStruct((T, D_MODEL), F32),
            jax.ShapeDtypeStruct((D_MODEL, T), BF16),
        ],
        compiler_params=_params(("parallel",), 48),
        name="outproj",
    )(y_pool, y_gla, x2, w_out, ln_w, ln_b)


N_SORT = PEER_TOPK + 1


def _sort_network(n):
    pairs = []

    def merge(lo, m, r):
        step = 2 * r
        if step < m:
            merge(lo, m, step)
            merge(lo + r, m, step)
            pairs.extend((i, i + r) for i in range(lo + r, lo + m - r, step))
        else:
            pairs.append((lo, lo + r))

    def sort(lo, m):
        if m > 1:
            sort(lo, m // 2)
            sort(lo + m // 2, m // 2)
            merge(lo, m, 1)

    sort(0, n)
    return tuple(pairs)


def _pop_sorted(v, n_out):
    nv = len(v)
    width = 1 << (nv - 1).bit_length()
    for i, j in _sort_network(width):
        if j < nv:
            v[i], v[j] = jnp.maximum(v[i], v[j]), jnp.minimum(v[i], v[j])
    tops = []
    for kk in range(n_out):
        m = jnp.max(v[0], axis=0, keepdims=True)
        tops.append(m)
        hit = v[0] == m
        for k in range(min(n_out - 1 - kk, nv)):
            v[k] = jnp.where(hit, v[k + 1] if k + 1 < nv else NEG_INF, v[k])
    return tops


def _sorted_top(arr):
    return _pop_sorted([arr[k:k + SUBLANES] for k in range(0, arr.shape[0], SUBLANES)], N_SORT)


def _rows_to_tile(rows):
    rid = lax.broadcasted_iota(jnp.int32, (SUBLANES, LANES), 0)
    tile = jnp.full((SUBLANES, LANES), NEG_INF, F32)
    for k, r in enumerate(rows):
        tile = jnp.where(rid == k, r, tile)
    return tile


def _pair_stats(a, b):
    r8 = lax.broadcasted_iota(jnp.int32, (SUBLANES, LANES), 0)
    b_lo, b_hi, a_hi = _rows_to_tile(b[0:8]), _rows_to_tile(b[8:16]), _rows_to_tile(a[8:16])
    p2 = jnp.where(r8 < 5, a[2] + b_lo, jnp.where(r8 == 5, a[16] + b[0], jnp.where(r8 == 6, a[0] + b[16], NEG_INF)))
    pieces = [
        a[0] + b_lo, a[0] + b_hi, a[1] + b_lo, p2,
        jnp.where(r8 < 4, a[3] + b_lo, NEG_INF),
        jnp.where(r8 < 3, a[4] + b_lo, NEG_INF),
        jnp.where(r8 < 2, a[5] + b_lo, NEG_INF),
        jnp.where(r8 < 2, a[6] + b_lo, NEG_INF),
        jnp.where(r8 < 2, a[7] + b_lo, NEG_INF),
        a_hi + b[0],
    ]
    sums = _pop_sorted(pieces, N_SORT)
    top16 = jnp.concatenate([_rows_to_tile(sums[0:8]), _rows_to_tile(sums[8:16])], axis=0)
    z = jnp.sum(jnp.exp(top16 - sums[0]), axis=0, keepdims=True)
    return 0.5 * (sums[PEER_TOPK - 1] + sums[PEER_TOPK]), 1.0 / z


def _split_bf16(x):
    hi = x.astype(BF16)
    return hi, (x - hi.astype(F32)).astype(BF16)


def _query_kernel(x1t_ref, wq_ref, khi_ref, klo_ref, c1_ref, n1_ref, e2_ref, r2_ref, q_ref, s1_scr, s2_scr, *, tm):
    q_ref[...] = jnp.dot(wq_ref[...], x1t_ref[...], preferred_element_type=F32)
    for h in range(PEER_HEADS):
        for p in range(2):
            hp = 2 * h + p
            q_hi, q_lo = _split_bf16(q_ref[hp * PEER_HALF:(hp + 1) * PEER_HALF, :])
            k_hi, k_lo = khi_ref[hp], klo_ref[hp]
            sc = (jnp.dot(k_hi, q_hi, preferred_element_type=F32)
                  + jnp.dot(k_hi, q_lo, preferred_element_type=F32)
                  + jnp.dot(k_lo, q_hi, preferred_element_type=F32))
            if p == 0:
                s1_scr[...] = sc
            else:
                s2_scr[...] = sc
        for tc in range(tm // LANES):
            lanes = slice(tc * LANES, (tc + 1) * LANES)
            s1 = s1_scr[:, lanes]
            s2 = s2_scr[:, lanes]
            ta, tb = _sorted_top(s1), _sorted_top(s2)
            tau, rz = _pair_stats(ta, tb)
            n1 = jnp.zeros_like(s1)
            r2 = jnp.zeros_like(s2)
            for jj in range(PEER_TOPK):
                n1 = jnp.where(s1 >= tau - tb[jj], float(jj + 1), n1)
                r2 = jnp.where(s2 < tb[jj], float(jj + 1), r2)
            c1_ref[h, tc] = jnp.exp(s1 - ta[0])
            n1_ref[h, tc] = n1
            e2_ref[h, :, lanes] = (jnp.exp(s2 - tb[0]) * rz).astype(BF16)
            r2_ref[h, :, lanes] = r2.astype(BF16)


def _query(x1t, wq_t, keys_hi, keys_lo, tm=256):
    T = x1t.shape[1]
    nc = tm // LANES
    row_spec = pl.BlockSpec((PEER_HEADS, nc, PEER_NKEYS, LANES), lambda i: (0, i, 0, 0))
    col_spec = pl.BlockSpec((PEER_HEADS, PEER_NKEYS, tm), lambda i: (0, 0, i))
    key_spec = pl.BlockSpec((2 * PEER_HEADS, PEER_NKEYS, PEER_HALF), lambda i: (0, 0, 0))
    row_shape = jax.ShapeDtypeStruct((PEER_HEADS, T // LANES, PEER_NKEYS, LANES), F32)
    col_shape = jax.ShapeDtypeStruct((PEER_HEADS, PEER_NKEYS, T), BF16)
    return pl.pallas_call(
        functools.partial(_query_kernel, tm=tm),
        grid=(T // tm,),
        in_specs=[
            pl.BlockSpec((D_MODEL, tm), lambda i: (0, i)),
            pl.BlockSpec((D_MODEL, D_MODEL), lambda i: (0, 0)),
            key_spec, key_spec,
        ],
        out_specs=[row_spec, row_spec, col_spec, col_spec],
        out_shape=[row_shape, row_shape, col_shape, col_shape],
        scratch_shapes=[pltpu.VMEM((D_MODEL, tm), F32), pltpu.VMEM((PEER_NKEYS, tm), F32),
                        pltpu.VMEM((PEER_NKEYS, tm), F32)],
        compiler_params=_params(("parallel",), 48),
        name="query",
    )(x1t, wq_t, keys_hi, keys_lo)


PEER_SB = 64
PEER_RG = 2
SUBLANES = 8
PEER_TE = 1024


def _gelu(x):
    return 0.5 * x * (1.0 + lax.erf(x * (1.0 / math.sqrt(2.0))))


def _bcast_row_bf16(tile, ri, rows):
    packed = jnp.broadcast_to(tile[ri:ri + 1, :], (2 * SUBLANES, LANES)).astype(BF16)
    return jnp.concatenate([packed] * (rows // (2 * SUBLANES)), axis=0)


def _peer_kernel(x1t_ref, u_ref, vt_ref, c1_ref, n1_ref, e2_ref, r2_ref, y_ref,
                 acc_ref, st_scr, ht_scr, *, tm, te):
    j = pl.program_id(1)
    n1 = te // PEER_NKEYS
    grows = PEER_RG * PEER_NKEYS

    @pl.when(j == 0)
    def _():
        acc_ref[...] = jnp.zeros_like(acc_ref)

    for gi in range(n1 // PEER_RG):
        crows = slice(gi * grows, (gi + 1) * grows)
        st_scr[crows, :] = jnp.dot(u_ref[crows, :], x1t_ref[...], preferred_element_type=F32)

    nsb = PEER_NKEYS // PEER_SB
    tile_rows = pl.ds(pl.multiple_of(j * n1, SUBLANES), n1)
    for gi in range(n1 // PEER_RG):
        crows = slice(gi * grows, (gi + 1) * grows)
        for tc in range(tm // LANES):
            lanes = slice(tc * LANES, (tc + 1) * LANES)
            g = [[jnp.zeros((PEER_SB, LANES), BF16) for _ in range(nsb)] for _ in range(PEER_RG)]
            for h in range(PEER_HEADS):
                c1_t = c1_ref[h, tc, tile_rows, :]
                n1_t = n1_ref[h, tc, tile_rows, :]
                c1b = [_bcast_row_bf16(c1_t, gi * PEER_RG + r, PEER_SB) for r in range(PEER_RG)]
                n1b = [_bcast_row_bf16(n1_t, gi * PEER_RG + r, PEER_SB) for r in range(PEER_RG)]
                for sb in range(nsb):
                    rows = slice(sb * PEER_SB, (sb + 1) * PEER_SB)
                    r2c = r2_ref[h, rows, lanes]
                    e2c = e2_ref[h, rows, lanes]
                    for r in range(PEER_RG):
                        g[r][sb] = g[r][sb] + c1b[r] * jnp.where(r2c < n1b[r], e2c, jnp.zeros_like(e2c))
            for r in range(PEER_RG):
                for sb in range(nsb):
                    base = gi * grows + r * PEER_NKEYS + sb * PEER_SB
                    srows = slice(base, base + PEER_SB)
                    ht_scr[srows, lanes] = g[r][sb] * _gelu(st_scr[srows, lanes]).astype(BF16)
        acc_ref[...] += jnp.dot(vt_ref[:, crows], ht_scr[crows, :], preferred_element_type=F32)

    @pl.when(j == pl.num_programs(1) - 1)
    def _():
        y_ref[...] = jnp.transpose(acc_ref[...])


def _peer(x1t, u_tab, vt_tiles, c1, n1, e2, r2, tm=512):
    T = x1t.shape[1]
    nj, _, te = vt_tiles.shape
    once = pl.Buffered(1)
    row_spec = pl.BlockSpec((PEER_HEADS, tm // LANES, PEER_NKEYS, LANES), lambda i, j: (0, i, 0, 0),
                            pipeline_mode=once)
    col_spec = pl.BlockSpec((PEER_HEADS, PEER_NKEYS, tm), lambda i, j: (0, 0, i), pipeline_mode=once)
    return pl.pallas_call(
        functools.partial(_peer_kernel, tm=tm, te=te),
        grid=(T // tm, nj),
        in_specs=[
            pl.BlockSpec((D_MODEL, tm), lambda i, j: (0, i), pipeline_mode=once),
            pl.BlockSpec((te, D_MODEL), lambda i, j: (j, 0)),
            pl.BlockSpec((None, D_MODEL, te), lambda i, j: (j, 0, 0)),
            row_spec, row_spec, col_spec, col_spec,
        ],
        out_specs=pl.BlockSpec((tm, D_MODEL), lambda i, j: (i, 0)),
        out_shape=jax.ShapeDtypeStruct((T, D_MODEL), F32),
        scratch_shapes=[
            pltpu.VMEM((D_MODEL, tm), F32),
            pltpu.VMEM((te, tm), F32),
            pltpu.VMEM((te, tm), BF16),
        ],
        compiler_params=_params(("parallel", "arbitrary"), 56),
        name="peer",
    )(x1t, u_tab, vt_tiles, c1, n1, e2, r2)


def _final_kernel(x1_ref, y_ref, p_ref, wg_ref, wp_ref, lw_ref, lb_ref, o_ref):
    x1 = x1_ref[...]
    gate = jax.nn.sigmoid(jnp.dot(x1.astype(BF16), wg_ref[...], preferred_element_type=F32))
    emb = jnp.dot(p_ref[...].astype(BF16), wp_ref[...], preferred_element_type=F32)
    o_ref[...] = _layer_norm(ALPHA * x1 + y_ref[...] + gate * emb, lw_ref[...], lb_ref[...])


def _final(x1, y_ffn, p2, w_gate, w_proj, ln_w, ln_b, tm=256):
    T = x1.shape[0]
    return pl.pallas_call(
        _final_kernel,
        grid=(T // tm,),
        in_specs=[
            pl.BlockSpec((tm, D_MODEL), lambda i: (i, 0)),
            pl.BlockSpec((tm, D_MODEL), lambda i: (i, 0)),
            pl.BlockSpec((tm, PLE_DIM), lambda i: (i, 0)),
            pl.BlockSpec((D_MODEL, D_MODEL), lambda i: (0, 0)),
            pl.BlockSpec((PLE_DIM, D_MODEL), lambda i: (0, 0)),
            pl.BlockSpec((1, D_MODEL), lambda i: (0, 0)),
            pl.BlockSpec((1, D_MODEL), lambda i: (0, 0)),
        ],
        out_specs=pl.BlockSpec((tm, D_MODEL), lambda i: (i, 0)),
        out_shape=jax.ShapeDtypeStruct((T, D_MODEL), F32),
        compiler_params=_params(("parallel",), 48),
        name="final",
    )(x1, y_ffn, p2, w_gate, w_proj, ln_w, ln_b)


def _layer(x2, p2, B, S, w_in, gla_w_gate_up, gla_b_gate, gla_norm_w, pool_w, pool_scale, w_out,
           ln1_w, ln1_b, peer_w_query, peer_sub_keys, peer_u, peer_v, ple_w_gate, ple_w_proj, ln2_w, ln2_b):
    glr0 = COL_R
    w_main = jnp.concatenate([w_in[:, :glr0], w_in[:, glr0 + GLA_GATE_RANK:]], axis=1).astype(BF16)
    w_glr = jnp.pad(w_in[:, glr0:glr0 + GLA_GATE_RANK], ((0, 0), (0, LANES - GLA_GATE_RANK))).astype(BF16)
    proj, glr = _proj(x2, w_main, w_glr)

    y_pool = _pool(proj, pool_w.astype(BF16), pool_scale.reshape(1, POOL_WIDTH), S)

    wg = jnp.pad(gla_w_gate_up, ((0, LANES - GLA_GATE_RANK), (0, 0))).astype(BF16)
    y_gla = _gla(proj, glr, wg, gla_b_gate.reshape(1, GLA_KEY_WIDTH),
                 gla_norm_w.reshape(1, GLA_WIDTH), B, S)

    x1, x1t = _outproj(y_pool, y_gla, x2, w_out.astype(BF16),
                       ln1_w.reshape(1, D_MODEL), ln1_b.reshape(1, D_MODEL))

    keys = peer_sub_keys.reshape(2 * PEER_HEADS, PEER_NKEYS, PEER_HALF)
    keys_hi, keys_lo = _split_bf16(keys)
    c1, n1, e2, r2 = _query(x1t, peer_w_query.T.astype(BF16), keys_hi, keys_lo)
    n_exp = peer_v.shape[0]
    vt_tiles = peer_v.reshape(n_exp // PEER_TE, PEER_TE, D_MODEL).transpose(0, 2, 1).astype(BF16)
    y_ffn = _peer(x1t, peer_u.astype(BF16), vt_tiles, c1, n1, e2, r2)

    return _final(x1, y_ffn, p2, ple_w_gate.astype(BF16), ple_w_proj.astype(BF16),
                  ln2_w.reshape(1, D_MODEL), ln2_b.reshape(1, D_MODEL))


def kernel(x, p, w_in, gla_w_gate_up, gla_b_gate, gla_norm_w, pool_w, pool_scale, w_out, ln1_w, ln1_b,
           peer_w_query, peer_sub_keys, peer_u, peer_v, ple_w_gate, ple_w_proj, ln2_w, ln2_b):
    B, S, D = x.shape
    x2 = x.reshape(B * S, D)
    for i in range(w_in.shape[0]):
        x2 = _layer(x2, p[i].reshape(B * S, PLE_DIM), B, S, w_in[i], gla_w_gate_up[i], gla_b_gate[i],
                    gla_norm_w[i], pool_w[i], pool_scale[i], w_out[i], ln1_w[i], ln1_b[i],
                    peer_w_query[i], peer_sub_keys[i], peer_u[i], peer_v[i], ple_w_gate[i],
                    ple_w_proj[i], ln2_w[i], ln2_b[i])
    return x2.reshape(B, S, D)
```

```python
import functools
import math

import jax
import jax.numpy as jnp
from jax import lax
from jax.experimental import pallas as pl
from jax.experimental.pallas import tpu as pltpu

F32 = jnp.float32
BF16 = jnp.bfloat16

D_MODEL = 2048
PLE_DIM = 256
POOL_WIDTH = 1024
POOL_WINDOWS = (2, 4, 8, 16)
POOL_GC = 256
POOL_HALO = 16
GLA_WIDTH = 1024
GLA_HEADS = 4
GLA_DV = 256
GLA_DK = 128
GLA_KEY_WIDTH = 512
GLA_GATE_RANK = 16
GLA_GATE_TEMP = 16.0
GLA_CHUNK = 64
PEER_HEADS = 8
PEER_NKEYS = 128
PEER_HALF = 128
PEER_TOPK = 16
DEPTH = 1
ALPHA = float((2 * DEPTH) ** 0.25)
LN_EPS = 1e-5
RMS_EPS = 1e-6
LANES = 128
NEG_INF = float("-inf")

COL_Q = POOL_WIDTH
COL_K = COL_Q + GLA_KEY_WIDTH
COL_V = COL_K + GLA_KEY_WIDTH
COL_R = COL_V + GLA_WIDTH
PROJ_COLS = COL_R + GLA_WIDTH


def _params(sem, vmem_mib):
    return pltpu.CompilerParams(dimension_semantics=sem, vmem_limit_bytes=vmem_mib * 1024 * 1024)


def _proj_kernel(x_ref, w_ref, wg_ref, o_ref, glr_ref, xb_ref):
    @pl.when(pl.program_id(1) == 0)
    def _():
        xb = x_ref[...].astype(BF16)
        xb_ref[...] = xb
        glr_ref[...] = jnp.dot(xb, wg_ref[...], preferred_element_type=F32)

    o_ref[...] = jnp.dot(xb_ref[...], w_ref[...], preferred_element_type=F32).astype(o_ref.dtype)


def _proj(x2, w_main, w_glr, tm=512, tn=2048):
    T = x2.shape[0]
    return pl.pallas_call(
        _proj_kernel,
        grid=(T // tm, PROJ_COLS // tn),
        in_specs=[
            pl.BlockSpec((tm, D_MODEL), lambda i, n: (i, 0)),
            pl.BlockSpec((D_MODEL, tn), lambda i, n: (0, n)),
            pl.BlockSpec((D_MODEL, LANES), lambda i, n: (0, 0)),
        ],
        out_specs=[
            pl.BlockSpec((tm, tn), lambda i, n: (i, n)),
            pl.BlockSpec((tm, LANES), lambda i, n: (i, 0)),
        ],
        out_shape=[
            jax.ShapeDtypeStruct((T, PROJ_COLS), BF16),
            jax.ShapeDtypeStruct((T, LANES), F32),
        ],
        scratch_shapes=[pltpu.VMEM((tm, D_MODEL), BF16)],
        compiler_params=_params(("parallel", "arbitrary"), 40),
        name="proj",
    )(x2, w_main, w_glr)


def _pool_kernel(u_ref, halo_ref, w_ref, sc_ref, o_ref, ext_ref, *, tiles_per_seq, tm):
    t = pl.program_id(0) % tiles_per_seq
    halo = jnp.where(t == 0, 0.0, halo_ref[...].astype(F32))
    ext_ref[0:POOL_HALO, :] = halo
    ext_ref[POOL_HALO:, :] = u_ref[...].astype(F32)
    pos = t * tm + lax.broadcasted_iota(jnp.int32, (tm, 1), 0)
    for g, w in enumerate(POOL_WINDOWS):
        cols = slice(g * POOL_GC, (g + 1) * POOL_GC)
        u = ext_ref[POOL_HALO:, cols]
        acc = u
        for j in range(1, w):
            acc = acc + ext_ref[POOL_HALO - j:POOL_HALO - j + tm, cols]
        cnt = jnp.minimum(pos + 1, w).astype(F32)
        d = acc / cnt - u
        y = jnp.dot(d.astype(BF16), w_ref[g], preferred_element_type=F32)
        o_ref[:, cols] = (y * sc_ref[:, cols]).astype(o_ref.dtype)


def _pool(proj, pool_w, pool_scale, S, tm=512):
    T = proj.shape[0]
    hb = tm // POOL_HALO
    return pl.pallas_call(
        functools.partial(_pool_kernel, tiles_per_seq=S // tm, tm=tm),
        grid=(T // tm,),
        in_specs=[
            pl.BlockSpec((tm, POOL_WIDTH), lambda i: (i, 0)),
            pl.BlockSpec((POOL_HALO, POOL_WIDTH), lambda i: (jnp.maximum(i * hb - 1, 0), 0)),
            pl.BlockSpec((len(POOL_WINDOWS), POOL_GC, POOL_GC), lambda i: (0, 0, 0)),
            pl.BlockSpec((1, POOL_WIDTH), lambda i: (0, 0)),
        ],
        out_specs=pl.BlockSpec((tm, POOL_WIDTH), lambda i: (i, 0)),
        out_shape=jax.ShapeDtypeStruct((T, POOL_WIDTH), BF16),
        scratch_shapes=[pltpu.VMEM((POOL_HALO + tm, POOL_WIDTH), F32)],
        compiler_params=_params(("parallel",), 32),
        name="pool",
    )(proj, proj, pool_w, pool_scale)


def _gla_kernel(q_ref, k_ref, v_ref, r_ref, glr_ref, wg_ref, bg_ref, nw_ref, o_ref, s_ref, *, n_chunks):
    @pl.when(pl.program_id(1) == 0)
    def _():
        s_ref[...] = jnp.zeros_like(s_ref)

    C = GLA_CHUNK
    row = lax.broadcasted_iota(jnp.int32, (C, C), 0)
    col = lax.broadcasted_iota(jnp.int32, (C, C), 1)
    causal = col <= row
    tril = causal.astype(BF16)
    wg = wg_ref[...]
    bg = bg_ref[...]
    nt = (((1,), (1,)), ((), ()))
    for c in range(n_chunks):
        rows = slice(c * C, (c + 1) * C)
        z = jnp.dot(glr_ref[rows, :].astype(BF16), wg, preferred_element_type=F32) + bg
        g = jax.nn.log_sigmoid(z) / GLA_GATE_TEMP
        g_hi, g_lo = _split_bf16(g)
        b_all = (jnp.dot(tril, g_hi, preferred_element_type=F32)
                 + jnp.dot(tril, g_lo, preferred_element_type=F32))
        for h in range(GLA_HEADS):
            kc = slice(h * GLA_DK, (h + 1) * GLA_DK)
            vc = slice(h * GLA_DV, (h + 1) * GLA_DV)
            b = b_all[:, kc]
            b_last = b[C - 1:C, :]
            b_mid = b[C // 2 - 1:C // 2, :]
            q = q_ref[rows, kc].astype(F32) * (GLA_DK ** -0.5)
            k = k_ref[rows, kc].astype(F32)
            v = v_ref[rows, vc]
            q_state = (q * jnp.exp(b)).astype(BF16)
            q_in = (q * jnp.exp(b - b_mid)).astype(BF16)
            k_in = (k * jnp.exp(b_mid - b)).astype(BF16)
            k_out = k * jnp.exp(b_last - b)
            attn = lax.dot_general(q_in, k_in, nt, preferred_element_type=F32)
            attn = jnp.where(causal, attn, 0.0).astype(BF16)
            s = s_ref[h]
            o = (jnp.dot(attn, v, preferred_element_type=F32)
                 + jnp.dot(q_state, s.astype(BF16), preferred_element_type=F32))
            decay = jnp.transpose(jnp.broadcast_to(jnp.exp(b_last), (C, GLA_DK)))[:, 0:1]
            s_ref[h] = decay * s + jnp.dot(jnp.transpose(k_out).astype(BF16), v, preferred_element_type=F32)
            o = o * lax.rsqrt(jnp.mean(jnp.square(o), axis=-1, keepdims=True) + RMS_EPS)
            o = o * nw_ref[:, vc]
            r = r_ref[rows, vc].astype(F32)
            o_ref[rows, vc] = (o * (r * jax.nn.sigmoid(r))).astype(o_ref.dtype)


def _gla(proj, glr, wg, bg, nw, B, S, L=256):
    T = proj.shape[0]
    nl = S // L
    rb = lambda b, l: b * nl + l
    return pl.pallas_call(
        functools.partial(_gla_kernel, n_chunks=L // GLA_CHUNK),
        grid=(B, nl),
        in_specs=[
            pl.BlockSpec((L, GLA_KEY_WIDTH), lambda b, l: (rb(b, l), COL_Q // GLA_KEY_WIDTH)),
            pl.BlockSpec((L, GLA_KEY_WIDTH), lambda b, l: (rb(b, l), COL_K // GLA_KEY_WIDTH)),
            pl.BlockSpec((L, GLA_WIDTH), lambda b, l: (rb(b, l), COL_V // GLA_WIDTH)),
            pl.BlockSpec((L, GLA_WIDTH), lambda b, l: (rb(b, l), COL_R // GLA_WIDTH)),
            pl.BlockSpec((L, LANES), lambda b, l: (rb(b, l), 0)),
            pl.BlockSpec((LANES, GLA_KEY_WIDTH), lambda b, l: (0, 0)),
            pl.BlockSpec((1, GLA_KEY_WIDTH), lambda b, l: (0, 0)),
            pl.BlockSpec((1, GLA_WIDTH), lambda b, l: (0, 0)),
        ],
        out_specs=pl.BlockSpec((L, GLA_WIDTH), lambda b, l: (rb(b, l), 0)),
        out_shape=jax.ShapeDtypeStruct((T, GLA_WIDTH), BF16),
        scratch_shapes=[pltpu.VMEM((GLA_HEADS, GLA_DK, GLA_DV), F32)],
        compiler_params=_params(("parallel", "arbitrary"), 32),
        name="gla",
    )(proj, proj, proj, proj, glr, wg, bg, nw)


def _layer_norm(h, w, b):
    mu = jnp.mean(h, axis=-1, keepdims=True)
    hc = h - mu
    var = jnp.mean(jnp.square(hc), axis=-1, keepdims=True)
    return hc * lax.rsqrt(var + LN_EPS) * w + b


def _outproj_kernel(yp_ref, yg_ref, x_ref, w_ref, lw_ref, lb_ref, x1_ref, x1t_ref):
    mix = (jnp.dot(yp_ref[...], w_ref[0:POOL_WIDTH, :], preferred_element_type=F32)
           + jnp.dot(yg_ref[...], w_ref[POOL_WIDTH:, :], preferred_element_type=F32))
    x1 = _layer_norm(ALPHA * x_ref[...] + mix, lw_ref[...], lb_ref[...])
    x1_ref[...] = x1
    x1t_ref[...] = jnp.transpose(x1).astype(BF16)


def _outproj(y_pool, y_gla, x2, w_out, ln_w, ln_b, tm=256):
    T = x2.shape[0]
    return pl.pallas_call(
        _outproj_kernel,
        grid=(T // tm,),
        in_specs=[
            pl.BlockSpec((tm, POOL_WIDTH), lambda i: (i, 0)),
            pl.BlockSpec((tm, GLA_WIDTH), lambda i: (i, 0)),
            pl.BlockSpec((tm, D_MODEL), lambda i: (i, 0)),
            pl.BlockSpec((D_MODEL, D_MODEL), lambda i: (0, 0)),
            pl.BlockSpec((1, D_MODEL), lambda i: (0, 0)),
            pl.BlockSpec((1, D_MODEL), lambda i: (0, 0)),
        ],
        out_specs=[
            pl.BlockSpec((tm, D_MODEL), lambda i: (i, 0)),
            pl.BlockSpec((D_MODEL, tm), lambda i: (0, i)),
        ],
        out_shape=[
            jax.ShapeDtypeStruct((T, D_MODEL), F32),
            jax.ShapeDtypeStruct((D_MODEL, T), BF16),
        ],
        compiler_params=_params(("parallel",), 48),
        name="outproj",
    )(y_pool, y_gla, x2, w_out, ln_w, ln_b)


N_SORT = PEER_TOPK + 1


def _sort_network(n):
    pairs = []

    def merge(lo, m, r):
        step = 2 * r
        if step < m:
            merge(lo, m, step)
            merge(lo + r, m, step)
            pairs.extend((i, i + r) for i in range(lo + r, lo + m - r, step))
        else:
            pairs.append((lo, lo + r))

    def sort(lo, m):
        if m > 1:
            sort(lo, m // 2)
            sort(lo + m // 2, m // 2)
            merge(lo, m, 1)

    sort(0, n)
    return tuple(pairs)


def _pop_sorted(v, n_out):
    nv = len(v)
    width = 1 << (nv - 1).bit_length()
    for i, j in _sort_network(width):
        if j < nv:
            v[i], v[j] = jnp.maximum(v[i], v[j]), jnp.minimum(v[i], v[j])
    tops = []
    for kk in range(n_out):
        m = jnp.max(v[0], axis=0, keepdims=True)
        tops.append(m)
        hit = v[0] == m
        for k in range(min(n_out - 1 - kk, nv)):
            v[k] = jnp.where(hit, v[k + 1] if k + 1 < nv else NEG_INF, v[k])
    return tops


def _sorted_top(arr):
    return _pop_sorted([arr[k:k + SUBLANES] for k in range(0, arr.shape[0], SUBLANES)], N_SORT)


def _rows_to_tile(rows):
    rid = lax.broadcasted_iota(jnp.int32, (SUBLANES, LANES), 0)
    tile = jnp.full((SUBLANES, LANES), NEG_INF, F32)
    for k, r in enumerate(rows):
        tile = jnp.where(rid == k, r, tile)
    return tile


def _pair_stats(a, b):
    r8 = lax.broadcasted_iota(jnp.int32, (SUBLANES, LANES), 0)
    b_lo, b_hi, a_hi = _rows_to_tile(b[0:8]), _rows_to_tile(b[8:16]), _rows_to_tile(a[8:16])
    p2 = jnp.where(r8 < 5, a[2] + b_lo, jnp.where(r8 == 5, a[16] + b[0], jnp.where(r8 == 6, a[0] + b[16], NEG_INF)))
    pieces = [
        a[0] + b_lo, a[0] + b_hi, a[1] + b_lo, p2,
        jnp.where(r8 < 4, a[3] + b_lo, NEG_INF),
        jnp.where(r8 < 3, a[4] + b_lo, NEG_INF),
        jnp.where(r8 < 2, a[5] + b_lo, NEG_INF),
        jnp.where(r8 < 2, a[6] + b_lo, NEG_INF),
        jnp.where(r8 < 2, a[7] + b_lo, NEG_INF),
        a_hi + b[0],
    ]
    sums = _pop_sorted(pieces, N_SORT)
    top16 = jnp.concatenate([_rows_to_tile(sums[0:8]), _rows_to_tile(sums[8:16])], axis=0)
    z = jnp.sum(jnp.exp(top16 - sums[0]), axis=0, keepdims=True)
    return 0.5 * (sums[PEER_TOPK - 1] + sums[PEER_TOPK]), 1.0 / z


def _split_bf16(x):
    hi = x.astype(BF16)
    return hi, (x - hi.astype(F32)).astype(BF16)


def _query_kernel(x1t_ref, wq_ref, khi_ref, klo_ref, c1_ref, n1_ref, e2_ref, r2_ref, q_ref, s1_scr, s2_scr, *, tm):
    q_ref[...] = jnp.dot(wq_ref[...], x1t_ref[...], preferred_element_type=F32)
    for h in range(PEER_HEADS):
        for p in range(2):
            hp = 2 * h + p
            q_hi, q_lo = _split_bf16(q_ref[hp * PEER_HALF:(hp + 1) * PEER_HALF, :])
            k_hi, k_lo = khi_ref[hp], klo_ref[hp]
            sc = (jnp.dot(k_hi, q_hi, preferred_element_type=F32)
                  + jnp.dot(k_hi, q_lo, preferred_element_type=F32)
                  + jnp.dot(k_lo, q_hi, preferred_element_type=F32))
            if p == 0:
                s1_scr[...] = sc
            else:
                s2_scr[...] = sc
        for tc in range(tm // LANES):
            lanes = slice(tc * LANES, (tc + 1) * LANES)
            s1 = s1_scr[:, lanes]
            s2 = s2_scr[:, lanes]
            ta, tb = _sorted_top(s1), _sorted_top(s2)
            tau, rz = _pair_stats(ta, tb)
            n1 = jnp.zeros_like(s1)
            r2 = jnp.zeros_like(s2)
            for jj in range(PEER_TOPK):
                n1 = jnp.where(s1 >= tau - tb[jj], float(jj + 1), n1)
                r2 = jnp.where(s2 < tb[jj], float(jj + 1), r2)
            c1_ref[h, tc] = jnp.exp(s1 - ta[0])
            n1_ref[h, tc] = n1
            e2_ref[h, :, lanes] = (jnp.exp(s2 - tb[0]) * rz).astype(BF16)
            r2_ref[h, :, lanes] = r2.astype(BF16)


def _query(x1t, wq_t, keys_hi, keys_lo, tm=256):
    T = x1t.shape[1]
    nc = tm // LANES
    row_spec = pl.BlockSpec((PEER_HEADS, nc, PEER_NKEYS, LANES), lambda i: (0, i, 0, 0))
    col_spec = pl.BlockSpec((PEER_HEADS, PEER_NKEYS, tm), lambda i: (0, 0, i))
    key_spec = pl.BlockSpec((2 * PEER_HEADS, PEER_NKEYS, PEER_HALF), lambda i: (0, 0, 0))
    row_shape = jax.ShapeDtypeStruct((PEER_HEADS, T // LANES, PEER_NKEYS, LANES), F32)
    col_shape = jax.ShapeDtypeStruct((PEER_HEADS, PEER_NKEYS, T), BF16)
    return pl.pallas_call(
        functools.partial(_query_kernel, tm=tm),
        grid=(T // tm,),
        in_specs=[
            pl.BlockSpec((D_MODEL, tm), lambda i: (0, i)),
            pl.BlockSpec((D_MODEL, D_MODEL), lambda i: (0, 0)),
            key_spec, key_spec,
        ],
        out_specs=[row_spec, row_spec, col_spec, col_spec],
        out_shape=[row_shape, row_shape, col_shape, col_shape],
        scratch_shapes=[pltpu.VMEM((D_MODEL, tm), F32), pltpu.VMEM((PEER_NKEYS, tm), F32),
                        pltpu.VMEM((PEER_NKEYS, tm), F32)],
        compiler_params=_params(("parallel",), 48),
        name="query",
    )(x1t, wq_t, keys_hi, keys_lo)


PEER_SB = 64
PEER_RG = 2
SUBLANES = 8
PEER_TE = 1024


def _gelu(x):
    return 0.5 * x * (1.0 + lax.erf(x * (1.0 / math.sqrt(2.0))))


def _bcast_row_bf16(tile, ri, rows):
    packed = jnp.broadcast_to(tile[ri:ri + 1, :], (2 * SUBLANES, LANES)).astype(BF16)
    return jnp.concatenate([packed] * (rows // (2 * SUBLANES)), axis=0)


def _peer_kernel(x1t_ref, u_ref, vt_ref, c1_ref, n1_ref, e2_ref, r2_ref, y_ref,
                 acc_ref, st_scr, ht_scr, *, tm, te):
    j = pl.program_id(1)
    n1 = te // PEER_NKEYS
    grows = PEER_RG * PEER_NKEYS

    @pl.when(j == 0)
    def _():
        acc_ref[...] = jnp.zeros_like(acc_ref)

    st_scr[...] = jnp.dot(u_ref[...], x1t_ref[...], preferred_element_type=F32)

    nsb = PEER_NKEYS // PEER_SB
    tile_rows = pl.ds(pl.multiple_of(j * n1, SUBLANES), n1)
    for gi in range(n1 // PEER_RG):
        crows = slice(gi * grows, (gi + 1) * grows)
        for tc in range(tm // LANES):
            lanes = slice(tc * LANES, (tc + 1) * LANES)
            g = [[jnp.zeros((PEER_SB, LANES), BF16) for _ in range(nsb)] for _ in range(PEER_RG)]
            for h in range(PEER_HEADS):
                c1_t = c1_ref[h, tc, tile_rows, :]
                n1_t = n1_ref[h, tc, tile_rows, :]
                c1b = [_bcast_row_bf16(c1_t, gi * PEER_RG + r, PEER_SB) for r in range(PEER_RG)]
                n1b = [_bcast_row_bf16(n1_t, gi * PEER_RG + r, PEER_SB) for r in range(PEER_RG)]
                for sb in range(nsb):
                    rows = slice(sb * PEER_SB, (sb + 1) * PEER_SB)
                    r2c = r2_ref[h, rows, lanes]
                    e2c = e2_ref[h, rows, lanes]
                    for r in range(PEER_RG):
                        g[r][sb] = g[r][sb] + c1b[r] * jnp.where(r2c < n1b[r], e2c, jnp.zeros_like(e2c))
            for r in range(PEER_RG):
                for sb in range(nsb):
                    base = gi * grows + r * PEER_NKEYS + sb * PEER_SB
                    srows = slice(base, base + PEER_SB)
                    ht_scr[srows, lanes] = g[r][sb] * _gelu(st_scr[srows, lanes]).astype(BF16)
    acc_ref[...] += jnp.dot(vt_ref[...], ht_scr[...], preferred_element_type=F32)

    @pl.when(j == pl.num_programs(1) - 1)
    def _():
        y_ref[...] = jnp.transpose(acc_ref[...])


def _peer(x1t, u_tab, vt_tiles, c1, n1, e2, r2, tm=512):
    T = x1t.shape[1]
    nj, _, te = vt_tiles.shape
    once = pl.Buffered(1)
    row_spec = pl.BlockSpec((PEER_HEADS, tm // LANES, PEER_NKEYS, LANES), lambda i, j: (0, i, 0, 0),
                            pipeline_mode=once)
    col_spec = pl.BlockSpec((PEER_HEADS, PEER_NKEYS, tm), lambda i, j: (0, 0, i), pipeline_mode=once)
    return pl.pallas_call(
        functools.partial(_peer_kernel, tm=tm, te=te),
        grid=(T // tm, nj),
        in_specs=[
            pl.BlockSpec((D_MODEL, tm), lambda i, j: (0, i), pipeline_mode=once),
            pl.BlockSpec((te, D_MODEL), lambda i, j: (j, 0)),
            pl.BlockSpec((None, D_MODEL, te), lambda i, j: (j, 0, 0)),
            row_spec, row_spec, col_spec, col_spec,
        ],
        out_specs=pl.BlockSpec((tm, D_MODEL), lambda i, j: (i, 0)),
        out_shape=jax.ShapeDtypeStruct((T, D_MODEL), F32),
        scratch_shapes=[
            pltpu.VMEM((D_MODEL, tm), F32),
            pltpu.VMEM((te, tm), F32),
            pltpu.VMEM((te, tm), BF16),
        ],
        compiler_params=_params(("parallel", "arbitrary"), 56),
        name="peer",
    )(x1t, u_tab, vt_tiles, c1, n1, e2, r2)


def _final_kernel(x1_ref, y_ref, p_ref, wg_ref, wp_ref, lw_ref, lb_ref, o_ref):
    x1 = x1_ref[...]
    gate = jax.nn.sigmoid(jnp.dot(x1.astype(BF16), wg_ref[...], preferred_element_type=F32))
    emb = jnp.dot(p_ref[...].astype(BF16), wp_ref[...], preferred_element_type=F32)
    o_ref[...] = _layer_norm(ALPHA * x1 + y_ref[...] + gate * emb, lw_ref[...], lb_ref[...])


def _final(x1, y_ffn, p2, w_gate, w_proj, ln_w, ln_b, tm=256):
    T = x1.shape[0]
    return pl.pallas_call(
        _final_kernel,
        grid=(T // tm,),
        in_specs=[
            pl.BlockSpec((tm, D_MODEL), lambda i: (i, 0)),
            pl.BlockSpec((tm, D_MODEL), lambda i: (i, 0)),
            pl.BlockSpec((tm, PLE_DIM), lambda i: (i, 0)),
            pl.BlockSpec((D_MODEL, D_MODEL), lambda i: (0, 0)),
            pl.BlockSpec((PLE_DIM, D_MODEL), lambda i: (0, 0)),
            pl.BlockSpec((1, D_MODEL), lambda i: (0, 0)),
            pl.BlockSpec((1, D_MODEL), lambda i: (0, 0)),
        ],
        out_specs=pl.BlockSpec((tm, D_MODEL), lambda i: (i, 0)),
        out_shape=jax.ShapeDtypeStruct((T, D_MODEL), F32),
        compiler_params=_params(("parallel",), 48),
        name="final",
    )(x1, y_ffn, p2, w_gate, w_proj, ln_w, ln_b)


def _layer(x2, p2, B, S, w_in, gla_w_gate_up, gla_b_gate, gla_norm_w, pool_w, pool_scale, w_out,
           ln1_w, ln1_b, peer_w_query, peer_sub_keys, peer_u, peer_v, ple_w_gate, ple_w_proj, ln2_w, ln2_b):
    glr0 = COL_R
    w_main = jnp.concatenate([w_in[:, :glr0], w_in[:, glr0 + GLA_GATE_RANK:]], axis=1).astype(BF16)
    w_glr = jnp.pad(w_in[:, glr0:glr0 + GLA_GATE_RANK], ((0, 0), (0, LANES - GLA_GATE_RANK))).astype(BF16)
    proj, glr = _proj(x2, w_main, w_glr)

    y_pool = _pool(proj, pool_w.astype(BF16), pool_scale.reshape(1, POOL_WIDTH), S)

    wg = jnp.pad(gla_w_gate_up, ((0, LANES - GLA_GATE_RANK), (0, 0))).astype(BF16)
    y_gla = _gla(proj, glr, wg, gla_b_gate.reshape(1, GLA_KEY_WIDTH),
                 gla_norm_w.reshape(1, GLA_WIDTH), B, S)

    x1, x1t = _outproj(y_pool, y_gla, x2, w_out.astype(BF16),
                       ln1_w.reshape(1, D_MODEL), ln1_b.reshape(1, D_MODEL))

    keys = peer_sub_keys.reshape(2 * PEER_HEADS, PEER_NKEYS, PEER_HALF)
    keys_hi, keys_lo = _split_bf16(keys)
    c1, n1, e2, r2 = _query(x1t, peer_w_query.T.astype(BF16), keys_hi, keys_lo)
    n_exp = peer_v.shape[0]
    vt_tiles = peer_v.reshape(n_exp // PEER_TE, PEER_TE, D_MODEL).transpose(0, 2, 1).astype(BF16)
    y_ffn = _peer(x1t, peer_u.astype(BF16), vt_tiles, c1, n1, e2, r2)

    return _final(x1, y_ffn, p2, ple_w_gate.astype(BF16), ple_w_proj.astype(BF16),
                  ln2_w.reshape(1, D_MODEL), ln2_b.reshape(1, D_MODEL))


def kernel(x, p, w_in, gla_w_gate_up, gla_b_gate, gla_norm_w, pool_w, pool_scale, w_out, ln1_w, ln1_b,
           peer_w_query, peer_sub_keys, peer_u, peer_v, ple_w_gate, ple_w_proj, ln2_w, ln2_b):
    B, S, D = x.shape
    x2 = x.reshape(B * S, D)
    for i in range(w_in.shape[0]):
        x2 = _layer(x2, p[i].reshape(B * S, PLE_DIM), B, S, w_in[i], gla_w_gate_up[i], gla_b_gate[i],
                    gla_norm_w[i], pool_w[i], pool_scale[i], w_out[i], ln1_w[i], ln1_b[i],
                    peer_w_query[i], peer_sub_keys[i], peer_u[i], peer_v[i], ple_w_gate[i],
                    ple_w_proj[i], ln2_w[i], ln2_b[i])
    return x2.reshape(B, S, D)
```

```python
import functools
import math

import jax
import jax.numpy as jnp
from jax import lax
from jax.experimental import pallas as pl
from jax.experimental.pallas import tpu as pltpu

F32 = jnp.float32
BF16 = jnp.bfloat16

D_MODEL = 2048
PLE_DIM = 256
POOL_WIDTH = 1024
POOL_WINDOWS = (2, 4, 8, 16)
POOL_GC = 256
POOL_HALO = 16
GLA_WIDTH = 1024
GLA_HEADS = 4
GLA_DV = 256
GLA_DK = 128
GLA_KEY_WIDTH = 512
GLA_GATE_RANK = 16
GLA_GATE_TEMP = 16.0
GLA_CHUNK = 64
PEER_HEADS = 8
PEER_NKEYS = 128
PEER_HALF = 128
PEER_TOPK = 16
DEPTH = 1
ALPHA = float((2 * DEPTH) ** 0.25)
LN_EPS = 1e-5
RMS_EPS = 1e-6
LANES = 128
NEG_INF = float("-inf")

COL_Q = POOL_WIDTH
COL_K = COL_Q + GLA_KEY_WIDTH
COL_V = COL_K + GLA_KEY_WIDTH
COL_R = COL_V + GLA_WIDTH
PROJ_COLS = COL_R + GLA_WIDTH


def _params(sem, vmem_mib):
    return pltpu.CompilerParams(dimension_semantics=sem, vmem_limit_bytes=vmem_mib * 1024 * 1024)


def _proj_kernel(x_ref, w_ref, wg_ref, o_ref, glr_ref, xb_ref):
    @pl.when(pl.program_id(1) == 0)
    def _():
        xb = x_ref[...].astype(BF16)
        xb_ref[...] = xb
        glr_ref[...] = jnp.dot(xb, wg_ref[...], preferred_element_type=F32)

    o_ref[...] = jnp.dot(xb_ref[...], w_ref[...], preferred_element_type=F32).astype(o_ref.dtype)


def _proj(x2, w_main, w_glr, tm=512, tn=2048):
    T = x2.shape[0]
    return pl.pallas_call(
        _proj_kernel,
        grid=(T // tm, PROJ_COLS // tn),
        in_specs=[
            pl.BlockSpec((tm, D_MODEL), lambda i, n: (i, 0)),
            pl.BlockSpec((D_MODEL, tn), lambda i, n: (0, n)),
            pl.BlockSpec((D_MODEL, LANES), lambda i, n: (0, 0)),
        ],
        out_specs=[
            pl.BlockSpec((tm, tn), lambda i, n: (i, n)),
            pl.BlockSpec((tm, LANES), lambda i, n: (i, 0)),
        ],
        out_shape=[
            jax.ShapeDtypeStruct((T, PROJ_COLS), BF16),
            jax.ShapeDtypeStruct((T, LANES), F32),
        ],
        scratch_shapes=[pltpu.VMEM((tm, D_MODEL), BF16)],
        compiler_params=_params(("parallel", "arbitrary"), 40),
        name="proj",
    )(x2, w_main, w_glr)


def _pool_kernel(u_ref, halo_ref, w_ref, sc_ref, o_ref, ext_ref, *, tiles_per_seq, tm):
    t = pl.program_id(0) % tiles_per_seq
    halo = jnp.where(t == 0, 0.0, halo_ref[...].astype(F32))
    ext_ref[0:POOL_HALO, :] = halo
    ext_ref[POOL_HALO:, :] = u_ref[...].astype(F32)
    pos = t * tm + lax.broadcasted_iota(jnp.int32, (tm, 1), 0)
    for g, w in enumerate(POOL_WINDOWS):
        cols = slice(g * POOL_GC, (g + 1) * POOL_GC)
        u = ext_ref[POOL_HALO:, cols]
        acc = u
        for j in range(1, w):
            acc = acc + ext_ref[POOL_HALO - j:POOL_HALO - j + tm, cols]
        cnt = jnp.minimum(pos + 1, w).astype(F32)
        d = acc / cnt - u
        y = jnp.dot(d.astype(BF16), w_ref[g], preferred_element_type=F32)
        o_ref[:, cols] = (y * sc_ref[:, cols]).astype(o_ref.dtype)


def _pool(proj, pool_w, pool_scale, S, tm=512):
    T = proj.shape[0]
    hb = tm // POOL_HALO
    return pl.pallas_call(
        functools.partial(_pool_kernel, tiles_per_seq=S // tm, tm=tm),
        grid=(T // tm,),
        in_specs=[
            pl.BlockSpec((tm, POOL_WIDTH), lambda i: (i, 0)),
            pl.BlockSpec((POOL_HALO, POOL_WIDTH), lambda i: (jnp.maximum(i * hb - 1, 0), 0)),
            pl.BlockSpec((len(POOL_WINDOWS), POOL_GC, POOL_GC), lambda i: (0, 0, 0)),
            pl.BlockSpec((1, POOL_WIDTH), lambda i: (0, 0)),
        ],
        out_specs=pl.BlockSpec((tm, POOL_WIDTH), lambda i: (i, 0)),
        out_shape=jax.ShapeDtypeStruct((T, POOL_WIDTH), BF16),
        scratch_shapes=[pltpu.VMEM((POOL_HALO + tm, POOL_WIDTH), F32)],
        compiler_params=_params(("parallel",), 32),
        name="pool",
    )(proj, proj, pool_w, pool_scale)


def _gla_kernel(q_ref, k_ref, v_ref, r_ref, glr_ref, wg_ref, bg_ref, nw_ref, o_ref, s_ref, *, n_chunks):
    @pl.when(pl.program_id(1) == 0)
    def _():
        s_ref[...] = jnp.zeros_like(s_ref)

    C = GLA_CHUNK
    row = lax.broadcasted_iota(jnp.int32, (C, C), 0)
    col = lax.broadcasted_iota(jnp.int32, (C, C), 1)
    causal = col <= row
    tril = causal.astype(BF16)
    wg = wg_ref[...]
    bg = bg_ref[...]
    nt = (((1,), (1,)), ((), ()))
    for c in range(n_chunks):
        rows = slice(c * C, (c + 1) * C)
        z = jnp.dot(glr_ref[rows, :].astype(BF16), wg, preferred_element_type=F32) + bg
        g = jax.nn.log_sigmoid(z) / GLA_GATE_TEMP
        g_hi, g_lo = _split_bf16(g)
        b_all = (jnp.dot(tril, g_hi, preferred_element_type=F32)
                 + jnp.dot(tril, g_lo, preferred_element_type=F32))
        for h in range(GLA_HEADS):
            kc = slice(h * GLA_DK, (h + 1) * GLA_DK)
            vc = slice(h * GLA_DV, (h + 1) * GLA_DV)
            b = b_all[:, kc]
            b_last = b[C - 1:C, :]
            b_mid = b[C // 2 - 1:C // 2, :]
            q = q_ref[rows, kc].astype(F32) * (GLA_DK ** -0.5)
            k = k_ref[rows, kc].astype(F32)
            v = v_ref[rows, vc]
            q_state = (q * jnp.exp(b)).astype(BF16)
            q_in = (q * jnp.exp(b - b_mid)).astype(BF16)
            k_in = (k * jnp.exp(b_mid - b)).astype(BF16)
            k_out = k * jnp.exp(b_last - b)
            attn = lax.dot_general(q_in, k_in, nt, preferred_element_type=F32)
            attn = jnp.where(causal, attn, 0.0).astype(BF16)
            s = s_ref[h]
            o = (jnp.dot(attn, v, preferred_element_type=F32)
                 + jnp.dot(q_state, s.astype(BF16), preferred_element_type=F32))
            decay = jnp.transpose(jnp.broadcast_to(jnp.exp(b_last), (C, GLA_DK)))[:, 0:1]
            s_ref[h] = decay * s + jnp.dot(jnp.transpose(k_out).astype(BF16), v, preferred_element_type=F32)
            o = o * lax.rsqrt(jnp.mean(jnp.square(o), axis=-1, keepdims=True) + RMS_EPS)
            o = o * nw_ref[:, vc]
            r = r_ref[rows, vc].astype(F32)
            o_ref[rows, vc] = (o * (r * jax.nn.sigmoid(r))).astype(o_ref.dtype)


def _gla(proj, glr, wg, bg, nw, B, S, L=256):
    T = proj.shape[0]
    nl = S // L
    rb = lambda b, l: b * nl + l
    return pl.pallas_call(
        functools.partial(_gla_kernel, n_chunks=L // GLA_CHUNK),
        grid=(B, nl),
        in_specs=[
            pl.BlockSpec((L, GLA_KEY_WIDTH), lambda b, l: (rb(b, l), COL_Q // GLA_KEY_WIDTH)),
            pl.BlockSpec((L, GLA_KEY_WIDTH), lambda b, l: (rb(b, l), COL_K // GLA_KEY_WIDTH)),
            pl.BlockSpec((L, GLA_WIDTH), lambda b, l: (rb(b, l), COL_V // GLA_WIDTH)),
            pl.BlockSpec((L, GLA_WIDTH), lambda b, l: (rb(b, l), COL_R // GLA_WIDTH)),
            pl.BlockSpec((L, LANES), lambda b, l: (rb(b, l), 0)),
            pl.BlockSpec((LANES, GLA_KEY_WIDTH), lambda b, l: (0, 0)),
            pl.BlockSpec((1, GLA_KEY_WIDTH), lambda b, l: (0, 0)),
            pl.BlockSpec((1, GLA_WIDTH), lambda b, l: (0, 0)),
        ],
        out_specs=pl.BlockSpec((L, GLA_WIDTH), lambda b, l: (rb(b, l), 0)),
        out_shape=jax.ShapeDtypeStruct((T, GLA_WIDTH), BF16),
        scratch_shapes=[pltpu.VMEM((GLA_HEADS, GLA_DK, GLA_DV), F32)],
        compiler_params=_params(("parallel", "arbitrary"), 32),
        name="gla",
    )(proj, proj, proj, proj, glr, wg, bg, nw)


def _layer_norm(h, w, b):
    mu = jnp.mean(h, axis=-1, keepdims=True)
    hc = h - mu
    var = jnp.mean(jnp.square(hc), axis=-1, keepdims=True)
    return hc * lax.rsqrt(var + LN_EPS) * w + b


def _outproj_kernel(yp_ref, yg_ref, x_ref, w_ref, lw_ref, lb_ref, x1_ref, x1t_ref):
    mix = (jnp.dot(yp_ref[...], w_ref[0:POOL_WIDTH, :], preferred_element_type=F32)
           + jnp.dot(yg_ref[...], w_ref[POOL_WIDTH:, :], preferred_element_type=F32))
    x1 = _layer_norm(ALPHA * x_ref[...] + mix, lw_ref[...], lb_ref[...])
    x1_ref[...] = x1
    x1t_ref[...] = jnp.transpose(x1).astype(BF16)


def _outproj(y_pool, y_gla, x2, w_out, ln_w, ln_b, tm=256):
    T = x2.shape[0]
    return pl.pallas_call(
        _outproj_kernel,
        grid=(T // tm,),
        in_specs=[
            pl.BlockSpec((tm, POOL_WIDTH), lambda i: (i, 0)),
            pl.BlockSpec((tm, GLA_WIDTH), lambda i: (i, 0)),
            pl.BlockSpec((tm, D_MODEL), lambda i: (i, 0)),
            pl.BlockSpec((D_MODEL, D_MODEL), lambda i: (0, 0)),
            pl.BlockSpec((1, D_MODEL), lambda i: (0, 0)),
            pl.BlockSpec((1, D_MODEL), lambda i: (0, 0)),
        ],
        out_specs=[
            pl.BlockSpec((tm, D_MODEL), lambda i: (i, 0)),
            pl.BlockSpec((D_MODEL, tm), lambda i: (0, i)),
        ],
        out_shape=[
            jax.ShapeDtypeStruct((T, D_MODEL), F32),
            jax.ShapeDtypeStruct((D_MODEL, T), BF16),
        ],
        compiler_params=_params(("parallel",), 48),
        name="outproj",
    )(y_pool, y_gla, x2, w_out, ln_w, ln_b)


N_SORT = PEER_TOPK + 1


def _sort_network(n):
    pairs = []

    def merge(lo, m, r):
        step = 2 * r
        if step < m:
            merge(lo, m, step)
            merge(lo + r, m, step)
            pairs.extend((i, i + r) for i in range(lo + r, lo + m - r, step))
        else:
            pairs.append((lo, lo + r))

    def sort(lo, m):
        if m > 1:
            sort(lo, m // 2)
            sort(lo + m // 2, m // 2)
            merge(lo, m, 1)

    sort(0, n)
    return tuple(pairs)


def _pop_sorted(v, n_out):
    nv = len(v)
    width = 1 << (nv - 1).bit_length()
    for i, j in _sort_network(width):
        if j < nv:
            v[i], v[j] = jnp.maximum(v[i], v[j]), jnp.minimum(v[i], v[j])
    tops = []
    for kk in range(n_out):
        m = jnp.max(v[0], axis=0, keepdims=True)
        tops.append(m)
        hit = v[0] == m
        for k in range(min(n_out - 1 - kk, nv)):
            v[k] = jnp.where(hit, v[k + 1] if k + 1 < nv else NEG_INF, v[k])
    return tops


def _sorted_top(arr):
    return _pop_sorted([arr[k:k + SUBLANES] for k in range(0, arr.shape[0], SUBLANES)], N_SORT)


def _rows_to_tile(rows):
    rid = lax.broadcasted_iota(jnp.int32, (SUBLANES, LANES), 0)
    tile = jnp.full((SUBLANES, LANES), NEG_INF, F32)
    for k, r in enumerate(rows):
        tile = jnp.where(rid == k, r, tile)
    return tile


def _pair_stats(a, b):
    r8 = lax.broadcasted_iota(jnp.int32, (SUBLANES, LANES), 0)
    b_lo, b_hi, a_hi = _rows_to_tile(b[0:8]), _rows_to_tile(b[8:16]), _rows_to_tile(a[8:16])
    p2 = jnp.where(r8 < 5, a[2] + b_lo, jnp.where(r8 == 5, a[16] + b[0], jnp.where(r8 == 6, a[0] + b[16], NEG_INF)))
    pieces = [
        a[0] + b_lo, a[0] + b_hi, a[1] + b_lo, p2,
        jnp.where(r8 < 4, a[3] + b_lo, NEG_INF),
        jnp.where(r8 < 3, a[4] + b_lo, NEG_INF),
        jnp.where(r8 < 2, a[5] + b_lo, NEG_INF),
        jnp.where(r8 < 2, a[6] + b_lo, NEG_INF),
        jnp.where(r8 < 2, a[7] + b_lo, NEG_INF),
        a_hi + b[0],
    ]
    sums = _pop_sorted(pieces, N_SORT)
    top16 = jnp.concatenate([_rows_to_tile(sums[0:8]), _rows_to_tile(sums[8:16])], axis=0)
    z = jnp.sum(jnp.exp(top16 - sums[0]), axis=0, keepdims=True)
    return 0.5 * (sums[PEER_TOPK - 1] + sums[PEER_TOPK]), 1.0 / z


def _split_bf16(x):
    hi = x.astype(BF16)
    return hi, (x - hi.astype(F32)).astype(BF16)


def _query_kernel(x1t_ref, wq_ref, khi_ref, klo_ref, c1_ref, n1_ref, e2_ref, r2_ref, q_ref, s1_scr, s2_scr, *, tm):
    q_ref[...] = jnp.dot(wq_ref[...], x1t_ref[...], preferred_element_type=F32)
    for h in range(PEER_HEADS):
        for p in range(2):
            hp = 2 * h + p
            q_hi, q_lo = _split_bf16(q_ref[hp * PEER_HALF:(hp + 1) * PEER_HALF, :])
            k_hi, k_lo = khi_ref[hp], klo_ref[hp]
            sc = (jnp.dot(k_hi, q_hi, preferred_element_type=F32)
                  + jnp.dot(k_hi, q_lo, preferred_element_type=F32)
                  + jnp.dot(k_lo, q_hi, preferred_element_type=F32))
            if p == 0:
                s1_scr[...] = sc
            else:
                s2_scr[...] = sc
        for tc in range(tm // LANES):
            lanes = slice(tc * LANES, (tc + 1) * LANES)
            s1 = s1_scr[:, lanes]
            s2 = s2_scr[:, lanes]
            ta, tb = _sorted_top(s1), _sorted_top(s2)
            tau, rz = _pair_stats(ta, tb)
            n1 = jnp.zeros_like(s1)
            r2 = jnp.zeros_like(s2)
            for jj in range(PEER_TOPK):
                n1 = jnp.where(s1 >= tau - tb[jj], float(jj + 1), n1)
                r2 = jnp.where(s2 < tb[jj], float(jj + 1), r2)
            c1_ref[h, tc] = jnp.exp(s1 - ta[0])
            n1_ref[h, tc] = n1
            e2_ref[h, :, lanes] = (jnp.exp(s2 - tb[0]) * rz).astype(BF16)
            r2_ref[h, :, lanes] = r2.astype(BF16)


def _query(x1t, wq_t, keys_hi, keys_lo, tm=256):
    T = x1t.shape[1]
    nc = tm // LANES
    row_spec = pl.BlockSpec((PEER_HEADS, nc, PEER_NKEYS, LANES), lambda i: (0, i, 0, 0))
    col_spec = pl.BlockSpec((PEER_HEADS, PEER_NKEYS, tm), lambda i: (0, 0, i))
    key_spec = pl.BlockSpec((2 * PEER_HEADS, PEER_NKEYS, PEER_HALF), lambda i: (0, 0, 0))
    row_shape = jax.ShapeDtypeStruct((PEER_HEADS, T // LANES, PEER_NKEYS, LANES), F32)
    col_shape = jax.ShapeDtypeStruct((PEER_HEADS, PEER_NKEYS, T), BF16)
    return pl.pallas_call(
        functools.partial(_query_kernel, tm=tm),
        grid=(T // tm,),
        in_specs=[
            pl.BlockSpec((D_MODEL, tm), lambda i: (0, i)),
            pl.BlockSpec((D_MODEL, D_MODEL), lambda i: (0, 0)),
            key_spec, key_spec,
        ],
        out_specs=[row_spec, row_spec, col_spec, col_spec],
        out_shape=[row_shape, row_shape, col_shape, col_shape],
        scratch_shapes=[pltpu.VMEM((D_MODEL, tm), F32), pltpu.VMEM((PEER_NKEYS, tm), F32),
                        pltpu.VMEM((PEER_NKEYS, tm), F32)],
        compiler_params=_params(("parallel",), 48),
        name="query",
    )(x1t, wq_t, keys_hi, keys_lo)


PEER_SB = 64
PEER_RG = 2
SUBLANES = 8
PEER_TE = 1024


def _gelu(x):
    return 0.5 * x * (1.0 + lax.erf(x * (1.0 / math.sqrt(2.0))))


def _bcast_row_bf16(tile, ri, rows):
    packed = jnp.broadcast_to(tile[ri:ri + 1, :], (2 * SUBLANES, LANES)).astype(BF16)
    return jnp.concatenate([packed] * (rows // (2 * SUBLANES)), axis=0)


def _peer_kernel(x1t_ref, u_ref, vt_ref, c1_ref, n1_ref, e2_ref, r2_ref, y_ref,
                 acc_ref, st_scr, ht_scr, *, tm, te):
    j = pl.program_id(1)
    n1 = te // PEER_NKEYS
    grows = PEER_RG * PEER_NKEYS

    @pl.when(j == 0)
    def _():
        acc_ref[...] = jnp.zeros_like(acc_ref)

    st_scr[...] = jnp.dot(u_ref[...], x1t_ref[...], preferred_element_type=F32)

    nsb = PEER_NKEYS // PEER_SB
    tile_rows = pl.ds(pl.multiple_of(j * n1, SUBLANES), n1)
    for gi in range(n1 // PEER_RG):
        crows = slice(gi * grows, (gi + 1) * grows)
        for tc in range(tm // LANES):
            lanes = slice(tc * LANES, (tc + 1) * LANES)
            g = [[jnp.zeros((PEER_SB, LANES), BF16) for _ in range(nsb)] for _ in range(PEER_RG)]
            for h in range(PEER_HEADS):
                c1_t = c1_ref[h, tc, tile_rows, :]
                n1_t = n1_ref[h, tc, tile_rows, :]
                c1b = [_bcast_row_bf16(c1_t, gi * PEER_RG + r, PEER_SB) for r in range(PEER_RG)]
                n1b = [_bcast_row_bf16(n1_t, gi * PEER_RG + r, PEER_SB) for r in range(PEER_RG)]
                for sb in range(nsb):
                    rows = slice(sb * PEER_SB, (sb + 1) * PEER_SB)
                    r2c = r2_ref[h, rows, lanes]
                    e2c = e2_ref[h, rows, lanes]
                    for r in range(PEER_RG):
                        g[r][sb] = g[r][sb] + c1b[r] * jnp.where(r2c < n1b[r], e2c, jnp.zeros_like(e2c))
            for r in range(PEER_RG):
                for sb in range(nsb):
                    base = gi * grows + r * PEER_NKEYS + sb * PEER_SB
                    srows = slice(base, base + PEER_SB)
                    ht_scr[srows, lanes] = g[r][sb] * _gelu(st_scr[srows, lanes].astype(BF16))
    acc_ref[...] += jnp.dot(vt_ref[...], ht_scr[...], preferred_element_type=F32)

    @pl.when(j == pl.num_programs(1) - 1)
    def _():
        y_ref[...] = jnp.transpose(acc_ref[...])


def _peer(x1t, u_tab, vt_tiles, c1, n1, e2, r2, tm=512):
    T = x1t.shape[1]
    nj, _, te = vt_tiles.shape
    once = pl.Buffered(1)
    row_spec = pl.BlockSpec((PEER_HEADS, tm // LANES, PEER_NKEYS, LANES), lambda i, j: (0, i, 0, 0),
                            pipeline_mode=once)
    col_spec = pl.BlockSpec((PEER_HEADS, PEER_NKEYS, tm), lambda i, j: (0, 0, i), pipeline_mode=once)
    return pl.pallas_call(
        functools.partial(_peer_kernel, tm=tm, te=te),
        grid=(T // tm, nj),
        in_specs=[
            pl.BlockSpec((D_MODEL, tm), lambda i, j: (0, i), pipeline_mode=once),
            pl.BlockSpec((te, D_MODEL), lambda i, j: (j, 0)),
            pl.BlockSpec((None, D_MODEL, te), lambda i, j: (j, 0, 0)),
            row_spec, row_spec, col_spec, col_spec,
        ],
        out_specs=pl.BlockSpec((tm, D_MODEL), lambda i, j: (i, 0)),
        out_shape=jax.ShapeDtypeStruct((T, D_MODEL), F32),
        scratch_shapes=[
            pltpu.VMEM((D_MODEL, tm), F32),
            pltpu.VMEM((te, tm), F32),
            pltpu.VMEM((te, tm), BF16),
        ],
        compiler_params=_params(("parallel", "arbitrary"), 56),
        name="peer",
    )(x1t, u_tab, vt_tiles, c1, n1, e2, r2)


def _final_kernel(x1_ref, y_ref, p_ref, wg_ref, wp_ref, lw_ref, lb_ref, o_ref):
    x1 = x1_ref[...]
    gate = jax.nn.sigmoid(jnp.dot(x1.astype(BF16), wg_ref[...], preferred_element_type=F32))
    emb = jnp.dot(p_ref[...].astype(BF16), wp_ref[...], preferred_element_type=F32)
    o_ref[...] = _layer_norm(ALPHA * x1 + y_ref[...] + gate * emb, lw_ref[...], lb_ref[...])


def _final(x1, y_ffn, p2, w_gate, w_proj, ln_w, ln_b, tm=256):
    T = x1.shape[0]
    return pl.pallas_call(
        _final_kernel,
        grid=(T // tm,),
        in_specs=[
            pl.BlockSpec((tm, D_MODEL), lambda i: (i, 0)),
            pl.BlockSpec((tm, D_MODEL), lambda i: (i, 0)),
            pl.BlockSpec((tm, PLE_DIM), lambda i: (i, 0)),
            pl.BlockSpec((D_MODEL, D_MODEL), lambda i: (0, 0)),
            pl.BlockSpec((PLE_DIM, D_MODEL), lambda i: (0, 0)),
            pl.BlockSpec((1, D_MODEL), lambda i: (0, 0)),
            pl.BlockSpec((1, D_MODEL), lambda i: (0, 0)),
        ],
        out_specs=pl.BlockSpec((tm, D_MODEL), lambda i: (i, 0)),
        out_shape=jax.ShapeDtypeStruct((T, D_MODEL), F32),
        compiler_params=_params(("parallel",), 48),
        name="final",
    )(x1, y_ffn, p2, w_gate, w_proj, ln_w, ln_b)


def _layer(x2, p2, B, S, w_in, gla_w_gate_up, gla_b_gate, gla_norm_w, pool_w, pool_scale, w_out,
           ln1_w, ln1_b, peer_w_query, peer_sub_keys, peer_u, peer_v, ple_w_gate, ple_w_proj, ln2_w, ln2_b):
    glr0 = COL_R
    w_main = jnp.concatenate([w_in[:, :glr0], w_in[:, glr0 + GLA_GATE_RANK:]], axis=1).astype(BF16)
    w_glr = jnp.pad(w_in[:, glr0:glr0 + GLA_GATE_RANK], ((0, 0), (0, LANES - GLA_GATE_RANK))).astype(BF16)
    proj, glr = _proj(x2, w_main, w_glr)

    y_pool = _pool(proj, pool_w.astype(BF16), pool_scale.reshape(1, POOL_WIDTH), S)

    wg = jnp.pad(gla_w_gate_up, ((0, LANES - GLA_GATE_RANK), (0, 0))).astype(BF16)
    y_gla = _gla(proj, glr, wg, gla_b_gate.reshape(1, GLA_KEY_WIDTH),
                 gla_norm_w.reshape(1, GLA_WIDTH), B, S)

    x1, x1t = _outproj(y_pool, y_gla, x2, w_out.astype(BF16),
                       ln1_w.reshape(1, D_MODEL), ln1_b.reshape(1, D_MODEL))

    keys = peer_sub_keys.reshape(2 * PEER_HEADS, PEER_NKEYS, PEER_HALF)
    keys_hi, keys_lo = _split_bf16(keys)
    c1, n1, e2, r2 = _query(x1t, peer_w_query.T.astype(BF16), keys_hi, keys_lo)
    n_exp = peer_v.shape[0]
    vt_tiles = peer_v.reshape(n_exp // PEER_TE, PEER_TE, D_MODEL).transpose(0, 2, 1).astype(BF16)
    y_ffn = _peer(x1t, peer_u.astype(BF16), vt_tiles, c1, n1, e2, r2)

    return _final(x1, y_ffn, p2, ple_w_gate.astype(BF16), ple_w_proj.astype(BF16),
                  ln2_w.reshape(1, D_MODEL), ln2_b.reshape(1, D_MODEL))


def kernel(x, p, w_in, gla_w_gate_up, gla_b_gate, gla_norm_w, pool_w, pool_scale, w_out, ln1_w, ln1_b,
           peer_w_query, peer_sub_keys, peer_u, peer_v, ple_w_gate, ple_w_proj, ln2_w, ln2_b):
    B, S, D = x.shape
    x2 = x.reshape(B * S, D)
    for i in range(w_in.shape[0]):
        x2 = _layer(x2, p[i].reshape(B * S, PLE_DIM), B, S, w_in[i], gla_w_gate_up[i], gla_b_gate[i],
                    gla_norm_w[i], pool_w[i], pool_scale[i], w_out[i], ln1_w[i], ln1_b[i],
                    peer_w_query[i], peer_sub_keys[i], peer_u[i], peer_v[i], ple_w_gate[i],
                    ple_w_proj[i], ln2_w[i], ln2_b[i])
    return x2.reshape(B, S, D)
```

```python
import functools
import math

import jax
import jax.numpy as jnp
from jax import lax
from jax.experimental import pallas as pl
from jax.experimental.pallas import tpu as pltpu

F32 = jnp.float32
BF16 = jnp.bfloat16

D_MODEL = 2048
PLE_DIM = 256
POOL_WIDTH = 1024
POOL_WINDOWS = (2, 4, 8, 16)
POOL_GC = 256
POOL_HALO = 16
GLA_WIDTH = 1024
GLA_HEADS = 4
GLA_DV = 256
GLA_DK = 128
GLA_KEY_WIDTH = 512
GLA_GATE_RANK = 16
GLA_GATE_TEMP = 16.0
GLA_CHUNK = 64
PEER_HEADS = 8
PEER_NKEYS = 128
PEER_HALF = 128
PEER_TOPK = 16
DEPTH = 1
ALPHA = float((2 * DEPTH) ** 0.25)
LN_EPS = 1e-5
RMS_EPS = 1e-6
LANES = 128
NEG_INF = float("-inf")

COL_Q = POOL_WIDTH
COL_K = COL_Q + GLA_KEY_WIDTH
COL_V = COL_K + GLA_KEY_WIDTH
COL_R = COL_V + GLA_WIDTH
PROJ_COLS = COL_R + GLA_WIDTH


def _params(sem, vmem_mib):
    return pltpu.CompilerParams(dimension_semantics=sem, vmem_limit_bytes=vmem_mib * 1024 * 1024)


def _proj_kernel(x_ref, w_ref, wg_ref, o_ref, glr_ref, xb_ref):
    @pl.when(pl.program_id(1) == 0)
    def _():
        xb = x_ref[...].astype(BF16)
        xb_ref[...] = xb
        glr_ref[...] = jnp.dot(xb, wg_ref[...], preferred_element_type=F32)

    o_ref[...] = jnp.dot(xb_ref[...], w_ref[...], preferred_element_type=F32).astype(o_ref.dtype)


def _proj(x2, w_main, w_glr, tm=1024, tn=2048):
    T = x2.shape[0]
    return pl.pallas_call(
        _proj_kernel,
        grid=(T // tm, PROJ_COLS // tn),
        in_specs=[
            pl.BlockSpec((tm, D_MODEL), lambda i, n: (i, 0)),
            pl.BlockSpec((D_MODEL, tn), lambda i, n: (0, n)),
            pl.BlockSpec((D_MODEL, LANES), lambda i, n: (0, 0)),
        ],
        out_specs=[
            pl.BlockSpec((tm, tn), lambda i, n: (i, n)),
            pl.BlockSpec((tm, LANES), lambda i, n: (i, 0)),
        ],
        out_shape=[
            jax.ShapeDtypeStruct((T, PROJ_COLS), BF16),
            jax.ShapeDtypeStruct((T, LANES), F32),
        ],
        scratch_shapes=[pltpu.VMEM((tm, D_MODEL), BF16)],
        compiler_params=_params(("parallel", "arbitrary"), 52),
        name="proj",
    )(x2, w_main, w_glr)


def _pool_kernel(u_ref, halo_ref, w_ref, sc_ref, o_ref, ext_ref, *, tiles_per_seq, tm):
    t = pl.program_id(0) % tiles_per_seq
    halo = jnp.where(t == 0, 0.0, halo_ref[...].astype(F32))
    ext_ref[0:POOL_HALO, :] = halo
    ext_ref[POOL_HALO:, :] = u_ref[...].astype(F32)
    pos = t * tm + lax.broadcasted_iota(jnp.int32, (tm, 1), 0)
    for g, w in enumerate(POOL_WINDOWS):
        cols = slice(g * POOL_GC, (g + 1) * POOL_GC)
        u = ext_ref[POOL_HALO:, cols]
        acc = u
        for j in range(1, w):
            acc = acc + ext_ref[POOL_HALO - j:POOL_HALO - j + tm, cols]
        cnt = jnp.minimum(pos + 1, w).astype(F32)
        d = acc / cnt - u
        y = jnp.dot(d.astype(BF16), w_ref[g], preferred_element_type=F32)
        o_ref[:, cols] = (y * sc_ref[:, cols]).astype(o_ref.dtype)


def _pool(proj, pool_w, pool_scale, S, tm=512):
    T = proj.shape[0]
    hb = tm // POOL_HALO
    return pl.pallas_call(
        functools.partial(_pool_kernel, tiles_per_seq=S // tm, tm=tm),
        grid=(T // tm,),
        in_specs=[
            pl.BlockSpec((tm, POOL_WIDTH), lambda i: (i, 0)),
            pl.BlockSpec((POOL_HALO, POOL_WIDTH), lambda i: (jnp.maximum(i * hb - 1, 0), 0)),
            pl.BlockSpec((len(POOL_WINDOWS), POOL_GC, POOL_GC), lambda i: (0, 0, 0)),
            pl.BlockSpec((1, POOL_WIDTH), lambda i: (0, 0)),
        ],
        out_specs=pl.BlockSpec((tm, POOL_WIDTH), lambda i: (i, 0)),
        out_shape=jax.ShapeDtypeStruct((T, POOL_WIDTH), BF16),
        scratch_shapes=[pltpu.VMEM((POOL_HALO + tm, POOL_WIDTH), F32)],
        compiler_params=_params(("parallel",), 32),
        name="pool",
    )(proj, proj, pool_w, pool_scale)


def _gla_kernel(q_ref, k_ref, v_ref, r_ref, glr_ref, wg_ref, bg_ref, nw_ref, o_ref, s_ref, *, n_chunks):
    @pl.when(pl.program_id(1) == 0)
    def _():
        s_ref[...] = jnp.zeros_like(s_ref)

    C = GLA_CHUNK
    row = lax.broadcasted_iota(jnp.int32, (C, C), 0)
    col = lax.broadcasted_iota(jnp.int32, (C, C), 1)
    causal = col <= row
    tril = causal.astype(BF16)
    wg = wg_ref[...]
    bg = bg_ref[...]
    nt = (((1,), (1,)), ((), ()))
    for c in range(n_chunks):
        rows = slice(c * C, (c + 1) * C)
        z = jnp.dot(glr_ref[rows, :].astype(BF16), wg, preferred_element_type=F32) + bg
        g = jax.nn.log_sigmoid(z) / GLA_GATE_TEMP
        g_hi, g_lo = _split_bf16(g)
        b_all = (jnp.dot(tril, g_hi, preferred_element_type=F32)
                 + jnp.dot(tril, g_lo, preferred_element_type=F32))
        for h in range(GLA_HEADS):
            kc = slice(h * GLA_DK, (h + 1) * GLA_DK)
            vc = slice(h * GLA_DV, (h + 1) * GLA_DV)
            b = b_all[:, kc]
            b_last = b[C - 1:C, :]
            b_mid = b[C // 2 - 1:C // 2, :]
            q = q_ref[rows, kc].astype(F32) * (GLA_DK ** -0.5)
            k = k_ref[rows, kc].astype(F32)
            v = v_ref[rows, vc]
            q_state = (q * jnp.exp(b)).astype(BF16)
            q_in = (q * jnp.exp(b - b_mid)).astype(BF16)
            k_in = (k * jnp.exp(b_mid - b)).astype(BF16)
            k_out = k * jnp.exp(b_last - b)
            attn = lax.dot_general(q_in, k_in, nt, preferred_element_type=F32)
            attn = jnp.where(causal, attn, 0.0).astype(BF16)
            s = s_ref[h]
            o = (jnp.dot(attn, v, preferred_element_type=F32)
                 + jnp.dot(q_state, s.astype(BF16), preferred_element_type=F32))
            decay = jnp.transpose(jnp.broadcast_to(jnp.exp(b_last), (C, GLA_DK)))[:, 0:1]
            s_ref[h] = decay * s + jnp.dot(jnp.transpose(k_out).astype(BF16), v, preferred_element_type=F32)
            o = o * lax.rsqrt(jnp.mean(jnp.square(o), axis=-1, keepdims=True) + RMS_EPS)
            o = o * nw_ref[:, vc]
            r = r_ref[rows, vc].astype(F32)
            o_ref[rows, vc] = (o * (r * jax.nn.sigmoid(r))).astype(o_ref.dtype)


def _gla(proj, glr, wg, bg, nw, B, S, L=256):
    T = proj.shape[0]
    nl = S // L
    rb = lambda b, l: b * nl + l
    return pl.pallas_call(
        functools.partial(_gla_kernel, n_chunks=L // GLA_CHUNK),
        grid=(B, nl),
        in_specs=[
            pl.BlockSpec((L, GLA_KEY_WIDTH), lambda b, l: (rb(b, l), COL_Q // GLA_KEY_WIDTH)),
            pl.BlockSpec((L, GLA_KEY_WIDTH), lambda b, l: (rb(b, l), COL_K // GLA_KEY_WIDTH)),
            pl.BlockSpec((L, GLA_WIDTH), lambda b, l: (rb(b, l), COL_V // GLA_WIDTH)),
            pl.BlockSpec((L, GLA_WIDTH), lambda b, l: (rb(b, l), COL_R // GLA_WIDTH)),
            pl.BlockSpec((L, LANES), lambda b, l: (rb(b, l), 0)),
            pl.BlockSpec((LANES, GLA_KEY_WIDTH), lambda b, l: (0, 0)),
            pl.BlockSpec((1, GLA_KEY_WIDTH), lambda b, l: (0, 0)),
            pl.BlockSpec((1, GLA_WIDTH), lambda b, l: (0, 0)),
        ],
        out_specs=pl.BlockSpec((L, GLA_WIDTH), lambda b, l: (rb(b, l), 0)),
        out_shape=jax.ShapeDtypeStruct((T, GLA_WIDTH), BF16),
        scratch_shapes=[pltpu.VMEM((GLA_HEADS, GLA_DK, GLA_DV), F32)],
        compiler_params=_params(("parallel", "arbitrary"), 32),
        name="gla",
    )(proj, proj, proj, proj, glr, wg, bg, nw)


def _layer_norm(h, w, b):
    mu = jnp.mean(h, axis=-1, keepdims=True)
    hc = h - mu
    var = jnp.mean(jnp.square(hc), axis=-1, keepdims=True)
    return hc * lax.rsqrt(var + LN_EPS) * w + b


def _outproj_kernel(yp_ref, yg_ref, x_ref, w_ref, lw_ref, lb_ref, x1_ref, x1t_ref):
    mix = (jnp.dot(yp_ref[...], w_ref[0:POOL_WIDTH, :], preferred_element_type=F32)
           + jnp.dot(yg_ref[...], w_ref[POOL_WIDTH:, :], preferred_element_type=F32))
    x1 = _layer_norm(ALPHA * x_ref[...] + mix, lw_ref[...], lb_ref[...])
    x1_ref[...] = x1
    x1t_ref[...] = jnp.transpose(x1).astype(BF16)


def _outproj(y_pool, y_gla, x2, w_out, ln_w, ln_b, tm=512):
    T = x2.shape[0]
    return pl.pallas_call(
        _outproj_kernel,
        grid=(T // tm,),
        in_specs=[
            pl.BlockSpec((tm, POOL_WIDTH), lambda i: (i, 0)),
            pl.BlockSpec((tm, GLA_WIDTH), lambda i: (i, 0)),
            pl.BlockSpec((tm, D_MODEL), lambda i: (i, 0)),
            pl.BlockSpec((D_MODEL, D_MODEL), lambda i: (0, 0), pipeline_mode=pl.Buffered(1)),
            pl.BlockSpec((1, D_MODEL), lambda i: (0, 0)),
            pl.BlockSpec((1, D_MODEL), lambda i: (0, 0)),
        ],
        out_specs=[
            pl.BlockSpec((tm, D_MODEL), lambda i: (i, 0)),
            pl.BlockSpec((D_MODEL, tm), lambda i: (0, i)),
        ],
        out_shape=[
            jax.ShapeDtypeStruct((T, D_MODEL), F32),
            jax.ShapeDtypeStruct((D_MODEL, T), BF16),
        ],
        compiler_params=_params(("parallel",), 48),
        name="outproj",
    )(y_pool, y_gla, x2, w_out, ln_w, ln_b)


N_SORT = PEER_TOPK + 1


def _sort_network(n):
    pairs = []

    def merge(lo, m, r):
        step = 2 * r
        if step < m:
            merge(lo, m, step)
            merge(lo + r, m, step)
            pairs.extend((i, i + r) for i in range(lo + r, lo + m - r, step))
        else:
            pairs.append((lo, lo + r))

    def sort(lo, m):
        if m > 1:
            sort(lo, m // 2)
            sort(lo + m // 2, m // 2)
            merge(lo, m, 1)

    sort(0, n)
    return tuple(pairs)


def _pop_sorted(v, n_out):
    nv = len(v)
    width = 1 << (nv - 1).bit_length()
    for i, j in _sort_network(width):
        if j < nv:
            v[i], v[j] = jnp.maximum(v[i], v[j]), jnp.minimum(v[i], v[j])
    tops = []
    for kk in range(n_out):
        m = jnp.max(v[0], axis=0, keepdims=True)
        tops.append(m)
        hit = v[0] == m
        for k in range(min(n_out - 1 - kk, nv)):
            v[k] = jnp.where(hit, v[k + 1] if k + 1 < nv else NEG_INF, v[k])
    return tops


def _sorted_top(arr):
    return _pop_sorted([arr[k:k + SUBLANES] for k in range(0, arr.shape[0], SUBLANES)], N_SORT)


def _rows_to_tile(rows):
    rid = lax.broadcasted_iota(jnp.int32, (SUBLANES, LANES), 0)
    tile = jnp.full((SUBLANES, LANES), NEG_INF, F32)
    for k, r in enumerate(rows):
        tile = jnp.where(rid == k, r, tile)
    return tile


def _pair_stats(a, b):
    r8 = lax.broadcasted_iota(jnp.int32, (SUBLANES, LANES), 0)
    b_lo, b_hi, a_hi = _rows_to_tile(b[0:8]), _rows_to_tile(b[8:16]), _rows_to_tile(a[8:16])
    p2 = jnp.where(r8 < 5, a[2] + b_lo, jnp.where(r8 == 5, a[16] + b[0], jnp.where(r8 == 6, a[0] + b[16], NEG_INF)))
    pieces = [
        a[0] + b_lo, a[0] + b_hi, a[1] + b_lo, p2,
        jnp.where(r8 < 4, a[3] + b_lo, NEG_INF),
        jnp.where(r8 < 3, a[4] + b_lo, NEG_INF),
        jnp.where(r8 < 2, a[5] + b_lo, NEG_INF),
        jnp.where(r8 < 2, a[6] + b_lo, NEG_INF),
        jnp.where(r8 < 2, a[7] + b_lo, NEG_INF),
        a_hi + b[0],
    ]
    sums = _pop_sorted(pieces, N_SORT)
    top16 = jnp.concatenate([_rows_to_tile(sums[0:8]), _rows_to_tile(sums[8:16])], axis=0)
    z = jnp.sum(jnp.exp(top16 - sums[0]), axis=0, keepdims=True)
    return 0.5 * (sums[PEER_TOPK - 1] + sums[PEER_TOPK]), 1.0 / z


def _split_bf16(x):
    hi = x.astype(BF16)
    return hi, (x - hi.astype(F32)).astype(BF16)


def _query_kernel(x1t_ref, wq_ref, khi_ref, klo_ref, c1_ref, n1_ref, e2_ref, r2_ref, q_ref, s1_scr, s2_scr, *, tm):
    q_ref[...] = jnp.dot(wq_ref[...], x1t_ref[...], preferred_element_type=F32)
    for h in range(PEER_HEADS):
        for p in range(2):
            hp = 2 * h + p
            q_hi, q_lo = _split_bf16(q_ref[hp * PEER_HALF:(hp + 1) * PEER_HALF, :])
            k_hi, k_lo = khi_ref[hp], klo_ref[hp]
            sc = (jnp.dot(k_hi, q_hi, preferred_element_type=F32)
                  + jnp.dot(k_hi, q_lo, preferred_element_type=F32)
                  + jnp.dot(k_lo, q_hi, preferred_element_type=F32))
            if p == 0:
                s1_scr[...] = sc
            else:
                s2_scr[...] = sc
        for tc in range(tm // LANES):
            lanes = slice(tc * LANES, (tc + 1) * LANES)
            s1 = s1_scr[:, lanes]
            s2 = s2_scr[:, lanes]
            ta, tb = _sorted_top(s1), _sorted_top(s2)
            tau, rz = _pair_stats(ta, tb)
            n1 = jnp.zeros_like(s1)
            r2 = jnp.zeros_like(s2)
            for jj in range(PEER_TOPK):
                n1 = jnp.where(s1 >= tau - tb[jj], float(jj + 1), n1)
                r2 = jnp.where(s2 < tb[jj], float(jj + 1), r2)
            c1_ref[h, tc] = jnp.exp(s1 - ta[0])
            n1_ref[h, tc] = n1
            e2_ref[h, :, lanes] = (jnp.exp(s2 - tb[0]) * rz).astype(BF16)
            r2_ref[h, :, lanes] = r2.astype(BF16)


def _query(x1t, wq_t, keys_hi, keys_lo, tm=256):
    T = x1t.shape[1]
    nc = tm // LANES
    row_spec = pl.BlockSpec((PEER_HEADS, nc, PEER_NKEYS, LANES), lambda i: (0, i, 0, 0))
    col_spec = pl.BlockSpec((PEER_HEADS, PEER_NKEYS, tm), lambda i: (0, 0, i))
    key_spec = pl.BlockSpec((2 * PEER_HEADS, PEER_NKEYS, PEER_HALF), lambda i: (0, 0, 0))
    row_shape = jax.ShapeDtypeStruct((PEER_HEADS, T // LANES, PEER_NKEYS, LANES), F32)
    col_shape = jax.ShapeDtypeStruct((PEER_HEADS, PEER_NKEYS, T), BF16)
    return pl.pallas_call(
        functools.partial(_query_kernel, tm=tm),
        grid=(T // tm,),
        in_specs=[
            pl.BlockSpec((D_MODEL, tm), lambda i: (0, i)),
            pl.BlockSpec((D_MODEL, D_MODEL), lambda i: (0, 0), pipeline_mode=pl.Buffered(1)),
            key_spec, key_spec,
        ],
        out_specs=[row_spec, row_spec, col_spec, col_spec],
        out_shape=[row_shape, row_shape, col_shape, col_shape],
        scratch_shapes=[pltpu.VMEM((D_MODEL, tm), F32), pltpu.VMEM((PEER_NKEYS, tm), F32),
                        pltpu.VMEM((PEER_NKEYS, tm), F32)],
        compiler_params=_params(("parallel",), 48),
        name="query",
    )(x1t, wq_t, keys_hi, keys_lo)


PEER_SB = 64
PEER_RG = 2
SUBLANES = 8
PEER_TE = 1024


def _gelu(x):
    return 0.5 * x * (1.0 + lax.erf(x * (1.0 / math.sqrt(2.0))))


def _bcast_row_bf16(tile, ri, rows):
    packed = jnp.broadcast_to(tile[ri:ri + 1, :], (2 * SUBLANES, LANES)).astype(BF16)
    return jnp.concatenate([packed] * (rows // (2 * SUBLANES)), axis=0)


def _peer_kernel(x1t_ref, u_ref, vt_ref, c1_ref, n1_ref, e2_ref, r2_ref, y_ref,
                 acc_ref, st_scr, ht_scr, *, tm, te):
    j = pl.program_id(1)
    n1 = te // PEER_NKEYS
    grows = PEER_RG * PEER_NKEYS

    @pl.when(j == 0)
    def _():
        acc_ref[...] = jnp.zeros_like(acc_ref)

    st_scr[...] = jnp.dot(u_ref[...], x1t_ref[...], preferred_element_type=F32)

    nsb = PEER_NKEYS // PEER_SB
    tile_rows = pl.ds(pl.multiple_of(j * n1, SUBLANES), n1)
    for gi in range(n1 // PEER_RG):
        crows = slice(gi * grows, (gi + 1) * grows)
        for tc in range(tm // LANES):
            lanes = slice(tc * LANES, (tc + 1) * LANES)
            g = [[jnp.zeros((PEER_SB, LANES), BF16) for _ in range(nsb)] for _ in range(PEER_RG)]
            for h in range(PEER_HEADS):
                c1_t = c1_ref[h, tc, tile_rows, :]
                n1_t = n1_ref[h, tc, tile_rows, :]
                c1b = [_bcast_row_bf16(c1_t, gi * PEER_RG + r, PEER_SB) for r in range(PEER_RG)]
                n1b = [_bcast_row_bf16(n1_t, gi * PEER_RG + r, PEER_SB) for r in range(PEER_RG)]
                for sb in range(nsb):
                    rows = slice(sb * PEER_SB, (sb + 1) * PEER_SB)
                    r2c = r2_ref[h, rows, lanes]
                    e2c = e2_ref[h, rows, lanes]
                    for r in range(PEER_RG):
                        g[r][sb] = g[r][sb] + c1b[r] * jnp.where(r2c < n1b[r], e2c, jnp.zeros_like(e2c))
            for r in range(PEER_RG):
                for sb in range(nsb):
                    base = gi * grows + r * PEER_NKEYS + sb * PEER_SB
                    srows = slice(base, base + PEER_SB)
                    ht_scr[srows, lanes] = g[r][sb] * _gelu(st_scr[srows, lanes].astype(BF16))
    acc_ref[...] += jnp.dot(vt_ref[...], ht_scr[...], preferred_element_type=F32)

    @pl.when(j == pl.num_programs(1) - 1)
    def _():
        y_ref[...] = jnp.transpose(acc_ref[...])


def _peer(x1t, u_tab, vt_tiles, c1, n1, e2, r2, tm=512):
    T = x1t.shape[1]
    nj, _, te = vt_tiles.shape
    once = pl.Buffered(1)
    row_spec = pl.BlockSpec((PEER_HEADS, tm // LANES, PEER_NKEYS, LANES), lambda i, j: (0, i, 0, 0),
                            pipeline_mode=once)
    col_spec = pl.BlockSpec((PEER_HEADS, PEER_NKEYS, tm), lambda i, j: (0, 0, i), pipeline_mode=once)
    return pl.pallas_call(
        functools.partial(_peer_kernel, tm=tm, te=te),
        grid=(T // tm, nj),
        in_specs=[
            pl.BlockSpec((D_MODEL, tm), lambda i, j: (0, i), pipeline_mode=once),
            pl.BlockSpec((te, D_MODEL), lambda i, j: (j, 0)),
            pl.BlockSpec((None, D_MODEL, te), lambda i, j: (j, 0, 0)),
            row_spec, row_spec, col_spec, col_spec,
        ],
        out_specs=pl.BlockSpec((tm, D_MODEL), lambda i, j: (i, 0)),
        out_shape=jax.ShapeDtypeStruct((T, D_MODEL), F32),
        scratch_shapes=[
            pltpu.VMEM((D_MODEL, tm), F32),
            pltpu.VMEM((te, tm), F32),
            pltpu.VMEM((te, tm), BF16),
        ],
        compiler_params=_params(("parallel", "arbitrary"), 56),
        name="peer",
    )(x1t, u_tab, vt_tiles, c1, n1, e2, r2)


def _final_kernel(x1_ref, y_ref, p_ref, wg_ref, wp_ref, lw_ref, lb_ref, o_ref):
    x1 = x1_ref[...]
    gate = jax.nn.sigmoid(jnp.dot(x1.astype(BF16), wg_ref[...], preferred_element_type=F32))
    emb = jnp.dot(p_ref[...].astype(BF16), wp_ref[...], preferred_element_type=F32)
    o_ref[...] = _layer_norm(ALPHA * x1 + y_ref[...] + gate * emb, lw_ref[...], lb_ref[...])


def _final(x1, y_ffn, p2, w_gate, w_proj, ln_w, ln_b, tm=512):
    T = x1.shape[0]
    return pl.pallas_call(
        _final_kernel,
        grid=(T // tm,),
        in_specs=[
            pl.BlockSpec((tm, D_MODEL), lambda i: (i, 0)),
            pl.BlockSpec((tm, D_MODEL), lambda i: (i, 0)),
            pl.BlockSpec((tm, PLE_DIM), lambda i: (i, 0)),
            pl.BlockSpec((D_MODEL, D_MODEL), lambda i: (0, 0), pipeline_mode=pl.Buffered(1)),
            pl.BlockSpec((PLE_DIM, D_MODEL), lambda i: (0, 0), pipeline_mode=pl.Buffered(1)),
            pl.BlockSpec((1, D_MODEL), lambda i: (0, 0)),
            pl.BlockSpec((1, D_MODEL), lambda i: (0, 0)),
        ],
        out_specs=pl.BlockSpec((tm, D_MODEL), lambda i: (i, 0)),
        out_shape=jax.ShapeDtypeStruct((T, D_MODEL), F32),
        compiler_params=_params(("parallel",), 48),
        name="final",
    )(x1, y_ffn, p2, w_gate, w_proj, ln_w, ln_b)


def _layer(x2, p2, B, S, w_in, gla_w_gate_up, gla_b_gate, gla_norm_w, pool_w, pool_scale, w_out,
           ln1_w, ln1_b, peer_w_query, peer_sub_keys, peer_u, peer_v, ple_w_gate, ple_w_proj, ln2_w, ln2_b):
    glr0 = COL_R
    w_main = jnp.concatenate([w_in[:, :glr0], w_in[:, glr0 + GLA_GATE_RANK:]], axis=1).astype(BF16)
    w_glr = jnp.pad(w_in[:, glr0:glr0 + GLA_GATE_RANK], ((0, 0), (0, LANES - GLA_GATE_RANK))).astype(BF16)
    proj, glr = _proj(x2, w_main, w_glr)

    y_pool = _pool(proj, pool_w.astype(BF16), pool_scale.reshape(1, POOL_WIDTH), S)

    wg = jnp.pad(gla_w_gate_up, ((0, LANES - GLA_GATE_RANK), (0, 0))).astype(BF16)
    y_gla = _gla(proj, glr, wg, gla_b_gate.reshape(1, GLA_KEY_WIDTH),
                 gla_norm_w.reshape(1, GLA_WIDTH), B, S)

    x1, x1t = _outproj(y_pool, y_gla, x2, w_out.astype(BF16),
                       ln1_w.reshape(1, D_MODEL), ln1_b.reshape(1, D_MODEL))

    keys = peer_sub_keys.reshape(2 * PEER_HEADS, PEER_NKEYS, PEER_HALF)
    keys_hi, keys_lo = _split_bf16(keys)
    c1, n1, e2, r2 = _query(x1t, peer_w_query.T.astype(BF16), keys_hi, keys_lo)
    n_exp = peer_v.shape[0]
    vt_tiles = peer_v.reshape(n_exp // PEER_TE, PEER_TE, D_MODEL).transpose(0, 2, 1).astype(BF16)
    y_ffn = _peer(x1t, peer_u.astype(BF16), vt_tiles, c1, n1, e2, r2)

    return _final(x1, y_ffn, p2, ple_w_gate.astype(BF16), ple_w_proj.astype(BF16),
                  ln2_w.reshape(1, D_MODEL), ln2_b.reshape(1, D_MODEL))


def kernel(x, p, w_in, gla_w_gate_up, gla_b_gate, gla_norm_w, pool_w, pool_scale, w_out, ln1_w, ln1_b,
           peer_w_query, peer_sub_keys, peer_u, peer_v, ple_w_gate, ple_w_proj, ln2_w, ln2_b):
    B, S, D = x.shape
    x2 = x.reshape(B * S, D)
    for i in range(w_in.shape[0]):
        x2 = _layer(x2, p[i].reshape(B * S, PLE_DIM), B, S, w_in[i], gla_w_gate_up[i], gla_b_gate[i],
                    gla_norm_w[i], pool_w[i], pool_scale[i], w_out[i], ln1_w[i], ln1_b[i],
                    peer_w_query[i], peer_sub_keys[i], peer_u[i], peer_v[i], ple_w_gate[i],
                    ple_w_proj[i], ln2_w[i], ln2_b[i])
    return x2.reshape(B, S, D)
```

```python
import functools
import math

import jax
import jax.numpy as jnp
from jax import lax
from jax.experimental import pallas as pl
from jax.experimental.pallas import tpu as pltpu

F32 = jnp.float32
BF16 = jnp.bfloat16

D_MODEL = 2048
PLE_DIM = 256
POOL_WIDTH = 1024
POOL_WINDOWS = (2, 4, 8, 16)
POOL_GC = 256
POOL_HALO = 16
GLA_WIDTH = 1024
GLA_HEADS = 4
GLA_DV = 256
GLA_DK = 128
GLA_KEY_WIDTH = 512
GLA_GATE_RANK = 16
GLA_GATE_TEMP = 16.0
GLA_CHUNK = 64
PEER_HEADS = 8
PEER_NKEYS = 128
PEER_HALF = 128
PEER_TOPK = 16
DEPTH = 1
ALPHA = float((2 * DEPTH) ** 0.25)
LN_EPS = 1e-5
RMS_EPS = 1e-6
LANES = 128
NEG_INF = float("-inf")

COL_Q = POOL_WIDTH
COL_K = COL_Q + GLA_KEY_WIDTH
COL_V = COL_K + GLA_KEY_WIDTH
COL_R = COL_V + GLA_WIDTH
PROJ_COLS = COL_R + GLA_WIDTH


def _params(sem, vmem_mib):
    return pltpu.CompilerParams(dimension_semantics=sem, vmem_limit_bytes=vmem_mib * 1024 * 1024)


def _proj_kernel(x_ref, w_ref, wg_ref, o_ref, glr_ref, xb_ref):
    @pl.when(pl.program_id(1) == 0)
    def _():
        xb = x_ref[...].astype(BF16)
        xb_ref[...] = xb
        glr_ref[...] = jnp.dot(xb, wg_ref[...], preferred_element_type=F32)

    o_ref[...] = jnp.dot(xb_ref[...], w_ref[...], preferred_element_type=F32).astype(o_ref.dtype)


def _proj(x2, w_main, w_glr, tm=1024, tn=2048):
    T = x2.shape[0]
    return pl.pallas_call(
        _proj_kernel,
        grid=(T // tm, PROJ_COLS // tn),
        in_specs=[
            pl.BlockSpec((tm, D_MODEL), lambda i, n: (i, 0)),
            pl.BlockSpec((D_MODEL, tn), lambda i, n: (0, n)),
            pl.BlockSpec((D_MODEL, LANES), lambda i, n: (0, 0)),
        ],
        out_specs=[
            pl.BlockSpec((tm, tn), lambda i, n: (i, n)),
            pl.BlockSpec((tm, LANES), lambda i, n: (i, 0)),
        ],
        out_shape=[
            jax.ShapeDtypeStruct((T, PROJ_COLS), BF16),
            jax.ShapeDtypeStruct((T, LANES), F32),
        ],
        scratch_shapes=[pltpu.VMEM((tm, D_MODEL), BF16)],
        compiler_params=_params(("parallel", "arbitrary"), 52),
        name="proj",
    )(x2, w_main, w_glr)


def _pool_kernel(u_ref, halo_ref, w_ref, sc_ref, o_ref, ext_ref, *, tiles_per_seq, tm):
    t = pl.program_id(0) % tiles_per_seq
    halo = jnp.where(t == 0, 0.0, halo_ref[...].astype(F32))
    ext_ref[0:POOL_HALO, :] = halo
    ext_ref[POOL_HALO:, :] = u_ref[...].astype(F32)
    pos = t * tm + lax.broadcasted_iota(jnp.int32, (tm, 1), 0)
    for g, w in enumerate(POOL_WINDOWS):
        cols = slice(g * POOL_GC, (g + 1) * POOL_GC)
        u = ext_ref[POOL_HALO:, cols]
        acc = u
        for j in range(1, w):
            acc = acc + ext_ref[POOL_HALO - j:POOL_HALO - j + tm, cols]
        cnt = jnp.minimum(pos + 1, w).astype(F32)
        d = acc / cnt - u
        y = jnp.dot(d.astype(BF16), w_ref[g], preferred_element_type=F32)
        o_ref[:, cols] = (y * sc_ref[:, cols]).astype(o_ref.dtype)


def _pool(proj, pool_w, pool_scale, S, tm=512):
    T = proj.shape[0]
    hb = tm // POOL_HALO
    return pl.pallas_call(
        functools.partial(_pool_kernel, tiles_per_seq=S // tm, tm=tm),
        grid=(T // tm,),
        in_specs=[
            pl.BlockSpec((tm, POOL_WIDTH), lambda i: (i, 0)),
            pl.BlockSpec((POOL_HALO, POOL_WIDTH), lambda i: (jnp.maximum(i * hb - 1, 0), 0)),
            pl.BlockSpec((len(POOL_WINDOWS), POOL_GC, POOL_GC), lambda i: (0, 0, 0)),
            pl.BlockSpec((1, POOL_WIDTH), lambda i: (0, 0)),
        ],
        out_specs=pl.BlockSpec((tm, POOL_WIDTH), lambda i: (i, 0)),
        out_shape=jax.ShapeDtypeStruct((T, POOL_WIDTH), BF16),
        scratch_shapes=[pltpu.VMEM((POOL_HALO + tm, POOL_WIDTH), F32)],
        compiler_params=_params(("parallel",), 32),
        name="pool",
    )(proj, proj, pool_w, pool_scale)


def _gla_kernel(q_ref, k_ref, v_ref, r_ref, glr_ref, wg_ref, bg_ref, nw_ref, o_ref, s_ref, *, n_chunks):
    @pl.when(pl.program_id(1) == 0)
    def _():
        s_ref[...] = jnp.zeros_like(s_ref)

    C = GLA_CHUNK
    row = lax.broadcasted_iota(jnp.int32, (C, C), 0)
    col = lax.broadcasted_iota(jnp.int32, (C, C), 1)
    causal = col <= row
    tril = causal.astype(BF16)
    wg = wg_ref[...]
    bg = bg_ref[...]
    nt = (((1,), (1,)), ((), ()))
    for c in range(n_chunks):
        rows = slice(c * C, (c + 1) * C)
        z = jnp.dot(glr_ref[rows, :].astype(BF16), wg, preferred_element_type=F32) + bg
        g = jax.nn.log_sigmoid(z) / GLA_GATE_TEMP
        g_hi, g_lo = _split_bf16(g)
        b_all = (jnp.dot(tril, g_hi, preferred_element_type=F32)
                 + jnp.dot(tril, g_lo, preferred_element_type=F32))
        for h in range(GLA_HEADS):
            kc = slice(h * GLA_DK, (h + 1) * GLA_DK)
            vc = slice(h * GLA_DV, (h + 1) * GLA_DV)
            b = b_all[:, kc]
            b_last = b[C - 1:C, :]
            b_mid = b[C // 2 - 1:C // 2, :]
            q = q_ref[rows, kc].astype(F32) * (GLA_DK ** -0.5)
            k = k_ref[rows, kc].astype(F32)
            v = v_ref[rows, vc]
            q_state = (q * jnp.exp(b)).astype(BF16)
            q_in = (q * jnp.exp(b - b_mid)).astype(BF16)
            k_in = (k * jnp.exp(b_mid - b)).astype(BF16)
            k_out = k * jnp.exp(b_last - b)
            attn = lax.dot_general(q_in, k_in, nt, preferred_element_type=F32)
            attn = jnp.where(causal, attn, 0.0).astype(BF16)
            s = s_ref[h]
            o = (jnp.dot(attn, v, preferred_element_type=F32)
                 + jnp.dot(q_state, s.astype(BF16), preferred_element_type=F32))
            decay = jnp.transpose(jnp.broadcast_to(jnp.exp(b_last), (C, GLA_DK)))[:, 0:1]
            s_ref[h] = decay * s + jnp.dot(jnp.transpose(k_out).astype(BF16), v, preferred_element_type=F32)
            o = o * lax.rsqrt(jnp.mean(jnp.square(o), axis=-1, keepdims=True) + RMS_EPS)
            o = o * nw_ref[:, vc]
            r = r_ref[rows, vc].astype(F32)
            o_ref[rows, vc] = (o * (r * jax.nn.sigmoid(r))).astype(o_ref.dtype)


def _gla(proj, glr, wg, bg, nw, B, S, L=512):
    T = proj.shape[0]
    nl = S // L
    rb = lambda b, l: b * nl + l
    return pl.pallas_call(
        functools.partial(_gla_kernel, n_chunks=L // GLA_CHUNK),
        grid=(B, nl),
        in_specs=[
            pl.BlockSpec((L, GLA_KEY_WIDTH), lambda b, l: (rb(b, l), COL_Q // GLA_KEY_WIDTH)),
            pl.BlockSpec((L, GLA_KEY_WIDTH), lambda b, l: (rb(b, l), COL_K // GLA_KEY_WIDTH)),
            pl.BlockSpec((L, GLA_WIDTH), lambda b, l: (rb(b, l), COL_V // GLA_WIDTH)),
            pl.BlockSpec((L, GLA_WIDTH), lambda b, l: (rb(b, l), COL_R // GLA_WIDTH)),
            pl.BlockSpec((L, LANES), lambda b, l: (rb(b, l), 0)),
            pl.BlockSpec((LANES, GLA_KEY_WIDTH), lambda b, l: (0, 0)),
            pl.BlockSpec((1, GLA_KEY_WIDTH), lambda b, l: (0, 0)),
            pl.BlockSpec((1, GLA_WIDTH), lambda b, l: (0, 0)),
        ],
        out_specs=pl.BlockSpec((L, GLA_WIDTH), lambda b, l: (rb(b, l), 0)),
        out_shape=jax.ShapeDtypeStruct((T, GLA_WIDTH), BF16),
        scratch_shapes=[pltpu.VMEM((GLA_HEADS, GLA_DK, GLA_DV), F32)],
        compiler_params=_params(("parallel", "arbitrary"), 32),
        name="gla",
    )(proj, proj, proj, proj, glr, wg, bg, nw)


def _layer_norm(h, w, b):
    mu = jnp.mean(h, axis=-1, keepdims=True)
    hc = h - mu
    var = jnp.mean(jnp.square(hc), axis=-1, keepdims=True)
    return hc * lax.rsqrt(var + LN_EPS) * w + b


def _outproj_kernel(yp_ref, yg_ref, x_ref, w_ref, lw_ref, lb_ref, x1_ref, x1t_ref):
    mix = (jnp.dot(yp_ref[...], w_ref[0:POOL_WIDTH, :], preferred_element_type=F32)
           + jnp.dot(yg_ref[...], w_ref[POOL_WIDTH:, :], preferred_element_type=F32))
    x1 = _layer_norm(ALPHA * x_ref[...] + mix, lw_ref[...], lb_ref[...])
    x1_ref[...] = x1
    x1t_ref[...] = jnp.transpose(x1).astype(BF16)


def _outproj(y_pool, y_gla, x2, w_out, ln_w, ln_b, tm=512):
    T = x2.shape[0]
    return pl.pallas_call(
        _outproj_kernel,
        grid=(T // tm,),
        in_specs=[
            pl.BlockSpec((tm, POOL_WIDTH), lambda i: (i, 0)),
            pl.BlockSpec((tm, GLA_WIDTH), lambda i: (i, 0)),
            pl.BlockSpec((tm, D_MODEL), lambda i: (i, 0)),
            pl.BlockSpec((D_MODEL, D_MODEL), lambda i: (0, 0), pipeline_mode=pl.Buffered(1)),
            pl.BlockSpec((1, D_MODEL), lambda i: (0, 0)),
            pl.BlockSpec((1, D_MODEL), lambda i: (0, 0)),
        ],
        out_specs=[
            pl.BlockSpec((tm, D_MODEL), lambda i: (i, 0)),
            pl.BlockSpec((D_MODEL, tm), lambda i: (0, i)),
        ],
        out_shape=[
            jax.ShapeDtypeStruct((T, D_MODEL), F32),
            jax.ShapeDtypeStruct((D_MODEL, T), BF16),
        ],
        compiler_params=_params(("parallel",), 48),
        name="outproj",
    )(y_pool, y_gla, x2, w_out, ln_w, ln_b)


N_SORT = PEER_TOPK + 1


def _sort_network(n):
    pairs = []

    def merge(lo, m, r):
        step = 2 * r
        if step < m:
            merge(lo, m, step)
            merge(lo + r, m, step)
            pairs.extend((i, i + r) for i in range(lo + r, lo + m - r, step))
        else:
            pairs.append((lo, lo + r))

    def sort(lo, m):
        if m > 1:
            sort(lo, m // 2)
            sort(lo + m // 2, m // 2)
            merge(lo, m, 1)

    sort(0, n)
    return tuple(pairs)


def _pop_sorted(v, n_out):
    nv = len(v)
    width = 1 << (nv - 1).bit_length()
    for i, j in _sort_network(width):
        if j < nv:
            v[i], v[j] = jnp.maximum(v[i], v[j]), jnp.minimum(v[i], v[j])
    tops = []
    for kk in range(n_out):
        m = jnp.max(v[0], axis=0, keepdims=True)
        tops.append(m)
        hit = v[0] == m
        for k in range(min(n_out - 1 - kk, nv)):
            v[k] = jnp.where(hit, v[k + 1] if k + 1 < nv else NEG_INF, v[k])
    return tops


def _sorted_top(arr):
    return _pop_sorted([arr[k:k + SUBLANES] for k in range(0, arr.shape[0], SUBLANES)], N_SORT)


def _rows_to_tile(rows):
    rid = lax.broadcasted_iota(jnp.int32, (SUBLANES, LANES), 0)
    tile = jnp.full((SUBLANES, LANES), NEG_INF, F32)
    for k, r in enumerate(rows):
        tile = jnp.where(rid == k, r, tile)
    return tile


def _pair_stats(a, b):
    r8 = lax.broadcasted_iota(jnp.int32, (SUBLANES, LANES), 0)
    b_lo, b_hi, a_hi = _rows_to_tile(b[0:8]), _rows_to_tile(b[8:16]), _rows_to_tile(a[8:16])
    p2 = jnp.where(r8 < 5, a[2] + b_lo, jnp.where(r8 == 5, a[16] + b[0], jnp.where(r8 == 6, a[0] + b[16], NEG_INF)))
    pieces = [
        a[0] + b_lo, a[0] + b_hi, a[1] + b_lo, p2,
        jnp.where(r8 < 4, a[3] + b_lo, NEG_INF),
        jnp.where(r8 < 3, a[4] + b_lo, NEG_INF),
        jnp.where(r8 < 2, a[5] + b_lo, NEG_INF),
        jnp.where(r8 < 2, a[6] + b_lo, NEG_INF),
        jnp.where(r8 < 2, a[7] + b_lo, NEG_INF),
        a_hi + b[0],
    ]
    sums = _pop_sorted(pieces, N_SORT)
    top16 = jnp.concatenate([_rows_to_tile(sums[0:8]), _rows_to_tile(sums[8:16])], axis=0)
    z = jnp.sum(jnp.exp(top16 - sums[0]), axis=0, keepdims=True)
    return 0.5 * (sums[PEER_TOPK - 1] + sums[PEER_TOPK]), 1.0 / z


def _split_bf16(x):
    hi = x.astype(BF16)
    return hi, (x - hi.astype(F32)).astype(BF16)


def _query_kernel(x1t_ref, wq_ref, khi_ref, klo_ref, c1_ref, n1_ref, e2_ref, r2_ref, q_ref, s1_scr, s2_scr, *, tm):
    q_ref[...] = jnp.dot(wq_ref[...], x1t_ref[...], preferred_element_type=F32)
    for h in range(PEER_HEADS):
        for p in range(2):
            hp = 2 * h + p
            q_hi, q_lo = _split_bf16(q_ref[hp * PEER_HALF:(hp + 1) * PEER_HALF, :])
            k_hi, k_lo = khi_ref[hp], klo_ref[hp]
            sc = (jnp.dot(k_hi, q_hi, preferred_element_type=F32)
                  + jnp.dot(k_hi, q_lo, preferred_element_type=F32)
                  + jnp.dot(k_lo, q_hi, preferred_element_type=F32))
            if p == 0:
                s1_scr[...] = sc
            else:
                s2_scr[...] = sc
        for tc in range(tm // LANES):
            lanes = slice(tc * LANES, (tc + 1) * LANES)
            s1 = s1_scr[:, lanes]
            s2 = s2_scr[:, lanes]
            ta, tb = _sorted_top(s1), _sorted_top(s2)
            tau, rz = _pair_stats(ta, tb)
            n1 = jnp.zeros_like(s1)
            r2 = jnp.zeros_like(s2)
            for jj in range(PEER_TOPK):
                n1 = jnp.where(s1 >= tau - tb[jj], float(jj + 1), n1)
                r2 = jnp.where(s2 < tb[jj], float(jj + 1), r2)
            c1_ref[h, tc] = jnp.exp(s1 - ta[0])
            n1_ref[h, tc] = n1
            e2_ref[h, :, lanes] = (jnp.exp(s2 - tb[0]) * rz).astype(BF16)
            r2_ref[h, :, lanes] = r2.astype(BF16)


def _query(x1t, wq_t, keys_hi, keys_lo, tm=256):
    T = x1t.shape[1]
    nc = tm // LANES
    row_spec = pl.BlockSpec((PEER_HEADS, nc, PEER_NKEYS, LANES), lambda i: (0, i, 0, 0))
    col_spec = pl.BlockSpec((PEER_HEADS, PEER_NKEYS, tm), lambda i: (0, 0, i))
    key_spec = pl.BlockSpec((2 * PEER_HEADS, PEER_NKEYS, PEER_HALF), lambda i: (0, 0, 0))
    row_shape = jax.ShapeDtypeStruct((PEER_HEADS, T // LANES, PEER_NKEYS, LANES), F32)
    col_shape = jax.ShapeDtypeStruct((PEER_HEADS, PEER_NKEYS, T), BF16)
    return pl.pallas_call(
        functools.partial(_query_kernel, tm=tm),
        grid=(T // tm,),
        in_specs=[
            pl.BlockSpec((D_MODEL, tm), lambda i: (0, i)),
            pl.BlockSpec((D_MODEL, D_MODEL), lambda i: (0, 0), pipeline_mode=pl.Buffered(1)),
            key_spec, key_spec,
        ],
        out_specs=[row_spec, row_spec, col_spec, col_spec],
        out_shape=[row_shape, row_shape, col_shape, col_shape],
        scratch_shapes=[pltpu.VMEM((D_MODEL, tm), F32), pltpu.VMEM((PEER_NKEYS, tm), F32),
                        pltpu.VMEM((PEER_NKEYS, tm), F32)],
        compiler_params=_params(("parallel",), 48),
        name="query",
    )(x1t, wq_t, keys_hi, keys_lo)


PEER_SB = 64
PEER_RG = 2
SUBLANES = 8
PEER_TE = 1024


def _gelu(x):
    return 0.5 * x * (1.0 + lax.erf(x * (1.0 / math.sqrt(2.0))))


def _bcast_row_bf16(tile, ri, rows):
    packed = jnp.broadcast_to(tile[ri:ri + 1, :], (2 * SUBLANES, LANES)).astype(BF16)
    return jnp.concatenate([packed] * (rows // (2 * SUBLANES)), axis=0)


def _peer_kernel(x1t_ref, u_ref, vt_ref, c1_ref, n1_ref, e2_ref, r2_ref, y_ref,
                 acc_ref, st_scr, ht_scr, *, tm, te):
    j = pl.program_id(1)
    n1 = te // PEER_NKEYS
    grows = PEER_RG * PEER_NKEYS

    @pl.when(j == 0)
    def _():
        acc_ref[...] = jnp.zeros_like(acc_ref)

    st_scr[...] = jnp.dot(u_ref[...], x1t_ref[...], preferred_element_type=F32)

    nsb = PEER_NKEYS // PEER_SB
    tile_rows = pl.ds(pl.multiple_of(j * n1, SUBLANES), n1)
    for gi in range(n1 // PEER_RG):
        crows = slice(gi * grows, (gi + 1) * grows)
        for tc in range(tm // LANES):
            lanes = slice(tc * LANES, (tc + 1) * LANES)
            g = [[jnp.zeros((PEER_SB, LANES), BF16) for _ in range(nsb)] for _ in range(PEER_RG)]
            for h in range(PEER_HEADS):
                c1_t = c1_ref[h, tc, tile_rows, :]
                n1_t = n1_ref[h, tc, tile_rows, :]
                c1b = [_bcast_row_bf16(c1_t, gi * PEER_RG + r, PEER_SB) for r in range(PEER_RG)]
                n1b = [_bcast_row_bf16(n1_t, gi * PEER_RG + r, PEER_SB) for r in range(PEER_RG)]
                for sb in range(nsb):
                    rows = slice(sb * PEER_SB, (sb + 1) * PEER_SB)
                    r2c = r2_ref[h, rows, lanes]
                    e2c = e2_ref[h, rows, lanes]
                    for r in range(PEER_RG):
                        g[r][sb] = g[r][sb] + c1b[r] * jnp.where(r2c < n1b[r], e2c, jnp.zeros_like(e2c))
            for r in range(PEER_RG):
                for sb in range(nsb):
                    base = gi * grows + r * PEER_NKEYS + sb * PEER_SB
                    srows = slice(base, base + PEER_SB)
                    ht_scr[srows, lanes] = g[r][sb] * _gelu(st_scr[srows, lanes].astype(BF16))
    acc_ref[...] += jnp.dot(vt_ref[...], ht_scr[...], preferred_element_type=F32)

    @pl.when(j == pl.num_programs(1) - 1)
    def _():
        y_ref[...] = jnp.transpose(acc_ref[...])


def _peer(x1t, u_tab, vt_tiles, c1, n1, e2, r2, tm=512):
    T = x1t.shape[1]
    nj, _, te = vt_tiles.shape
    once = pl.Buffered(1)
    row_spec = pl.BlockSpec((PEER_HEADS, tm // LANES, PEER_NKEYS, LANES), lambda i, j: (0, i, 0, 0),
                            pipeline_mode=once)
    col_spec = pl.BlockSpec((PEER_HEADS, PEER_NKEYS, tm), lambda i, j: (0, 0, i), pipeline_mode=once)
    return pl.pallas_call(
        functools.partial(_peer_kernel, tm=tm, te=te),
        grid=(T // tm, nj),
        in_specs=[
            pl.BlockSpec((D_MODEL, tm), lambda i, j: (0, i), pipeline_mode=once),
            pl.BlockSpec((te, D_MODEL), lambda i, j: (j, 0)),
            pl.BlockSpec((None, D_MODEL, te), lambda i, j: (j, 0, 0)),
            row_spec, row_spec, col_spec, col_spec,
        ],
        out_specs=pl.BlockSpec((tm, D_MODEL), lambda i, j: (i, 0)),
        out_shape=jax.ShapeDtypeStruct((T, D_MODEL), F32),
        scratch_shapes=[
            pltpu.VMEM((D_MODEL, tm), F32),
            pltpu.VMEM((te, tm), F32),
            pltpu.VMEM((te, tm), BF16),
        ],
        compiler_params=_params(("parallel", "arbitrary"), 56),
        name="peer",
    )(x1t, u_tab, vt_tiles, c1, n1, e2, r2)


def _final_kernel(x1_ref, y_ref, p_ref, wg_ref, wp_ref, lw_ref, lb_ref, o_ref):
    x1 = x1_ref[...]
    gate = jax.nn.sigmoid(jnp.dot(x1.astype(BF16), wg_ref[...], preferred_element_type=F32))
    emb = jnp.dot(p_ref[...].astype(BF16), wp_ref[...], preferred_element_type=F32)
    o_ref[...] = _layer_norm(ALPHA * x1 + y_ref[...] + gate * emb, lw_ref[...], lb_ref[...])


def _final(x1, y_ffn, p2, w_gate, w_proj, ln_w, ln_b, tm=512):
    T = x1.shape[0]
    return pl.pallas_call(
        _final_kernel,
        grid=(T // tm,),
        in_specs=[
            pl.BlockSpec((tm, D_MODEL), lambda i: (i, 0)),
            pl.BlockSpec((tm, D_MODEL), lambda i: (i, 0)),
            pl.BlockSpec((tm, PLE_DIM), lambda i: (i, 0)),
            pl.BlockSpec((D_MODEL, D_MODEL), lambda i: (0, 0), pipeline_mode=pl.Buffered(1)),
            pl.BlockSpec((PLE_DIM, D_MODEL), lambda i: (0, 0), pipeline_mode=pl.Buffered(1)),
            pl.BlockSpec((1, D_MODEL), lambda i: (0, 0)),
            pl.BlockSpec((1, D_MODEL), lambda i: (0, 0)),
        ],
        out_specs=pl.BlockSpec((tm, D_MODEL), lambda i: (i, 0)),
        out_shape=jax.ShapeDtypeStruct((T, D_MODEL), F32),
        compiler_params=_params(("parallel",), 48),
        name="final",
    )(x1, y_ffn, p2, w_gate, w_proj, ln_w, ln_b)


def _layer(x2, p2, B, S, w_in, gla_w_gate_up, gla_b_gate, gla_norm_w, pool_w, pool_scale, w_out,
           ln1_w, ln1_b, peer_w_query, peer_sub_keys, peer_u, peer_v, ple_w_gate, ple_w_proj, ln2_w, ln2_b):
    glr0 = COL_R
    w_main = jnp.concatenate([w_in[:, :glr0], w_in[:, glr0 + GLA_GATE_RANK:]], axis=1).astype(BF16)
    w_glr = jnp.pad(w_in[:, glr0:glr0 + GLA_GATE_RANK], ((0, 0), (0, LANES - GLA_GATE_RANK))).astype(BF16)
    proj, glr = _proj(x2, w_main, w_glr)

    y_pool = _pool(proj, pool_w.astype(BF16), pool_scale.reshape(1, POOL_WIDTH), S)

    wg = jnp.pad(gla_w_gate_up, ((0, LANES - GLA_GATE_RANK), (0, 0))).astype(BF16)
    y_gla = _gla(proj, glr, wg, gla_b_gate.reshape(1, GLA_KEY_WIDTH),
                 gla_norm_w.reshape(1, GLA_WIDTH), B, S)

    x1, x1t = _outproj(y_pool, y_gla, x2, w_out.astype(BF16),
                       ln1_w.reshape(1, D_MODEL), ln1_b.reshape(1, D_MODEL))

    keys = peer_sub_keys.reshape(2 * PEER_HEADS, PEER_NKEYS, PEER_HALF)
    keys_hi, keys_lo = _split_bf16(keys)
    c1, n1, e2, r2 = _query(x1t, peer_w_query.T.astype(BF16), keys_hi, keys_lo)
    n_exp = peer_v.shape[0]
    vt_tiles = peer_v.reshape(n_exp // PEER_TE, PEER_TE, D_MODEL).transpose(0, 2, 1).astype(BF16)
    y_ffn = _peer(x1t, peer_u.astype(BF16), vt_tiles, c1, n1, e2, r2)

    return _final(x1, y_ffn, p2, ple_w_gate.astype(BF16), ple_w_proj.astype(BF16),
                  ln2_w.reshape(1, D_MODEL), ln2_b.reshape(1, D_MODEL))


def kernel(x, p, w_in, gla_w_gate_up, gla_b_gate, gla_norm_w, pool_w, pool_scale, w_out, ln1_w, ln1_b,
           peer_w_query, peer_sub_keys, peer_u, peer_v, ple_w_gate, ple_w_proj, ln2_w, ln2_b):
    B, S, D = x.shape
    x2 = x.reshape(B * S, D)
    for i in range(w_in.shape[0]):
        x2 = _layer(x2, p[i].reshape(B * S, PLE_DIM), B, S, w_in[i], gla_w_gate_up[i], gla_b_gate[i],
                    gla_norm_w[i], pool_w[i], pool_scale[i], w_out[i], ln1_w[i], ln1_b[i],
                    peer_w_query[i], peer_sub_keys[i], peer_u[i], peer_v[i], ple_w_gate[i],
                    ple_w_proj[i], ln2_w[i], ln2_b[i])
    return x2.reshape(B, S, D)
```

```python
import functools
import math

import jax
import jax.numpy as jnp
from jax import lax
from jax.experimental import pallas as pl
from jax.experimental.pallas import tpu as pltpu

F32 = jnp.float32
BF16 = jnp.bfloat16

D_MODEL = 2048
PLE_DIM = 256
POOL_WIDTH = 1024
POOL_WINDOWS = (2, 4, 8, 16)
POOL_GC = 256
POOL_HALO = 16
GLA_WIDTH = 1024
GLA_HEADS = 4
GLA_DV = 256
GLA_DK = 128
GLA_KEY_WIDTH = 512
GLA_GATE_RANK = 16
GLA_GATE_TEMP = 16.0
GLA_CHUNK = 64
PEER_HEADS = 8
PEER_NKEYS = 128
PEER_HALF = 128
PEER_TOPK = 16
DEPTH = 1
ALPHA = float((2 * DEPTH) ** 0.25)
LN_EPS = 1e-5
RMS_EPS = 1e-6
LANES = 128
NEG_INF = float("-inf")

COL_Q = POOL_WIDTH
COL_K = COL_Q + GLA_KEY_WIDTH
COL_V = COL_K + GLA_KEY_WIDTH
COL_R = COL_V + GLA_WIDTH
PROJ_COLS = COL_R + GLA_WIDTH


def _params(sem, vmem_mib):
    return pltpu.CompilerParams(dimension_semantics=sem, vmem_limit_bytes=vmem_mib * 1024 * 1024)


def _proj_kernel(x_ref, w_ref, wg_ref, o_ref, glr_ref, xb_ref):
    @pl.when(pl.program_id(1) == 0)
    def _():
        xb = x_ref[...].astype(BF16)
        xb_ref[...] = xb
        glr_ref[...] = jnp.dot(xb, wg_ref[...], preferred_element_type=F32)

    o_ref[...] = jnp.dot(xb_ref[...], w_ref[...], preferred_element_type=F32).astype(o_ref.dtype)


def _proj(x2, w_main, w_glr, tm=1024, tn=2048):
    T = x2.shape[0]
    return pl.pallas_call(
        _proj_kernel,
        grid=(T // tm, PROJ_COLS // tn),
        in_specs=[
            pl.BlockSpec((tm, D_MODEL), lambda i, n: (i, 0)),
            pl.BlockSpec((D_MODEL, tn), lambda i, n: (0, n)),
            pl.BlockSpec((D_MODEL, LANES), lambda i, n: (0, 0)),
        ],
        out_specs=[
            pl.BlockSpec((tm, tn), lambda i, n: (i, n)),
            pl.BlockSpec((tm, LANES), lambda i, n: (i, 0)),
        ],
        out_shape=[
            jax.ShapeDtypeStruct((T, PROJ_COLS), BF16),
            jax.ShapeDtypeStruct((T, LANES), F32),
        ],
        scratch_shapes=[pltpu.VMEM((tm, D_MODEL), BF16)],
        compiler_params=_params(("parallel", "arbitrary"), 52),
        name="proj",
    )(x2, w_main, w_glr)


def _pool_kernel(u_ref, halo_ref, w_ref, sc_ref, o_ref, ext_ref, *, tiles_per_seq, tm):
    t = pl.program_id(0) % tiles_per_seq
    halo = jnp.where(t == 0, 0.0, halo_ref[...].astype(F32))
    ext_ref[0:POOL_HALO, :] = halo
    ext_ref[POOL_HALO:, :] = u_ref[...].astype(F32)
    pos = t * tm + lax.broadcasted_iota(jnp.int32, (tm, 1), 0)
    for g, w in enumerate(POOL_WINDOWS):
        cols = slice(g * POOL_GC, (g + 1) * POOL_GC)
        u = ext_ref[POOL_HALO:, cols]
        acc = u
        for j in range(1, w):
            acc = acc + ext_ref[POOL_HALO - j:POOL_HALO - j + tm, cols]
        cnt = jnp.minimum(pos + 1, w).astype(F32)
        d = acc / cnt - u
        y = jnp.dot(d.astype(BF16), w_ref[g], preferred_element_type=F32)
        o_ref[:, cols] = (y * sc_ref[:, cols]).astype(o_ref.dtype)


def _pool(proj, pool_w, pool_scale, S, tm=512):
    T = proj.shape[0]
    hb = tm // POOL_HALO
    return pl.pallas_call(
        functools.partial(_pool_kernel, tiles_per_seq=S // tm, tm=tm),
        grid=(T // tm,),
        in_specs=[
            pl.BlockSpec((tm, POOL_WIDTH), lambda i: (i, 0)),
            pl.BlockSpec((POOL_HALO, POOL_WIDTH), lambda i: (jnp.maximum(i * hb - 1, 0), 0)),
            pl.BlockSpec((len(POOL_WINDOWS), POOL_GC, POOL_GC), lambda i: (0, 0, 0)),
            pl.BlockSpec((1, POOL_WIDTH), lambda i: (0, 0)),
        ],
        out_specs=pl.BlockSpec((tm, POOL_WIDTH), lambda i: (i, 0)),
        out_shape=jax.ShapeDtypeStruct((T, POOL_WIDTH), BF16),
        scratch_shapes=[pltpu.VMEM((POOL_HALO + tm, POOL_WIDTH), F32)],
        compiler_params=_params(("parallel",), 32),
        name="pool",
    )(proj, proj, pool_w, pool_scale)


def _gla_kernel(q_ref, k_ref, v_ref, r_ref, glr_ref, wg_ref, bg_ref, nw_ref, o_ref, s_ref, *, n_chunks):
    @pl.when(pl.program_id(1) == 0)
    def _():
        s_ref[...] = jnp.zeros_like(s_ref)

    C = GLA_CHUNK
    row = lax.broadcasted_iota(jnp.int32, (C, C), 0)
    col = lax.broadcasted_iota(jnp.int32, (C, C), 1)
    causal = col <= row
    tril = causal.astype(BF16)
    wg = wg_ref[...]
    bg = bg_ref[...]
    nt = (((1,), (1,)), ((), ()))
    for c in range(n_chunks):
        rows = slice(c * C, (c + 1) * C)
        z = jnp.dot(glr_ref[rows, :].astype(BF16), wg, preferred_element_type=F32) + bg
        g = jax.nn.log_sigmoid(z) / GLA_GATE_TEMP
        g_hi, g_lo = _split_bf16(g)
        b_all = (jnp.dot(tril, g_hi, preferred_element_type=F32)
                 + jnp.dot(tril, g_lo, preferred_element_type=F32))
        for h in range(GLA_HEADS):
            kc = slice(h * GLA_DK, (h + 1) * GLA_DK)
            vc = slice(h * GLA_DV, (h + 1) * GLA_DV)
            b = b_all[:, kc]
            b_last = b[C - 1:C, :]
            b_mid = b[C // 2 - 1:C // 2, :]
            q = q_ref[rows, kc].astype(F32) * (GLA_DK ** -0.5)
            k = k_ref[rows, kc].astype(F32)
            v = v_ref[rows, vc]
            q_state = (q * jnp.exp(b)).astype(BF16)
            q_in = (q * jnp.exp(b - b_mid)).astype(BF16)
            k_in = (k * jnp.exp(b_mid - b)).astype(BF16)
            k_out = k * jnp.exp(b_last - b)
            attn = lax.dot_general(q_in, k_in, nt, preferred_element_type=F32)
            attn = jnp.where(causal, attn, 0.0).astype(BF16)
            s = s_ref[h]
            o = (jnp.dot(attn, v, preferred_element_type=F32)
                 + jnp.dot(q_state, s.astype(BF16), preferred_element_type=F32))
            decay = jnp.transpose(jnp.broadcast_to(jnp.exp(b_last), (C, GLA_DK)))[:, 0:1]
            s_ref[h] = decay * s + jnp.dot(jnp.transpose(k_out).astype(BF16), v, preferred_element_type=F32)
            o = o * lax.rsqrt(jnp.mean(jnp.square(o), axis=-1, keepdims=True) + RMS_EPS)
            o = o * nw_ref[:, vc]
            r = r_ref[rows, vc].astype(F32)
            o_ref[rows, vc] = (o * (r * jax.nn.sigmoid(r))).astype(o_ref.dtype)


def _gla(proj, glr, wg, bg, nw, B, S, L=512):
    T = proj.shape[0]
    nl = S // L
    rb = lambda b, l: b * nl + l
    return pl.pallas_call(
        functools.partial(_gla_kernel, n_chunks=L // GLA_CHUNK),
        grid=(B, nl),
        in_specs=[
            pl.BlockSpec((L, GLA_KEY_WIDTH), lambda b, l: (rb(b, l), COL_Q // GLA_KEY_WIDTH)),
            pl.BlockSpec((L, GLA_KEY_WIDTH), lambda b, l: (rb(b, l), COL_K // GLA_KEY_WIDTH)),
            pl.BlockSpec((L, GLA_WIDTH), lambda b, l: (rb(b, l), COL_V // GLA_WIDTH)),
            pl.BlockSpec((L, GLA_WIDTH), lambda b, l: (rb(b, l), COL_R // GLA_WIDTH)),
            pl.BlockSpec((L, LANES), lambda b, l: (rb(b, l), 0)),
            pl.BlockSpec((LANES, GLA_KEY_WIDTH), lambda b, l: (0, 0)),
            pl.BlockSpec((1, GLA_KEY_WIDTH), lambda b, l: (0, 0)),
            pl.BlockSpec((1, GLA_WIDTH), lambda b, l: (0, 0)),
        ],
        out_specs=pl.BlockSpec((L, GLA_WIDTH), lambda b, l: (rb(b, l), 0)),
        out_shape=jax.ShapeDtypeStruct((T, GLA_WIDTH), BF16),
        scratch_shapes=[pltpu.VMEM((GLA_HEADS, GLA_DK, GLA_DV), F32)],
        compiler_params=_params(("parallel", "arbitrary"), 32),
        name="gla",
    )(proj, proj, proj, proj, glr, wg, bg, nw)


def _layer_norm(h, w, b):
    mu = jnp.mean(h, axis=-1, keepdims=True)
    hc = h - mu
    var = jnp.mean(jnp.square(hc), axis=-1, keepdims=True)
    return hc * lax.rsqrt(var + LN_EPS) * w + b


def _outproj_kernel(yp_ref, yg_ref, x_ref, w_ref, lw_ref, lb_ref, x1_ref, x1t_ref):
    mix = (jnp.dot(yp_ref[...], w_ref[0:POOL_WIDTH, :], preferred_element_type=F32)
           + jnp.dot(yg_ref[...], w_ref[POOL_WIDTH:, :], preferred_element_type=F32))
    x1 = _layer_norm(ALPHA * x_ref[...] + mix, lw_ref[...], lb_ref[...])
    x1_ref[...] = x1
    x1t_ref[...] = jnp.transpose(x1).astype(BF16)


def _outproj(y_pool, y_gla, x2, w_out, ln_w, ln_b, tm=512):
    T = x2.shape[0]
    return pl.pallas_call(
        _outproj_kernel,
        grid=(T // tm,),
        in_specs=[
            pl.BlockSpec((tm, POOL_WIDTH), lambda i: (i, 0)),
            pl.BlockSpec((tm, GLA_WIDTH), lambda i: (i, 0)),
            pl.BlockSpec((tm, D_MODEL), lambda i: (i, 0)),
            pl.BlockSpec((D_MODEL, D_MODEL), lambda i: (0, 0), pipeline_mode=pl.Buffered(1)),
            pl.BlockSpec((1, D_MODEL), lambda i: (0, 0)),
            pl.BlockSpec((1, D_MODEL), lambda i: (0, 0)),
        ],
        out_specs=[
            pl.BlockSpec((tm, D_MODEL), lambda i: (i, 0)),
            pl.BlockSpec((D_MODEL, tm), lambda i: (0, i)),
        ],
        out_shape=[
            jax.ShapeDtypeStruct((T, D_MODEL), F32),
            jax.ShapeDtypeStruct((D_MODEL, T), BF16),
        ],
        compiler_params=_params(("parallel",), 48),
        name="outproj",
    )(y_pool, y_gla, x2, w_out, ln_w, ln_b)


N_SORT = PEER_TOPK + 1


def _sort_network(n):
    pairs = []

    def merge(lo, m, r):
        step = 2 * r
        if step < m:
            merge(lo, m, step)
            merge(lo + r, m, step)
            pairs.extend((i, i + r) for i in range(lo + r, lo + m - r, step))
        else:
            pairs.append((lo, lo + r))

    def sort(lo, m):
        if m > 1:
            sort(lo, m // 2)
            sort(lo + m // 2, m // 2)
            merge(lo, m, 1)

    sort(0, n)
    return tuple(pairs)


def _pop_sorted(v, n_out):
    nv = len(v)
    width = 1 << (nv - 1).bit_length()
    for i, j in _sort_network(width):
        if j < nv:
            v[i], v[j] = jnp.maximum(v[i], v[j]), jnp.minimum(v[i], v[j])
    tops = []
    for kk in range(n_out):
        m = jnp.max(v[0], axis=0, keepdims=True)
        tops.append(m)
        hit = v[0] == m
        for k in range(min(n_out - 1 - kk, nv)):
            v[k] = jnp.where(hit, v[k + 1] if k + 1 < nv else NEG_INF, v[k])
    return tops


def _sorted_top(arr):
    return _pop_sorted([arr[k:k + SUBLANES] for k in range(0, arr.shape[0], SUBLANES)], N_SORT)


def _rows_to_tile(rows):
    rid = lax.broadcasted_iota(jnp.int32, (SUBLANES, LANES), 0)
    tile = jnp.full((SUBLANES, LANES), NEG_INF, F32)
    for k, r in enumerate(rows):
        tile = jnp.where(rid == k, r, tile)
    return tile


def _pair_stats(a, b):
    r8 = lax.broadcasted_iota(jnp.int32, (SUBLANES, LANES), 0)
    b_lo, b_hi, a_hi = _rows_to_tile(b[0:8]), _rows_to_tile(b[8:16]), _rows_to_tile(a[8:16])
    p2 = jnp.where(r8 < 5, a[2] + b_lo, jnp.where(r8 == 5, a[16] + b[0], jnp.where(r8 == 6, a[0] + b[16], NEG_INF)))
    pieces = [
        a[0] + b_lo, a[0] + b_hi, a[1] + b_lo, p2,
        jnp.where(r8 < 4, a[3] + b_lo, NEG_INF),
        jnp.where(r8 < 3, a[4] + b_lo, NEG_INF),
        jnp.where(r8 < 2, a[5] + b_lo, NEG_INF),
        jnp.where(r8 < 2, a[6] + b_lo, NEG_INF),
        jnp.where(r8 < 2, a[7] + b_lo, NEG_INF),
        a_hi + b[0],
    ]
    sums = _pop_sorted(pieces, N_SORT)
    top16 = jnp.concatenate([_rows_to_tile(sums[0:8]), _rows_to_tile(sums[8:16])], axis=0)
    z = jnp.sum(jnp.exp(top16 - sums[0]), axis=0, keepdims=True)
    return 0.5 * (sums[PEER_TOPK - 1] + sums[PEER_TOPK]), 1.0 / z


def _split_bf16(x):
    hi = x.astype(BF16)
    return hi, (x - hi.astype(F32)).astype(BF16)


def _query_kernel(x1t_ref, wq_ref, khi_ref, klo_ref, c1_ref, n1_ref, e2_ref, r2_ref, q_ref, s1_scr, s2_scr, *, tm):
    q_ref[...] = jnp.dot(wq_ref[...], x1t_ref[...], preferred_element_type=F32)
    for h in range(PEER_HEADS):
        for p in range(2):
            hp = 2 * h + p
            q_hi, q_lo = _split_bf16(q_ref[hp * PEER_HALF:(hp + 1) * PEER_HALF, :])
            k_hi, k_lo = khi_ref[hp], klo_ref[hp]
            sc = (jnp.dot(k_hi, q_hi, preferred_element_type=F32)
                  + jnp.dot(k_hi, q_lo, preferred_element_type=F32)
                  + jnp.dot(k_lo, q_hi, preferred_element_type=F32))
            if p == 0:
                s1_scr[...] = sc
            else:
                s2_scr[...] = sc
        for tc in range(tm // LANES):
            lanes = slice(tc * LANES, (tc + 1) * LANES)
            s1 = s1_scr[:, lanes]
            s2 = s2_scr[:, lanes]
            ta, tb = _sorted_top(s1), _sorted_top(s2)
            tau, rz = _pair_stats(ta, tb)
            n1 = jnp.zeros_like(s1)
            r2 = jnp.zeros_like(s2)
            for jj in range(PEER_TOPK):
                n1 = jnp.where(s1 >= tau - tb[jj], float(jj + 1), n1)
                r2 = jnp.where(s2 < tb[jj], float(jj + 1), r2)
            c1_ref[h, tc] = jnp.exp(s1 - ta[0])
            n1_ref[h, tc] = n1
            e2_ref[h, :, lanes] = (jnp.exp(s2 - tb[0]) * rz).astype(BF16)
            r2_ref[h, :, lanes] = r2.astype(BF16)


def _query(x1t, wq_t, keys_hi, keys_lo, tm=256):
    T = x1t.shape[1]
    nc = tm // LANES
    row_spec = pl.BlockSpec((PEER_HEADS, nc, PEER_NKEYS, LANES), lambda i: (0, i, 0, 0))
    col_spec = pl.BlockSpec((PEER_HEADS, PEER_NKEYS, tm), lambda i: (0, 0, i))
    key_spec = pl.BlockSpec((2 * PEER_HEADS, PEER_NKEYS, PEER_HALF), lambda i: (0, 0, 0))
    row_shape = jax.ShapeDtypeStruct((PEER_HEADS, T // LANES, PEER_NKEYS, LANES), F32)
    col_shape = jax.ShapeDtypeStruct((PEER_HEADS, PEER_NKEYS, T), BF16)
    return pl.pallas_call(
        functools.partial(_query_kernel, tm=tm),
        grid=(T // tm,),
        in_specs=[
            pl.BlockSpec((D_MODEL, tm), lambda i: (0, i)),
            pl.BlockSpec((D_MODEL, D_MODEL), lambda i: (0, 0), pipeline_mode=pl.Buffered(1)),
            key_spec, key_spec,
        ],
        out_specs=[row_spec, row_spec, col_spec, col_spec],
        out_shape=[row_shape, row_shape, col_shape, col_shape],
        scratch_shapes=[pltpu.VMEM((D_MODEL, tm), F32), pltpu.VMEM((PEER_NKEYS, tm), F32),
                        pltpu.VMEM((PEER_NKEYS, tm), F32)],
        compiler_params=_params(("parallel",), 48),
        name="query",
    )(x1t, wq_t, keys_hi, keys_lo)


PEER_SB = 64
PEER_RG = 2
SUBLANES = 8
PEER_TE = 1024


def _gelu(x):
    return 0.5 * x * (1.0 + lax.erf(x * (1.0 / math.sqrt(2.0))))


def _tables_kernel(u_ref, v_ref, ub_ref, vt_ref):
    ub_ref[...] = u_ref[...].astype(BF16)
    vt_ref[...] = jnp.transpose(v_ref[...]).astype(BF16)


def _tables(peer_u, peer_v):
    n_exp = peer_u.shape[0]
    nj = n_exp // PEER_TE
    return pl.pallas_call(
        _tables_kernel,
        grid=(nj,),
        in_specs=[pl.BlockSpec((PEER_TE, D_MODEL), lambda j: (j, 0)),
                  pl.BlockSpec((PEER_TE, D_MODEL), lambda j: (j, 0))],
        out_specs=[pl.BlockSpec((PEER_TE, D_MODEL), lambda j: (j, 0)),
                   pl.BlockSpec((None, D_MODEL, PEER_TE), lambda j: (j, 0, 0))],
        out_shape=[jax.ShapeDtypeStruct((n_exp, D_MODEL), BF16),
                   jax.ShapeDtypeStruct((nj, D_MODEL, PEER_TE), BF16)],
        compiler_params=_params(("parallel",), 56),
        name="tables",
    )(peer_u, peer_v)


def _bcast_row_bf16(tile, ri, rows):
    packed = jnp.broadcast_to(tile[ri:ri + 1, :], (2 * SUBLANES, LANES)).astype(BF16)
    return jnp.concatenate([packed] * (rows // (2 * SUBLANES)), axis=0)


def _peer_kernel(x1t_ref, u_ref, vt_ref, c1_ref, n1_ref, e2_ref, r2_ref, y_ref,
                 acc_ref, st_scr, ht_scr, *, tm, te):
    j = pl.program_id(1)
    n1 = te // PEER_NKEYS
    grows = PEER_RG * PEER_NKEYS

    @pl.when(j == 0)
    def _():
        acc_ref[...] = jnp.zeros_like(acc_ref)

    st_scr[...] = jnp.dot(u_ref[...], x1t_ref[...], preferred_element_type=F32)

    nsb = PEER_NKEYS // PEER_SB
    tile_rows = pl.ds(pl.multiple_of(j * n1, SUBLANES), n1)
    for gi in range(n1 // PEER_RG):
        crows = slice(gi * grows, (gi + 1) * grows)
        for tc in range(tm // LANES):
            lanes = slice(tc * LANES, (tc + 1) * LANES)
            g = [[jnp.zeros((PEER_SB, LANES), BF16) for _ in range(nsb)] for _ in range(PEER_RG)]
            for h in range(PEER_HEADS):
                c1_t = c1_ref[h, tc, tile_rows, :]
                n1_t = n1_ref[h, tc, tile_rows, :]
                c1b = [_bcast_row_bf16(c1_t, gi * PEER_RG + r, PEER_SB) for r in range(PEER_RG)]
                n1b = [_bcast_row_bf16(n1_t, gi * PEER_RG + r, PEER_SB) for r in range(PEER_RG)]
                for sb in range(nsb):
                    rows = slice(sb * PEER_SB, (sb + 1) * PEER_SB)
                    r2c = r2_ref[h, rows, lanes]
                    e2c = e2_ref[h, rows, lanes]
                    for r in range(PEER_RG):
                        g[r][sb] = g[r][sb] + c1b[r] * jnp.where(r2c < n1b[r], e2c, jnp.zeros_like(e2c))
            for r in range(PEER_RG):
                for sb in range(nsb):
                    base = gi * grows + r * PEER_NKEYS + sb * PEER_SB
                    srows = slice(base, base + PEER_SB)
                    ht_scr[srows, lanes] = g[r][sb] * _gelu(st_scr[srows, lanes].astype(BF16))
    acc_ref[...] += jnp.dot(vt_ref[...], ht_scr[...], preferred_element_type=F32)

    @pl.when(j == pl.num_programs(1) - 1)
    def _():
        y_ref[...] = jnp.transpose(acc_ref[...])


def _peer(x1t, u_tab, vt_tiles, c1, n1, e2, r2, tm=512):
    T = x1t.shape[1]
    nj, _, te = vt_tiles.shape
    once = pl.Buffered(1)
    row_spec = pl.BlockSpec((PEER_HEADS, tm // LANES, PEER_NKEYS, LANES), lambda i, j: (0, i, 0, 0),
                            pipeline_mode=once)
    col_spec = pl.BlockSpec((PEER_HEADS, PEER_NKEYS, tm), lambda i, j: (0, 0, i), pipeline_mode=once)
    return pl.pallas_call(
        functools.partial(_peer_kernel, tm=tm, te=te),
        grid=(T // tm, nj),
        in_specs=[
            pl.BlockSpec((D_MODEL, tm), lambda i, j: (0, i), pipeline_mode=once),
            pl.BlockSpec((te, D_MODEL), lambda i, j: (j, 0)),
            pl.BlockSpec((None, D_MODEL, te), lambda i, j: (j, 0, 0)),
            row_spec, row_spec, col_spec, col_spec,
        ],
        out_specs=pl.BlockSpec((tm, D_MODEL), lambda i, j: (i, 0)),
        out_shape=jax.ShapeDtypeStruct((T, D_MODEL), F32),
        scratch_shapes=[
            pltpu.VMEM((D_MODEL, tm), F32),
            pltpu.VMEM((te, tm), F32),
            pltpu.VMEM((te, tm), BF16),
        ],
        compiler_params=_params(("parallel", "arbitrary"), 56),
        name="peer",
    )(x1t, u_tab, vt_tiles, c1, n1, e2, r2)


def _final_kernel(x1_ref, y_ref, p_ref, wg_ref, wp_ref, lw_ref, lb_ref, o_ref):
    x1 = x1_ref[...]
    gate = jax.nn.sigmoid(jnp.dot(x1.astype(BF16), wg_ref[...], preferred_element_type=F32))
    emb = jnp.dot(p_ref[...].astype(BF16), wp_ref[...], preferred_element_type=F32)
    o_ref[...] = _layer_norm(ALPHA * x1 + y_ref[...] + gate * emb, lw_ref[...], lb_ref[...])


def _final(x1, y_ffn, p2, w_gate, w_proj, ln_w, ln_b, tm=512):
    T = x1.shape[0]
    return pl.pallas_call(
        _final_kernel,
        grid=(T // tm,),
        in_specs=[
            pl.BlockSpec((tm, D_MODEL), lambda i: (i, 0)),
            pl.BlockSpec((tm, D_MODEL), lambda i: (i, 0)),
            pl.BlockSpec((tm, PLE_DIM), lambda i: (i, 0)),
            pl.BlockSpec((D_MODEL, D_MODEL), lambda i: (0, 0), pipeline_mode=pl.Buffered(1)),
            pl.BlockSpec((PLE_DIM, D_MODEL), lambda i: (0, 0), pipeline_mode=pl.Buffered(1)),
            pl.BlockSpec((1, D_MODEL), lambda i: (0, 0)),
            pl.BlockSpec((1, D_MODEL), lambda i: (0, 0)),
        ],
        out_specs=pl.BlockSpec((tm, D_MODEL), lambda i: (i, 0)),
        out_shape=jax.ShapeDtypeStruct((T, D_MODEL), F32),
        compiler_params=_params(("parallel",), 48),
        name="final",
    )(x1, y_ffn, p2, w_gate, w_proj, ln_w, ln_b)


def _layer(x2, p2, B, S, w_in, gla_w_gate_up, gla_b_gate, gla_norm_w, pool_w, pool_scale, w_out,
           ln1_w, ln1_b, peer_w_query, peer_sub_keys, peer_u, peer_v, ple_w_gate, ple_w_proj, ln2_w, ln2_b):
    glr0 = COL_R
    w_main = jnp.concatenate([w_in[:, :glr0], w_in[:, glr0 + GLA_GATE_RANK:]], axis=1).astype(BF16)
    w_glr = jnp.pad(w_in[:, glr0:glr0 + GLA_GATE_RANK], ((0, 0), (0, LANES - GLA_GATE_RANK))).astype(BF16)
    proj, glr = _proj(x2, w_main, w_glr)

    y_pool = _pool(proj, pool_w.astype(BF16), pool_scale.reshape(1, POOL_WIDTH), S)

    wg = jnp.pad(gla_w_gate_up, ((0, LANES - GLA_GATE_RANK), (0, 0))).astype(BF16)
    y_gla = _gla(proj, glr, wg, gla_b_gate.reshape(1, GLA_KEY_WIDTH),
                 gla_norm_w.reshape(1, GLA_WIDTH), B, S)

    x1, x1t = _outproj(y_pool, y_gla, x2, w_out.astype(BF16),
                       ln1_w.reshape(1, D_MODEL), ln1_b.reshape(1, D_MODEL))

    keys = peer_sub_keys.reshape(2 * PEER_HEADS, PEER_NKEYS, PEER_HALF)
    keys_hi, keys_lo = _split_bf16(keys)
    c1, n1, e2, r2 = _query(x1t, peer_w_query.T.astype(BF16), keys_hi, keys_lo)
    u_bf16, vt_tiles = _tables(peer_u, peer_v)
    y_ffn = _peer(x1t, u_bf16, vt_tiles, c1, n1, e2, r2)

    return _final(x1, y_ffn, p2, ple_w_gate.astype(BF16), ple_w_proj.astype(BF16),
                  ln2_w.reshape(1, D_MODEL), ln2_b.reshape(1, D_MODEL))


def kernel(x, p, w_in, gla_w_gate_up, gla_b_gate, gla_norm_w, pool_w, pool_scale, w_out, ln1_w, ln1_b,
           peer_w_query, peer_sub_keys, peer_u, peer_v, ple_w_gate, ple_w_proj, ln2_w, ln2_b):
    B, S, D = x.shape
    x2 = x.reshape(B * S, D)
    for i in range(w_in.shape[0]):
        x2 = _layer(x2, p[i].reshape(B * S, PLE_DIM), B, S, w_in[i], gla_w_gate_up[i], gla_b_gate[i],
                    gla_norm_w[i], pool_w[i], pool_scale[i], w_out[i], ln1_w[i], ln1_b[i],
                    peer_w_query[i], peer_sub_keys[i], peer_u[i], peer_v[i], ple_w_gate[i],
                    ple_w_proj[i], ln2_w[i], ln2_b[i])
    return x2.reshape(B, S, D)
```

```python
import functools
import math

import jax
import jax.numpy as jnp
from jax import lax
from jax.experimental import pallas as pl
from jax.experimental.pallas import tpu as pltpu

F32 = jnp.float32
BF16 = jnp.bfloat16

D_MODEL = 2048
PLE_DIM = 256
POOL_WIDTH = 1024
POOL_WINDOWS = (2, 4, 8, 16)
POOL_GC = 256
POOL_HALO = 16
GLA_WIDTH = 1024
GLA_HEADS = 4
GLA_DV = 256
GLA_DK = 128
GLA_KEY_WIDTH = 512
GLA_GATE_RANK = 16
GLA_GATE_TEMP = 16.0
GLA_CHUNK = 64
PEER_HEADS = 8
PEER_NKEYS = 128
PEER_HALF = 128
PEER_TOPK = 16
DEPTH = 1
ALPHA = float((2 * DEPTH) ** 0.25)
LN_EPS = 1e-5
RMS_EPS = 1e-6
LANES = 128
NEG_INF = float("-inf")

COL_Q = POOL_WIDTH
COL_K = COL_Q + GLA_KEY_WIDTH
COL_V = COL_K + GLA_KEY_WIDTH
COL_R = COL_V + GLA_WIDTH
PROJ_COLS = COL_R + GLA_WIDTH


def _params(sem, vmem_mib):
    return pltpu.CompilerParams(dimension_semantics=sem, vmem_limit_bytes=vmem_mib * 1024 * 1024)


def _proj_kernel(x_ref, w_ref, wg_ref, o_ref, glr_ref, xb_ref):
    @pl.when(pl.program_id(1) == 0)
    def _():
        xb = x_ref[...].astype(BF16)
        xb_ref[...] = xb
        glr_ref[...] = jnp.dot(xb, wg_ref[...], preferred_element_type=F32)

    o_ref[...] = jnp.dot(xb_ref[...], w_ref[...], preferred_element_type=F32).astype(o_ref.dtype)


def _proj(x2, w_main, w_glr, tm=1024, tn=2048):
    T = x2.shape[0]
    return pl.pallas_call(
        _proj_kernel,
        grid=(T // tm, PROJ_COLS // tn),
        in_specs=[
            pl.BlockSpec((tm, D_MODEL), lambda i, n: (i, 0)),
            pl.BlockSpec((D_MODEL, tn), lambda i, n: (0, n)),
            pl.BlockSpec((D_MODEL, LANES), lambda i, n: (0, 0)),
        ],
        out_specs=[
            pl.BlockSpec((tm, tn), lambda i, n: (i, n)),
            pl.BlockSpec((tm, LANES), lambda i, n: (i, 0)),
        ],
        out_shape=[
            jax.ShapeDtypeStruct((T, PROJ_COLS), BF16),
            jax.ShapeDtypeStruct((T, LANES), F32),
        ],
        scratch_shapes=[pltpu.VMEM((tm, D_MODEL), BF16)],
        compiler_params=_params(("parallel", "arbitrary"), 52),
        name="proj",
    )(x2, w_main, w_glr)


def _pool_kernel(u_ref, halo_ref, w_ref, sc_ref, o_ref, ext_ref, *, tiles_per_seq, tm):
    t = pl.program_id(0) % tiles_per_seq
    halo = jnp.where(t == 0, 0.0, halo_ref[...].astype(F32))
    ext_ref[0:POOL_HALO, :] = halo
    ext_ref[POOL_HALO:, :] = u_ref[...].astype(F32)
    pos = t * tm + lax.broadcasted_iota(jnp.int32, (tm, 1), 0)
    for g, w in enumerate(POOL_WINDOWS):
        cols = slice(g * POOL_GC, (g + 1) * POOL_GC)
        u = ext_ref[POOL_HALO:, cols]
        acc = u
        for j in range(1, w):
            acc = acc + ext_ref[POOL_HALO - j:POOL_HALO - j + tm, cols]
        cnt = jnp.minimum(pos + 1, w).astype(F32)
        d = acc / cnt - u
        y = jnp.dot(d.astype(BF16), w_ref[g], preferred_element_type=F32)
        o_ref[:, cols] = (y * sc_ref[:, cols]).astype(o_ref.dtype)


def _pool(proj, pool_w, pool_scale, S, tm=512):
    T = proj.shape[0]
    hb = tm // POOL_HALO
    return pl.pallas_call(
        functools.partial(_pool_kernel, tiles_per_seq=S // tm, tm=tm),
        grid=(T // tm,),
        in_specs=[
            pl.BlockSpec((tm, POOL_WIDTH), lambda i: (i, 0)),
            pl.BlockSpec((POOL_HALO, POOL_WIDTH), lambda i: (jnp.maximum(i * hb - 1, 0), 0)),
            pl.BlockSpec((len(POOL_WINDOWS), POOL_GC, POOL_GC), lambda i: (0, 0, 0)),
            pl.BlockSpec((1, POOL_WIDTH), lambda i: (0, 0)),
        ],
        out_specs=pl.BlockSpec((tm, POOL_WIDTH), lambda i: (i, 0)),
        out_shape=jax.ShapeDtypeStruct((T, POOL_WIDTH), BF16),
        scratch_shapes=[pltpu.VMEM((POOL_HALO + tm, POOL_WIDTH), F32)],
        compiler_params=_params(("parallel",), 32),
        name="pool",
    )(proj, proj, pool_w, pool_scale)


def _gla_kernel(q_ref, k_ref, v_ref, r_ref, glr_ref, wg_ref, bg_ref, nw_ref, o_ref, s_ref, *, n_chunks):
    @pl.when(pl.program_id(1) == 0)
    def _():
        s_ref[...] = jnp.zeros_like(s_ref)

    C = GLA_CHUNK
    row = lax.broadcasted_iota(jnp.int32, (C, C), 0)
    col = lax.broadcasted_iota(jnp.int32, (C, C), 1)
    causal = col <= row
    tril = causal.astype(BF16)
    wg = wg_ref[...]
    bg = bg_ref[...]
    nt = (((1,), (1,)), ((), ()))
    for c in range(n_chunks):
        rows = slice(c * C, (c + 1) * C)
        z = jnp.dot(glr_ref[rows, :].astype(BF16), wg, preferred_element_type=F32) + bg
        g = jax.nn.log_sigmoid(z) / GLA_GATE_TEMP
        g_hi, g_lo = _split_bf16(g)
        b_all = (jnp.dot(tril, g_hi, preferred_element_type=F32)
                 + jnp.dot(tril, g_lo, preferred_element_type=F32))
        for h in range(GLA_HEADS):
            kc = slice(h * GLA_DK, (h + 1) * GLA_DK)
            vc = slice(h * GLA_DV, (h + 1) * GLA_DV)
            b = b_all[:, kc]
            b_last = b[C - 1:C, :]
            b_mid = b[C // 2 - 1:C // 2, :]
            q = q_ref[rows, kc].astype(F32) * (GLA_DK ** -0.5)
            k = k_ref[rows, kc].astype(F32)
            v = v_ref[rows, vc]
            q_state = (q * jnp.exp(b)).astype(BF16)
            q_in = (q * jnp.exp(b - b_mid)).astype(BF16)
            k_in = (k * jnp.exp(b_mid - b)).astype(BF16)
            k_out = k * jnp.exp(b_last - b)
            attn = lax.dot_general(q_in, k_in, nt, preferred_element_type=F32)
            attn = jnp.where(causal, attn, 0.0).astype(BF16)
            s = s_ref[h]
            o = (jnp.dot(attn, v, preferred_element_type=F32)
                 + jnp.dot(q_state, s.astype(BF16), preferred_element_type=F32))
            decay = jnp.transpose(jnp.broadcast_to(jnp.exp(b_last), (C, GLA_DK)))[:, 0:1]
            s_ref[h] = decay * s + jnp.dot(jnp.transpose(k_out).astype(BF16), v, preferred_element_type=F32)
            o = o * lax.rsqrt(jnp.mean(jnp.square(o), axis=-1, keepdims=True) + RMS_EPS)
            o = o * nw_ref[:, vc]
            r = r_ref[rows, vc].astype(F32)
            o_ref[rows, vc] = (o * (r * jax.nn.sigmoid(r))).astype(o_ref.dtype)


def _gla(proj, glr, wg, bg, nw, B, S, L=512):
    T = proj.shape[0]
    nl = S // L
    rb = lambda b, l: b * nl + l
    return pl.pallas_call(
        functools.partial(_gla_kernel, n_chunks=L // GLA_CHUNK),
        grid=(B, nl),
        in_specs=[
            pl.BlockSpec((L, GLA_KEY_WIDTH), lambda b, l: (rb(b, l), COL_Q // GLA_KEY_WIDTH)),
            pl.BlockSpec((L, GLA_KEY_WIDTH), lambda b, l: (rb(b, l), COL_K // GLA_KEY_WIDTH)),
            pl.BlockSpec((L, GLA_WIDTH), lambda b, l: (rb(b, l), COL_V // GLA_WIDTH)),
            pl.BlockSpec((L, GLA_WIDTH), lambda b, l: (rb(b, l), COL_R // GLA_WIDTH)),
            pl.BlockSpec((L, LANES), lambda b, l: (rb(b, l), 0)),
            pl.BlockSpec((LANES, GLA_KEY_WIDTH), lambda b, l: (0, 0)),
            pl.BlockSpec((1, GLA_KEY_WIDTH), lambda b, l: (0, 0)),
            pl.BlockSpec((1, GLA_WIDTH), lambda b, l: (0, 0)),
        ],
        out_specs=pl.BlockSpec((L, GLA_WIDTH), lambda b, l: (rb(b, l), 0)),
        out_shape=jax.ShapeDtypeStruct((T, GLA_WIDTH), BF16),
        scratch_shapes=[pltpu.VMEM((GLA_HEADS, GLA_DK, GLA_DV), F32)],
        compiler_params=_params(("parallel", "arbitrary"), 32),
        name="gla",
    )(proj, proj, proj, proj, glr, wg, bg, nw)


def _layer_norm(h, w, b):
    mu = jnp.mean(h, axis=-1, keepdims=True)
    hc = h - mu
    var = jnp.mean(jnp.square(hc), axis=-1, keepdims=True)
    return hc * lax.rsqrt(var + LN_EPS) * w + b


def _outproj_kernel(yp_ref, yg_ref, x_ref, w_ref, lw_ref, lb_ref, x1_ref, x1t_ref):
    mix = (jnp.dot(yp_ref[...], w_ref[0:POOL_WIDTH, :], preferred_element_type=F32)
           + jnp.dot(yg_ref[...], w_ref[POOL_WIDTH:, :], preferred_element_type=F32))
    x1 = _layer_norm(ALPHA * x_ref[...] + mix, lw_ref[...], lb_ref[...])
    x1_ref[...] = x1
    x1t_ref[...] = jnp.transpose(x1).astype(BF16)


def _outproj(y_pool, y_gla, x2, w_out, ln_w, ln_b, tm=512):
    T = x2.shape[0]
    return pl.pallas_call(
        _outproj_kernel,
        grid=(T // tm,),
        in_specs=[
            pl.BlockSpec((tm, POOL_WIDTH), lambda i: (i, 0)),
            pl.BlockSpec((tm, GLA_WIDTH), lambda i: (i, 0)),
            pl.BlockSpec((tm, D_MODEL), lambda i: (i, 0)),
            pl.BlockSpec((D_MODEL, D_MODEL), lambda i: (0, 0), pipeline_mode=pl.Buffered(1)),
            pl.BlockSpec((1, D_MODEL), lambda i: (0, 0)),
            pl.BlockSpec((1, D_MODEL), lambda i: (0, 0)),
        ],
        out_specs=[
            pl.BlockSpec((tm, D_MODEL), lambda i: (i, 0)),
            pl.BlockSpec((D_MODEL, tm), lambda i: (0, i)),
        ],
        out_shape=[
            jax.ShapeDtypeStruct((T, D_MODEL), F32),
            jax.ShapeDtypeStruct((D_MODEL, T), BF16),
        ],
        compiler_params=_params(("parallel",), 48),
        name="outproj",
    )(y_pool, y_gla, x2, w_out, ln_w, ln_b)


N_SORT = PEER_TOPK + 1


def _sort_network(n):
    pairs = []

    def merge(lo, m, r):
        step = 2 * r
        if step < m:
            merge(lo, m, step)
            merge(lo + r, m, step)
            pairs.extend((i, i + r) for i in range(lo + r, lo + m - r, step))
        else:
            pairs.append((lo, lo + r))

    def sort(lo, m):
        if m > 1:
            sort(lo, m // 2)
            sort(lo + m // 2, m // 2)
            merge(lo, m, 1)

    sort(0, n)
    return tuple(pairs)


def _pop_sorted(v, n_out):
    nv = len(v)
    width = 1 << (nv - 1).bit_length()
    for i, j in _sort_network(width):
        if j < nv:
            v[i], v[j] = jnp.maximum(v[i], v[j]), jnp.minimum(v[i], v[j])
    tops = []
    for kk in range(n_out):
        m = jnp.max(v[0], axis=0, keepdims=True)
        tops.append(m)
        hit = v[0] == m
        for k in range(min(n_out - 1 - kk, nv)):
            v[k] = jnp.where(hit, v[k + 1] if k + 1 < nv else NEG_INF, v[k])
    return tops


def _sorted_top(arr):
    return _pop_sorted([arr[k:k + SUBLANES] for k in range(0, arr.shape[0], SUBLANES)], N_SORT)


def _rows_to_tile(rows):
    rid = lax.broadcasted_iota(jnp.int32, (SUBLANES, LANES), 0)
    tile = jnp.full((SUBLANES, LANES), NEG_INF, F32)
    for k, r in enumerate(rows):
        tile = jnp.where(rid == k, r, tile)
    return tile


def _pair_stats(a, b):
    r8 = lax.broadcasted_iota(jnp.int32, (SUBLANES, LANES), 0)
    b_lo, b_hi, a_hi = _rows_to_tile(b[0:8]), _rows_to_tile(b[8:16]), _rows_to_tile(a[8:16])
    p2 = jnp.where(r8 < 5, a[2] + b_lo, jnp.where(r8 == 5, a[16] + b[0], jnp.where(r8 == 6, a[0] + b[16], NEG_INF)))
    pieces = [
        a[0] + b_lo, a[0] + b_hi, a[1] + b_lo, p2,
        jnp.where(r8 < 4, a[3] + b_lo, NEG_INF),
        jnp.where(r8 < 3, a[4] + b_lo, NEG_INF),
        jnp.where(r8 < 2, a[5] + b_lo, NEG_INF),
        jnp.where(r8 < 2, a[6] + b_lo, NEG_INF),
        jnp.where(r8 < 2, a[7] + b_lo, NEG_INF),
        a_hi + b[0],
    ]
    sums = _pop_sorted(pieces, N_SORT)
    top16 = jnp.concatenate([_rows_to_tile(sums[0:8]), _rows_to_tile(sums[8:16])], axis=0)
    z = jnp.sum(jnp.exp(top16 - sums[0]), axis=0, keepdims=True)
    return 0.5 * (sums[PEER_TOPK - 1] + sums[PEER_TOPK]), 1.0 / z


def _split_bf16(x):
    hi = x.astype(BF16)
    return hi, (x - hi.astype(F32)).astype(BF16)


def _query_kernel(x1t_ref, wq_ref, khi_ref, klo_ref, c1_ref, n1_ref, e2_ref, r2_ref, q_ref, s1_scr, s2_scr, *, tm):
    q_ref[...] = jnp.dot(wq_ref[...], x1t_ref[...], preferred_element_type=F32)
    for h in range(PEER_HEADS):
        for p in range(2):
            hp = 2 * h + p
            q_hi, q_lo = _split_bf16(q_ref[hp * PEER_HALF:(hp + 1) * PEER_HALF, :])
            k_hi, k_lo = khi_ref[hp], klo_ref[hp]
            sc = (jnp.dot(k_hi, q_hi, preferred_element_type=F32)
                  + jnp.dot(k_hi, q_lo, preferred_element_type=F32)
                  + jnp.dot(k_lo, q_hi, preferred_element_type=F32))
            if p == 0:
                s1_scr[...] = sc
            else:
                s2_scr[...] = sc
        for tc in range(tm // LANES):
            lanes = slice(tc * LANES, (tc + 1) * LANES)
            s1 = s1_scr[:, lanes]
            s2 = s2_scr[:, lanes]
            ta, tb = _sorted_top(s1), _sorted_top(s2)
            tau, rz = _pair_stats(ta, tb)
            n1 = jnp.zeros_like(s1)
            r2 = jnp.zeros_like(s2)
            for jj in range(PEER_TOPK):
                n1 = jnp.where(s1 >= tau - tb[jj], float(jj + 1), n1)
                r2 = jnp.where(s2 < tb[jj], float(jj + 1), r2)
            c1_ref[h, tc] = jnp.exp(s1 - ta[0])
            n1_ref[h, tc] = n1
            e2_ref[h, :, lanes] = (jnp.exp(s2 - tb[0]) * rz).astype(BF16)
            r2_ref[h, :, lanes] = r2.astype(BF16)


def _query(x1t, wq_t, keys_hi, keys_lo, tm=256):
    T = x1t.shape[1]
    nc = tm // LANES
    row_spec = pl.BlockSpec((PEER_HEADS, nc, PEER_NKEYS, LANES), lambda i: (0, i, 0, 0))
    col_spec = pl.BlockSpec((PEER_HEADS, PEER_NKEYS, tm), lambda i: (0, 0, i))
    key_spec = pl.BlockSpec((2 * PEER_HEADS, PEER_NKEYS, PEER_HALF), lambda i: (0, 0, 0))
    row_shape = jax.ShapeDtypeStruct((PEER_HEADS, T // LANES, PEER_NKEYS, LANES), F32)
    col_shape = jax.ShapeDtypeStruct((PEER_HEADS, PEER_NKEYS, T), BF16)
    return pl.pallas_call(
        functools.partial(_query_kernel, tm=tm),
        grid=(T // tm,),
        in_specs=[
            pl.BlockSpec((D_MODEL, tm), lambda i: (0, i)),
            pl.BlockSpec((D_MODEL, D_MODEL), lambda i: (0, 0), pipeline_mode=pl.Buffered(1)),
            key_spec, key_spec,
        ],
        out_specs=[row_spec, row_spec, col_spec, col_spec],
        out_shape=[row_shape, row_shape, col_shape, col_shape],
        scratch_shapes=[pltpu.VMEM((D_MODEL, tm), F32), pltpu.VMEM((PEER_NKEYS, tm), F32),
                        pltpu.VMEM((PEER_NKEYS, tm), F32)],
        compiler_params=_params(("parallel",), 48),
        name="query",
    )(x1t, wq_t, keys_hi, keys_lo)


PEER_SB = 64
PEER_RG = 2
SUBLANES = 8
PEER_TE = 1024


def _gelu(x):
    return 0.5 * x * (1.0 + lax.erf(x * (1.0 / math.sqrt(2.0))))


def _tables_kernel(u_ref, v_ref, ub_ref, vt_ref):
    ub_ref[...] = u_ref[...].astype(BF16)
    vt_ref[...] = jnp.transpose(v_ref[...]).astype(BF16)


def _tables(peer_u, peer_v):
    n_exp = peer_u.shape[0]
    nj = n_exp // PEER_TE
    return pl.pallas_call(
        _tables_kernel,
        grid=(nj,),
        in_specs=[pl.BlockSpec((PEER_TE, D_MODEL), lambda j: (j, 0)),
                  pl.BlockSpec((PEER_TE, D_MODEL), lambda j: (j, 0))],
        out_specs=[pl.BlockSpec((PEER_TE, D_MODEL), lambda j: (j, 0)),
                   pl.BlockSpec((None, D_MODEL, PEER_TE), lambda j: (j, 0, 0))],
        out_shape=[jax.ShapeDtypeStruct((n_exp, D_MODEL), BF16),
                   jax.ShapeDtypeStruct((nj, D_MODEL, PEER_TE), BF16)],
        compiler_params=_params(("parallel",), 56),
        name="tables",
    )(peer_u, peer_v)


def _bcast_row_bf16(tile, ri, rows):
    packed = jnp.broadcast_to(tile[ri:ri + 1, :], (2 * SUBLANES, LANES)).astype(BF16)
    return jnp.concatenate([packed] * (rows // (2 * SUBLANES)), axis=0)


def _peer_kernel(x1t_ref, u_ref, vt_ref, c1_ref, n1_ref, e2_ref, r2_ref, y_ref,
                 acc_ref, st_scr, ht_scr, *, tm, te):
    j = pl.program_id(1)
    n1 = te // PEER_NKEYS
    grows = PEER_RG * PEER_NKEYS

    @pl.when(j == 0)
    def _():
        acc_ref[...] = jnp.zeros_like(acc_ref)

    st_scr[...] = jnp.dot(u_ref[...], x1t_ref[...], preferred_element_type=F32)

    nsb = PEER_NKEYS // PEER_SB
    tile_rows = pl.ds(pl.multiple_of(j * n1, SUBLANES), n1)
    for gi in range(n1 // PEER_RG):
        crows = slice(gi * grows, (gi + 1) * grows)
        for tc in range(tm // LANES):
            lanes = slice(tc * LANES, (tc + 1) * LANES)
            g = [[jnp.zeros((PEER_SB, LANES), BF16) for _ in range(nsb)] for _ in range(PEER_RG)]
            for h in range(PEER_HEADS):
                c1_t = c1_ref[h, tc, tile_rows, :]
                n1_t = n1_ref[h, tc, tile_rows, :]
                c1b = [_bcast_row_bf16(c1_t, gi * PEER_RG + r, PEER_SB) for r in range(PEER_RG)]
                n1b = [_bcast_row_bf16(n1_t, gi * PEER_RG + r, PEER_SB) for r in range(PEER_RG)]
                for sb in range(nsb):
                    rows = slice(sb * PEER_SB, (sb + 1) * PEER_SB)
                    r2c = r2_ref[h, rows, lanes]
                    e2c = e2_ref[h, rows, lanes]
                    for r in range(PEER_RG):
                        g[r][sb] = g[r][sb] + c1b[r] * jnp.where(r2c < n1b[r], e2c, jnp.zeros_like(e2c))
            for r in range(PEER_RG):
                for sb in range(nsb):
                    base = gi * grows + r * PEER_NKEYS + sb * PEER_SB
                    srows = slice(base, base + PEER_SB)
                    ht_scr[srows, lanes] = g[r][sb] * _gelu(st_scr[srows, lanes].astype(BF16))
    acc_ref[...] += jnp.dot(vt_ref[...], ht_scr[...], preferred_element_type=F32)

    @pl.when(j == pl.num_programs(1) - 1)
    def _():
        y_ref[...] = jnp.transpose(acc_ref[...])


def _peer(x1t, u_tab, vt_tiles, c1, n1, e2, r2, tm=512):
    T = x1t.shape[1]
    nj, _, te = vt_tiles.shape
    row_spec = pl.BlockSpec((PEER_HEADS, tm // LANES, PEER_NKEYS, LANES), lambda i, j: (0, i, 0, 0))
    col_spec = pl.BlockSpec((PEER_HEADS, PEER_NKEYS, tm), lambda i, j: (0, 0, i))
    return pl.pallas_call(
        functools.partial(_peer_kernel, tm=tm, te=te),
        grid=(T // tm, nj),
        in_specs=[
            pl.BlockSpec((D_MODEL, tm), lambda i, j: (0, i)),
            pl.BlockSpec((te, D_MODEL), lambda i, j: (j, 0)),
            pl.BlockSpec((None, D_MODEL, te), lambda i, j: (j, 0, 0)),
            row_spec, row_spec, col_spec, col_spec,
        ],
        out_specs=pl.BlockSpec((tm, D_MODEL), lambda i, j: (i, 0)),
        out_shape=jax.ShapeDtypeStruct((T, D_MODEL), F32),
        scratch_shapes=[
            pltpu.VMEM((D_MODEL, tm), F32),
            pltpu.VMEM((te, tm), F32),
            pltpu.VMEM((te, tm), BF16),
        ],
        compiler_params=_params(("parallel", "arbitrary"), 56),
        name="peer",
    )(x1t, u_tab, vt_tiles, c1, n1, e2, r2)


def _final_kernel(x1_ref, y_ref, p_ref, wg_ref, wp_ref, lw_ref, lb_ref, o_ref):
    x1 = x1_ref[...]
    gate = jax.nn.sigmoid(jnp.dot(x1.astype(BF16), wg_ref[...], preferred_element_type=F32))
    emb = jnp.dot(p_ref[...].astype(BF16), wp_ref[...], preferred_element_type=F32)
    o_ref[...] = _layer_norm(ALPHA * x1 + y_ref[...] + gate * emb, lw_ref[...], lb_ref[...])


def _final(x1, y_ffn, p2, w_gate, w_proj, ln_w, ln_b, tm=512):
    T = x1.shape[0]
    return pl.pallas_call(
        _final_kernel,
        grid=(T // tm,),
        in_specs=[
            pl.BlockSpec((tm, D_MODEL), lambda i: (i, 0)),
            pl.BlockSpec((tm, D_MODEL), lambda i: (i, 0)),
            pl.BlockSpec((tm, PLE_DIM), lambda i: (i, 0)),
            pl.BlockSpec((D_MODEL, D_MODEL), lambda i: (0, 0), pipeline_mode=pl.Buffered(1)),
            pl.BlockSpec((PLE_DIM, D_MODEL), lambda i: (0, 0), pipeline_mode=pl.Buffered(1)),
            pl.BlockSpec((1, D_MODEL), lambda i: (0, 0)),
            pl.BlockSpec((1, D_MODEL), lambda i: (0, 0)),
        ],
        out_specs=pl.BlockSpec((tm, D_MODEL), lambda i: (i, 0)),
        out_shape=jax.ShapeDtypeStruct((T, D_MODEL), F32),
        compiler_params=_params(("parallel",), 48),
        name="final",
    )(x1, y_ffn, p2, w_gate, w_proj, ln_w, ln_b)


def _layer(x2, p2, B, S, w_in, gla_w_gate_up, gla_b_gate, gla_norm_w, pool_w, pool_scale, w_out,
           ln1_w, ln1_b, peer_w_query, peer_sub_keys, peer_u, peer_v, ple_w_gate, ple_w_proj, ln2_w, ln2_b):
    glr0 = COL_R
    w_main = jnp.concatenate([w_in[:, :glr0], w_in[:, glr0 + GLA_GATE_RANK:]], axis=1).astype(BF16)
    w_glr = jnp.pad(w_in[:, glr0:glr0 + GLA_GATE_RANK], ((0, 0), (0, LANES - GLA_GATE_RANK))).astype(BF16)
    proj, glr = _proj(x2, w_main, w_glr)

    y_pool = _pool(proj, pool_w.astype(BF16), pool_scale.reshape(1, POOL_WIDTH), S)

    wg = jnp.pad(gla_w_gate_up, ((0, LANES - GLA_GATE_RANK), (0, 0))).astype(BF16)
    y_gla = _gla(proj, glr, wg, gla_b_gate.reshape(1, GLA_KEY_WIDTH),
                 gla_norm_w.reshape(1, GLA_WIDTH), B, S)

    x1, x1t = _outproj(y_pool, y_gla, x2, w_out.astype(BF16),
                       ln1_w.reshape(1, D_MODEL), ln1_b.reshape(1, D_MODEL))

    keys = peer_sub_keys.reshape(2 * PEER_HEADS, PEER_NKEYS, PEER_HALF)
    keys_hi, keys_lo = _split_bf16(keys)
    c1, n1, e2, r2 = _query(x1t, peer_w_query.T.astype(BF16), keys_hi, keys_lo)
    u_bf16, vt_tiles = _tables(peer_u, peer_v)
    y_ffn = _peer(x1t, u_bf16, vt_tiles, c1, n1, e2, r2)

    return _final(x1, y_ffn, p2, ple_w_gate.astype(BF16), ple_w_proj.astype(BF16),
                  ln2_w.reshape(1, D_MODEL), ln2_b.reshape(1, D_MODEL))


def kernel(x, p, w_in, gla_w_gate_up, gla_b_gate, gla_norm_w, pool_w, pool_scale, w_out, ln1_w, ln1_b,
           peer_w_query, peer_sub_keys, peer_u, peer_v, ple_w_gate, ple_w_proj, ln2_w, ln2_b):
    B, S, D = x.shape
    x2 = x.reshape(B * S, D)
    for i in range(w_in.shape[0]):
        x2 = _layer(x2, p[i].reshape(B * S, PLE_DIM), B, S, w_in[i], gla_w_gate_up[i], gla_b_gate[i],
                    gla_norm_w[i], pool_w[i], pool_scale[i], w_out[i], ln1_w[i], ln1_b[i],
                    peer_w_query[i], peer_sub_keys[i], peer_u[i], peer_v[i], ple_w_gate[i],
                    ple_w_proj[i], ln2_w[i], ln2_b[i])
    return x2.reshape(B, S, D)
```

```python
import functools
import math

import jax
import jax.numpy as jnp
from jax import lax
from jax.experimental import pallas as pl
from jax.experimental.pallas import tpu as pltpu

F32 = jnp.float32
BF16 = jnp.bfloat16

D_MODEL = 2048
PLE_DIM = 256
POOL_WIDTH = 1024
POOL_WINDOWS = (2, 4, 8, 16)
POOL_GC = 256
POOL_HALO = 16
GLA_WIDTH = 1024
GLA_HEADS = 4
GLA_DV = 256
GLA_DK = 128
GLA_KEY_WIDTH = 512
GLA_GATE_RANK = 16
GLA_GATE_TEMP = 16.0
GLA_CHUNK = 64
PEER_HEADS = 8
PEER_NKEYS = 128
PEER_HALF = 128
PEER_TOPK = 16
DEPTH = 1
ALPHA = float((2 * DEPTH) ** 0.25)
LN_EPS = 1e-5
RMS_EPS = 1e-6
LANES = 128
NEG_INF = float("-inf")

COL_Q = POOL_WIDTH
COL_K = COL_Q + GLA_KEY_WIDTH
COL_V = COL_K + GLA_KEY_WIDTH
COL_R = COL_V + GLA_WIDTH
PROJ_COLS = COL_R + GLA_WIDTH


def _params(sem, vmem_mib):
    return pltpu.CompilerParams(dimension_semantics=sem, vmem_limit_bytes=vmem_mib * 1024 * 1024)


def _proj_kernel(x_ref, w_ref, wg_ref, o_ref, glr_ref, xb_ref):
    @pl.when(pl.program_id(1) == 0)
    def _():
        xb = x_ref[...].astype(BF16)
        xb_ref[...] = xb
        glr_ref[...] = jnp.dot(xb, wg_ref[...], preferred_element_type=F32)

    o_ref[...] = jnp.dot(xb_ref[...], w_ref[...], preferred_element_type=F32).astype(o_ref.dtype)


def _proj(x2, w_main, w_glr, tm=1024, tn=2048):
    T = x2.shape[0]
    return pl.pallas_call(
        _proj_kernel,
        grid=(T // tm, PROJ_COLS // tn),
        in_specs=[
            pl.BlockSpec((tm, D_MODEL), lambda i, n: (i, 0)),
            pl.BlockSpec((D_MODEL, tn), lambda i, n: (0, n)),
            pl.BlockSpec((D_MODEL, LANES), lambda i, n: (0, 0)),
        ],
        out_specs=[
            pl.BlockSpec((tm, tn), lambda i, n: (i, n)),
            pl.BlockSpec((tm, LANES), lambda i, n: (i, 0)),
        ],
        out_shape=[
            jax.ShapeDtypeStruct((T, PROJ_COLS), BF16),
            jax.ShapeDtypeStruct((T, LANES), F32),
        ],
        scratch_shapes=[pltpu.VMEM((tm, D_MODEL), BF16)],
        compiler_params=_params(("parallel", "arbitrary"), 52),
        name="proj",
    )(x2, w_main, w_glr)


def _pool_kernel(u_ref, halo_ref, w_ref, sc_ref, o_ref, ext_ref, *, tiles_per_seq, tm):
    t = pl.program_id(0) % tiles_per_seq
    halo = jnp.where(t == 0, 0.0, halo_ref[...].astype(F32))
    ext_ref[0:POOL_HALO, :] = halo
    ext_ref[POOL_HALO:, :] = u_ref[...].astype(F32)
    pos = t * tm + lax.broadcasted_iota(jnp.int32, (tm, 1), 0)
    for g, w in enumerate(POOL_WINDOWS):
        cols = slice(g * POOL_GC, (g + 1) * POOL_GC)
        u = ext_ref[POOL_HALO:, cols]
        acc = u
        for j in range(1, w):
            acc = acc + ext_ref[POOL_HALO - j:POOL_HALO - j + tm, cols]
        cnt = jnp.minimum(pos + 1, w).astype(F32)
        d = acc / cnt - u
        y = jnp.dot(d.astype(BF16), w_ref[g], preferred_element_type=F32)
        o_ref[:, cols] = (y * sc_ref[:, cols]).astype(o_ref.dtype)


def _pool(proj, pool_w, pool_scale, S, tm=512):
    T = proj.shape[0]
    hb = tm // POOL_HALO
    return pl.pallas_call(
        functools.partial(_pool_kernel, tiles_per_seq=S // tm, tm=tm),
        grid=(T // tm,),
        in_specs=[
            pl.BlockSpec((tm, POOL_WIDTH), lambda i: (i, 0)),
            pl.BlockSpec((POOL_HALO, POOL_WIDTH), lambda i: (jnp.maximum(i * hb - 1, 0), 0)),
            pl.BlockSpec((len(POOL_WINDOWS), POOL_GC, POOL_GC), lambda i: (0, 0, 0)),
            pl.BlockSpec((1, POOL_WIDTH), lambda i: (0, 0)),
        ],
        out_specs=pl.BlockSpec((tm, POOL_WIDTH), lambda i: (i, 0)),
        out_shape=jax.ShapeDtypeStruct((T, POOL_WIDTH), BF16),
        scratch_shapes=[pltpu.VMEM((POOL_HALO + tm, POOL_WIDTH), F32)],
        compiler_params=_params(("parallel",), 32),
        name="pool",
    )(proj, proj, pool_w, pool_scale)


def _gla_kernel(q_ref, k_ref, v_ref, r_ref, glr_ref, wg_ref, bg_ref, nw_ref, o_ref, s_ref, *, n_chunks):
    @pl.when(pl.program_id(1) == 0)
    def _():
        s_ref[...] = jnp.zeros_like(s_ref)

    C = GLA_CHUNK
    row = lax.broadcasted_iota(jnp.int32, (C, C), 0)
    col = lax.broadcasted_iota(jnp.int32, (C, C), 1)
    causal = col <= row
    tril = causal.astype(BF16)
    wg = wg_ref[...]
    bg = bg_ref[...]
    nt = (((1,), (1,)), ((), ()))
    for c in range(n_chunks):
        rows = slice(c * C, (c + 1) * C)
        z = jnp.dot(glr_ref[rows, :].astype(BF16), wg, preferred_element_type=F32) + bg
        g = jax.nn.log_sigmoid(z) / GLA_GATE_TEMP
        g_hi, g_lo = _split_bf16(g)
        b_all = (jnp.dot(tril, g_hi, preferred_element_type=F32)
                 + jnp.dot(tril, g_lo, preferred_element_type=F32))
        for h in range(GLA_HEADS):
            kc = slice(h * GLA_DK, (h + 1) * GLA_DK)
            vc = slice(h * GLA_DV, (h + 1) * GLA_DV)
            b = b_all[:, kc]
            b_last = b[C - 1:C, :]
            b_mid = b[C // 2 - 1:C // 2, :]
            q = q_ref[rows, kc].astype(F32) * (GLA_DK ** -0.5)
            k = k_ref[rows, kc].astype(F32)
            v = v_ref[rows, vc]
            q_state = (q * jnp.exp(b)).astype(BF16)
            q_in = (q * jnp.exp(b - b_mid)).astype(BF16)
            k_in = (k * jnp.exp(b_mid - b)).astype(BF16)
            k_out = k * jnp.exp(b_last - b)
            attn = lax.dot_general(q_in, k_in, nt, preferred_element_type=F32)
            attn = jnp.where(causal, attn, 0.0).astype(BF16)
            s = s_ref[h]
            o = (jnp.dot(attn, v, preferred_element_type=F32)
                 + jnp.dot(q_state, s.astype(BF16), preferred_element_type=F32))
            decay = jnp.transpose(jnp.broadcast_to(jnp.exp(b_last), (C, GLA_DK)))[:, 0:1]
            s_ref[h] = decay * s + jnp.dot(jnp.transpose(k_out).astype(BF16), v, preferred_element_type=F32)
            o = o * lax.rsqrt(jnp.mean(jnp.square(o), axis=-1, keepdims=True) + RMS_EPS)
            o = o * nw_ref[:, vc]
            r = r_ref[rows, vc].astype(F32)
            o_ref[rows, vc] = (o * (r * jax.nn.sigmoid(r))).astype(o_ref.dtype)


def _gla(proj, glr, wg, bg, nw, B, S, L=512):
    T = proj.shape[0]
    nl = S // L
    rb = lambda b, l: b * nl + l
    return pl.pallas_call(
        functools.partial(_gla_kernel, n_chunks=L // GLA_CHUNK),
        grid=(B, nl),
        in_specs=[
            pl.BlockSpec((L, GLA_KEY_WIDTH), lambda b, l: (rb(b, l), COL_Q // GLA_KEY_WIDTH)),
            pl.BlockSpec((L, GLA_KEY_WIDTH), lambda b, l: (rb(b, l), COL_K // GLA_KEY_WIDTH)),
            pl.BlockSpec((L, GLA_WIDTH), lambda b, l: (rb(b, l), COL_V // GLA_WIDTH)),
            pl.BlockSpec((L, GLA_WIDTH), lambda b, l: (rb(b, l), COL_R // GLA_WIDTH)),
            pl.BlockSpec((L, LANES), lambda b, l: (rb(b, l), 0)),
            pl.BlockSpec((LANES, GLA_KEY_WIDTH), lambda b, l: (0, 0)),
            pl.BlockSpec((1, GLA_KEY_WIDTH), lambda b, l: (0, 0)),
            pl.BlockSpec((1, GLA_WIDTH), lambda b, l: (0, 0)),
        ],
        out_specs=pl.BlockSpec((L, GLA_WIDTH), lambda b, l: (rb(b, l), 0)),
        out_shape=jax.ShapeDtypeStruct((T, GLA_WIDTH), BF16),
        scratch_shapes=[pltpu.VMEM((GLA_HEADS, GLA_DK, GLA_DV), F32)],
        compiler_params=_params(("parallel", "arbitrary"), 32),
        name="gla",
    )(proj, proj, proj, proj, glr, wg, bg, nw)


def _layer_norm(h, w, b):
    mu = jnp.mean(h, axis=-1, keepdims=True)
    hc = h - mu
    var = jnp.mean(jnp.square(hc), axis=-1, keepdims=True)
    return hc * lax.rsqrt(var + LN_EPS) * w + b


def _outproj_kernel(yp_ref, yg_ref, x_ref, w_ref, lw_ref, lb_ref, x1_ref, x1t_ref):
    mix = (jnp.dot(yp_ref[...], w_ref[0:POOL_WIDTH, :], preferred_element_type=F32)
           + jnp.dot(yg_ref[...], w_ref[POOL_WIDTH:, :], preferred_element_type=F32))
    x1 = _layer_norm(ALPHA * x_ref[...] + mix, lw_ref[...], lb_ref[...])
    x1_ref[...] = x1
    x1t_ref[...] = jnp.transpose(x1).astype(BF16)


def _outproj(y_pool, y_gla, x2, w_out, ln_w, ln_b, tm=512):
    T = x2.shape[0]
    return pl.pallas_call(
        _outproj_kernel,
        grid=(T // tm,),
        in_specs=[
            pl.BlockSpec((tm, POOL_WIDTH), lambda i: (i, 0)),
            pl.BlockSpec((tm, GLA_WIDTH), lambda i: (i, 0)),
            pl.BlockSpec((tm, D_MODEL), lambda i: (i, 0)),
            pl.BlockSpec((D_MODEL, D_MODEL), lambda i: (0, 0), pipeline_mode=pl.Buffered(1)),
            pl.BlockSpec((1, D_MODEL), lambda i: (0, 0)),
            pl.BlockSpec((1, D_MODEL), lambda i: (0, 0)),
        ],
        out_specs=[
            pl.BlockSpec((tm, D_MODEL), lambda i: (i, 0)),
            pl.BlockSpec((D_MODEL, tm), lambda i: (0, i)),
        ],
        out_shape=[
            jax.ShapeDtypeStruct((T, D_MODEL), F32),
            jax.ShapeDtypeStruct((D_MODEL, T), BF16),
        ],
        compiler_params=_params(("parallel",), 48),
        name="outproj",
    )(y_pool, y_gla, x2, w_out, ln_w, ln_b)


N_SORT = PEER_TOPK + 1


def _sort_network(n):
    pairs = []

    def merge(lo, m, r):
        step = 2 * r
        if step < m:
            merge(lo, m, step)
            merge(lo + r, m, step)
            pairs.extend((i, i + r) for i in range(lo + r, lo + m - r, step))
        else:
            pairs.append((lo, lo + r))

    def sort(lo, m):
        if m > 1:
            sort(lo, m // 2)
            sort(lo + m // 2, m // 2)
            merge(lo, m, 1)

    sort(0, n)
    return tuple(pairs)


def _pop_sorted(v, n_out):
    nv = len(v)
    width = 1 << (nv - 1).bit_length()
    for i, j in _sort_network(width):
        if j < nv:
            v[i], v[j] = jnp.maximum(v[i], v[j]), jnp.minimum(v[i], v[j])
    tops = []
    rid = lax.broadcasted_iota(jnp.int32, (SUBLANES, LANES), 0).astype(F32)
    for kk in range(n_out):
        m = jnp.max(v[0], axis=0, keepdims=True)
        tops.append(m)
        first = jnp.min(jnp.where(v[0] == m, rid, float(SUBLANES)), axis=0, keepdims=True)
        hit = rid == first
        for k in range(min(n_out - 1 - kk, nv)):
            v[k] = jnp.where(hit, v[k + 1] if k + 1 < nv else NEG_INF, v[k])
    return tops


def _sorted_top(arr):
    return _pop_sorted([arr[k:k + SUBLANES] for k in range(0, arr.shape[0], SUBLANES)], N_SORT)


def _rows_to_tile(rows):
    rid = lax.broadcasted_iota(jnp.int32, (SUBLANES, LANES), 0)
    tile = jnp.full((SUBLANES, LANES), NEG_INF, F32)
    for k, r in enumerate(rows):
        tile = jnp.where(rid == k, r, tile)
    return tile


def _pair_stats(a, b):
    r8 = lax.broadcasted_iota(jnp.int32, (SUBLANES, LANES), 0)
    b_lo, b_hi, a_hi = _rows_to_tile(b[0:8]), _rows_to_tile(b[8:16]), _rows_to_tile(a[8:16])
    p2 = jnp.where(r8 < 5, a[2] + b_lo, jnp.where(r8 == 5, a[16] + b[0], jnp.where(r8 == 6, a[0] + b[16], NEG_INF)))
    pieces = [
        a[0] + b_lo, a[0] + b_hi, a[1] + b_lo, p2,
        jnp.where(r8 < 4, a[3] + b_lo, NEG_INF),
        jnp.where(r8 < 3, a[4] + b_lo, NEG_INF),
        jnp.where(r8 < 2, a[5] + b_lo, NEG_INF),
        jnp.where(r8 < 2, a[6] + b_lo, NEG_INF),
        jnp.where(r8 < 2, a[7] + b_lo, NEG_INF),
        a_hi + b[0],
    ]
    sums = _pop_sorted(pieces, N_SORT)
    top16 = jnp.concatenate([_rows_to_tile(sums[0:8]), _rows_to_tile(sums[8:16])], axis=0)
    z = jnp.sum(jnp.exp(top16 - sums[0]), axis=0, keepdims=True)
    return 0.5 * (sums[PEER_TOPK - 1] + sums[PEER_TOPK]), 1.0 / z


def _split_bf16(x):
    hi = x.astype(BF16)
    return hi, (x - hi.astype(F32)).astype(BF16)


def _query_kernel(x1t_ref, wq_ref, khi_ref, klo_ref, c1_ref, n1_ref, e2_ref, r2_ref, q_ref, s1_scr, s2_scr, *, tm):
    q_ref[...] = jnp.dot(wq_ref[...], x1t_ref[...], preferred_element_type=F32)
    for h in range(PEER_HEADS):
        for p in range(2):
            hp = 2 * h + p
            q_hi, q_lo = _split_bf16(q_ref[hp * PEER_HALF:(hp + 1) * PEER_HALF, :])
            k_hi, k_lo = khi_ref[hp], klo_ref[hp]
            sc = (jnp.dot(k_hi, q_hi, preferred_element_type=F32)
                  + jnp.dot(k_hi, q_lo, preferred_element_type=F32)
                  + jnp.dot(k_lo, q_hi, preferred_element_type=F32))
            if p == 0:
                s1_scr[...] = sc
            else:
                s2_scr[...] = sc
        for tc in range(tm // LANES):
            lanes = slice(tc * LANES, (tc + 1) * LANES)
            s1 = s1_scr[:, lanes]
            s2 = s2_scr[:, lanes]
            ta, tb = _sorted_top(s1), _sorted_top(s2)
            tau, rz = _pair_stats(ta, tb)
            n1 = jnp.zeros_like(s1)
            r2 = jnp.zeros_like(s2)
            for jj in range(PEER_TOPK):
                n1 = jnp.where(s1 >= tau - tb[jj], float(jj + 1), n1)
                r2 = jnp.where(s2 < tb[jj], float(jj + 1), r2)
            c1_ref[h, tc] = jnp.exp(s1 - ta[0])
            n1_ref[h, tc] = n1
            e2_ref[h, :, lanes] = (jnp.exp(s2 - tb[0]) * rz).astype(BF16)
            r2_ref[h, :, lanes] = r2.astype(BF16)


def _query(x1t, wq_t, keys_hi, keys_lo, tm=256):
    T = x1t.shape[1]
    nc = tm // LANES
    row_spec = pl.BlockSpec((PEER_HEADS, nc, PEER_NKEYS, LANES), lambda i: (0, i, 0, 0))
    col_spec = pl.BlockSpec((PEER_HEADS, PEER_NKEYS, tm), lambda i: (0, 0, i))
    key_spec = pl.BlockSpec((2 * PEER_HEADS, PEER_NKEYS, PEER_HALF), lambda i: (0, 0, 0))
    row_shape = jax.ShapeDtypeStruct((PEER_HEADS, T // LANES, PEER_NKEYS, LANES), F32)
    col_shape = jax.ShapeDtypeStruct((PEER_HEADS, PEER_NKEYS, T), BF16)
    return pl.pallas_call(
        functools.partial(_query_kernel, tm=tm),
        grid=(T // tm,),
        in_specs=[
            pl.BlockSpec((D_MODEL, tm), lambda i: (0, i)),
            pl.BlockSpec((D_MODEL, D_MODEL), lambda i: (0, 0), pipeline_mode=pl.Buffered(1)),
            key_spec, key_spec,
        ],
        out_specs=[row_spec, row_spec, col_spec, col_spec],
        out_shape=[row_shape, row_shape, col_shape, col_shape],
        scratch_shapes=[pltpu.VMEM((D_MODEL, tm), F32), pltpu.VMEM((PEER_NKEYS, tm), F32),
                        pltpu.VMEM((PEER_NKEYS, tm), F32)],
        compiler_params=_params(("parallel",), 48),
        name="query",
    )(x1t, wq_t, keys_hi, keys_lo)


PEER_SB = 64
PEER_RG = 2
SUBLANES = 8
PEER_TE = 1024


def _gelu(x):
    return 0.5 * x * (1.0 + lax.erf(x * (1.0 / math.sqrt(2.0))))


def _tables_kernel(u_ref, v_ref, ub_ref, vt_ref):
    ub_ref[...] = u_ref[...].astype(BF16)
    vt_ref[...] = jnp.transpose(v_ref[...]).astype(BF16)


def _tables(peer_u, peer_v):
    n_exp = peer_u.shape[0]
    nj = n_exp // PEER_TE
    return pl.pallas_call(
        _tables_kernel,
        grid=(nj,),
        in_specs=[pl.BlockSpec((PEER_TE, D_MODEL), lambda j: (j, 0)),
                  pl.BlockSpec((PEER_TE, D_MODEL), lambda j: (j, 0))],
        out_specs=[pl.BlockSpec((PEER_TE, D_MODEL), lambda j: (j, 0)),
                   pl.BlockSpec((None, D_MODEL, PEER_TE), lambda j: (j, 0, 0))],
        out_shape=[jax.ShapeDtypeStruct((n_exp, D_MODEL), BF16),
                   jax.ShapeDtypeStruct((nj, D_MODEL, PEER_TE), BF16)],
        compiler_params=_params(("parallel",), 56),
        name="tables",
    )(peer_u, peer_v)


def _bcast_row_bf16(tile, ri, rows):
    packed = jnp.broadcast_to(tile[ri:ri + 1, :], (2 * SUBLANES, LANES)).astype(BF16)
    return jnp.concatenate([packed] * (rows // (2 * SUBLANES)), axis=0)


def _peer_kernel(x1t_ref, u_ref, vt_ref, c1_ref, n1_ref, e2_ref, r2_ref, y_ref,
                 acc_ref, st_scr, ht_scr, *, tm, te):
    j = pl.program_id(1)
    n1 = te // PEER_NKEYS
    grows = PEER_RG * PEER_NKEYS

    @pl.when(j == 0)
    def _():
        acc_ref[...] = jnp.zeros_like(acc_ref)

    st_scr[...] = jnp.dot(u_ref[...], x1t_ref[...], preferred_element_type=F32)

    nsb = PEER_NKEYS // PEER_SB
    tile_rows = pl.ds(pl.multiple_of(j * n1, SUBLANES), n1)
    for gi in range(n1 // PEER_RG):
        crows = slice(gi * grows, (gi + 1) * grows)
        for tc in range(tm // LANES):
            lanes = slice(tc * LANES, (tc + 1) * LANES)
            g = [[jnp.zeros((PEER_SB, LANES), BF16) for _ in range(nsb)] for _ in range(PEER_RG)]
            for h in range(PEER_HEADS):
                c1_t = c1_ref[h, tc, tile_rows, :]
                n1_t = n1_ref[h, tc, tile_rows, :]
                c1b = [_bcast_row_bf16(c1_t, gi * PEER_RG + r, PEER_SB) for r in range(PEER_RG)]
                n1b = [_bcast_row_bf16(n1_t, gi * PEER_RG + r, PEER_SB) for r in range(PEER_RG)]
                for sb in range(nsb):
                    rows = slice(sb * PEER_SB, (sb + 1) * PEER_SB)
                    r2c = r2_ref[h, rows, lanes]
                    e2c = e2_ref[h, rows, lanes]
                    for r in range(PEER_RG):
                        g[r][sb] = g[r][sb] + c1b[r] * jnp.where(r2c < n1b[r], e2c, jnp.zeros_like(e2c))
            for r in range(PEER_RG):
                for sb in range(nsb):
                    base = gi * grows + r * PEER_NKEYS + sb * PEER_SB
                    srows = slice(base, base + PEER_SB)
                    ht_scr[srows, lanes] = g[r][sb] * _gelu(st_scr[srows, lanes].astype(BF16))
    acc_ref[...] += jnp.dot(vt_ref[...], ht_scr[...], preferred_element_type=F32)

    @pl.when(j == pl.num_programs(1) - 1)
    def _():
        y_ref[...] = jnp.transpose(acc_ref[...])


def _peer(x1t, u_tab, vt_tiles, c1, n1, e2, r2, tm=512):
    T = x1t.shape[1]
    nj, _, te = vt_tiles.shape
    row_spec = pl.BlockSpec((PEER_HEADS, tm // LANES, PEER_NKEYS, LANES), lambda i, j: (0, i, 0, 0))
    col_spec = pl.BlockSpec((PEER_HEADS, PEER_NKEYS, tm), lambda i, j: (0, 0, i))
    return pl.pallas_call(
        functools.partial(_peer_kernel, tm=tm, te=te),
        grid=(T // tm, nj),
        in_specs=[
            pl.BlockSpec((D_MODEL, tm), lambda i, j: (0, i)),
            pl.BlockSpec((te, D_MODEL), lambda i, j: (j, 0)),
            pl.BlockSpec((None, D_MODEL, te), lambda i, j: (j, 0, 0)),
            row_spec, row_spec, col_spec, col_spec,
        ],
        out_specs=pl.BlockSpec((tm, D_MODEL), lambda i, j: (i, 0)),
        out_shape=jax.ShapeDtypeStruct((T, D_MODEL), F32),
        scratch_shapes=[
            pltpu.VMEM((D_MODEL, tm), F32),
            pltpu.VMEM((te, tm), F32),
            pltpu.VMEM((te, tm), BF16),
        ],
        compiler_params=_params(("parallel", "arbitrary"), 56),
        name="peer",
    )(x1t, u_tab, vt_tiles, c1, n1, e2, r2)


def _final_kernel(x1_ref, y_ref, p_ref, wg_ref, wp_ref, lw_ref, lb_ref, o_ref):
    x1 = x1_ref[...]
    gate = jax.nn.sigmoid(jnp.dot(x1.astype(BF16), wg_ref[...], preferred_element_type=F32))
    emb = jnp.dot(p_ref[...].astype(BF16), wp_ref[...], preferred_element_type=F32)
    o_ref[...] = _layer_norm(ALPHA * x1 + y_ref[...] + gate * emb, lw_ref[...], lb_ref[...])


def _final(x1, y_ffn, p2, w_gate, w_proj, ln_w, ln_b, tm=512):
    T = x1.shape[0]
    return pl.pallas_call(
        _final_kernel,
        grid=(T // tm,),
        in_specs=[
            pl.BlockSpec((tm, D_MODEL), lambda i: (i, 0)),
            pl.BlockSpec((tm, D_MODEL), lambda i: (i, 0)),
            pl.BlockSpec((tm, PLE_DIM), lambda i: (i, 0)),
            pl.BlockSpec((D_MODEL, D_MODEL), lambda i: (0, 0), pipeline_mode=pl.Buffered(1)),
            pl.BlockSpec((PLE_DIM, D_MODEL), lambda i: (0, 0), pipeline_mode=pl.Buffered(1)),
            pl.BlockSpec((1, D_MODEL), lambda i: (0, 0)),
            pl.BlockSpec((1, D_MODEL), lambda i: (0, 0)),
        ],
        out_specs=pl.BlockSpec((tm, D_MODEL), lambda i: (i, 0)),
        out_shape=jax.ShapeDtypeStruct((T, D_MODEL), F32),
        compiler_params=_params(("parallel",), 48),
        name="final",
    )(x1, y_ffn, p2, w_gate, w_proj, ln_w, ln_b)


def _layer(x2, p2, B, S, w_in, gla_w_gate_up, gla_b_gate, gla_norm_w, pool_w, pool_scale, w_out,
           ln1_w, ln1_b, peer_w_query, peer_sub_keys, peer_u, peer_v, ple_w_gate, ple_w_proj, ln2_w, ln2_b):
    glr0 = COL_R
    w_main = jnp.concatenate([w_in[:, :glr0], w_in[:, glr0 + GLA_GATE_RANK:]], axis=1).astype(BF16)
    w_glr = jnp.pad(w_in[:, glr0:glr0 + GLA_GATE_RANK], ((0, 0), (0, LANES - GLA_GATE_RANK))).astype(BF16)
    proj, glr = _proj(x2, w_main, w_glr)

    y_pool = _pool(proj, pool_w.astype(BF16), pool_scale.reshape(1, POOL_WIDTH), S)

    wg = jnp.pad(gla_w_gate_up, ((0, LANES - GLA_GATE_RANK), (0, 0))).astype(BF16)
    y_gla = _gla(proj, glr, wg, gla_b_gate.reshape(1, GLA_KEY_WIDTH),
                 gla_norm_w.reshape(1, GLA_WIDTH), B, S)

    x1, x1t = _outproj(y_pool, y_gla, x2, w_out.astype(BF16),
                       ln1_w.reshape(1, D_MODEL), ln1_b.reshape(1, D_MODEL))

    keys = peer_sub_keys.reshape(2 * PEER_HEADS, PEER_NKEYS, PEER_HALF)
    keys_hi, keys_lo = _split_bf16(keys)
    c1, n1, e2, r2 = _query(x1t, peer_w_query.T.astype(BF16), keys_hi, keys_lo)
    u_bf16, vt_tiles = _tables(peer_u, peer_v)
    y_ffn = _peer(x1t, u_bf16, vt_tiles, c1, n1, e2, r2)

    return _final(x1, y_ffn, p2, ple_w_gate.astype(BF16), ple_w_proj.astype(BF16),
                  ln2_w.reshape(1, D_MODEL), ln2_b.reshape(1, D_MODEL))


def kernel(x, p, w_in, gla_w_gate_up, gla_b_gate, gla_norm_w, pool_w, pool_scale, w_out, ln1_w, ln1_b,
           peer_w_query, peer_sub_keys, peer_u, peer_v, ple_w_gate, ple_w_proj, ln2_w, ln2_b):
    B, S, D = x.shape
    x2 = x.reshape(B * S, D)
    for i in range(w_in.shape[0]):
        x2 = _layer(x2, p[i].reshape(B * S, PLE_DIM), B, S, w_in[i], gla_w_gate_up[i], gla_b_gate[i],
                    gla_norm_w[i], pool_w[i], pool_scale[i], w_out[i], ln1_w[i], ln1_b[i],
                    peer_w_query[i], peer_sub_keys[i], peer_u[i], peer_v[i], ple_w_gate[i],
                    ple_w_proj[i], ln2_w[i], ln2_b[i])
    return x2.reshape(B, S, D)
```

```python
import functools
import math

import jax
import jax.numpy as jnp
from jax import lax
from jax.experimental import pallas as pl
from jax.experimental.pallas import tpu as pltpu

F32 = jnp.float32
BF16 = jnp.bfloat16

D_MODEL = 2048
PLE_DIM = 256
POOL_WIDTH = 1024
POOL_WINDOWS = (2, 4, 8, 16)
POOL_GC = 256
POOL_HALO = 16
GLA_WIDTH = 1024
GLA_HEADS = 4
GLA_DV = 256
GLA_DK = 128
GLA_KEY_WIDTH = 512
GLA_GATE_RANK = 16
GLA_GATE_TEMP = 16.0
GLA_CHUNK = 64
PEER_HEADS = 8
PEER_NKEYS = 128
PEER_HALF = 128
PEER_TOPK = 16
DEPTH = 1
ALPHA = float((2 * DEPTH) ** 0.25)
LN_EPS = 1e-5
RMS_EPS = 1e-6
LANES = 128
NEG_INF = float("-inf")
ROW_SUB = 256

COL_Q = POOL_WIDTH
COL_K = COL_Q + GLA_KEY_WIDTH
COL_V = COL_K + GLA_KEY_WIDTH
COL_R = COL_V + GLA_WIDTH
PROJ_COLS = COL_R + GLA_WIDTH


def _params(sem, vmem_mib):
    return pltpu.CompilerParams(dimension_semantics=sem, vmem_limit_bytes=vmem_mib * 1024 * 1024)


def _proj_kernel(x_ref, w_ref, wg_ref, o_ref, glr_ref, xb_ref):
    @pl.when(pl.program_id(1) == 0)
    def _():
        xb = x_ref[...].astype(BF16)
        xb_ref[...] = xb
        glr_ref[...] = jnp.dot(xb, wg_ref[...], preferred_element_type=F32)

    o_ref[...] = jnp.dot(xb_ref[...], w_ref[...], preferred_element_type=F32).astype(o_ref.dtype)


def _proj(x2, w_main, w_glr, tm=1024, tn=2048):
    T = x2.shape[0]
    return pl.pallas_call(
        _proj_kernel,
        grid=(T // tm, PROJ_COLS // tn),
        in_specs=[
            pl.BlockSpec((tm, D_MODEL), lambda i, n: (i, 0)),
            pl.BlockSpec((D_MODEL, tn), lambda i, n: (0, n)),
            pl.BlockSpec((D_MODEL, LANES), lambda i, n: (0, 0)),
        ],
        out_specs=[
            pl.BlockSpec((tm, tn), lambda i, n: (i, n)),
            pl.BlockSpec((tm, LANES), lambda i, n: (i, 0)),
        ],
        out_shape=[
            jax.ShapeDtypeStruct((T, PROJ_COLS), BF16),
            jax.ShapeDtypeStruct((T, LANES), F32),
        ],
        scratch_shapes=[pltpu.VMEM((tm, D_MODEL), BF16)],
        compiler_params=_params(("parallel", "arbitrary"), 52),
        name="proj",
    )(x2, w_main, w_glr)


def _pool_kernel(u_ref, halo_ref, w_ref, sc_ref, o_ref, ext_ref, *, tiles_per_seq, tm):
    t = pl.program_id(0) % tiles_per_seq
    halo = jnp.where(t == 0, 0.0, halo_ref[...].astype(F32))
    ext_ref[0:POOL_HALO, :] = halo
    ext_ref[POOL_HALO:, :] = u_ref[...].astype(F32)
    pos = t * tm + lax.broadcasted_iota(jnp.int32, (tm, 1), 0)
    for g, w in enumerate(POOL_WINDOWS):
        cols = slice(g * POOL_GC, (g + 1) * POOL_GC)
        u = ext_ref[POOL_HALO:, cols]
        acc = u
        for j in range(1, w):
            acc = acc + ext_ref[POOL_HALO - j:POOL_HALO - j + tm, cols]
        cnt = jnp.minimum(pos + 1, w).astype(F32)
        d = acc / cnt - u
        y = jnp.dot(d.astype(BF16), w_ref[g], preferred_element_type=F32)
        o_ref[:, cols] = (y * sc_ref[:, cols]).astype(o_ref.dtype)


def _pool(proj, pool_w, pool_scale, S, tm=512):
    T = proj.shape[0]
    hb = tm // POOL_HALO
    return pl.pallas_call(
        functools.partial(_pool_kernel, tiles_per_seq=S // tm, tm=tm),
        grid=(T // tm,),
        in_specs=[
            pl.BlockSpec((tm, POOL_WIDTH), lambda i: (i, 0)),
            pl.BlockSpec((POOL_HALO, POOL_WIDTH), lambda i: (jnp.maximum(i * hb - 1, 0), 0)),
            pl.BlockSpec((len(POOL_WINDOWS), POOL_GC, POOL_GC), lambda i: (0, 0, 0)),
            pl.BlockSpec((1, POOL_WIDTH), lambda i: (0, 0)),
        ],
        out_specs=pl.BlockSpec((tm, POOL_WIDTH), lambda i: (i, 0)),
        out_shape=jax.ShapeDtypeStruct((T, POOL_WIDTH), BF16),
        scratch_shapes=[pltpu.VMEM((POOL_HALO + tm, POOL_WIDTH), F32)],
        compiler_params=_params(("parallel",), 32),
        name="pool",
    )(proj, proj, pool_w, pool_scale)


def _gla_kernel(q_ref, k_ref, v_ref, r_ref, glr_ref, wg_ref, bg_ref, nw_ref, o_ref, s_ref, *, n_chunks):
    @pl.when(pl.program_id(1) == 0)
    def _():
        s_ref[...] = jnp.zeros_like(s_ref)

    C = GLA_CHUNK
    row = lax.broadcasted_iota(jnp.int32, (C, C), 0)
    col = lax.broadcasted_iota(jnp.int32, (C, C), 1)
    causal = col <= row
    tril = causal.astype(BF16)
    wg = wg_ref[...]
    bg = bg_ref[...]
    nt = (((1,), (1,)), ((), ()))
    for c in range(n_chunks):
        rows = slice(c * C, (c + 1) * C)
        z = jnp.dot(glr_ref[rows, :].astype(BF16), wg, preferred_element_type=F32) + bg
        g = jax.nn.log_sigmoid(z) / GLA_GATE_TEMP
        g_hi, g_lo = _split_bf16(g)
        b_all = (jnp.dot(tril, g_hi, preferred_element_type=F32)
                 + jnp.dot(tril, g_lo, preferred_element_type=F32))
        for h in range(GLA_HEADS):
            kc = slice(h * GLA_DK, (h + 1) * GLA_DK)
            vc = slice(h * GLA_DV, (h + 1) * GLA_DV)
            b = b_all[:, kc]
            b_last = b[C - 1:C, :]
            b_mid = b[C // 2 - 1:C // 2, :]
            q = q_ref[rows, kc].astype(F32) * (GLA_DK ** -0.5)
            k = k_ref[rows, kc].astype(F32)
            v = v_ref[rows, vc]
            q_state = (q * jnp.exp(b)).astype(BF16)
            q_in = (q * jnp.exp(b - b_mid)).astype(BF16)
            k_in = (k * jnp.exp(b_mid - b)).astype(BF16)
            k_out = k * jnp.exp(b_last - b)
            attn = lax.dot_general(q_in, k_in, nt, preferred_element_type=F32)
            attn = jnp.where(causal, attn, 0.0).astype(BF16)
            s = s_ref[h]
            o = (jnp.dot(attn, v, preferred_element_type=F32)
                 + jnp.dot(q_state, s.astype(BF16), preferred_element_type=F32))
            decay = jnp.transpose(jnp.broadcast_to(jnp.exp(b_last), (C, GLA_DK)))[:, 0:1]
            s_ref[h] = decay * s + jnp.dot(jnp.transpose(k_out).astype(BF16), v, preferred_element_type=F32)
            o = o * lax.rsqrt(jnp.mean(jnp.square(o), axis=-1, keepdims=True) + RMS_EPS)
            o = o * nw_ref[:, vc]
            r = r_ref[rows, vc].astype(F32)
            o_ref[rows, vc] = (o * (r * jax.nn.sigmoid(r))).astype(o_ref.dtype)


def _gla(proj, glr, wg, bg, nw, B, S, L=512):
    T = proj.shape[0]
    nl = S // L
    rb = lambda b, l: b * nl + l
    return pl.pallas_call(
        functools.partial(_gla_kernel, n_chunks=L // GLA_CHUNK),
        grid=(B, nl),
        in_specs=[
            pl.BlockSpec((L, GLA_KEY_WIDTH), lambda b, l: (rb(b, l), COL_Q // GLA_KEY_WIDTH)),
            pl.BlockSpec((L, GLA_KEY_WIDTH), lambda b, l: (rb(b, l), COL_K // GLA_KEY_WIDTH)),
            pl.BlockSpec((L, GLA_WIDTH), lambda b, l: (rb(b, l), COL_V // GLA_WIDTH)),
            pl.BlockSpec((L, GLA_WIDTH), lambda b, l: (rb(b, l), COL_R // GLA_WIDTH)),
            pl.BlockSpec((L, LANES), lambda b, l: (rb(b, l), 0)),
            pl.BlockSpec((LANES, GLA_KEY_WIDTH), lambda b, l: (0, 0)),
            pl.BlockSpec((1, GLA_KEY_WIDTH), lambda b, l: (0, 0)),
            pl.BlockSpec((1, GLA_WIDTH), lambda b, l: (0, 0)),
        ],
        out_specs=pl.BlockSpec((L, GLA_WIDTH), lambda b, l: (rb(b, l), 0)),
        out_shape=jax.ShapeDtypeStruct((T, GLA_WIDTH), BF16),
        scratch_shapes=[pltpu.VMEM((GLA_HEADS, GLA_DK, GLA_DV), F32)],
        compiler_params=_params(("parallel", "arbitrary"), 32),
        name="gla",
    )(proj, proj, proj, proj, glr, wg, bg, nw)


def _layer_norm(h, w, b):
    mu = jnp.mean(h, axis=-1, keepdims=True)
    hc = h - mu
    var = jnp.mean(jnp.square(hc), axis=-1, keepdims=True)
    return hc * lax.rsqrt(var + LN_EPS) * w + b


def _outproj_kernel(yp_ref, yg_ref, x_ref, w_ref, lw_ref, lb_ref, x1_ref, x1t_ref):
    for r0 in range(0, x_ref.shape[0], ROW_SUB):
        rows = slice(r0, r0 + ROW_SUB)
        mix = (jnp.dot(yp_ref[rows, :], w_ref[0:POOL_WIDTH, :], preferred_element_type=F32)
               + jnp.dot(yg_ref[rows, :], w_ref[POOL_WIDTH:, :], preferred_element_type=F32))
        x1 = _layer_norm(ALPHA * x_ref[rows, :] + mix, lw_ref[...], lb_ref[...])
        x1_ref[rows, :] = x1
        x1t_ref[:, rows] = jnp.transpose(x1).astype(BF16)


def _outproj(y_pool, y_gla, x2, w_out, ln_w, ln_b, tm=512):
    T = x2.shape[0]
    return pl.pallas_call(
        _outproj_kernel,
        grid=(T // tm,),
        in_specs=[
            pl.BlockSpec((tm, POOL_WIDTH), lambda i: (i, 0)),
            pl.BlockSpec((tm, GLA_WIDTH), lambda i: (i, 0)),
            pl.BlockSpec((tm, D_MODEL), lambda i: (i, 0)),
            pl.BlockSpec((D_MODEL, D_MODEL), lambda i: (0, 0), pipeline_mode=pl.Buffered(1)),
            pl.BlockSpec((1, D_MODEL), lambda i: (0, 0)),
            pl.BlockSpec((1, D_MODEL), lambda i: (0, 0)),
        ],
        out_specs=[
            pl.BlockSpec((tm, D_MODEL), lambda i: (i, 0)),
            pl.BlockSpec((D_MODEL, tm), lambda i: (0, i)),
        ],
        out_shape=[
            jax.ShapeDtypeStruct((T, D_MODEL), F32),
            jax.ShapeDtypeStruct((D_MODEL, T), BF16),
        ],
        compiler_params=_params(("parallel",), 48),
        name="outproj",
    )(y_pool, y_gla, x2, w_out, ln_w, ln_b)


N_SORT = PEER_TOPK + 1


def _sort_network(n):
    pairs = []

    def merge(lo, m, r):
        step = 2 * r
        if step < m:
            merge(lo, m, step)
            merge(lo + r, m, step)
            pairs.extend((i, i + r) for i in range(lo + r, lo + m - r, step))
        else:
            pairs.append((lo, lo + r))

    def sort(lo, m):
        if m > 1:
            sort(lo, m // 2)
            sort(lo + m // 2, m // 2)
            merge(lo, m, 1)

    sort(0, n)
    return tuple(pairs)


def _pop_sorted(v, n_out):
    nv = len(v)
    width = 1 << (nv - 1).bit_length()
    for i, j in _sort_network(width):
        if j < nv:
            v[i], v[j] = jnp.maximum(v[i], v[j]), jnp.minimum(v[i], v[j])
    tops = []
    rid = lax.broadcasted_iota(jnp.int32, (SUBLANES, LANES), 0).astype(F32)
    for kk in range(n_out):
        m = jnp.max(v[0], axis=0, keepdims=True)
        tops.append(m)
        first = jnp.min(jnp.where(v[0] == m, rid, float(SUBLANES)), axis=0, keepdims=True)
        hit = rid == first
        for k in range(min(n_out - 1 - kk, nv)):
            v[k] = jnp.where(hit, v[k + 1] if k + 1 < nv else NEG_INF, v[k])
    return tops


def _sorted_top(arr):
    return _pop_sorted([arr[k:k + SUBLANES] for k in range(0, arr.shape[0], SUBLANES)], N_SORT)


def _rows_to_tile(rows):
    rid = lax.broadcasted_iota(jnp.int32, (SUBLANES, LANES), 0)
    tile = jnp.full((SUBLANES, LANES), NEG_INF, F32)
    for k, r in enumerate(rows):
        tile = jnp.where(rid == k, r, tile)
    return tile


def _pair_stats(a, b):
    r8 = lax.broadcasted_iota(jnp.int32, (SUBLANES, LANES), 0)
    b_lo, b_hi, a_hi = _rows_to_tile(b[0:8]), _rows_to_tile(b[8:16]), _rows_to_tile(a[8:16])
    p2 = jnp.where(r8 < 5, a[2] + b_lo, jnp.where(r8 == 5, a[16] + b[0], jnp.where(r8 == 6, a[0] + b[16], NEG_INF)))
    pieces = [
        a[0] + b_lo, a[0] + b_hi, a[1] + b_lo, p2,
        jnp.where(r8 < 4, a[3] + b_lo, NEG_INF),
        jnp.where(r8 < 3, a[4] + b_lo, NEG_INF),
        jnp.where(r8 < 2, a[5] + b_lo, NEG_INF),
        jnp.where(r8 < 2, a[6] + b_lo, NEG_INF),
        jnp.where(r8 < 2, a[7] + b_lo, NEG_INF),
        a_hi + b[0],
    ]
    sums = _pop_sorted(pieces, N_SORT)
    top16 = jnp.concatenate([_rows_to_tile(sums[0:8]), _rows_to_tile(sums[8:16])], axis=0)
    z = jnp.sum(jnp.exp(top16 - sums[0]), axis=0, keepdims=True)
    return 0.5 * (sums[PEER_TOPK - 1] + sums[PEER_TOPK]), 1.0 / z


def _split_bf16(x):
    hi = x.astype(BF16)
    return hi, (x - hi.astype(F32)).astype(BF16)


def _query_kernel(x1t_ref, wq_ref, khi_ref, klo_ref, c1_ref, n1_ref, e2_ref, r2_ref, q_ref, s1_scr, s2_scr, *, tm):
    q_ref[...] = jnp.dot(wq_ref[...], x1t_ref[...], preferred_element_type=F32)
    for h in range(PEER_HEADS):
        for p in range(2):
            hp = 2 * h + p
            q_hi, q_lo = _split_bf16(q_ref[hp * PEER_HALF:(hp + 1) * PEER_HALF, :])
            k_hi, k_lo = khi_ref[hp], klo_ref[hp]
            sc = (jnp.dot(k_hi, q_hi, preferred_element_type=F32)
                  + jnp.dot(k_hi, q_lo, preferred_element_type=F32)
                  + jnp.dot(k_lo, q_hi, preferred_element_type=F32))
            if p == 0:
                s1_scr[...] = sc
            else:
                s2_scr[...] = sc
        for tc in range(tm // LANES):
            lanes = slice(tc * LANES, (tc + 1) * LANES)
            s1 = s1_scr[:, lanes]
            s2 = s2_scr[:, lanes]
            ta, tb = _sorted_top(s1), _sorted_top(s2)
            tau, rz = _pair_stats(ta, tb)
            n1 = jnp.zeros_like(s1)
            r2 = jnp.zeros_like(s2)
            for jj in range(PEER_TOPK):
                n1 = jnp.where(s1 >= tau - tb[jj], float(jj + 1), n1)
                r2 = jnp.where(s2 < tb[jj], float(jj + 1), r2)
            c1_ref[h, tc] = jnp.exp(s1 - ta[0])
            n1_ref[h, tc] = n1
            e2_ref[h, :, lanes] = (jnp.exp(s2 - tb[0]) * rz).astype(BF16)
            r2_ref[h, :, lanes] = r2.astype(BF16)


def _query(x1t, wq_t, keys_hi, keys_lo, tm=256):
    T = x1t.shape[1]
    nc = tm // LANES
    row_spec = pl.BlockSpec((PEER_HEADS, nc, PEER_NKEYS, LANES), lambda i: (0, i, 0, 0))
    col_spec = pl.BlockSpec((PEER_HEADS, PEER_NKEYS, tm), lambda i: (0, 0, i))
    key_spec = pl.BlockSpec((2 * PEER_HEADS, PEER_NKEYS, PEER_HALF), lambda i: (0, 0, 0))
    row_shape = jax.ShapeDtypeStruct((PEER_HEADS, T // LANES, PEER_NKEYS, LANES), F32)
    col_shape = jax.ShapeDtypeStruct((PEER_HEADS, PEER_NKEYS, T), BF16)
    return pl.pallas_call(
        functools.partial(_query_kernel, tm=tm),
        grid=(T // tm,),
        in_specs=[
            pl.BlockSpec((D_MODEL, tm), lambda i: (0, i)),
            pl.BlockSpec((D_MODEL, D_MODEL), lambda i: (0, 0), pipeline_mode=pl.Buffered(1)),
            key_spec, key_spec,
        ],
        out_specs=[row_spec, row_spec, col_spec, col_spec],
        out_shape=[row_shape, row_shape, col_shape, col_shape],
        scratch_shapes=[pltpu.VMEM((D_MODEL, tm), F32), pltpu.VMEM((PEER_NKEYS, tm), F32),
                        pltpu.VMEM((PEER_NKEYS, tm), F32)],
        compiler_params=_params(("parallel",), 48),
        name="query",
    )(x1t, wq_t, keys_hi, keys_lo)


PEER_SB = 64
PEER_RG = 2
SUBLANES = 8
PEER_TE = 1024


def _gelu(x):
    return 0.5 * x * (1.0 + lax.erf(x * (1.0 / math.sqrt(2.0))))


def _tables_kernel(u_ref, v_ref, ub_ref, vt_ref):
    ub_ref[...] = u_ref[...].astype(BF16)
    vt_ref[...] = jnp.transpose(v_ref[...]).astype(BF16)


def _tables(peer_u, peer_v):
    n_exp = peer_u.shape[0]
    nj = n_exp // PEER_TE
    return pl.pallas_call(
        _tables_kernel,
        grid=(nj,),
        in_specs=[pl.BlockSpec((PEER_TE, D_MODEL), lambda j: (j, 0)),
                  pl.BlockSpec((PEER_TE, D_MODEL), lambda j: (j, 0))],
        out_specs=[pl.BlockSpec((PEER_TE, D_MODEL), lambda j: (j, 0)),
                   pl.BlockSpec((None, D_MODEL, PEER_TE), lambda j: (j, 0, 0))],
        out_shape=[jax.ShapeDtypeStruct((n_exp, D_MODEL), BF16),
                   jax.ShapeDtypeStruct((nj, D_MODEL, PEER_TE), BF16)],
        compiler_params=_params(("parallel",), 56),
        name="tables",
    )(peer_u, peer_v)


def _bcast_row_bf16(tile, ri, rows):
    packed = jnp.broadcast_to(tile[ri:ri + 1, :], (2 * SUBLANES, LANES)).astype(BF16)
    return jnp.concatenate([packed] * (rows // (2 * SUBLANES)), axis=0)


def _peer_kernel(x1t_ref, u_ref, vt_ref, c1_ref, n1_ref, e2_ref, r2_ref, y_ref,
                 acc_ref, st_scr, ht_scr, *, tm, te):
    j = pl.program_id(1)
    n1 = te // PEER_NKEYS
    grows = PEER_RG * PEER_NKEYS

    @pl.when(j == 0)
    def _():
        acc_ref[...] = jnp.zeros_like(acc_ref)

    st_scr[...] = jnp.dot(u_ref[...], x1t_ref[...], preferred_element_type=F32)

    nsb = PEER_NKEYS // PEER_SB
    tile_rows = pl.ds(pl.multiple_of(j * n1, SUBLANES), n1)
    for gi in range(n1 // PEER_RG):
        crows = slice(gi * grows, (gi + 1) * grows)
        for tc in range(tm // LANES):
            lanes = slice(tc * LANES, (tc + 1) * LANES)
            g = [[jnp.zeros((PEER_SB, LANES), BF16) for _ in range(nsb)] for _ in range(PEER_RG)]
            for h in range(PEER_HEADS):
                c1_t = c1_ref[h, tc, tile_rows, :]
                n1_t = n1_ref[h, tc, tile_rows, :]
                c1b = [_bcast_row_bf16(c1_t, gi * PEER_RG + r, PEER_SB) for r in range(PEER_RG)]
                n1b = [_bcast_row_bf16(n1_t, gi * PEER_RG + r, PEER_SB) for r in range(PEER_RG)]
                for sb in range(nsb):
                    rows = slice(sb * PEER_SB, (sb + 1) * PEER_SB)
                    r2c = r2_ref[h, rows, lanes]
                    e2c = e2_ref[h, rows, lanes]
                    for r in range(PEER_RG):
                        g[r][sb] = g[r][sb] + c1b[r] * jnp.where(r2c < n1b[r], e2c, jnp.zeros_like(e2c))
            for r in range(PEER_RG):
                for sb in range(nsb):
                    base = gi * grows + r * PEER_NKEYS + sb * PEER_SB
                    srows = slice(base, base + PEER_SB)
                    ht_scr[srows, lanes] = g[r][sb] * _gelu(st_scr[srows, lanes].astype(BF16))
    acc_ref[...] += jnp.dot(vt_ref[...], ht_scr[...], preferred_element_type=F32)

    @pl.when(j == pl.num_programs(1) - 1)
    def _():
        y_ref[...] = jnp.transpose(acc_ref[...])


def _peer(x1t, u_tab, vt_tiles, c1, n1, e2, r2, tm=512):
    T = x1t.shape[1]
    nj, _, te = vt_tiles.shape
    row_spec = pl.BlockSpec((PEER_HEADS, tm // LANES, PEER_NKEYS, LANES), lambda i, j: (0, i, 0, 0))
    col_spec = pl.BlockSpec((PEER_HEADS, PEER_NKEYS, tm), lambda i, j: (0, 0, i))
    return pl.pallas_call(
        functools.partial(_peer_kernel, tm=tm, te=te),
        grid=(T // tm, nj),
        in_specs=[
            pl.BlockSpec((D_MODEL, tm), lambda i, j: (0, i)),
            pl.BlockSpec((te, D_MODEL), lambda i, j: (j, 0)),
            pl.BlockSpec((None, D_MODEL, te), lambda i, j: (j, 0, 0)),
            row_spec, row_spec, col_spec, col_spec,
        ],
        out_specs=pl.BlockSpec((tm, D_MODEL), lambda i, j: (i, 0)),
        out_shape=jax.ShapeDtypeStruct((T, D_MODEL), F32),
        scratch_shapes=[
            pltpu.VMEM((D_MODEL, tm), F32),
            pltpu.VMEM((te, tm), F32),
            pltpu.VMEM((te, tm), BF16),
        ],
        compiler_params=_params(("parallel", "arbitrary"), 56),
        name="peer",
    )(x1t, u_tab, vt_tiles, c1, n1, e2, r2)


def _final_kernel(x1_ref, y_ref, p_ref, wg_ref, wp_ref, lw_ref, lb_ref, o_ref):
    for r0 in range(0, x1_ref.shape[0], ROW_SUB):
        rows = slice(r0, r0 + ROW_SUB)
        x1 = x1_ref[rows, :]
        gate = jax.nn.sigmoid(jnp.dot(x1.astype(BF16), wg_ref[...], preferred_element_type=F32))
        emb = jnp.dot(p_ref[rows, :].astype(BF16), wp_ref[...], preferred_element_type=F32)
        o_ref[rows, :] = _layer_norm(ALPHA * x1 + y_ref[rows, :] + gate * emb, lw_ref[...], lb_ref[...])


def _final(x1, y_ffn, p2, w_gate, w_proj, ln_w, ln_b, tm=512):
    T = x1.shape[0]
    return pl.pallas_call(
        _final_kernel,
        grid=(T // tm,),
        in_specs=[
            pl.BlockSpec((tm, D_MODEL), lambda i: (i, 0)),
            pl.BlockSpec((tm, D_MODEL), lambda i: (i, 0)),
            pl.BlockSpec((tm, PLE_DIM), lambda i: (i, 0)),
            pl.BlockSpec((D_MODEL, D_MODEL), lambda i: (0, 0), pipeline_mode=pl.Buffered(1)),
            pl.BlockSpec((PLE_DIM, D_MODEL), lambda i: (0, 0), pipeline_mode=pl.Buffered(1)),
            pl.BlockSpec((1, D_MODEL), lambda i: (0, 0)),
            pl.BlockSpec((1, D_MODEL), lambda i: (0, 0)),
        ],
        out_specs=pl.BlockSpec((tm, D_MODEL), lambda i: (i, 0)),
        out_shape=jax.ShapeDtypeStruct((T, D_MODEL), F32),
        compiler_params=_params(("parallel",), 48),
        name="final",
    )(x1, y_ffn, p2, w_gate, w_proj, ln_w, ln_b)


def _layer(x2, p2, B, S, w_in, gla_w_gate_up, gla_b_gate, gla_norm_w, pool_w, pool_scale, w_out,
           ln1_w, ln1_b, peer_w_query, peer_sub_keys, peer_u, peer_v, ple_w_gate, ple_w_proj, ln2_w, ln2_b):
    glr0 = COL_R
    w_main = jnp.concatenate([w_in[:, :glr0], w_in[:, glr0 + GLA_GATE_RANK:]], axis=1).astype(BF16)
    w_glr = jnp.pad(w_in[:, glr0:glr0 + GLA_GATE_RANK], ((0, 0), (0, LANES - GLA_GATE_RANK))).astype(BF16)
    proj, glr = _proj(x2, w_main, w_glr)

    y_pool = _pool(proj, pool_w.astype(BF16), pool_scale.reshape(1, POOL_WIDTH), S)

    wg = jnp.pad(gla_w_gate_up, ((0, LANES - GLA_GATE_RANK), (0, 0))).astype(BF16)
    y_gla = _gla(proj, glr, wg, gla_b_gate.reshape(1, GLA_KEY_WIDTH),
                 gla_norm_w.reshape(1, GLA_WIDTH), B, S)

    x1, x1t = _outproj(y_pool, y_gla, x2, w_out.astype(BF16),
                       ln1_w.reshape(1, D_MODEL), ln1_b.reshape(1, D_MODEL))

    keys = peer_sub_keys.reshape(2 * PEER_HEADS, PEER_NKEYS, PEER_HALF)
    keys_hi, keys_lo = _split_bf16(keys)
    c1, n1, e2, r2 = _query(x1t, peer_w_query.T.astype(BF16), keys_hi, keys_lo)
    u_bf16, vt_tiles = _tables(peer_u, peer_v)
    y_ffn = _peer(x1t, u_bf16, vt_tiles, c1, n1, e2, r2)

    return _final(x1, y_ffn, p2, ple_w_gate.astype(BF16), ple_w_proj.astype(BF16),
                  ln2_w.reshape(1, D_MODEL), ln2_b.reshape(1, D_MODEL))


def kernel(x, p, w_in, gla_w_gate_up, gla_b_gate, gla_norm_w, pool_w, pool_scale, w_out, ln1_w, ln1_b,
           peer_w_query, peer_sub_keys, peer_u, peer_v, ple_w_gate, ple_w_proj, ln2_w, ln2_b):
    B, S, D = x.shape
    x2 = x.reshape(B * S, D)
    for i in range(w_in.shape[0]):
        x2 = _layer(x2, p[i].reshape(B * S, PLE_DIM), B, S, w_in[i], gla_w_gate_up[i], gla_b_gate[i],
                    gla_norm_w[i], pool_w[i], pool_scale[i], w_out[i], ln1_w[i], ln1_b[i],
                    peer_w_query[i], peer_sub_keys[i], peer_u[i], peer_v[i], ple_w_gate[i],
                    ple_w_proj[i], ln2_w[i], ln2_b[i])
    return x2.reshape(B, S, D)
```

```python
import functools
import math

import jax
import jax.numpy as jnp
from jax import lax
from jax.experimental import pallas as pl
from jax.experimental.pallas import tpu as pltpu

F32 = jnp.float32
BF16 = jnp.bfloat16

D_MODEL = 2048
PLE_DIM = 256
POOL_WIDTH = 1024
POOL_WINDOWS = (2, 4, 8, 16)
POOL_GC = 256
POOL_HALO = 16
GLA_WIDTH = 1024
GLA_HEADS = 4
GLA_DV = 256
GLA_DK = 128
GLA_KEY_WIDTH = 512
GLA_GATE_RANK = 16
GLA_GATE_TEMP = 16.0
GLA_CHUNK = 64
PEER_HEADS = 8
PEER_NKEYS = 128
PEER_HALF = 128
PEER_TOPK = 16
DEPTH = 1
ALPHA = float((2 * DEPTH) ** 0.25)
LN_EPS = 1e-5
RMS_EPS = 1e-6
LANES = 128
NEG_INF = float("-inf")
ROW_SUB = 256

COL_Q = POOL_WIDTH
COL_K = COL_Q + GLA_KEY_WIDTH
COL_V = COL_K + GLA_KEY_WIDTH
COL_R = COL_V + GLA_WIDTH
PROJ_COLS = COL_R + GLA_WIDTH


def _params(sem, vmem_mib):
    return pltpu.CompilerParams(dimension_semantics=sem, vmem_limit_bytes=vmem_mib * 1024 * 1024)


def _proj_kernel(x_ref, w_ref, wg_ref, o_ref, glr_ref, xb_ref):
    @pl.when(pl.program_id(1) == 0)
    def _():
        xb = x_ref[...].astype(BF16)
        xb_ref[...] = xb
        glr_ref[...] = jnp.dot(xb, wg_ref[...], preferred_element_type=F32)

    o_ref[...] = jnp.dot(xb_ref[...], w_ref[...], preferred_element_type=F32).astype(o_ref.dtype)


def _proj(x2, w_main, w_glr, tm=1024, tn=2048):
    T = x2.shape[0]
    return pl.pallas_call(
        _proj_kernel,
        grid=(T // tm, PROJ_COLS // tn),
        in_specs=[
            pl.BlockSpec((tm, D_MODEL), lambda i, n: (i, 0)),
            pl.BlockSpec((D_MODEL, tn), lambda i, n: (0, n)),
            pl.BlockSpec((D_MODEL, LANES), lambda i, n: (0, 0)),
        ],
        out_specs=[
            pl.BlockSpec((tm, tn), lambda i, n: (i, n)),
            pl.BlockSpec((tm, LANES), lambda i, n: (i, 0)),
        ],
        out_shape=[
            jax.ShapeDtypeStruct((T, PROJ_COLS), BF16),
            jax.ShapeDtypeStruct((T, LANES), F32),
        ],
        scratch_shapes=[pltpu.VMEM((tm, D_MODEL), BF16)],
        compiler_params=_params(("parallel", "arbitrary"), 52),
        name="proj",
    )(x2, w_main, w_glr)


def _pool_kernel(u_ref, halo_ref, w_ref, sc_ref, o_ref, ext_ref, *, tiles_per_seq, tm):
    t = pl.program_id(0) % tiles_per_seq
    halo = jnp.where(t == 0, 0.0, halo_ref[...].astype(F32))
    ext_ref[0:POOL_HALO, :] = halo
    ext_ref[POOL_HALO:, :] = u_ref[...].astype(F32)
    pos = t * tm + lax.broadcasted_iota(jnp.int32, (tm, 1), 0)
    for g, w in enumerate(POOL_WINDOWS):
        cols = slice(g * POOL_GC, (g + 1) * POOL_GC)
        u = ext_ref[POOL_HALO:, cols]
        acc = u
        for j in range(1, w):
            acc = acc + ext_ref[POOL_HALO - j:POOL_HALO - j + tm, cols]
        cnt = jnp.minimum(pos + 1, w).astype(F32)
        d = acc / cnt - u
        y = jnp.dot(d.astype(BF16), w_ref[g], preferred_element_type=F32)
        o_ref[:, cols] = (y * sc_ref[:, cols]).astype(o_ref.dtype)


def _pool(proj, pool_w, pool_scale, S, tm=512):
    T = proj.shape[0]
    hb = tm // POOL_HALO
    return pl.pallas_call(
        functools.partial(_pool_kernel, tiles_per_seq=S // tm, tm=tm),
        grid=(T // tm,),
        in_specs=[
            pl.BlockSpec((tm, POOL_WIDTH), lambda i: (i, 0)),
            pl.BlockSpec((POOL_HALO, POOL_WIDTH), lambda i: (jnp.maximum(i * hb - 1, 0), 0)),
            pl.BlockSpec((len(POOL_WINDOWS), POOL_GC, POOL_GC), lambda i: (0, 0, 0)),
            pl.BlockSpec((1, POOL_WIDTH), lambda i: (0, 0)),
        ],
        out_specs=pl.BlockSpec((tm, POOL_WIDTH), lambda i: (i, 0)),
        out_shape=jax.ShapeDtypeStruct((T, POOL_WIDTH), BF16),
        scratch_shapes=[pltpu.VMEM((POOL_HALO + tm, POOL_WIDTH), F32)],
        compiler_params=_params(("parallel",), 32),
        name="pool",
    )(proj, proj, pool_w, pool_scale)


def _gla_kernel(q_ref, k_ref, v_ref, r_ref, glr_ref, wg_ref, bg_ref, nw_ref, o_ref, s_ref, *, n_chunks):
    @pl.when(pl.program_id(1) == 0)
    def _():
        s_ref[...] = jnp.zeros_like(s_ref)

    C = GLA_CHUNK
    row = lax.broadcasted_iota(jnp.int32, (C, C), 0)
    col = lax.broadcasted_iota(jnp.int32, (C, C), 1)
    causal = col <= row
    tril = causal.astype(BF16)
    wg = wg_ref[...]
    bg = bg_ref[...]
    nt = (((1,), (1,)), ((), ()))
    for c in range(n_chunks):
        rows = slice(c * C, (c + 1) * C)
        z = jnp.dot(glr_ref[rows, :].astype(BF16), wg, preferred_element_type=F32) + bg
        g = jax.nn.log_sigmoid(z) / GLA_GATE_TEMP
        g_hi, g_lo = _split_bf16(g)
        b_all = (jnp.dot(tril, g_hi, preferred_element_type=F32)
                 + jnp.dot(tril, g_lo, preferred_element_type=F32))
        for h in range(GLA_HEADS):
            kc = slice(h * GLA_DK, (h + 1) * GLA_DK)
            vc = slice(h * GLA_DV, (h + 1) * GLA_DV)
            b = b_all[:, kc]
            b_last = b[C - 1:C, :]
            b_mid = b[C // 2 - 1:C // 2, :]
            q = q_ref[rows, kc].astype(F32) * (GLA_DK ** -0.5)
            k = k_ref[rows, kc].astype(F32)
            v = v_ref[rows, vc]
            q_state = (q * jnp.exp(b)).astype(BF16)
            q_in = (q * jnp.exp(b - b_mid)).astype(BF16)
            k_in = (k * jnp.exp(b_mid - b)).astype(BF16)
            k_out = k * jnp.exp(b_last - b)
            attn = lax.dot_general(q_in, k_in, nt, preferred_element_type=F32)
            attn = jnp.where(causal, attn, 0.0).astype(BF16)
            s = s_ref[h]
            o = (jnp.dot(attn, v, preferred_element_type=F32)
                 + jnp.dot(q_state, s.astype(BF16), preferred_element_type=F32))
            decay = jnp.transpose(jnp.broadcast_to(jnp.exp(b_last), (C, GLA_DK)))[:, 0:1]
            s_ref[h] = decay * s + jnp.dot(jnp.transpose(k_out).astype(BF16), v, preferred_element_type=F32)
            o = o * lax.rsqrt(jnp.mean(jnp.square(o), axis=-1, keepdims=True) + RMS_EPS)
            o = o * nw_ref[:, vc]
            r = r_ref[rows, vc].astype(F32)
            o_ref[rows, vc] = (o * (r * jax.nn.sigmoid(r))).astype(o_ref.dtype)


def _gla(proj, glr, wg, bg, nw, B, S, L=512):
    T = proj.shape[0]
    nl = S // L
    rb = lambda b, l: b * nl + l
    return pl.pallas_call(
        functools.partial(_gla_kernel, n_chunks=L // GLA_CHUNK),
        grid=(B, nl),
        in_specs=[
            pl.BlockSpec((L, GLA_KEY_WIDTH), lambda b, l: (rb(b, l), COL_Q // GLA_KEY_WIDTH)),
            pl.BlockSpec((L, GLA_KEY_WIDTH), lambda b, l: (rb(b, l), COL_K // GLA_KEY_WIDTH)),
            pl.BlockSpec((L, GLA_WIDTH), lambda b, l: (rb(b, l), COL_V // GLA_WIDTH)),
            pl.BlockSpec((L, GLA_WIDTH), lambda b, l: (rb(b, l), COL_R // GLA_WIDTH)),
            pl.BlockSpec((L, LANES), lambda b, l: (rb(b, l), 0)),
            pl.BlockSpec((LANES, GLA_KEY_WIDTH), lambda b, l: (0, 0)),
            pl.BlockSpec((1, GLA_KEY_WIDTH), lambda b, l: (0, 0)),
            pl.BlockSpec((1, GLA_WIDTH), lambda b, l: (0, 0)),
        ],
        out_specs=pl.BlockSpec((L, GLA_WIDTH), lambda b, l: (rb(b, l), 0)),
        out_shape=jax.ShapeDtypeStruct((T, GLA_WIDTH), BF16),
        scratch_shapes=[pltpu.VMEM((GLA_HEADS, GLA_DK, GLA_DV), F32)],
        compiler_params=_params(("parallel", "arbitrary"), 32),
        name="gla",
    )(proj, proj, proj, proj, glr, wg, bg, nw)


def _layer_norm(h, w, b):
    mu = jnp.mean(h, axis=-1, keepdims=True)
    hc = h - mu
    var = jnp.mean(jnp.square(hc), axis=-1, keepdims=True)
    return hc * lax.rsqrt(var + LN_EPS) * w + b


def _outproj_kernel(yp_ref, yg_ref, x_ref, w_ref, lw_ref, lb_ref, x1_ref, x1t_ref):
    for r0 in range(0, x_ref.shape[0], ROW_SUB):
        rows = slice(r0, r0 + ROW_SUB)
        mix = (jnp.dot(yp_ref[rows, :], w_ref[0:POOL_WIDTH, :], preferred_element_type=F32)
               + jnp.dot(yg_ref[rows, :], w_ref[POOL_WIDTH:, :], preferred_element_type=F32))
        x1 = _layer_norm(ALPHA * x_ref[rows, :] + mix, lw_ref[...], lb_ref[...])
        x1_ref[rows, :] = x1
        x1t_ref[:, rows] = jnp.transpose(x1).astype(BF16)


def _outproj(y_pool, y_gla, x2, w_out, ln_w, ln_b, tm=512):
    T = x2.shape[0]
    return pl.pallas_call(
        _outproj_kernel,
        grid=(T // tm,),
        in_specs=[
            pl.BlockSpec((tm, POOL_WIDTH), lambda i: (i, 0)),
            pl.BlockSpec((tm, GLA_WIDTH), lambda i: (i, 0)),
            pl.BlockSpec((tm, D_MODEL), lambda i: (i, 0)),
            pl.BlockSpec((D_MODEL, D_MODEL), lambda i: (0, 0), pipeline_mode=pl.Buffered(1)),
            pl.BlockSpec((1, D_MODEL), lambda i: (0, 0)),
            pl.BlockSpec((1, D_MODEL), lambda i: (0, 0)),
        ],
        out_specs=[
            pl.BlockSpec((tm, D_MODEL), lambda i: (i, 0)),
            pl.BlockSpec((D_MODEL, tm), lambda i: (0, i)),
        ],
        out_shape=[
            jax.ShapeDtypeStruct((T, D_MODEL), F32),
            jax.ShapeDtypeStruct((D_MODEL, T), BF16),
        ],
        compiler_params=_params(("parallel",), 48),
        name="outproj",
    )(y_pool, y_gla, x2, w_out, ln_w, ln_b)


N_SORT = PEER_TOPK + 1


def _sort_network(n):
    pairs = []

    def merge(lo, m, r):
        step = 2 * r
        if step < m:
            merge(lo, m, step)
            merge(lo + r, m, step)
            pairs.extend((i, i + r) for i in range(lo + r, lo + m - r, step))
        else:
            pairs.append((lo, lo + r))

    def sort(lo, m):
        if m > 1:
            sort(lo, m // 2)
            sort(lo + m // 2, m // 2)
            merge(lo, m, 1)

    sort(0, n)
    return tuple(pairs)


def _pop_sorted(v, n_out):
    nv = len(v)
    width = 1 << (nv - 1).bit_length()
    for i, j in _sort_network(width):
        if j < nv:
            v[i], v[j] = jnp.maximum(v[i], v[j]), jnp.minimum(v[i], v[j])
    tops = []
    rid = lax.broadcasted_iota(jnp.int32, (SUBLANES, LANES), 0).astype(F32)
    for kk in range(n_out):
        m = jnp.max(v[0], axis=0, keepdims=True)
        tops.append(m)
        first = jnp.min(jnp.where(v[0] == m, rid, float(SUBLANES)), axis=0, keepdims=True)
        hit = rid == first
        for k in range(min(n_out - 1 - kk, nv)):
            v[k] = jnp.where(hit, v[k + 1] if k + 1 < nv else NEG_INF, v[k])
    return tops


def _sorted_top(arr):
    return _pop_sorted([arr[k:k + SUBLANES] for k in range(0, arr.shape[0], SUBLANES)], N_SORT)


def _rows_to_tile(rows):
    rid = lax.broadcasted_iota(jnp.int32, (SUBLANES, LANES), 0)
    tile = jnp.full((SUBLANES, LANES), NEG_INF, F32)
    for k, r in enumerate(rows):
        tile = jnp.where(rid == k, r, tile)
    return tile


def _pair_stats(a, b):
    r8 = lax.broadcasted_iota(jnp.int32, (SUBLANES, LANES), 0)
    b_lo, b_hi, a_hi = _rows_to_tile(b[0:8]), _rows_to_tile(b[8:16]), _rows_to_tile(a[8:16])
    p2 = jnp.where(r8 < 5, a[2] + b_lo, jnp.where(r8 == 5, a[16] + b[0], jnp.where(r8 == 6, a[0] + b[16], NEG_INF)))
    pieces = [
        a[0] + b_lo, a[0] + b_hi, a[1] + b_lo, p2,
        jnp.where(r8 < 4, a[3] + b_lo, NEG_INF),
        jnp.where(r8 < 3, a[4] + b_lo, NEG_INF),
        jnp.where(r8 < 2, a[5] + b_lo, NEG_INF),
        jnp.where(r8 < 2, a[6] + b_lo, NEG_INF),
        jnp.where(r8 < 2, a[7] + b_lo, NEG_INF),
        a_hi + b[0],
    ]
    sums = _pop_sorted(pieces, N_SORT)
    top16 = jnp.concatenate([_rows_to_tile(sums[0:8]), _rows_to_tile(sums[8:16])], axis=0)
    z = jnp.sum(jnp.exp(top16 - sums[0]), axis=0, keepdims=True)
    return 0.5 * (sums[PEER_TOPK - 1] + sums[PEER_TOPK]), 1.0 / z


def _split_bf16(x):
    hi = x.astype(BF16)
    return hi, (x - hi.astype(F32)).astype(BF16)


def _query_kernel(x1t_ref, wq_ref, khi_ref, klo_ref, u_ref, v_ref,
                  c1_ref, n1_ref, e2_ref, r2_ref, ub_ref, vt_ref, q_ref, s1_scr, s2_scr, *, tm):
    ub_ref[...] = u_ref[...].astype(BF16)
    vt_ref[...] = jnp.transpose(v_ref[...]).astype(BF16)
    q_ref[...] = jnp.dot(wq_ref[...], x1t_ref[...], preferred_element_type=F32)
    for h in range(PEER_HEADS):
        for p in range(2):
            hp = 2 * h + p
            q_hi, q_lo = _split_bf16(q_ref[hp * PEER_HALF:(hp + 1) * PEER_HALF, :])
            k_hi, k_lo = khi_ref[hp], klo_ref[hp]
            sc = (jnp.dot(k_hi, q_hi, preferred_element_type=F32)
                  + jnp.dot(k_hi, q_lo, preferred_element_type=F32)
                  + jnp.dot(k_lo, q_hi, preferred_element_type=F32))
            if p == 0:
                s1_scr[...] = sc
            else:
                s2_scr[...] = sc
        for tc in range(tm // LANES):
            lanes = slice(tc * LANES, (tc + 1) * LANES)
            s1 = s1_scr[:, lanes]
            s2 = s2_scr[:, lanes]
            ta, tb = _sorted_top(s1), _sorted_top(s2)
            tau, rz = _pair_stats(ta, tb)
            n1 = jnp.zeros_like(s1)
            r2 = jnp.zeros_like(s2)
            for jj in range(PEER_TOPK):
                n1 = jnp.where(s1 >= tau - tb[jj], float(jj + 1), n1)
                r2 = jnp.where(s2 < tb[jj], float(jj + 1), r2)
            c1_ref[h, tc] = jnp.exp(s1 - ta[0])
            n1_ref[h, tc] = n1
            e2_ref[h, :, lanes] = (jnp.exp(s2 - tb[0]) * rz).astype(BF16)
            r2_ref[h, :, lanes] = r2.astype(BF16)


def _query(x1t, wq_t, keys_hi, keys_lo, peer_u, peer_v, tm=256):
    T = x1t.shape[1]
    n_exp = peer_u.shape[0]
    steps = T // tm
    slab = n_exp // steps
    per_tile = PEER_TE // slab
    assert slab * steps == n_exp and per_tile * slab == PEER_TE and slab % LANES == 0
    nc = tm // LANES
    row_spec = pl.BlockSpec((PEER_HEADS, nc, PEER_NKEYS, LANES), lambda i: (0, i, 0, 0))
    col_spec = pl.BlockSpec((PEER_HEADS, PEER_NKEYS, tm), lambda i: (0, 0, i))
    key_spec = pl.BlockSpec((2 * PEER_HEADS, PEER_NKEYS, PEER_HALF), lambda i: (0, 0, 0))
    tab_spec = pl.BlockSpec((slab, D_MODEL), lambda i: (i, 0))
    row_shape = jax.ShapeDtypeStruct((PEER_HEADS, T // LANES, PEER_NKEYS, LANES), F32)
    col_shape = jax.ShapeDtypeStruct((PEER_HEADS, PEER_NKEYS, T), BF16)
    return pl.pallas_call(
        functools.partial(_query_kernel, tm=tm),
        grid=(steps,),
        in_specs=[
            pl.BlockSpec((D_MODEL, tm), lambda i: (0, i)),
            pl.BlockSpec((D_MODEL, D_MODEL), lambda i: (0, 0), pipeline_mode=pl.Buffered(1)),
            key_spec, key_spec, tab_spec, tab_spec,
        ],
        out_specs=[row_spec, row_spec, col_spec, col_spec, tab_spec,
                   pl.BlockSpec((None, D_MODEL, slab), lambda i: (i // per_tile, 0, i % per_tile))],
        out_shape=[row_shape, row_shape, col_shape, col_shape,
                   jax.ShapeDtypeStruct((n_exp, D_MODEL), BF16),
                   jax.ShapeDtypeStruct((n_exp // PEER_TE, D_MODEL, PEER_TE), BF16)],
        scratch_shapes=[pltpu.VMEM((D_MODEL, tm), F32), pltpu.VMEM((PEER_NKEYS, tm), F32),
                        pltpu.VMEM((PEER_NKEYS, tm), F32)],
        compiler_params=_params(("parallel",), 56),
        name="query",
    )(x1t, wq_t, keys_hi, keys_lo, peer_u, peer_v)


PEER_SB = 64
PEER_RG = 2
SUBLANES = 8
PEER_TE = 1024


def _gelu(x):
    return 0.5 * x * (1.0 + lax.erf(x * (1.0 / math.sqrt(2.0))))


def _bcast_row_bf16(tile, ri, rows):
    packed = jnp.broadcast_to(tile[ri:ri + 1, :], (2 * SUBLANES, LANES)).astype(BF16)
    return jnp.concatenate([packed] * (rows // (2 * SUBLANES)), axis=0)


def _peer_kernel(x1t_ref, u_ref, vt_ref, c1_ref, n1_ref, e2_ref, r2_ref, y_ref,
                 acc_ref, st_scr, ht_scr, *, tm, te):
    j = pl.program_id(1)
    n1 = te // PEER_NKEYS
    grows = PEER_RG * PEER_NKEYS

    @pl.when(j == 0)
    def _():
        acc_ref[...] = jnp.zeros_like(acc_ref)

    st_scr[...] = jnp.dot(u_ref[...], x1t_ref[...], preferred_element_type=F32)

    nsb = PEER_NKEYS // PEER_SB
    tile_rows = pl.ds(pl.multiple_of(j * n1, SUBLANES), n1)
    for gi in range(n1 // PEER_RG):
        crows = slice(gi * grows, (gi + 1) * grows)
        for tc in range(tm // LANES):
            lanes = slice(tc * LANES, (tc + 1) * LANES)
            g = [[jnp.zeros((PEER_SB, LANES), BF16) for _ in range(nsb)] for _ in range(PEER_RG)]
            for h in range(PEER_HEADS):
                c1_t = c1_ref[h, tc, tile_rows, :]
                n1_t = n1_ref[h, tc, tile_rows, :]
                c1b = [_bcast_row_bf16(c1_t, gi * PEER_RG + r, PEER_SB) for r in range(PEER_RG)]
                n1b = [_bcast_row_bf16(n1_t, gi * PEER_RG + r, PEER_SB) for r in range(PEER_RG)]
                for sb in range(nsb):
                    rows = slice(sb * PEER_SB, (sb + 1) * PEER_SB)
                    r2c = r2_ref[h, rows, lanes]
                    e2c = e2_ref[h, rows, lanes]
                    for r in range(PEER_RG):
                        g[r][sb] = g[r][sb] + c1b[r] * jnp.where(r2c < n1b[r], e2c, jnp.zeros_like(e2c))
            for r in range(PEER_RG):
                for sb in range(nsb):
                    base = gi * grows + r * PEER_NKEYS + sb * PEER_SB
                    srows = slice(base, base + PEER_SB)
                    ht_scr[srows, lanes] = g[r][sb] * _gelu(st_scr[srows, lanes].astype(BF16))
    acc_ref[...] += jnp.dot(vt_ref[...], ht_scr[...], preferred_element_type=F32)

    @pl.when(j == pl.num_programs(1) - 1)
    def _():
        y_ref[...] = jnp.transpose(acc_ref[...])


def _peer(x1t, u_tab, vt_tiles, c1, n1, e2, r2, tm=512):
    T = x1t.shape[1]
    nj, _, te = vt_tiles.shape
    row_spec = pl.BlockSpec((PEER_HEADS, tm // LANES, PEER_NKEYS, LANES), lambda i, j: (0, i, 0, 0))
    col_spec = pl.BlockSpec((PEER_HEADS, PEER_NKEYS, tm), lambda i, j: (0, 0, i))
    return pl.pallas_call(
        functools.partial(_peer_kernel, tm=tm, te=te),
        grid=(T // tm, nj),
        in_specs=[
            pl.BlockSpec((D_MODEL, tm), lambda i, j: (0, i)),
            pl.BlockSpec((te, D_MODEL), lambda i, j: (j, 0)),
            pl.BlockSpec((None, D_MODEL, te), lambda i, j: (j, 0, 0)),
            row_spec, row_spec, col_spec, col_spec,
        ],
        out_specs=pl.BlockSpec((tm, D_MODEL), lambda i, j: (i, 0)),
        out_shape=jax.ShapeDtypeStruct((T, D_MODEL), F32),
        scratch_shapes=[
            pltpu.VMEM((D_MODEL, tm), F32),
            pltpu.VMEM((te, tm), F32),
            pltpu.VMEM((te, tm), BF16),
        ],
        compiler_params=_params(("parallel", "arbitrary"), 56),
        name="peer",
    )(x1t, u_tab, vt_tiles, c1, n1, e2, r2)


def _final_kernel(x1_ref, y_ref, p_ref, wg_ref, wp_ref, lw_ref, lb_ref, o_ref):
    for r0 in range(0, x1_ref.shape[0], ROW_SUB):
        rows = slice(r0, r0 + ROW_SUB)
        x1 = x1_ref[rows, :]
        gate = jax.nn.sigmoid(jnp.dot(x1.astype(BF16), wg_ref[...], preferred_element_type=F32))
        emb = jnp.dot(p_ref[rows, :].astype(BF16), wp_ref[...], preferred_element_type=F32)
        o_ref[rows, :] = _layer_norm(ALPHA * x1 + y_ref[rows, :] + gate * emb, lw_ref[...], lb_ref[...])


def _final(x1, y_ffn, p2, w_gate, w_proj, ln_w, ln_b, tm=512):
    T = x1.shape[0]
    return pl.pallas_call(
        _final_kernel,
        grid=(T // tm,),
        in_specs=[
            pl.BlockSpec((tm, D_MODEL), lambda i: (i, 0)),
            pl.BlockSpec((tm, D_MODEL), lambda i: (i, 0)),
            pl.BlockSpec((tm, PLE_DIM), lambda i: (i, 0)),
            pl.BlockSpec((D_MODEL, D_MODEL), lambda i: (0, 0), pipeline_mode=pl.Buffered(1)),
            pl.BlockSpec((PLE_DIM, D_MODEL), lambda i: (0, 0), pipeline_mode=pl.Buffered(1)),
            pl.BlockSpec((1, D_MODEL), lambda i: (0, 0)),
            pl.BlockSpec((1, D_MODEL), lambda i: (0, 0)),
        ],
        out_specs=pl.BlockSpec((tm, D_MODEL), lambda i: (i, 0)),
        out_shape=jax.ShapeDtypeStruct((T, D_MODEL), F32),
        compiler_params=_params(("parallel",), 48),
        name="final",
    )(x1, y_ffn, p2, w_gate, w_proj, ln_w, ln_b)


def _layer(x2, p2, B, S, w_in, gla_w_gate_up, gla_b_gate, gla_norm_w, pool_w, pool_scale, w_out,
           ln1_w, ln1_b, peer_w_query, peer_sub_keys, peer_u, peer_v, ple_w_gate, ple_w_proj, ln2_w, ln2_b):
    glr0 = COL_R
    w_main = jnp.concatenate([w_in[:, :glr0], w_in[:, glr0 + GLA_GATE_RANK:]], axis=1).astype(BF16)
    w_glr = jnp.pad(w_in[:, glr0:glr0 + GLA_GATE_RANK], ((0, 0), (0, LANES - GLA_GATE_RANK))).astype(BF16)
    proj, glr = _proj(x2, w_main, w_glr)

    y_pool = _pool(proj, pool_w.astype(BF16), pool_scale.reshape(1, POOL_WIDTH), S)

    wg = jnp.pad(gla_w_gate_up, ((0, LANES - GLA_GATE_RANK), (0, 0))).astype(BF16)
    y_gla = _gla(proj, glr, wg, gla_b_gate.reshape(1, GLA_KEY_WIDTH),
                 gla_norm_w.reshape(1, GLA_WIDTH), B, S)

    x1, x1t = _outproj(y_pool, y_gla, x2, w_out.astype(BF16),
                       ln1_w.reshape(1, D_MODEL), ln1_b.reshape(1, D_MODEL))

    keys = peer_sub_keys.reshape(2 * PEER_HEADS, PEER_NKEYS, PEER_HALF)
    keys_hi, keys_lo = _split_bf16(keys)
    c1, n1, e2, r2, u_bf16, vt_tiles = _query(x1t, peer_w_query.T.astype(BF16), keys_hi, keys_lo, peer_u, peer_v)
    y_ffn = _peer(x1t, u_bf16, vt_tiles, c1, n1, e2, r2)

    return _final(x1, y_ffn, p2, ple_w_gate.astype(BF16), ple_w_proj.astype(BF16),
                  ln2_w.reshape(1, D_MODEL), ln2_b.reshape(1, D_MODEL))


def kernel(x, p, w_in, gla_w_gate_up, gla_b_gate, gla_norm_w, pool_w, pool_scale, w_out, ln1_w, ln1_b,
           peer_w_query, peer_sub_keys, peer_u, peer_v, ple_w_gate, ple_w_proj, ln2_w, ln2_b):
    B, S, D = x.shape
    x2 = x.reshape(B * S, D)
    for i in range(w_in.shape[0]):
        x2 = _layer(x2, p[i].reshape(B * S, PLE_DIM), B, S, w_in[i], gla_w_gate_up[i], gla_b_gate[i],
                    gla_norm_w[i], pool_w[i], pool_scale[i], w_out[i], ln1_w[i], ln1_b[i],
                    peer_w_query[i], peer_sub_keys[i], peer_u[i], peer_v[i], ple_w_gate[i],
                    ple_w_proj[i], ln2_w[i], ln2_b[i])
    return x2.reshape(B, S, D)
```

```python
import functools
import math

import jax
import jax.numpy as jnp
from jax import lax
from jax.experimental import pallas as pl
from jax.experimental.pallas import tpu as pltpu

F32 = jnp.float32
BF16 = jnp.bfloat16

D_MODEL = 2048
PLE_DIM = 256
POOL_WIDTH = 1024
POOL_WINDOWS = (2, 4, 8, 16)
POOL_GC = 256
POOL_HALO = 16
GLA_WIDTH = 1024
GLA_HEADS = 4
GLA_DV = 256
GLA_DK = 128
GLA_KEY_WIDTH = 512
GLA_GATE_RANK = 16
GLA_GATE_TEMP = 16.0
GLA_CHUNK = 64
PEER_HEADS = 8
PEER_NKEYS = 128
PEER_HALF = 128
PEER_TOPK = 16
DEPTH = 1
ALPHA = float((2 * DEPTH) ** 0.25)
LN_EPS = 1e-5
RMS_EPS = 1e-6
LANES = 128
NEG_INF = float("-inf")
ROW_SUB = 256

COL_Q = POOL_WIDTH
COL_K = COL_Q + GLA_KEY_WIDTH
COL_V = COL_K + GLA_KEY_WIDTH
COL_R = COL_V + GLA_WIDTH
PROJ_COLS = COL_R + GLA_WIDTH


def _params(sem, vmem_mib):
    return pltpu.CompilerParams(dimension_semantics=sem, vmem_limit_bytes=vmem_mib * 1024 * 1024)


def _proj_kernel(x_ref, w_ref, wg_ref, o_ref, glr_ref, xb_ref):
    @pl.when(pl.program_id(1) == 0)
    def _():
        xb = x_ref[...].astype(BF16)
        xb_ref[...] = xb
        glr_ref[...] = jnp.dot(xb, wg_ref[...], preferred_element_type=F32)

    o_ref[...] = jnp.dot(xb_ref[...], w_ref[...], preferred_element_type=F32).astype(o_ref.dtype)


def _proj(x2, w_main, w_glr, tm=1024, tn=2048):
    T = x2.shape[0]
    return pl.pallas_call(
        _proj_kernel,
        grid=(T // tm, PROJ_COLS // tn),
        in_specs=[
            pl.BlockSpec((tm, D_MODEL), lambda i, n: (i, 0)),
            pl.BlockSpec((D_MODEL, tn), lambda i, n: (0, n)),
            pl.BlockSpec((D_MODEL, LANES), lambda i, n: (0, 0)),
        ],
        out_specs=[
            pl.BlockSpec((tm, tn), lambda i, n: (i, n)),
            pl.BlockSpec((tm, LANES), lambda i, n: (i, 0)),
        ],
        out_shape=[
            jax.ShapeDtypeStruct((T, PROJ_COLS), BF16),
            jax.ShapeDtypeStruct((T, LANES), F32),
        ],
        scratch_shapes=[pltpu.VMEM((tm, D_MODEL), BF16)],
        compiler_params=_params(("parallel", "arbitrary"), 52),
        name="proj",
    )(x2, w_main, w_glr)


def _pool_block(u_ref, halo_ref, w_ref, sc_ref, o_ref, ext_ref, t):
    tm = u_ref.shape[0]
    halo = jnp.where(t == 0, 0.0, halo_ref[...].astype(F32))
    ext_ref[0:POOL_HALO, :] = halo
    ext_ref[POOL_HALO:, :] = u_ref[...].astype(F32)
    pos = t * tm + lax.broadcasted_iota(jnp.int32, (tm, 1), 0)
    for g, w in enumerate(POOL_WINDOWS):
        cols = slice(g * POOL_GC, (g + 1) * POOL_GC)
        u = ext_ref[POOL_HALO:, cols]
        acc = u
        for j in range(1, w):
            acc = acc + ext_ref[POOL_HALO - j:POOL_HALO - j + tm, cols]
        cnt = jnp.minimum(pos + 1, w).astype(F32)
        d = acc / cnt - u
        y = jnp.dot(d.astype(BF16), w_ref[g], preferred_element_type=F32)
        o_ref[:, cols] = (y * sc_ref[:, cols]).astype(o_ref.dtype)


def _gla_kernel(q_ref, k_ref, v_ref, r_ref, glr_ref, wg_ref, bg_ref, nw_ref, u_ref, halo_ref, pw_ref, psc_ref,
                o_ref, yp_ref, s_ref, ext_ref, *, n_chunks):
    @pl.when(pl.program_id(1) == 0)
    def _():
        s_ref[...] = jnp.zeros_like(s_ref)

    _pool_block(u_ref, halo_ref, pw_ref, psc_ref, yp_ref, ext_ref, pl.program_id(1))
    C = GLA_CHUNK
    row = lax.broadcasted_iota(jnp.int32, (C, C), 0)
    col = lax.broadcasted_iota(jnp.int32, (C, C), 1)
    causal = col <= row
    tril = causal.astype(BF16)
    wg = wg_ref[...]
    bg = bg_ref[...]
    nt = (((1,), (1,)), ((), ()))
    for c in range(n_chunks):
        rows = slice(c * C, (c + 1) * C)
        z = jnp.dot(glr_ref[rows, :].astype(BF16), wg, preferred_element_type=F32) + bg
        g = jax.nn.log_sigmoid(z) / GLA_GATE_TEMP
        g_hi, g_lo = _split_bf16(g)
        b_all = (jnp.dot(tril, g_hi, preferred_element_type=F32)
                 + jnp.dot(tril, g_lo, preferred_element_type=F32))
        for h in range(GLA_HEADS):
            kc = slice(h * GLA_DK, (h + 1) * GLA_DK)
            vc = slice(h * GLA_DV, (h + 1) * GLA_DV)
            b = b_all[:, kc]
            b_last = b[C - 1:C, :]
            b_mid = b[C // 2 - 1:C // 2, :]
            q = q_ref[rows, kc].astype(F32) * (GLA_DK ** -0.5)
            k = k_ref[rows, kc].astype(F32)
            v = v_ref[rows, vc]
            q_state = (q * jnp.exp(b)).astype(BF16)
            q_in = (q * jnp.exp(b - b_mid)).astype(BF16)
            k_in = (k * jnp.exp(b_mid - b)).astype(BF16)
            k_out = k * jnp.exp(b_last - b)
            attn = lax.dot_general(q_in, k_in, nt, preferred_element_type=F32)
            attn = jnp.where(causal, attn, 0.0).astype(BF16)
            s = s_ref[h]
            o = (jnp.dot(attn, v, preferred_element_type=F32)
                 + jnp.dot(q_state, s.astype(BF16), preferred_element_type=F32))
            decay = jnp.transpose(jnp.broadcast_to(jnp.exp(b_last), (C, GLA_DK)))[:, 0:1]
            s_ref[h] = decay * s + jnp.dot(jnp.transpose(k_out).astype(BF16), v, preferred_element_type=F32)
            o = o * lax.rsqrt(jnp.mean(jnp.square(o), axis=-1, keepdims=True) + RMS_EPS)
            o = o * nw_ref[:, vc]
            r = r_ref[rows, vc].astype(F32)
            o_ref[rows, vc] = (o * (r * jax.nn.sigmoid(r))).astype(o_ref.dtype)


def _gla(proj, glr, wg, bg, nw, pool_w, pool_scale, B, S, L=512):
    T = proj.shape[0]
    nl = S // L
    hb = L // POOL_HALO
    rb = lambda b, l: b * nl + l
    out_spec = pl.BlockSpec((L, GLA_WIDTH), lambda b, l: (rb(b, l), 0))
    return pl.pallas_call(
        functools.partial(_gla_kernel, n_chunks=L // GLA_CHUNK),
        grid=(B, nl),
        in_specs=[
            pl.BlockSpec((L, GLA_KEY_WIDTH), lambda b, l: (rb(b, l), COL_Q // GLA_KEY_WIDTH)),
            pl.BlockSpec((L, GLA_KEY_WIDTH), lambda b, l: (rb(b, l), COL_K // GLA_KEY_WIDTH)),
            pl.BlockSpec((L, GLA_WIDTH), lambda b, l: (rb(b, l), COL_V // GLA_WIDTH)),
            pl.BlockSpec((L, GLA_WIDTH), lambda b, l: (rb(b, l), COL_R // GLA_WIDTH)),
            pl.BlockSpec((L, LANES), lambda b, l: (rb(b, l), 0)),
            pl.BlockSpec((LANES, GLA_KEY_WIDTH), lambda b, l: (0, 0)),
            pl.BlockSpec((1, GLA_KEY_WIDTH), lambda b, l: (0, 0)),
            pl.BlockSpec((1, GLA_WIDTH), lambda b, l: (0, 0)),
            pl.BlockSpec((L, POOL_WIDTH), lambda b, l: (rb(b, l), 0)),
            pl.BlockSpec((POOL_HALO, POOL_WIDTH), lambda b, l: (jnp.maximum(rb(b, l) * hb - 1, 0), 0)),
            pl.BlockSpec((len(POOL_WINDOWS), POOL_GC, POOL_GC), lambda b, l: (0, 0, 0)),
            pl.BlockSpec((1, POOL_WIDTH), lambda b, l: (0, 0)),
        ],
        out_specs=[out_spec, out_spec],
        out_shape=[jax.ShapeDtypeStruct((T, GLA_WIDTH), BF16), jax.ShapeDtypeStruct((T, POOL_WIDTH), BF16)],
        scratch_shapes=[pltpu.VMEM((GLA_HEADS, GLA_DK, GLA_DV), F32),
                        pltpu.VMEM((POOL_HALO + L, POOL_WIDTH), F32)],
        compiler_params=_params(("parallel", "arbitrary"), 40),
        name="gla",
    )(proj, proj, proj, proj, glr, wg, bg, nw, proj, proj, pool_w, pool_scale)


def _layer_norm(h, w, b):
    mu = jnp.mean(h, axis=-1, keepdims=True)
    hc = h - mu
    var = jnp.mean(jnp.square(hc), axis=-1, keepdims=True)
    return hc * lax.rsqrt(var + LN_EPS) * w + b


def _outproj_kernel(yp_ref, yg_ref, x_ref, w_ref, lw_ref, lb_ref, x1_ref, x1t_ref):
    for r0 in range(0, x_ref.shape[0], ROW_SUB):
        rows = slice(r0, r0 + ROW_SUB)
        mix = (jnp.dot(yp_ref[rows, :], w_ref[0:POOL_WIDTH, :], preferred_element_type=F32)
               + jnp.dot(yg_ref[rows, :], w_ref[POOL_WIDTH:, :], preferred_element_type=F32))
        x1 = _layer_norm(ALPHA * x_ref[rows, :] + mix, lw_ref[...], lb_ref[...])
        x1_ref[rows, :] = x1
        x1t_ref[:, rows] = jnp.transpose(x1).astype(BF16)


def _outproj(y_pool, y_gla, x2, w_out, ln_w, ln_b, tm=512):
    T = x2.shape[0]
    return pl.pallas_call(
        _outproj_kernel,
        grid=(T // tm,),
        in_specs=[
            pl.BlockSpec((tm, POOL_WIDTH), lambda i: (i, 0)),
            pl.BlockSpec((tm, GLA_WIDTH), lambda i: (i, 0)),
            pl.BlockSpec((tm, D_MODEL), lambda i: (i, 0)),
            pl.BlockSpec((D_MODEL, D_MODEL), lambda i: (0, 0), pipeline_mode=pl.Buffered(1)),
            pl.BlockSpec((1, D_MODEL), lambda i: (0, 0)),
            pl.BlockSpec((1, D_MODEL), lambda i: (0, 0)),
        ],
        out_specs=[
            pl.BlockSpec((tm, D_MODEL), lambda i: (i, 0)),
            pl.BlockSpec((D_MODEL, tm), lambda i: (0, i)),
        ],
        out_shape=[
            jax.ShapeDtypeStruct((T, D_MODEL), F32),
            jax.ShapeDtypeStruct((D_MODEL, T), BF16),
        ],
        compiler_params=_params(("parallel",), 48),
        name="outproj",
    )(y_pool, y_gla, x2, w_out, ln_w, ln_b)


N_SORT = PEER_TOPK + 1


def _sort_network(n):
    pairs = []

    def merge(lo, m, r):
        step = 2 * r
        if step < m:
            merge(lo, m, step)
            merge(lo + r, m, step)
            pairs.extend((i, i + r) for i in range(lo + r, lo + m - r, step))
        else:
            pairs.append((lo, lo + r))

    def sort(lo, m):
        if m > 1:
            sort(lo, m // 2)
            sort(lo + m // 2, m // 2)
            merge(lo, m, 1)

    sort(0, n)
    return tuple(pairs)


def _pop_sorted(v, n_out):
    nv = len(v)
    width = 1 << (nv - 1).bit_length()
    for i, j in _sort_network(width):
        if j < nv:
            v[i], v[j] = jnp.maximum(v[i], v[j]), jnp.minimum(v[i], v[j])
    tops = []
    rid = lax.broadcasted_iota(jnp.int32, (SUBLANES, LANES), 0).astype(F32)
    for kk in range(n_out):
        m = jnp.max(v[0], axis=0, keepdims=True)
        tops.append(m)
        first = jnp.min(jnp.where(v[0] == m, rid, float(SUBLANES)), axis=0, keepdims=True)
        hit = rid == first
        for k in range(min(n_out - 1 - kk, nv)):
            v[k] = jnp.where(hit, v[k + 1] if k + 1 < nv else NEG_INF, v[k])
    return tops


def _sorted_top(arr):
    return _pop_sorted([arr[k:k + SUBLANES] for k in range(0, arr.shape[0], SUBLANES)], N_SORT)


def _rows_to_tile(rows):
    rid = lax.broadcasted_iota(jnp.int32, (SUBLANES, LANES), 0)
    tile = jnp.full((SUBLANES, LANES), NEG_INF, F32)
    for k, r in enumerate(rows):
        tile = jnp.where(rid == k, r, tile)
    return tile


def _pair_stats(a, b):
    r8 = lax.broadcasted_iota(jnp.int32, (SUBLANES, LANES), 0)
    b_lo, b_hi, a_hi = _rows_to_tile(b[0:8]), _rows_to_tile(b[8:16]), _rows_to_tile(a[8:16])
    p2 = jnp.where(r8 < 5, a[2] + b_lo, jnp.where(r8 == 5, a[16] + b[0], jnp.where(r8 == 6, a[0] + b[16], NEG_INF)))
    pieces = [
        a[0] + b_lo, a[0] + b_hi, a[1] + b_lo, p2,
        jnp.where(r8 < 4, a[3] + b_lo, NEG_INF),
        jnp.where(r8 < 3, a[4] + b_lo, NEG_INF),
        jnp.where(r8 < 2, a[5] + b_lo, NEG_INF),
        jnp.where(r8 < 2, a[6] + b_lo, NEG_INF),
        jnp.where(r8 < 2, a[7] + b_lo, NEG_INF),
        a_hi + b[0],
    ]
    sums = _pop_sorted(pieces, N_SORT)
    top16 = jnp.concatenate([_rows_to_tile(sums[0:8]), _rows_to_tile(sums[8:16])], axis=0)
    z = jnp.sum(jnp.exp(top16 - sums[0]), axis=0, keepdims=True)
    return 0.5 * (sums[PEER_TOPK - 1] + sums[PEER_TOPK]), 1.0 / z


def _split_bf16(x):
    hi = x.astype(BF16)
    return hi, (x - hi.astype(F32)).astype(BF16)


def _query_kernel(x1t_ref, wq_ref, khi_ref, klo_ref, u_ref, v_ref,
                  c1_ref, n1_ref, e2_ref, r2_ref, ub_ref, vt_ref, q_ref, s1_scr, s2_scr, *, tm):
    ub_ref[...] = u_ref[...].astype(BF16)
    vt_ref[...] = jnp.transpose(v_ref[...]).astype(BF16)
    q_ref[...] = jnp.dot(wq_ref[...], x1t_ref[...], preferred_element_type=F32)
    for h in range(PEER_HEADS):
        for p in range(2):
            hp = 2 * h + p
            q_hi, q_lo = _split_bf16(q_ref[hp * PEER_HALF:(hp + 1) * PEER_HALF, :])
            k_hi, k_lo = khi_ref[hp], klo_ref[hp]
            sc = (jnp.dot(k_hi, q_hi, preferred_element_type=F32)
                  + jnp.dot(k_hi, q_lo, preferred_element_type=F32)
                  + jnp.dot(k_lo, q_hi, preferred_element_type=F32))
            if p == 0:
                s1_scr[...] = sc
            else:
                s2_scr[...] = sc
        for tc in range(tm // LANES):
            lanes = slice(tc * LANES, (tc + 1) * LANES)
            s1 = s1_scr[:, lanes]
            s2 = s2_scr[:, lanes]
            ta, tb = _sorted_top(s1), _sorted_top(s2)
            tau, rz = _pair_stats(ta, tb)
            n1 = jnp.zeros_like(s1)
            r2 = jnp.zeros_like(s2)
            for jj in range(PEER_TOPK):
                n1 = jnp.where(s1 >= tau - tb[jj], float(jj + 1), n1)
                r2 = jnp.where(s2 < tb[jj], float(jj + 1), r2)
            c1_ref[h, tc] = jnp.exp(s1 - ta[0])
            n1_ref[h, tc] = n1
            e2_ref[h, :, lanes] = (jnp.exp(s2 - tb[0]) * rz).astype(BF16)
            r2_ref[h, :, lanes] = r2.astype(BF16)


def _query(x1t, wq_t, keys_hi, keys_lo, peer_u, peer_v, tm=256):
    T = x1t.shape[1]
    n_exp = peer_u.shape[0]
    steps = T // tm
    slab = n_exp // steps
    per_tile = PEER_TE // slab
    assert slab * steps == n_exp and per_tile * slab == PEER_TE and slab % LANES == 0
    nc = tm // LANES
    row_spec = pl.BlockSpec((PEER_HEADS, nc, PEER_NKEYS, LANES), lambda i: (0, i, 0, 0))
    col_spec = pl.BlockSpec((PEER_HEADS, PEER_NKEYS, tm), lambda i: (0, 0, i))
    key_spec = pl.BlockSpec((2 * PEER_HEADS, PEER_NKEYS, PEER_HALF), lambda i: (0, 0, 0))
    tab_spec = pl.BlockSpec((slab, D_MODEL), lambda i: (i, 0))
    row_shape = jax.ShapeDtypeStruct((PEER_HEADS, T // LANES, PEER_NKEYS, LANES), F32)
    col_shape = jax.ShapeDtypeStruct((PEER_HEADS, PEER_NKEYS, T), BF16)
    return pl.pallas_call(
        functools.partial(_query_kernel, tm=tm),
        grid=(steps,),
        in_specs=[
            pl.BlockSpec((D_MODEL, tm), lambda i: (0, i)),
            pl.BlockSpec((D_MODEL, D_MODEL), lambda i: (0, 0), pipeline_mode=pl.Buffered(1)),
            key_spec, key_spec, tab_spec, tab_spec,
        ],
        out_specs=[row_spec, row_spec, col_spec, col_spec, tab_spec,
                   pl.BlockSpec((None, D_MODEL, slab), lambda i: (i // per_tile, 0, i % per_tile))],
        out_shape=[row_shape, row_shape, col_shape, col_shape,
                   jax.ShapeDtypeStruct((n_exp, D_MODEL), BF16),
                   jax.ShapeDtypeStruct((n_exp // PEER_TE, D_MODEL, PEER_TE), BF16)],
        scratch_shapes=[pltpu.VMEM((D_MODEL, tm), F32), pltpu.VMEM((PEER_NKEYS, tm), F32),
                        pltpu.VMEM((PEER_NKEYS, tm), F32)],
        compiler_params=_params(("parallel",), 56),
        name="query",
    )(x1t, wq_t, keys_hi, keys_lo, peer_u, peer_v)


PEER_SB = 64
PEER_RG = 2
SUBLANES = 8
PEER_TE = 1024


def _gelu(x):
    return 0.5 * x * (1.0 + lax.erf(x * (1.0 / math.sqrt(2.0))))


def _bcast_row_bf16(tile, ri, rows):
    packed = jnp.broadcast_to(tile[ri:ri + 1, :], (2 * SUBLANES, LANES)).astype(BF16)
    return jnp.concatenate([packed] * (rows // (2 * SUBLANES)), axis=0)


def _peer_kernel(x1t_ref, u_ref, vt_ref, c1_ref, n1_ref, e2_ref, r2_ref, y_ref,
                 acc_ref, st_scr, ht_scr, *, tm, te):
    j = pl.program_id(1)
    n1 = te // PEER_NKEYS
    grows = PEER_RG * PEER_NKEYS

    @pl.when(j == 0)
    def _():
        acc_ref[...] = jnp.zeros_like(acc_ref)

    st_scr[...] = jnp.dot(u_ref[...], x1t_ref[...], preferred_element_type=F32)

    nsb = PEER_NKEYS // PEER_SB
    tile_rows = pl.ds(pl.multiple_of(j * n1, SUBLANES), n1)
    for gi in range(n1 // PEER_RG):
        crows = slice(gi * grows, (gi + 1) * grows)
        for tc in range(tm // LANES):
            lanes = slice(tc * LANES, (tc + 1) * LANES)
            g = [[jnp.zeros((PEER_SB, LANES), BF16) for _ in range(nsb)] for _ in range(PEER_RG)]
            for h in range(PEER_HEADS):
                c1_t = c1_ref[h, tc, tile_rows, :]
                n1_t = n1_ref[h, tc, tile_rows, :]
                c1b = [_bcast_row_bf16(c1_t, gi * PEER_RG + r, PEER_SB) for r in range(PEER_RG)]
                n1b = [_bcast_row_bf16(n1_t, gi * PEER_RG + r, PEER_SB) for r in range(PEER_RG)]
                for sb in range(nsb):
                    rows = slice(sb * PEER_SB, (sb + 1) * PEER_SB)
                    r2c = r2_ref[h, rows, lanes]
                    e2c = e2_ref[h, rows, lanes]
                    for r in range(PEER_RG):
                        g[r][sb] = g[r][sb] + c1b[r] * jnp.where(r2c < n1b[r], e2c, jnp.zeros_like(e2c))
            for r in range(PEER_RG):
                for sb in range(nsb):
                    base = gi * grows + r * PEER_NKEYS + sb * PEER_SB
                    srows = slice(base, base + PEER_SB)
                    ht_scr[srows, lanes] = g[r][sb] * _gelu(st_scr[srows, lanes].astype(BF16))
    acc_ref[...] += jnp.dot(vt_ref[...], ht_scr[...], preferred_element_type=F32)

    @pl.when(j == pl.num_programs(1) - 1)
    def _():
        y_ref[...] = jnp.transpose(acc_ref[...])


def _peer(x1t, u_tab, vt_tiles, c1, n1, e2, r2, tm=512):
    T = x1t.shape[1]
    nj, _, te = vt_tiles.shape
    row_spec = pl.BlockSpec((PEER_HEADS, tm // LANES, PEER_NKEYS, LANES), lambda i, j: (0, i, 0, 0))
    col_spec = pl.BlockSpec((PEER_HEADS, PEER_NKEYS, tm), lambda i, j: (0, 0, i))
    return pl.pallas_call(
        functools.partial(_peer_kernel, tm=tm, te=te),
        grid=(T // tm, nj),
        in_specs=[
            pl.BlockSpec((D_MODEL, tm), lambda i, j: (0, i)),
            pl.BlockSpec((te, D_MODEL), lambda i, j: (j, 0)),
            pl.BlockSpec((None, D_MODEL, te), lambda i, j: (j, 0, 0)),
            row_spec, row_spec, col_spec, col_spec,
        ],
        out_specs=pl.BlockSpec((tm, D_MODEL), lambda i, j: (i, 0)),
        out_shape=jax.ShapeDtypeStruct((T, D_MODEL), F32),
        scratch_shapes=[
            pltpu.VMEM((D_MODEL, tm), F32),
            pltpu.VMEM((te, tm), F32),
            pltpu.VMEM((te, tm), BF16),
        ],
        compiler_params=_params(("parallel", "arbitrary"), 56),
        name="peer",
    )(x1t, u_tab, vt_tiles, c1, n1, e2, r2)


def _final_kernel(x1_ref, y_ref, p_ref, wg_ref, wp_ref, lw_ref, lb_ref, o_ref):
    for r0 in range(0, x1_ref.shape[0], ROW_SUB):
        rows = slice(r0, r0 + ROW_SUB)
        x1 = x1_ref[rows, :]
        gate = jax.nn.sigmoid(jnp.dot(x1.astype(BF16), wg_ref[...], preferred_element_type=F32))
        emb = jnp.dot(p_ref[rows, :].astype(BF16), wp_ref[...], preferred_element_type=F32)
        o_ref[rows, :] = _layer_norm(ALPHA * x1 + y_ref[rows, :] + gate * emb, lw_ref[...], lb_ref[...])


def _final(x1, y_ffn, p2, w_gate, w_proj, ln_w, ln_b, tm=512):
    T = x1.shape[0]
    return pl.pallas_call(
        _final_kernel,
        grid=(T // tm,),
        in_specs=[
            pl.BlockSpec((tm, D_MODEL), lambda i: (i, 0)),
            pl.BlockSpec((tm, D_MODEL), lambda i: (i, 0)),
            pl.BlockSpec((tm, PLE_DIM), lambda i: (i, 0)),
            pl.BlockSpec((D_MODEL, D_MODEL), lambda i: (0, 0), pipeline_mode=pl.Buffered(1)),
            pl.BlockSpec((PLE_DIM, D_MODEL), lambda i: (0, 0), pipeline_mode=pl.Buffered(1)),
            pl.BlockSpec((1, D_MODEL), lambda i: (0, 0)),
            pl.BlockSpec((1, D_MODEL), lambda i: (0, 0)),
        ],
        out_specs=pl.BlockSpec((tm, D_MODEL), lambda i: (i, 0)),
        out_shape=jax.ShapeDtypeStruct((T, D_MODEL), F32),
        compiler_params=_params(("parallel",), 48),
        name="final",
    )(x1, y_ffn, p2, w_gate, w_proj, ln_w, ln_b)


def _layer(x2, p2, B, S, w_in, gla_w_gate_up, gla_b_gate, gla_norm_w, pool_w, pool_scale, w_out,
           ln1_w, ln1_b, peer_w_query, peer_sub_keys, peer_u, peer_v, ple_w_gate, ple_w_proj, ln2_w, ln2_b):
    glr0 = COL_R
    w_main = jnp.concatenate([w_in[:, :glr0], w_in[:, glr0 + GLA_GATE_RANK:]], axis=1).astype(BF16)
    w_glr = jnp.pad(w_in[:, glr0:glr0 + GLA_GATE_RANK], ((0, 0), (0, LANES - GLA_GATE_RANK))).astype(BF16)
    proj, glr = _proj(x2, w_main, w_glr)

    wg = jnp.pad(gla_w_gate_up, ((0, LANES - GLA_GATE_RANK), (0, 0))).astype(BF16)
    y_gla, y_pool = _gla(proj, glr, wg, gla_b_gate.reshape(1, GLA_KEY_WIDTH), gla_norm_w.reshape(1, GLA_WIDTH),
                         pool_w.astype(BF16), pool_scale.reshape(1, POOL_WIDTH), B, S)

    x1, x1t = _outproj(y_pool, y_gla, x2, w_out.astype(BF16),
                       ln1_w.reshape(1, D_MODEL), ln1_b.reshape(1, D_MODEL))

    keys = peer_sub_keys.reshape(2 * PEER_HEADS, PEER_NKEYS, PEER_HALF)
    keys_hi, keys_lo = _split_bf16(keys)
    c1, n1, e2, r2, u_bf16, vt_tiles = _query(x1t, peer_w_query.T.astype(BF16), keys_hi, keys_lo, peer_u, peer_v)
    y_ffn = _peer(x1t, u_bf16, vt_tiles, c1, n1, e2, r2)

    return _final(x1, y_ffn, p2, ple_w_gate.astype(BF16), ple_w_proj.astype(BF16),
                  ln2_w.reshape(1, D_MODEL), ln2_b.reshape(1, D_MODEL))


def kernel(x, p, w_in, gla_w_gate_up, gla_b_gate, gla_norm_w, pool_w, pool_scale, w_out, ln1_w, ln1_b,
           peer_w_query, peer_sub_keys, peer_u, peer_v, ple_w_gate, ple_w_proj, ln2_w, ln2_b):
    B, S, D = x.shape
    x2 = x.reshape(B * S, D)
    for i in range(w_in.shape[0]):
        x2 = _layer(x2, p[i].reshape(B * S, PLE_DIM), B, S, w_in[i], gla_w_gate_up[i], gla_b_gate[i],
                    gla_norm_w[i], pool_w[i], pool_scale[i], w_out[i], ln1_w[i], ln1_b[i],
                    peer_w_query[i], peer_sub_keys[i], peer_u[i], peer_v[i], ple_w_gate[i],
                    ple_w_proj[i], ln2_w[i], ln2_b[i])
    return x2.reshape(B, S, D)
```

```python
import functools
import math

import jax
import jax.numpy as jnp
from jax import lax
from jax.experimental import pallas as pl
from jax.experimental.pallas import tpu as pltpu

F32 = jnp.float32
BF16 = jnp.bfloat16

D_MODEL = 2048
PLE_DIM = 256
POOL_WIDTH = 1024
POOL_WINDOWS = (2, 4, 8, 16)
POOL_GC = 256
POOL_HALO = 16
GLA_WIDTH = 1024
GLA_HEADS = 4
GLA_DV = 256
GLA_DK = 128
GLA_KEY_WIDTH = 512
GLA_GATE_RANK = 16
GLA_GATE_TEMP = 16.0
GLA_CHUNK = 64
PEER_HEADS = 8
PEER_NKEYS = 128
PEER_HALF = 128
PEER_TOPK = 16
DEPTH = 1
ALPHA = float((2 * DEPTH) ** 0.25)
LN_EPS = 1e-5
RMS_EPS = 1e-6
LANES = 128
NEG_INF = float("-inf")
ROW_SUB = 256

COL_Q = POOL_WIDTH
COL_K = COL_Q + GLA_KEY_WIDTH
COL_V = COL_K + GLA_KEY_WIDTH
COL_R = COL_V + GLA_WIDTH
PROJ_COLS = COL_R + GLA_WIDTH


def _params(sem, vmem_mib):
    return pltpu.CompilerParams(dimension_semantics=sem, vmem_limit_bytes=vmem_mib * 1024 * 1024)


def _proj_kernel(x_ref, w_ref, wg_ref, o_ref, glr_ref, xb_ref):
    @pl.when(pl.program_id(1) == 0)
    def _():
        xb = x_ref[...].astype(BF16)
        xb_ref[...] = xb
        glr_ref[...] = jnp.dot(xb, wg_ref[...], preferred_element_type=F32)

    o_ref[...] = jnp.dot(xb_ref[...], w_ref[...], preferred_element_type=F32).astype(o_ref.dtype)


def _proj(x2, w_main, w_glr, tm=1024, tn=2048):
    T = x2.shape[0]
    return pl.pallas_call(
        _proj_kernel,
        grid=(T // tm, PROJ_COLS // tn),
        in_specs=[
            pl.BlockSpec((tm, D_MODEL), lambda i, n: (i, 0)),
            pl.BlockSpec((D_MODEL, tn), lambda i, n: (0, n)),
            pl.BlockSpec((D_MODEL, LANES), lambda i, n: (0, 0)),
        ],
        out_specs=[
            pl.BlockSpec((tm, tn), lambda i, n: (i, n)),
            pl.BlockSpec((tm, LANES), lambda i, n: (i, 0)),
        ],
        out_shape=[
            jax.ShapeDtypeStruct((T, PROJ_COLS), BF16),
            jax.ShapeDtypeStruct((T, LANES), F32),
        ],
        scratch_shapes=[pltpu.VMEM((tm, D_MODEL), BF16)],
        compiler_params=_params(("parallel", "arbitrary"), 52),
        name="proj",
    )(x2, w_main, w_glr)


def _pool_block(u_ref, halo_ref, w_ref, sc_ref, o_ref, ext_ref, t):
    tm = u_ref.shape[0]
    halo = jnp.where(t == 0, 0.0, halo_ref[...].astype(F32))
    ext_ref[0:POOL_HALO, :] = halo
    ext_ref[POOL_HALO:, :] = u_ref[...].astype(F32)
    pos = t * tm + lax.broadcasted_iota(jnp.int32, (tm, 1), 0)
    for g, w in enumerate(POOL_WINDOWS):
        cols = slice(g * POOL_GC, (g + 1) * POOL_GC)
        u = ext_ref[POOL_HALO:, cols]
        acc = u
        for j in range(1, w):
            acc = acc + ext_ref[POOL_HALO - j:POOL_HALO - j + tm, cols]
        cnt = jnp.minimum(pos + 1, w).astype(F32)
        d = acc / cnt - u
        y = jnp.dot(d.astype(BF16), w_ref[g], preferred_element_type=F32)
        o_ref[:, cols] = (y * sc_ref[:, cols]).astype(o_ref.dtype)


def _gla_kernel(q_ref, k_ref, v_ref, r_ref, glr_ref, wg_ref, bg_ref, nw_ref, u_ref, halo_ref, pw_ref, psc_ref,
                o_ref, yp_ref, s_ref, ext_ref, *, n_chunks):
    @pl.when(pl.program_id(1) == 0)
    def _():
        s_ref[...] = jnp.zeros_like(s_ref)

    _pool_block(u_ref, halo_ref, pw_ref, psc_ref, yp_ref, ext_ref, pl.program_id(1))
    C = GLA_CHUNK
    row = lax.broadcasted_iota(jnp.int32, (C, C), 0)
    col = lax.broadcasted_iota(jnp.int32, (C, C), 1)
    causal = col <= row
    tril = causal.astype(BF16)
    wg = wg_ref[...]
    bg = bg_ref[...]
    nt = (((1,), (1,)), ((), ()))
    for c in range(n_chunks):
        rows = slice(c * C, (c + 1) * C)
        z = jnp.dot(glr_ref[rows, :].astype(BF16), wg, preferred_element_type=F32) + bg
        g = jax.nn.log_sigmoid(z) / GLA_GATE_TEMP
        g_hi, g_lo = _split_bf16(g)
        b_all = (jnp.dot(tril, g_hi, preferred_element_type=F32)
                 + jnp.dot(tril, g_lo, preferred_element_type=F32))
        for h in range(GLA_HEADS):
            kc = slice(h * GLA_DK, (h + 1) * GLA_DK)
            vc = slice(h * GLA_DV, (h + 1) * GLA_DV)
            b = b_all[:, kc]
            b_last = b[C - 1:C, :]
            b_mid = b[C // 2 - 1:C // 2, :]
            q = q_ref[rows, kc].astype(F32) * (GLA_DK ** -0.5)
            k = k_ref[rows, kc].astype(F32)
            v = v_ref[rows, vc]
            q_state = (q * jnp.exp(b)).astype(BF16)
            q_in = (q * jnp.exp(b - b_mid)).astype(BF16)
            k_in = (k * jnp.exp(b_mid - b)).astype(BF16)
            k_out = k * jnp.exp(b_last - b)
            attn = lax.dot_general(q_in, k_in, nt, preferred_element_type=F32)
            attn = jnp.where(causal, attn, 0.0).astype(BF16)
            s = s_ref[h]
            o = (jnp.dot(attn, v, preferred_element_type=F32)
                 + jnp.dot(q_state, s.astype(BF16), preferred_element_type=F32))
            decay = jnp.transpose(jnp.broadcast_to(jnp.exp(b_last), (C, GLA_DK)))[:, 0:1]
            s_ref[h] = decay * s + jnp.dot(jnp.transpose(k_out).astype(BF16), v, preferred_element_type=F32)
            o = o * lax.rsqrt(jnp.mean(jnp.square(o), axis=-1, keepdims=True) + RMS_EPS)
            o = o * nw_ref[:, vc]
            r = r_ref[rows, vc].astype(F32)
            o_ref[rows, vc] = (o * (r * jax.nn.sigmoid(r))).astype(o_ref.dtype)


def _gla(proj, glr, wg, bg, nw, pool_w, pool_scale, B, S, L=512):
    T = proj.shape[0]
    nl = S // L
    hb = L // POOL_HALO
    rb = lambda b, l: b * nl + l
    out_spec = pl.BlockSpec((L, GLA_WIDTH), lambda b, l: (rb(b, l), 0))
    return pl.pallas_call(
        functools.partial(_gla_kernel, n_chunks=L // GLA_CHUNK),
        grid=(B, nl),
        in_specs=[
            pl.BlockSpec((L, GLA_KEY_WIDTH), lambda b, l: (rb(b, l), COL_Q // GLA_KEY_WIDTH)),
            pl.BlockSpec((L, GLA_KEY_WIDTH), lambda b, l: (rb(b, l), COL_K // GLA_KEY_WIDTH)),
            pl.BlockSpec((L, GLA_WIDTH), lambda b, l: (rb(b, l), COL_V // GLA_WIDTH)),
            pl.BlockSpec((L, GLA_WIDTH), lambda b, l: (rb(b, l), COL_R // GLA_WIDTH)),
            pl.BlockSpec((L, LANES), lambda b, l: (rb(b, l), 0)),
            pl.BlockSpec((LANES, GLA_KEY_WIDTH), lambda b, l: (0, 0)),
            pl.BlockSpec((1, GLA_KEY_WIDTH), lambda b, l: (0, 0)),
            pl.BlockSpec((1, GLA_WIDTH), lambda b, l: (0, 0)),
            pl.BlockSpec((L, POOL_WIDTH), lambda b, l: (rb(b, l), 0)),
            pl.BlockSpec((POOL_HALO, POOL_WIDTH), lambda b, l: (jnp.maximum(rb(b, l) * hb - 1, 0), 0)),
            pl.BlockSpec((len(POOL_WINDOWS), POOL_GC, POOL_GC), lambda b, l: (0, 0, 0)),
            pl.BlockSpec((1, POOL_WIDTH), lambda b, l: (0, 0)),
        ],
        out_specs=[out_spec, out_spec],
        out_shape=[jax.ShapeDtypeStruct((T, GLA_WIDTH), BF16), jax.ShapeDtypeStruct((T, POOL_WIDTH), BF16)],
        scratch_shapes=[pltpu.VMEM((GLA_HEADS, GLA_DK, GLA_DV), F32),
                        pltpu.VMEM((POOL_HALO + L, POOL_WIDTH), F32)],
        compiler_params=_params(("parallel", "arbitrary"), 40),
        name="gla",
    )(proj, proj, proj, proj, glr, wg, bg, nw, proj, proj, pool_w, pool_scale)


def _layer_norm(h, w, b):
    mu = jnp.mean(h, axis=-1, keepdims=True)
    hc = h - mu
    var = jnp.mean(jnp.square(hc), axis=-1, keepdims=True)
    return hc * lax.rsqrt(var + LN_EPS) * w + b


def _outproj_kernel(yp_ref, yg_ref, x_ref, w_ref, lw_ref, lb_ref, x1_ref, x1t_ref):
    for r0 in range(0, x_ref.shape[0], ROW_SUB):
        rows = slice(r0, r0 + ROW_SUB)
        mix = (jnp.dot(yp_ref[rows, :], w_ref[0:POOL_WIDTH, :], preferred_element_type=F32)
               + jnp.dot(yg_ref[rows, :], w_ref[POOL_WIDTH:, :], preferred_element_type=F32))
        x1 = _layer_norm(ALPHA * x_ref[rows, :] + mix, lw_ref[...], lb_ref[...])
        x1_ref[rows, :] = x1
        x1t_ref[:, rows] = jnp.transpose(x1).astype(BF16)


def _outproj(y_pool, y_gla, x2, w_out, ln_w, ln_b, tm=512):
    T = x2.shape[0]
    return pl.pallas_call(
        _outproj_kernel,
        grid=(T // tm,),
        in_specs=[
            pl.BlockSpec((tm, POOL_WIDTH), lambda i: (i, 0)),
            pl.BlockSpec((tm, GLA_WIDTH), lambda i: (i, 0)),
            pl.BlockSpec((tm, D_MODEL), lambda i: (i, 0)),
            pl.BlockSpec((D_MODEL, D_MODEL), lambda i: (0, 0), pipeline_mode=pl.Buffered(1)),
            pl.BlockSpec((1, D_MODEL), lambda i: (0, 0)),
            pl.BlockSpec((1, D_MODEL), lambda i: (0, 0)),
        ],
        out_specs=[
            pl.BlockSpec((tm, D_MODEL), lambda i: (i, 0)),
            pl.BlockSpec((D_MODEL, tm), lambda i: (0, i)),
        ],
        out_shape=[
            jax.ShapeDtypeStruct((T, D_MODEL), F32),
            jax.ShapeDtypeStruct((D_MODEL, T), BF16),
        ],
        compiler_params=_params(("parallel",), 48),
        name="outproj",
    )(y_pool, y_gla, x2, w_out, ln_w, ln_b)


N_SORT = PEER_TOPK + 1


def _sort_network(n):
    pairs = []

    def merge(lo, m, r):
        step = 2 * r
        if step < m:
            merge(lo, m, step)
            merge(lo + r, m, step)
            pairs.extend((i, i + r) for i in range(lo + r, lo + m - r, step))
        else:
            pairs.append((lo, lo + r))

    def sort(lo, m):
        if m > 1:
            sort(lo, m // 2)
            sort(lo + m // 2, m // 2)
            merge(lo, m, 1)

    sort(0, n)
    return tuple(pairs)


def _pop_sorted(v, n_out):
    nv = len(v)
    width = 1 << (nv - 1).bit_length()
    for i, j in _sort_network(width):
        if j < nv:
            v[i], v[j] = jnp.maximum(v[i], v[j]), jnp.minimum(v[i], v[j])
    tops = []
    rid = lax.broadcasted_iota(jnp.int32, (SUBLANES, LANES), 0).astype(F32)
    for kk in range(n_out):
        m = jnp.max(v[0], axis=0, keepdims=True)
        tops.append(m)
        first = jnp.min(jnp.where(v[0] == m, rid, float(SUBLANES)), axis=0, keepdims=True)
        hit = rid == first
        for k in range(min(n_out - 1 - kk, nv)):
            v[k] = jnp.where(hit, v[k + 1] if k + 1 < nv else NEG_INF, v[k])
    return tops


def _sorted_top(arr):
    return _pop_sorted([arr[k:k + SUBLANES] for k in range(0, arr.shape[0], SUBLANES)], N_SORT)


def _rows_to_tile(rows):
    rid = lax.broadcasted_iota(jnp.int32, (SUBLANES, LANES), 0)
    tile = jnp.full((SUBLANES, LANES), NEG_INF, F32)
    for k, r in enumerate(rows):
        tile = jnp.where(rid == k, r, tile)
    return tile


def _pair_stats(a, b):
    r8 = lax.broadcasted_iota(jnp.int32, (SUBLANES, LANES), 0)
    b_lo, b_hi, a_hi = _rows_to_tile(b[0:8]), _rows_to_tile(b[8:16]), _rows_to_tile(a[8:16])
    p2 = jnp.where(r8 < 5, a[2] + b_lo, jnp.where(r8 == 5, a[16] + b[0], jnp.where(r8 == 6, a[0] + b[16], NEG_INF)))
    pieces = [
        a[0] + b_lo, a[0] + b_hi, a[1] + b_lo, p2,
        jnp.where(r8 < 4, a[3] + b_lo, NEG_INF),
        jnp.where(r8 < 3, a[4] + b_lo, NEG_INF),
        jnp.where(r8 < 2, a[5] + b_lo, NEG_INF),
        jnp.where(r8 < 2, a[6] + b_lo, NEG_INF),
        jnp.where(r8 < 2, a[7] + b_lo, NEG_INF),
        a_hi + b[0],
    ]
    sums = _pop_sorted(pieces, N_SORT)
    top16 = jnp.concatenate([_rows_to_tile(sums[0:8]), _rows_to_tile(sums[8:16])], axis=0)
    z = jnp.sum(jnp.exp(top16 - sums[0]), axis=0, keepdims=True)
    return 0.5 * (sums[PEER_TOPK - 1] + sums[PEER_TOPK]), 1.0 / z


def _split_bf16(x):
    hi = x.astype(BF16)
    return hi, (x - hi.astype(F32)).astype(BF16)


def _query_kernel(x1t_ref, wq_ref, khi_ref, klo_ref, u_ref, v_ref,
                  c1_ref, n1_ref, e2_ref, r2_ref, ub_ref, vt_ref, q_ref, s1_scr, s2_scr, *, tm):
    ub_ref[...] = u_ref[...].astype(BF16)
    vt_ref[...] = jnp.transpose(v_ref[...]).astype(BF16)
    q_ref[...] = jnp.dot(wq_ref[...], x1t_ref[...], preferred_element_type=F32)
    for h in range(PEER_HEADS):
        for p in range(2):
            hp = 2 * h + p
            q_hi, q_lo = _split_bf16(q_ref[hp * PEER_HALF:(hp + 1) * PEER_HALF, :])
            k_hi, k_lo = khi_ref[hp], klo_ref[hp]
            sc = (jnp.dot(k_hi, q_hi, preferred_element_type=F32)
                  + jnp.dot(k_hi, q_lo, preferred_element_type=F32)
                  + jnp.dot(k_lo, q_hi, preferred_element_type=F32))
            if p == 0:
                s1_scr[...] = sc
            else:
                s2_scr[...] = sc
        for tc in range(tm // LANES):
            lanes = slice(tc * LANES, (tc + 1) * LANES)
            s1 = s1_scr[:, lanes]
            s2 = s2_scr[:, lanes]
            ta, tb = _sorted_top(s1), _sorted_top(s2)
            tau, rz = _pair_stats(ta, tb)
            n1 = jnp.zeros_like(s1)
            r2 = jnp.zeros_like(s2)
            for jj in range(PEER_TOPK):
                n1 = jnp.where(s1 >= tau - tb[jj], float(jj + 1), n1)
                r2 = jnp.where(s2 < tb[jj], float(jj + 1), r2)
            c1_ref[h, tc] = jnp.exp(s1 - ta[0])
            n1_ref[h, tc] = n1
            e2_ref[h, :, lanes] = (jnp.exp(s2 - tb[0]) * rz).astype(BF16)
            r2_ref[h, :, lanes] = r2.astype(BF16)


def _query(x1t, wq_t, keys_hi, keys_lo, peer_u, peer_v, tm=256):
    T = x1t.shape[1]
    n_exp = peer_u.shape[0]
    steps = T // tm
    slab = n_exp // steps
    per_tile = PEER_TE // slab
    assert slab * steps == n_exp and per_tile * slab == PEER_TE and slab % LANES == 0
    nc = tm // LANES
    row_spec = pl.BlockSpec((PEER_HEADS, nc, PEER_NKEYS, LANES), lambda i: (0, i, 0, 0))
    col_spec = pl.BlockSpec((PEER_HEADS, PEER_NKEYS, tm), lambda i: (0, 0, i))
    key_spec = pl.BlockSpec((2 * PEER_HEADS, PEER_NKEYS, PEER_HALF), lambda i: (0, 0, 0))
    tab_spec = pl.BlockSpec((slab, D_MODEL), lambda i: (i, 0))
    row_shape = jax.ShapeDtypeStruct((PEER_HEADS, T // LANES, PEER_NKEYS, LANES), F32)
    col_shape = jax.ShapeDtypeStruct((PEER_HEADS, PEER_NKEYS, T), BF16)
    return pl.pallas_call(
        functools.partial(_query_kernel, tm=tm),
        grid=(steps,),
        in_specs=[
            pl.BlockSpec((D_MODEL, tm), lambda i: (0, i)),
            pl.BlockSpec((D_MODEL, D_MODEL), lambda i: (0, 0), pipeline_mode=pl.Buffered(1)),
            key_spec, key_spec, tab_spec, tab_spec,
        ],
        out_specs=[row_spec, row_spec, col_spec, col_spec, tab_spec,
                   pl.BlockSpec((None, D_MODEL, slab), lambda i: (i // per_tile, 0, i % per_tile))],
        out_shape=[row_shape, row_shape, col_shape, col_shape,
                   jax.ShapeDtypeStruct((n_exp, D_MODEL), BF16),
                   jax.ShapeDtypeStruct((n_exp // PEER_TE, D_MODEL, PEER_TE), BF16)],
        scratch_shapes=[pltpu.VMEM((D_MODEL, tm), F32), pltpu.VMEM((PEER_NKEYS, tm), F32),
                        pltpu.VMEM((PEER_NKEYS, tm), F32)],
        compiler_params=_params(("parallel",), 56),
        name="query",
    )(x1t, wq_t, keys_hi, keys_lo, peer_u, peer_v)


PEER_SB = 64
PEER_RG = 2
SUBLANES = 8
PEER_TE = 1024


def _gelu(x):
    return 0.5 * x * (1.0 + lax.erf(x * (1.0 / math.sqrt(2.0))))


def _bcast_row_bf16(tile, ri, rows):
    packed = jnp.broadcast_to(tile[ri:ri + 1, :], (2 * SUBLANES, LANES)).astype(BF16)
    return jnp.concatenate([packed] * (rows // (2 * SUBLANES)), axis=0)


def _peer_kernel(x1t_ref, u_ref, vt_ref, c1_ref, n1_ref, e2_ref, r2_ref, y_ref,
                 acc_ref, st_scr, ht_scr, *, tm, te):
    j = pl.program_id(1)
    n1 = te // PEER_NKEYS
    grows = PEER_RG * PEER_NKEYS

    @pl.when(j == 0)
    def _():
        acc_ref[...] = jnp.zeros_like(acc_ref)

    st_scr[...] = jnp.dot(u_ref[...], x1t_ref[...], preferred_element_type=F32).astype(BF16)

    nsb = PEER_NKEYS // PEER_SB
    tile_rows = pl.ds(pl.multiple_of(j * n1, SUBLANES), n1)
    for gi in range(n1 // PEER_RG):
        crows = slice(gi * grows, (gi + 1) * grows)
        for tc in range(tm // LANES):
            lanes = slice(tc * LANES, (tc + 1) * LANES)
            g = [[jnp.zeros((PEER_SB, LANES), BF16) for _ in range(nsb)] for _ in range(PEER_RG)]
            for h in range(PEER_HEADS):
                c1_t = c1_ref[h, tc, tile_rows, :]
                n1_t = n1_ref[h, tc, tile_rows, :]
                c1b = [_bcast_row_bf16(c1_t, gi * PEER_RG + r, PEER_SB) for r in range(PEER_RG)]
                n1b = [_bcast_row_bf16(n1_t, gi * PEER_RG + r, PEER_SB) for r in range(PEER_RG)]
                for sb in range(nsb):
                    rows = slice(sb * PEER_SB, (sb + 1) * PEER_SB)
                    r2c = r2_ref[h, rows, lanes]
                    e2c = e2_ref[h, rows, lanes]
                    for r in range(PEER_RG):
                        g[r][sb] = g[r][sb] + c1b[r] * jnp.where(r2c < n1b[r], e2c, jnp.zeros_like(e2c))
            for r in range(PEER_RG):
                for sb in range(nsb):
                    base = gi * grows + r * PEER_NKEYS + sb * PEER_SB
                    srows = slice(base, base + PEER_SB)
                    ht_scr[srows, lanes] = g[r][sb] * _gelu(st_scr[srows, lanes])
    acc_ref[...] += jnp.dot(vt_ref[...], ht_scr[...], preferred_element_type=F32)

    @pl.when(j == pl.num_programs(1) - 1)
    def _():
        y_ref[...] = jnp.transpose(acc_ref[...])


def _peer(x1t, u_tab, vt_tiles, c1, n1, e2, r2, tm=512):
    T = x1t.shape[1]
    nj, _, te = vt_tiles.shape
    row_spec = pl.BlockSpec((PEER_HEADS, tm // LANES, PEER_NKEYS, LANES), lambda i, j: (0, i, 0, 0))
    col_spec = pl.BlockSpec((PEER_HEADS, PEER_NKEYS, tm), lambda i, j: (0, 0, i))
    return pl.pallas_call(
        functools.partial(_peer_kernel, tm=tm, te=te),
        grid=(T // tm, nj),
        in_specs=[
            pl.BlockSpec((D_MODEL, tm), lambda i, j: (0, i)),
            pl.BlockSpec((te, D_MODEL), lambda i, j: (j, 0)),
            pl.BlockSpec((None, D_MODEL, te), lambda i, j: (j, 0, 0)),
            row_spec, row_spec, col_spec, col_spec,
        ],
        out_specs=pl.BlockSpec((tm, D_MODEL), lambda i, j: (i, 0)),
        out_shape=jax.ShapeDtypeStruct((T, D_MODEL), F32),
        scratch_shapes=[
            pltpu.VMEM((D_MODEL, tm), F32),
            pltpu.VMEM((te, tm), BF16),
            pltpu.VMEM((te, tm), BF16),
        ],
        compiler_params=_params(("parallel", "arbitrary"), 56),
        name="peer",
    )(x1t, u_tab, vt_tiles, c1, n1, e2, r2)


def _final_kernel(x1_ref, y_ref, p_ref, wg_ref, wp_ref, lw_ref, lb_ref, o_ref):
    for r0 in range(0, x1_ref.shape[0], ROW_SUB):
        rows = slice(r0, r0 + ROW_SUB)
        x1 = x1_ref[rows, :]
        gate = jax.nn.sigmoid(jnp.dot(x1.astype(BF16), wg_ref[...], preferred_element_type=F32))
        emb = jnp.dot(p_ref[rows, :].astype(BF16), wp_ref[...], preferred_element_type=F32)
        o_ref[rows, :] = _layer_norm(ALPHA * x1 + y_ref[rows, :] + gate * emb, lw_ref[...], lb_ref[...])


def _final(x1, y_ffn, p2, w_gate, w_proj, ln_w, ln_b, tm=512):
    T = x1.shape[0]
    return pl.pallas_call(
        _final_kernel,
        grid=(T // tm,),
        in_specs=[
            pl.BlockSpec((tm, D_MODEL), lambda i: (i, 0)),
            pl.BlockSpec((tm, D_MODEL), lambda i: (i, 0)),
            pl.BlockSpec((tm, PLE_DIM), lambda i: (i, 0)),
            pl.BlockSpec((D_MODEL, D_MODEL), lambda i: (0, 0), pipeline_mode=pl.Buffered(1)),
            pl.BlockSpec((PLE_DIM, D_MODEL), lambda i: (0, 0), pipeline_mode=pl.Buffered(1)),
            pl.BlockSpec((1, D_MODEL), lambda i: (0, 0)),
            pl.BlockSpec((1, D_MODEL), lambda i: (0, 0)),
        ],
        out_specs=pl.BlockSpec((tm, D_MODEL), lambda i: (i, 0)),
        out_shape=jax.ShapeDtypeStruct((T, D_MODEL), F32),
        compiler_params=_params(("parallel",), 48),
        name="final",
    )(x1, y_ffn, p2, w_gate, w_proj, ln_w, ln_b)


def _layer(x2, p2, B, S, w_in, gla_w_gate_up, gla_b_gate, gla_norm_w, pool_w, pool_scale, w_out,
           ln1_w, ln1_b, peer_w_query, peer_sub_keys, peer_u, peer_v, ple_w_gate, ple_w_proj, ln2_w, ln2_b):
    glr0 = COL_R
    w_main = jnp.concatenate([w_in[:, :glr0], w_in[:, glr0 + GLA_GATE_RANK:]], axis=1).astype(BF16)
    w_glr = jnp.pad(w_in[:, glr0:glr0 + GLA_GATE_RANK], ((0, 0), (0, LANES - GLA_GATE_RANK))).astype(BF16)
    proj, glr = _proj(x2, w_main, w_glr)

    wg = jnp.pad(gla_w_gate_up, ((0, LANES - GLA_GATE_RANK), (0, 0))).astype(BF16)
    y_gla, y_pool = _gla(proj, glr, wg, gla_b_gate.reshape(1, GLA_KEY_WIDTH), gla_norm_w.reshape(1, GLA_WIDTH),
                         pool_w.astype(BF16), pool_scale.reshape(1, POOL_WIDTH), B, S)

    x1, x1t = _outproj(y_pool, y_gla, x2, w_out.astype(BF16),
                       ln1_w.reshape(1, D_MODEL), ln1_b.reshape(1, D_MODEL))

    keys = peer_sub_keys.reshape(2 * PEER_HEADS, PEER_NKEYS, PEER_HALF)
    keys_hi, keys_lo = _split_bf16(keys)
    c1, n1, e2, r2, u_bf16, vt_tiles = _query(x1t, peer_w_query.T.astype(BF16), keys_hi, keys_lo, peer_u, peer_v)
    y_ffn = _peer(x1t, u_bf16, vt_tiles, c1, n1, e2, r2)

    return _final(x1, y_ffn, p2, ple_w_gate.astype(BF16), ple_w_proj.astype(BF16),
                  ln2_w.reshape(1, D_MODEL), ln2_b.reshape(1, D_MODEL))


def kernel(x, p, w_in, gla_w_gate_up, gla_b_gate, gla_norm_w, pool_w, pool_scale, w_out, ln1_w, ln1_b,
           peer_w_query, peer_sub_keys, peer_u, peer_v, ple_w_gate, ple_w_proj, ln2_w, ln2_b):
    B, S, D = x.shape
    x2 = x.reshape(B * S, D)
    for i in range(w_in.shape[0]):
        x2 = _layer(x2, p[i].reshape(B * S, PLE_DIM), B, S, w_in[i], gla_w_gate_up[i], gla_b_gate[i],
                    gla_norm_w[i], pool_w[i], pool_scale[i], w_out[i], ln1_w[i], ln1_b[i],
                    peer_w_query[i], peer_sub_keys[i], peer_u[i], peer_v[i], ple_w_gate[i],
                    ple_w_proj[i], ln2_w[i], ln2_b[i])
    return x2.reshape(B, S, D)
```

```python
import functools
import math

import jax
import jax.numpy as jnp
from jax import lax
from jax.experimental import pallas as pl
from jax.experimental.pallas import tpu as pltpu

F32 = jnp.float32
BF16 = jnp.bfloat16

D_MODEL = 2048
PLE_DIM = 256
POOL_WIDTH = 1024
POOL_WINDOWS = (2, 4, 8, 16)
POOL_GC = 256
POOL_HALO = 16
GLA_WIDTH = 1024
GLA_HEADS = 4
GLA_DV = 256
GLA_DK = 128
GLA_KEY_WIDTH = 512
GLA_GATE_RANK = 16
GLA_GATE_TEMP = 16.0
GLA_CHUNK = 64
PEER_HEADS = 8
PEER_NKEYS = 128
PEER_HALF = 128
PEER_TOPK = 16
DEPTH = 1
ALPHA = float((2 * DEPTH) ** 0.25)
LN_EPS = 1e-5
RMS_EPS = 1e-6
LANES = 128
NEG_INF = float("-inf")
ROW_SUB = 256

COL_Q = POOL_WIDTH
COL_K = COL_Q + GLA_KEY_WIDTH
COL_V = COL_K + GLA_KEY_WIDTH
COL_R = COL_V + GLA_WIDTH
PROJ_COLS = COL_R + GLA_WIDTH


def _params(sem, vmem_mib):
    return pltpu.CompilerParams(dimension_semantics=sem, vmem_limit_bytes=vmem_mib * 1024 * 1024)


def _proj_kernel(x_ref, w_ref, wg_ref, o_ref, glr_ref, xb_ref):
    @pl.when(pl.program_id(1) == 0)
    def _():
        xb = x_ref[...].astype(BF16)
        xb_ref[...] = xb
        glr_ref[...] = jnp.dot(xb, wg_ref[...], preferred_element_type=F32)

    o_ref[...] = jnp.dot(xb_ref[...], w_ref[...], preferred_element_type=F32).astype(o_ref.dtype)


def _proj(x2, w_main, w_glr, tm=1024, tn=2048):
    T = x2.shape[0]
    return pl.pallas_call(
        _proj_kernel,
        grid=(T // tm, PROJ_COLS // tn),
        in_specs=[
            pl.BlockSpec((tm, D_MODEL), lambda i, n: (i, 0)),
            pl.BlockSpec((D_MODEL, tn), lambda i, n: (0, n)),
            pl.BlockSpec((D_MODEL, LANES), lambda i, n: (0, 0)),
        ],
        out_specs=[
            pl.BlockSpec((tm, tn), lambda i, n: (i, n)),
            pl.BlockSpec((tm, LANES), lambda i, n: (i, 0)),
        ],
        out_shape=[
            jax.ShapeDtypeStruct((T, PROJ_COLS), BF16),
            jax.ShapeDtypeStruct((T, LANES), F32),
        ],
        scratch_shapes=[pltpu.VMEM((tm, D_MODEL), BF16)],
        compiler_params=_params(("parallel", "arbitrary"), 52),
        name="proj",
    )(x2, w_main, w_glr)


def _pool_block(u_ref, halo_ref, w_ref, sc_ref, o_ref, ext_ref, t):
    tm = u_ref.shape[0]
    halo = jnp.where(t == 0, 0.0, halo_ref[...].astype(F32))
    ext_ref[0:POOL_HALO, :] = halo
    ext_ref[POOL_HALO:, :] = u_ref[...].astype(F32)
    pos = t * tm + lax.broadcasted_iota(jnp.int32, (tm, 1), 0)
    for g, w in enumerate(POOL_WINDOWS):
        cols = slice(g * POOL_GC, (g + 1) * POOL_GC)
        u = ext_ref[POOL_HALO:, cols]
        acc = u
        for j in range(1, w):
            acc = acc + ext_ref[POOL_HALO - j:POOL_HALO - j + tm, cols]
        cnt = jnp.minimum(pos + 1, w).astype(F32)
        d = acc / cnt - u
        y = jnp.dot(d.astype(BF16), w_ref[g], preferred_element_type=F32)
        o_ref[:, cols] = (y * sc_ref[:, cols]).astype(o_ref.dtype)


def _gla_kernel(q_ref, k_ref, v_ref, r_ref, glr_ref, wg_ref, bg_ref, nw_ref, u_ref, halo_ref, pw_ref, psc_ref,
                o_ref, yp_ref, s_ref, ext_ref, *, n_chunks):
    @pl.when(pl.program_id(1) == 0)
    def _():
        s_ref[...] = jnp.zeros_like(s_ref)

    _pool_block(u_ref, halo_ref, pw_ref, psc_ref, yp_ref, ext_ref, pl.program_id(1))
    C = GLA_CHUNK
    row = lax.broadcasted_iota(jnp.int32, (C, C), 0)
    col = lax.broadcasted_iota(jnp.int32, (C, C), 1)
    causal = col <= row
    tril = causal.astype(BF16)
    wg = wg_ref[...]
    bg = bg_ref[...]
    nt = (((1,), (1,)), ((), ()))
    for c in range(n_chunks):
        rows = slice(c * C, (c + 1) * C)
        z = jnp.dot(glr_ref[rows, :].astype(BF16), wg, preferred_element_type=F32) + bg
        g = jax.nn.log_sigmoid(z) / GLA_GATE_TEMP
        g_hi, g_lo = _split_bf16(g)
        b_all = (jnp.dot(tril, g_hi, preferred_element_type=F32)
                 + jnp.dot(tril, g_lo, preferred_element_type=F32))
        for h in range(GLA_HEADS):
            kc = slice(h * GLA_DK, (h + 1) * GLA_DK)
            vc = slice(h * GLA_DV, (h + 1) * GLA_DV)
            b = b_all[:, kc]
            b_last = b[C - 1:C, :]
            b_mid = b[C // 2 - 1:C // 2, :]
            q = q_ref[rows, kc].astype(F32) * (GLA_DK ** -0.5)
            k = k_ref[rows, kc].astype(F32)
            v = v_ref[rows, vc]
            q_state = (q * jnp.exp(b)).astype(BF16)
            q_in = (q * jnp.exp(b - b_mid)).astype(BF16)
            k_in = (k * jnp.exp(b_mid - b)).astype(BF16)
            k_out = k * jnp.exp(b_last - b)
            attn = lax.dot_general(q_in, k_in, nt, preferred_element_type=F32)
            attn = jnp.where(causal, attn, 0.0).astype(BF16)
            s = s_ref[h]
            o = (jnp.dot(attn, v, preferred_element_type=F32)
                 + jnp.dot(q_state, s.astype(BF16), preferred_element_type=F32))
            decay = jnp.transpose(jnp.broadcast_to(jnp.exp(b_last), (C, GLA_DK)))[:, 0:1]
            s_ref[h] = decay * s + jnp.dot(jnp.transpose(k_out).astype(BF16), v, preferred_element_type=F32)
            o = o * lax.rsqrt(jnp.mean(jnp.square(o), axis=-1, keepdims=True) + RMS_EPS)
            o = o * nw_ref[:, vc]
            r = r_ref[rows, vc].astype(F32)
            o_ref[rows, vc] = (o * (r * jax.nn.sigmoid(r))).astype(o_ref.dtype)


def _gla(proj, glr, wg, bg, nw, pool_w, pool_scale, B, S, L=512):
    T = proj.shape[0]
    nl = S // L
    hb = L // POOL_HALO
    rb = lambda b, l: b * nl + l
    out_spec = pl.BlockSpec((L, GLA_WIDTH), lambda b, l: (rb(b, l), 0))
    return pl.pallas_call(
        functools.partial(_gla_kernel, n_chunks=L // GLA_CHUNK),
        grid=(B, nl),
        in_specs=[
            pl.BlockSpec((L, GLA_KEY_WIDTH), lambda b, l: (rb(b, l), COL_Q // GLA_KEY_WIDTH)),
            pl.BlockSpec((L, GLA_KEY_WIDTH), lambda b, l: (rb(b, l), COL_K // GLA_KEY_WIDTH)),
            pl.BlockSpec((L, GLA_WIDTH), lambda b, l: (rb(b, l), COL_V // GLA_WIDTH)),
            pl.BlockSpec((L, GLA_WIDTH), lambda b, l: (rb(b, l), COL_R // GLA_WIDTH)),
            pl.BlockSpec((L, LANES), lambda b, l: (rb(b, l), 0)),
            pl.BlockSpec((LANES, GLA_KEY_WIDTH), lambda b, l: (0, 0)),
            pl.BlockSpec((1, GLA_KEY_WIDTH), lambda b, l: (0, 0)),
            pl.BlockSpec((1, GLA_WIDTH), lambda b, l: (0, 0)),
            pl.BlockSpec((L, POOL_WIDTH), lambda b, l: (rb(b, l), 0)),
            pl.BlockSpec((POOL_HALO, POOL_WIDTH), lambda b, l: (jnp.maximum(rb(b, l) * hb - 1, 0), 0)),
            pl.BlockSpec((len(POOL_WINDOWS), POOL_GC, POOL_GC), lambda b, l: (0, 0, 0)),
            pl.BlockSpec((1, POOL_WIDTH), lambda b, l: (0, 0)),
        ],
        out_specs=[out_spec, out_spec],
        out_shape=[jax.ShapeDtypeStruct((T, GLA_WIDTH), BF16), jax.ShapeDtypeStruct((T, POOL_WIDTH), BF16)],
        scratch_shapes=[pltpu.VMEM((GLA_HEADS, GLA_DK, GLA_DV), F32),
                        pltpu.VMEM((POOL_HALO + L, POOL_WIDTH), F32)],
        compiler_params=_params(("parallel", "arbitrary"), 40),
        name="gla",
    )(proj, proj, proj, proj, glr, wg, bg, nw, proj, proj, pool_w, pool_scale)


def _layer_norm(h, w, b):
    mu = jnp.mean(h, axis=-1, keepdims=True)
    hc = h - mu
    var = jnp.mean(jnp.square(hc), axis=-1, keepdims=True)
    return hc * lax.rsqrt(var + LN_EPS) * w + b


def _outproj_kernel(yp_ref, yg_ref, x_ref, w_ref, lw_ref, lb_ref, x1_ref, x1t_ref):
    for r0 in range(0, x_ref.shape[0], ROW_SUB):
        rows = slice(r0, r0 + ROW_SUB)
        mix = (jnp.dot(yp_ref[rows, :], w_ref[0:POOL_WIDTH, :], preferred_element_type=F32)
               + jnp.dot(yg_ref[rows, :], w_ref[POOL_WIDTH:, :], preferred_element_type=F32))
        x1 = _layer_norm(ALPHA * x_ref[rows, :] + mix, lw_ref[...], lb_ref[...])
        x1_ref[rows, :] = x1
        x1t_ref[:, rows] = jnp.transpose(x1).astype(BF16)


def _outproj(y_pool, y_gla, x2, w_out, ln_w, ln_b, tm=512):
    T = x2.shape[0]
    return pl.pallas_call(
        _outproj_kernel,
        grid=(T // tm,),
        in_specs=[
            pl.BlockSpec((tm, POOL_WIDTH), lambda i: (i, 0)),
            pl.BlockSpec((tm, GLA_WIDTH), lambda i: (i, 0)),
            pl.BlockSpec((tm, D_MODEL), lambda i: (i, 0)),
            pl.BlockSpec((D_MODEL, D_MODEL), lambda i: (0, 0), pipeline_mode=pl.Buffered(1)),
            pl.BlockSpec((1, D_MODEL), lambda i: (0, 0)),
            pl.BlockSpec((1, D_MODEL), lambda i: (0, 0)),
        ],
        out_specs=[
            pl.BlockSpec((tm, D_MODEL), lambda i: (i, 0)),
            pl.BlockSpec((D_MODEL, tm), lambda i: (0, i)),
        ],
        out_shape=[
            jax.ShapeDtypeStruct((T, D_MODEL), F32),
            jax.ShapeDtypeStruct((D_MODEL, T), BF16),
        ],
        compiler_params=_params(("parallel",), 48),
        name="outproj",
    )(y_pool, y_gla, x2, w_out, ln_w, ln_b)


N_SORT = PEER_TOPK + 1


def _sort_network(n):
    pairs = []

    def merge(lo, m, r):
        step = 2 * r
        if step < m:
            merge(lo, m, step)
            merge(lo + r, m, step)
            pairs.extend((i, i + r) for i in range(lo + r, lo + m - r, step))
        else:
            pairs.append((lo, lo + r))

    def sort(lo, m):
        if m > 1:
            sort(lo, m // 2)
            sort(lo + m // 2, m // 2)
            merge(lo, m, 1)

    sort(0, n)
    return tuple(pairs)


def _pop_sorted(v, n_out):
    nv = len(v)
    width = 1 << (nv - 1).bit_length()
    for i, j in _sort_network(width):
        if j < nv:
            v[i], v[j] = jnp.maximum(v[i], v[j]), jnp.minimum(v[i], v[j])
    tops = []
    rid = lax.broadcasted_iota(jnp.int32, (SUBLANES, LANES), 0).astype(F32)
    for kk in range(n_out):
        m = jnp.max(v[0], axis=0, keepdims=True)
        tops.append(m)
        first = jnp.min(jnp.where(v[0] == m, rid, float(SUBLANES)), axis=0, keepdims=True)
        hit = rid == first
        for k in range(min(n_out - 1 - kk, nv)):
            v[k] = jnp.where(hit, v[k + 1] if k + 1 < nv else NEG_INF, v[k])
    return tops


def _sorted_top(arr):
    return _pop_sorted([arr[k:k + SUBLANES] for k in range(0, arr.shape[0], SUBLANES)], N_SORT)


def _rows_to_tile(rows):
    rid = lax.broadcasted_iota(jnp.int32, (SUBLANES, LANES), 0)
    tile = jnp.full((SUBLANES, LANES), NEG_INF, F32)
    for k, r in enumerate(rows):
        tile = jnp.where(rid == k, r, tile)
    return tile


def _pair_stats(a, b):
    r8 = lax.broadcasted_iota(jnp.int32, (SUBLANES, LANES), 0)
    b_lo, b_hi, a_hi = _rows_to_tile(b[0:8]), _rows_to_tile(b[8:16]), _rows_to_tile(a[8:16])
    p2 = jnp.where(r8 < 5, a[2] + b_lo, jnp.where(r8 == 5, a[16] + b[0], jnp.where(r8 == 6, a[0] + b[16], NEG_INF)))
    pieces = [
        a[0] + b_lo, a[0] + b_hi, a[1] + b_lo, p2,
        jnp.where(r8 < 4, a[3] + b_lo, NEG_INF),
        jnp.where(r8 < 3, a[4] + b_lo, NEG_INF),
        jnp.where(r8 < 2, a[5] + b_lo, NEG_INF),
        jnp.where(r8 < 2, a[6] + b_lo, NEG_INF),
        jnp.where(r8 < 2, a[7] + b_lo, NEG_INF),
        a_hi + b[0],
    ]
    sums = _pop_sorted(pieces, N_SORT)
    top16 = jnp.concatenate([_rows_to_tile(sums[0:8]), _rows_to_tile(sums[8:16])], axis=0)
    z = jnp.sum(jnp.exp(top16 - sums[0]), axis=0, keepdims=True)
    return 0.5 * (sums[PEER_TOPK - 1] + sums[PEER_TOPK]), 1.0 / z


def _prefix_count(passes, th):
    def pick(m, hi, lo):
        return jnp.where(m, hi, lo)

    m8 = passes(th[7])
    m4 = passes(pick(m8, th[11], th[3]))
    m2 = passes(pick(m8, pick(m4, th[13], th[9]), pick(m4, th[5], th[1])))
    lo = pick(m4, pick(m2, th[6], th[4]), pick(m2, th[2], th[0]))
    hi = pick(m4, pick(m2, th[14], th[12]), pick(m2, th[10], th[8]))
    m1 = passes(pick(m8, hi, lo))
    count = (jnp.where(m8, 8.0, 0.0) + jnp.where(m4, 4.0, 0.0)) + (jnp.where(m2, 2.0, 0.0) + jnp.where(m1, 1.0, 0.0))
    return jnp.where(passes(th[15]), float(len(th)), count)


def _split_bf16(x):
    hi = x.astype(BF16)
    return hi, (x - hi.astype(F32)).astype(BF16)


def _query_kernel(x1t_ref, wq_ref, khi_ref, klo_ref, u_ref, v_ref,
                  c1_ref, n1_ref, e2_ref, r2_ref, ub_ref, vt_ref, q_ref, s1_scr, s2_scr, *, tm):
    ub_ref[...] = u_ref[...].astype(BF16)
    vt_ref[...] = jnp.transpose(v_ref[...]).astype(BF16)
    q_ref[...] = jnp.dot(wq_ref[...], x1t_ref[...], preferred_element_type=F32)
    for h in range(PEER_HEADS):
        for p in range(2):
            hp = 2 * h + p
            q_hi, q_lo = _split_bf16(q_ref[hp * PEER_HALF:(hp + 1) * PEER_HALF, :])
            k_hi, k_lo = khi_ref[hp], klo_ref[hp]
            sc = (jnp.dot(k_hi, q_hi, preferred_element_type=F32)
                  + jnp.dot(k_hi, q_lo, preferred_element_type=F32)
                  + jnp.dot(k_lo, q_hi, preferred_element_type=F32))
            if p == 0:
                s1_scr[...] = sc
            else:
                s2_scr[...] = sc
        for tc in range(tm // LANES):
            lanes = slice(tc * LANES, (tc + 1) * LANES)
            s1 = s1_scr[:, lanes]
            s2 = s2_scr[:, lanes]
            ta, tb = _sorted_top(s1), _sorted_top(s2)
            tau, rz = _pair_stats(ta, tb)
            n1 = _prefix_count(lambda th: s1 >= th, [tau - tb[jj] for jj in range(PEER_TOPK)])
            r2 = _prefix_count(lambda th: s2 < th, tb[:PEER_TOPK])
            c1_ref[h, tc] = jnp.exp(s1 - ta[0])
            n1_ref[h, tc] = n1
            e2_ref[h, :, lanes] = (jnp.exp(s2 - tb[0]) * rz).astype(BF16)
            r2_ref[h, :, lanes] = r2.astype(BF16)


def _query(x1t, wq_t, keys_hi, keys_lo, peer_u, peer_v, tm=256):
    T = x1t.shape[1]
    n_exp = peer_u.shape[0]
    steps = T // tm
    slab = n_exp // steps
    per_tile = PEER_TE // slab
    assert slab * steps == n_exp and per_tile * slab == PEER_TE and slab % LANES == 0
    nc = tm // LANES
    row_spec = pl.BlockSpec((PEER_HEADS, nc, PEER_NKEYS, LANES), lambda i: (0, i, 0, 0))
    col_spec = pl.BlockSpec((PEER_HEADS, PEER_NKEYS, tm), lambda i: (0, 0, i))
    key_spec = pl.BlockSpec((2 * PEER_HEADS, PEER_NKEYS, PEER_HALF), lambda i: (0, 0, 0))
    tab_spec = pl.BlockSpec((slab, D_MODEL), lambda i: (i, 0))
    row_shape = jax.ShapeDtypeStruct((PEER_HEADS, T // LANES, PEER_NKEYS, LANES), F32)
    col_shape = jax.ShapeDtypeStruct((PEER_HEADS, PEER_NKEYS, T), BF16)
    return pl.pallas_call(
        functools.partial(_query_kernel, tm=tm),
        grid=(steps,),
        in_specs=[
            pl.BlockSpec((D_MODEL, tm), lambda i: (0, i)),
            pl.BlockSpec((D_MODEL, D_MODEL), lambda i: (0, 0), pipeline_mode=pl.Buffered(1)),
            key_spec, key_spec, tab_spec, tab_spec,
        ],
        out_specs=[row_spec, row_spec, col_spec, col_spec, tab_spec,
                   pl.BlockSpec((None, D_MODEL, slab), lambda i: (i // per_tile, 0, i % per_tile))],
        out_shape=[row_shape, row_shape, col_shape, col_shape,
                   jax.ShapeDtypeStruct((n_exp, D_MODEL), BF16),
                   jax.ShapeDtypeStruct((n_exp // PEER_TE, D_MODEL, PEER_TE), BF16)],
        scratch_shapes=[pltpu.VMEM((D_MODEL, tm), F32), pltpu.VMEM((PEER_NKEYS, tm), F32),
                        pltpu.VMEM((PEER_NKEYS, tm), F32)],
        compiler_params=_params(("parallel",), 56),
        name="query",
    )(x1t, wq_t, keys_hi, keys_lo, peer_u, peer_v)


PEER_SB = 64
PEER_RG = 2
SUBLANES = 8
PEER_TE = 1024


def _gelu(x):
    return 0.5 * x * (1.0 + lax.erf(x * (1.0 / math.sqrt(2.0))))


def _bcast_row_bf16(tile, ri, rows):
    packed = jnp.broadcast_to(tile[ri:ri + 1, :], (2 * SUBLANES, LANES)).astype(BF16)
    return jnp.concatenate([packed] * (rows // (2 * SUBLANES)), axis=0)


def _peer_kernel(x1t_ref, u_ref, vt_ref, c1_ref, n1_ref, e2_ref, r2_ref, y_ref,
                 acc_ref, st_scr, ht_scr, *, tm, te):
    j = pl.program_id(1)
    n1 = te // PEER_NKEYS
    grows = PEER_RG * PEER_NKEYS

    @pl.when(j == 0)
    def _():
        acc_ref[...] = jnp.zeros_like(acc_ref)

    st_scr[...] = jnp.dot(u_ref[...], x1t_ref[...], preferred_element_type=F32)

    nsb = PEER_NKEYS // PEER_SB
    tile_rows = pl.ds(pl.multiple_of(j * n1, SUBLANES), n1)
    for gi in range(n1 // PEER_RG):
        crows = slice(gi * grows, (gi + 1) * grows)
        for tc in range(tm // LANES):
            lanes = slice(tc * LANES, (tc + 1) * LANES)
            g = [[None] * nsb for _ in range(PEER_RG)]
            for h in range(PEER_HEADS):
                c1_t = c1_ref[h, tc, tile_rows, :]
                n1_t = n1_ref[h, tc, tile_rows, :]
                c1b = [_bcast_row_bf16(c1_t, gi * PEER_RG + r, PEER_SB) for r in range(PEER_RG)]
                n1b = [_bcast_row_bf16(n1_t, gi * PEER_RG + r, PEER_SB) for r in range(PEER_RG)]
                for sb in range(nsb):
                    rows = slice(sb * PEER_SB, (sb + 1) * PEER_SB)
                    r2c = r2_ref[h, rows, lanes]
                    e2c = e2_ref[h, rows, lanes]
                    for r in range(PEER_RG):
                        term = c1b[r] * jnp.where(r2c < n1b[r], e2c, jnp.zeros_like(e2c))
                        g[r][sb] = term if h == 0 else g[r][sb] + term
            for r in range(PEER_RG):
                for sb in range(nsb):
                    base = gi * grows + r * PEER_NKEYS + sb * PEER_SB
                    srows = slice(base, base + PEER_SB)
                    ht_scr[srows, lanes] = g[r][sb] * _gelu(st_scr[srows, lanes].astype(BF16))
    acc_ref[...] += jnp.dot(vt_ref[...], ht_scr[...], preferred_element_type=F32)

    @pl.when(j == pl.num_programs(1) - 1)
    def _():
        y_ref[...] = jnp.transpose(acc_ref[...])


def _peer(x1t, u_tab, vt_tiles, c1, n1, e2, r2, tm=512):
    T = x1t.shape[1]
    nj, _, te = vt_tiles.shape
    row_spec = pl.BlockSpec((PEER_HEADS, tm // LANES, PEER_NKEYS, LANES), lambda i, j: (0, i, 0, 0))
    col_spec = pl.BlockSpec((PEER_HEADS, PEER_NKEYS, tm), lambda i, j: (0, 0, i))
    return pl.pallas_call(
        functools.partial(_peer_kernel, tm=tm, te=te),
        grid=(T // tm, nj),
        in_specs=[
            pl.BlockSpec((D_MODEL, tm), lambda i, j: (0, i)),
            pl.BlockSpec((te, D_MODEL), lambda i, j: (j, 0)),
            pl.BlockSpec((None, D_MODEL, te), lambda i, j: (j, 0, 0)),
            row_spec, row_spec, col_spec, col_spec,
        ],
        out_specs=pl.BlockSpec((tm, D_MODEL), lambda i, j: (i, 0)),
        out_shape=jax.ShapeDtypeStruct((T, D_MODEL), F32),
        scratch_shapes=[
            pltpu.VMEM((D_MODEL, tm), F32),
            pltpu.VMEM((te, tm), F32),
            pltpu.VMEM((te, tm), BF16),
        ],
        compiler_params=_params(("parallel", "arbitrary"), 56),
        name="peer",
    )(x1t, u_tab, vt_tiles, c1, n1, e2, r2)


def _final_kernel(x1_ref, y_ref, p_ref, wg_ref, wp_ref, lw_ref, lb_ref, o_ref):
    for r0 in range(0, x1_ref.shape[0], ROW_SUB):
        rows = slice(r0, r0 + ROW_SUB)
        x1 = x1_ref[rows, :]
        gate = jax.nn.sigmoid(jnp.dot(x1.astype(BF16), wg_ref[...], preferred_element_type=F32))
        emb = jnp.dot(p_ref[rows, :].astype(BF16), wp_ref[...], preferred_element_type=F32)
        o_ref[rows, :] = _layer_norm(ALPHA * x1 + y_ref[rows, :] + gate * emb, lw_ref[...], lb_ref[...])


def _final(x1, y_ffn, p2, w_gate, w_proj, ln_w, ln_b, tm=512):
    T = x1.shape[0]
    return pl.pallas_call(
        _final_kernel,
        grid=(T // tm,),
        in_specs=[
            pl.BlockSpec((tm, D_MODEL), lambda i: (i, 0)),
            pl.BlockSpec((tm, D_MODEL), lambda i: (i, 0)),
            pl.BlockSpec((tm, PLE_DIM), lambda i: (i, 0)),
            pl.BlockSpec((D_MODEL, D_MODEL), lambda i: (0, 0), pipeline_mode=pl.Buffered(1)),
            pl.BlockSpec((PLE_DIM, D_MODEL), lambda i: (0, 0), pipeline_mode=pl.Buffered(1)),
            pl.BlockSpec((1, D_MODEL), lambda i: (0, 0)),
            pl.BlockSpec((1, D_MODEL), lambda i: (0, 0)),
        ],
        out_specs=pl.BlockSpec((tm, D_MODEL), lambda i: (i, 0)),
        out_shape=jax.ShapeDtypeStruct((T, D_MODEL), F32),
        compiler_params=_params(("parallel",), 48),
        name="final",
    )(x1, y_ffn, p2, w_gate, w_proj, ln_w, ln_b)


def _layer(x2, p2, B, S, w_in, gla_w_gate_up, gla_b_gate, gla_norm_w, pool_w, pool_scale, w_out,
           ln1_w, ln1_b, peer_w_query, peer_sub_keys, peer_u, peer_v, ple_w_gate, ple_w_proj, ln2_w, ln2_b):
    glr0 = COL_R
    w_main = jnp.concatenate([w_in[:, :glr0], w_in[:, glr0 + GLA_GATE_RANK:]], axis=1).astype(BF16)
    w_glr = jnp.pad(w_in[:, glr0:glr0 + GLA_GATE_RANK], ((0, 0), (0, LANES - GLA_GATE_RANK))).astype(BF16)
    proj, glr = _proj(x2, w_main, w_glr)

    wg = jnp.pad(gla_w_gate_up, ((0, LANES - GLA_GATE_RANK), (0, 0))).astype(BF16)
    y_gla, y_pool = _gla(proj, glr, wg, gla_b_gate.reshape(1, GLA_KEY_WIDTH), gla_norm_w.reshape(1, GLA_WIDTH),
                         pool_w.astype(BF16), pool_scale.reshape(1, POOL_WIDTH), B, S)

    x1, x1t = _outproj(y_pool, y_gla, x2, w_out.astype(BF16),
                       ln1_w.reshape(1, D_MODEL), ln1_b.reshape(1, D_MODEL))

    keys = peer_sub_keys.reshape(2 * PEER_HEADS, PEER_NKEYS, PEER_HALF)
    keys_hi, keys_lo = _split_bf16(keys)
    c1, n1, e2, r2, u_bf16, vt_tiles = _query(x1t, peer_w_query.T.astype(BF16), keys_hi, keys_lo, peer_u, peer_v)
    y_ffn = _peer(x1t, u_bf16, vt_tiles, c1, n1, e2, r2)

    return _final(x1, y_ffn, p2, ple_w_gate.astype(BF16), ple_w_proj.astype(BF16),
                  ln2_w.reshape(1, D_MODEL), ln2_b.reshape(1, D_MODEL))


def kernel(x, p, w_in, gla_w_gate_up, gla_b_gate, gla_norm_w, pool_w, pool_scale, w_out, ln1_w, ln1_b,
           peer_w_query, peer_sub_keys, peer_u, peer_v, ple_w_gate, ple_w_proj, ln2_w, ln2_b):
    B, S, D = x.shape
    x2 = x.reshape(B * S, D)
    for i in range(w_in.shape[0]):
        x2 = _layer(x2, p[i].reshape(B * S, PLE_DIM), B, S, w_in[i], gla_w_gate_up[i], gla_b_gate[i],
                    gla_norm_w[i], pool_w[i], pool_scale[i], w_out[i], ln1_w[i], ln1_b[i],
                    peer_w_query[i], peer_sub_keys[i], peer_u[i], peer_v[i], ple_w_gate[i],
                    ple_w_proj[i], ln2_w[i], ln2_b[i])
    return x2.reshape(B, S, D)
```

```python
import functools
import math

import jax
import jax.numpy as jnp
from jax import lax
from jax.experimental import pallas as pl
from jax.experimental.pallas import tpu as pltpu

F32 = jnp.float32
BF16 = jnp.bfloat16

D_MODEL = 2048
PLE_DIM = 256
POOL_WIDTH = 1024
POOL_WINDOWS = (2, 4, 8, 16)
POOL_GC = 256
POOL_HALO = 16
GLA_WIDTH = 1024
GLA_HEADS = 4
GLA_DV = 256
GLA_DK = 128
GLA_KEY_WIDTH = 512
GLA_GATE_RANK = 16
GLA_GATE_TEMP = 16.0
GLA_CHUNK = 64
PEER_HEADS = 8
PEER_NKEYS = 128
PEER_HALF = 128
PEER_TOPK = 16
DEPTH = 1
ALPHA = float((2 * DEPTH) ** 0.25)
LN_EPS = 1e-5
RMS_EPS = 1e-6
LANES = 128
NEG_INF = float("-inf")
ROW_SUB = 256

COL_Q = POOL_WIDTH
COL_K = COL_Q + GLA_KEY_WIDTH
COL_V = COL_K + GLA_KEY_WIDTH
COL_R = COL_V + GLA_WIDTH
PROJ_COLS = COL_R + GLA_WIDTH


def _params(sem, vmem_mib):
    return pltpu.CompilerParams(dimension_semantics=sem, vmem_limit_bytes=vmem_mib * 1024 * 1024)


def _proj_kernel(x_ref, w_ref, wg_ref, o_ref, glr_ref, xb_ref):
    @pl.when(pl.program_id(1) == 0)
    def _():
        xb = x_ref[...].astype(BF16)
        xb_ref[...] = xb
        glr_ref[...] = jnp.dot(xb, wg_ref[...], preferred_element_type=F32)

    o_ref[...] = jnp.dot(xb_ref[...], w_ref[...], preferred_element_type=F32).astype(o_ref.dtype)


def _proj(x2, w_main, w_glr, tm=1024, tn=2048):
    T = x2.shape[0]
    return pl.pallas_call(
        _proj_kernel,
        grid=(T // tm, PROJ_COLS // tn),
        in_specs=[
            pl.BlockSpec((tm, D_MODEL), lambda i, n: (i, 0)),
            pl.BlockSpec((D_MODEL, tn), lambda i, n: (0, n)),
            pl.BlockSpec((D_MODEL, LANES), lambda i, n: (0, 0)),
        ],
        out_specs=[
            pl.BlockSpec((tm, tn), lambda i, n: (i, n)),
            pl.BlockSpec((tm, LANES), lambda i, n: (i, 0)),
        ],
        out_shape=[
            jax.ShapeDtypeStruct((T, PROJ_COLS), BF16),
            jax.ShapeDtypeStruct((T, LANES), F32),
        ],
        scratch_shapes=[pltpu.VMEM((tm, D_MODEL), BF16)],
        compiler_params=_params(("parallel", "arbitrary"), 52),
        name="proj",
    )(x2, w_main, w_glr)


def _pool_block(u_ref, halo_ref, w_ref, sc_ref, o_ref, ext_ref, t):
    tm = u_ref.shape[0]
    halo = jnp.where(t == 0, 0.0, halo_ref[...].astype(F32))
    ext_ref[0:POOL_HALO, :] = halo
    ext_ref[POOL_HALO:, :] = u_ref[...].astype(F32)
    pos = t * tm + lax.broadcasted_iota(jnp.int32, (tm, 1), 0)
    for g, w in enumerate(POOL_WINDOWS):
        cols = slice(g * POOL_GC, (g + 1) * POOL_GC)
        u = ext_ref[POOL_HALO:, cols]
        acc = u
        for j in range(1, w):
            acc = acc + ext_ref[POOL_HALO - j:POOL_HALO - j + tm, cols]
        cnt = jnp.minimum(pos + 1, w).astype(F32)
        d = acc / cnt - u
        y = jnp.dot(d.astype(BF16), w_ref[g], preferred_element_type=F32)
        o_ref[:, cols] = (y * sc_ref[:, cols]).astype(o_ref.dtype)


def _gla_kernel(q_ref, k_ref, v_ref, r_ref, glr_ref, wg_ref, bg_ref, nw_ref, u_ref, halo_ref, pw_ref, psc_ref,
                o_ref, yp_ref, s_ref, ext_ref, *, n_chunks):
    @pl.when(pl.program_id(1) == 0)
    def _():
        s_ref[...] = jnp.zeros_like(s_ref)

    _pool_block(u_ref, halo_ref, pw_ref, psc_ref, yp_ref, ext_ref, pl.program_id(1))
    C = GLA_CHUNK
    row = lax.broadcasted_iota(jnp.int32, (C, C), 0)
    col = lax.broadcasted_iota(jnp.int32, (C, C), 1)
    causal = col <= row
    tril = causal.astype(BF16)
    wg = wg_ref[...]
    bg = bg_ref[...]
    nt = (((1,), (1,)), ((), ()))
    for c in range(n_chunks):
        rows = slice(c * C, (c + 1) * C)
        z = jnp.dot(glr_ref[rows, :].astype(BF16), wg, preferred_element_type=F32) + bg
        g = jax.nn.log_sigmoid(z) / GLA_GATE_TEMP
        g_hi, g_lo = _split_bf16(g)
        b_all = (jnp.dot(tril, g_hi, preferred_element_type=F32)
                 + jnp.dot(tril, g_lo, preferred_element_type=F32))
        for h in range(GLA_HEADS):
            kc = slice(h * GLA_DK, (h + 1) * GLA_DK)
            vc = slice(h * GLA_DV, (h + 1) * GLA_DV)
            b = b_all[:, kc]
            b_last = b[C - 1:C, :]
            b_mid = b[C // 2 - 1:C // 2, :]
            q = q_ref[rows, kc].astype(F32) * (GLA_DK ** -0.5)
            k = k_ref[rows, kc].astype(F32)
            v = v_ref[rows, vc]
            q_state = (q * jnp.exp(b)).astype(BF16)
            q_in = (q * jnp.exp(b - b_mid)).astype(BF16)
            k_in = (k * jnp.exp(b_mid - b)).astype(BF16)
            k_out = k * jnp.exp(b_last - b)
            attn = lax.dot_general(q_in, k_in, nt, preferred_element_type=F32)
            attn = jnp.where(causal, attn, 0.0).astype(BF16)
            s = s_ref[h]
            o = (jnp.dot(attn, v, preferred_element_type=F32)
                 + jnp.dot(q_state, s.astype(BF16), preferred_element_type=F32))
            decay = jnp.transpose(jnp.broadcast_to(jnp.exp(b_last), (C, GLA_DK)))[:, 0:1]
            s_ref[h] = decay * s + jnp.dot(jnp.transpose(k_out).astype(BF16), v, preferred_element_type=F32)
            o = o * lax.rsqrt(jnp.mean(jnp.square(o), axis=-1, keepdims=True) + RMS_EPS)
            o = o * nw_ref[:, vc]
            r = r_ref[rows, vc].astype(F32)
            o_ref[rows, vc] = (o * (r * jax.nn.sigmoid(r))).astype(o_ref.dtype)


def _gla(proj, glr, wg, bg, nw, pool_w, pool_scale, B, S, L=512):
    T = proj.shape[0]
    nl = S // L
    hb = L // POOL_HALO
    rb = lambda b, l: b * nl + l
    out_spec = pl.BlockSpec((L, GLA_WIDTH), lambda b, l: (rb(b, l), 0))
    return pl.pallas_call(
        functools.partial(_gla_kernel, n_chunks=L // GLA_CHUNK),
        grid=(B, nl),
        in_specs=[
            pl.BlockSpec((L, GLA_KEY_WIDTH), lambda b, l: (rb(b, l), COL_Q // GLA_KEY_WIDTH)),
            pl.BlockSpec((L, GLA_KEY_WIDTH), lambda b, l: (rb(b, l), COL_K // GLA_KEY_WIDTH)),
            pl.BlockSpec((L, GLA_WIDTH), lambda b, l: (rb(b, l), COL_V // GLA_WIDTH)),
            pl.BlockSpec((L, GLA_WIDTH), lambda b, l: (rb(b, l), COL_R // GLA_WIDTH)),
            pl.BlockSpec((L, LANES), lambda b, l: (rb(b, l), 0)),
            pl.BlockSpec((LANES, GLA_KEY_WIDTH), lambda b, l: (0, 0)),
            pl.BlockSpec((1, GLA_KEY_WIDTH), lambda b, l: (0, 0)),
            pl.BlockSpec((1, GLA_WIDTH), lambda b, l: (0, 0)),
            pl.BlockSpec((L, POOL_WIDTH), lambda b, l: (rb(b, l), 0)),
            pl.BlockSpec((POOL_HALO, POOL_WIDTH), lambda b, l: (jnp.maximum(rb(b, l) * hb - 1, 0), 0)),
            pl.BlockSpec((len(POOL_WINDOWS), POOL_GC, POOL_GC), lambda b, l: (0, 0, 0)),
            pl.BlockSpec((1, POOL_WIDTH), lambda b, l: (0, 0)),
        ],
        out_specs=[out_spec, out_spec],
        out_shape=[jax.ShapeDtypeStruct((T, GLA_WIDTH), BF16), jax.ShapeDtypeStruct((T, POOL_WIDTH), BF16)],
        scratch_shapes=[pltpu.VMEM((GLA_HEADS, GLA_DK, GLA_DV), F32),
                        pltpu.VMEM((POOL_HALO + L, POOL_WIDTH), F32)],
        compiler_params=_params(("parallel", "arbitrary"), 40),
        name="gla",
    )(proj, proj, proj, proj, glr, wg, bg, nw, proj, proj, pool_w, pool_scale)


def _layer_norm(h, w, b):
    mu = jnp.mean(h, axis=-1, keepdims=True)
    hc = h - mu
    var = jnp.mean(jnp.square(hc), axis=-1, keepdims=True)
    return hc * lax.rsqrt(var + LN_EPS) * w + b


def _outproj_kernel(yp_ref, yg_ref, x_ref, w_ref, lw_ref, lb_ref, x1_ref, x1t_ref):
    for r0 in range(0, x_ref.shape[0], ROW_SUB):
        rows = slice(r0, r0 + ROW_SUB)
        mix = (jnp.dot(yp_ref[rows, :], w_ref[0:POOL_WIDTH, :], preferred_element_type=F32)
               + jnp.dot(yg_ref[rows, :], w_ref[POOL_WIDTH:, :], preferred_element_type=F32))
        x1 = _layer_norm(ALPHA * x_ref[rows, :] + mix, lw_ref[...], lb_ref[...])
        x1_ref[rows, :] = x1
        x1t_ref[:, rows] = jnp.transpose(x1).astype(BF16)


def _outproj(y_pool, y_gla, x2, w_out, ln_w, ln_b, tm=512):
    T = x2.shape[0]
    return pl.pallas_call(
        _outproj_kernel,
        grid=(T // tm,),
        in_specs=[
            pl.BlockSpec((tm, POOL_WIDTH), lambda i: (i, 0)),
            pl.BlockSpec((tm, GLA_WIDTH), lambda i: (i, 0)),
            pl.BlockSpec((tm, D_MODEL), lambda i: (i, 0)),
            pl.BlockSpec((D_MODEL, D_MODEL), lambda i: (0, 0), pipeline_mode=pl.Buffered(1)),
            pl.BlockSpec((1, D_MODEL), lambda i: (0, 0)),
            pl.BlockSpec((1, D_MODEL), lambda i: (0, 0)),
        ],
        out_specs=[
            pl.BlockSpec((tm, D_MODEL), lambda i: (i, 0)),
            pl.BlockSpec((D_MODEL, tm), lambda i: (0, i)),
        ],
        out_shape=[
            jax.ShapeDtypeStruct((T, D_MODEL), F32),
            jax.ShapeDtypeStruct((D_MODEL, T), BF16),
        ],
        compiler_params=_params(("parallel",), 48),
        name="outproj",
    )(y_pool, y_gla, x2, w_out, ln_w, ln_b)


N_SORT = PEER_TOPK + 1


def _sort_network(n):
    pairs = []

    def merge(lo, m, r):
        step = 2 * r
        if step < m:
            merge(lo, m, step)
            merge(lo + r, m, step)
            pairs.extend((i, i + r) for i in range(lo + r, lo + m - r, step))
        else:
            pairs.append((lo, lo + r))

    def sort(lo, m):
        if m > 1:
            sort(lo, m // 2)
            sort(lo + m // 2, m // 2)
            merge(lo, m, 1)

    sort(0, n)
    return tuple(pairs)


def _pop_sorted(v, n_out):
    nv = len(v)
    width = 1 << (nv - 1).bit_length()
    for i, j in _sort_network(width):
        if j < nv:
            v[i], v[j] = jnp.maximum(v[i], v[j]), jnp.minimum(v[i], v[j])
    tops = []
    rid = lax.broadcasted_iota(jnp.int32, (SUBLANES, LANES), 0).astype(F32)
    for kk in range(n_out):
        m = jnp.max(v[0], axis=0, keepdims=True)
        tops.append(m)
        first = jnp.min(jnp.where(v[0] == m, rid, float(SUBLANES)), axis=0, keepdims=True)
        hit = rid == first
        for k in range(min(n_out - 1 - kk, nv)):
            v[k] = jnp.where(hit, v[k + 1] if k + 1 < nv else NEG_INF, v[k])
    return tops


def _sorted_top(arr):
    return _pop_sorted([arr[k:k + SUBLANES] for k in range(0, arr.shape[0], SUBLANES)], N_SORT)


def _rows_to_tile(rows):
    rid = lax.broadcasted_iota(jnp.int32, (SUBLANES, LANES), 0)
    tile = jnp.full((SUBLANES, LANES), NEG_INF, F32)
    for k, r in enumerate(rows):
        tile = jnp.where(rid == k, r, tile)
    return tile


def _pair_stats(a, b):
    r8 = lax.broadcasted_iota(jnp.int32, (SUBLANES, LANES), 0)
    b_lo, b_hi, a_hi = _rows_to_tile(b[0:8]), _rows_to_tile(b[8:16]), _rows_to_tile(a[8:16])
    p2 = jnp.where(r8 < 5, a[2] + b_lo, jnp.where(r8 == 5, a[16] + b[0], jnp.where(r8 == 6, a[0] + b[16], NEG_INF)))
    pieces = [
        a[0] + b_lo, a[0] + b_hi, a[1] + b_lo, p2,
        jnp.where(r8 < 4, a[3] + b_lo, NEG_INF),
        jnp.where(r8 < 3, a[4] + b_lo, NEG_INF),
        jnp.where(r8 < 2, a[5] + b_lo, NEG_INF),
        jnp.where(r8 < 2, a[6] + b_lo, NEG_INF),
        jnp.where(r8 < 2, a[7] + b_lo, NEG_INF),
        a_hi + b[0],
    ]
    sums = _pop_sorted(pieces, N_SORT)
    top16 = jnp.concatenate([_rows_to_tile(sums[0:8]), _rows_to_tile(sums[8:16])], axis=0)
    z = jnp.sum(jnp.exp(top16 - sums[0]), axis=0, keepdims=True)
    return 0.5 * (sums[PEER_TOPK - 1] + sums[PEER_TOPK]), 1.0 / z


def _split_bf16(x):
    hi = x.astype(BF16)
    return hi, (x - hi.astype(F32)).astype(BF16)


def _query_kernel(x1t_ref, wq_ref, khi_ref, klo_ref, u_ref, v_ref,
                  c1_ref, n1_ref, e2_ref, r2_ref, ub_ref, vt_ref, q_ref, s1_scr, s2_scr, *, tm):
    ub_ref[...] = u_ref[...].astype(BF16)
    vt_ref[...] = jnp.transpose(v_ref[...]).astype(BF16)
    q_ref[...] = jnp.dot(wq_ref[...], x1t_ref[...], preferred_element_type=F32)
    for h in range(PEER_HEADS):
        for p in range(2):
            hp = 2 * h + p
            q_hi, q_lo = _split_bf16(q_ref[hp * PEER_HALF:(hp + 1) * PEER_HALF, :])
            k_hi, k_lo = khi_ref[hp], klo_ref[hp]
            sc = (jnp.dot(k_hi, q_hi, preferred_element_type=F32)
                  + jnp.dot(k_hi, q_lo, preferred_element_type=F32)
                  + jnp.dot(k_lo, q_hi, preferred_element_type=F32))
            if p == 0:
                s1_scr[...] = sc
            else:
                s2_scr[...] = sc
        for tc in range(tm // LANES):
            lanes = slice(tc * LANES, (tc + 1) * LANES)
            s1 = s1_scr[:, lanes]
            s2 = s2_scr[:, lanes]
            ta, tb = _sorted_top(s1), _sorted_top(s2)
            tau, rz = _pair_stats(ta, tb)
            n1 = jnp.zeros_like(s1)
            r2 = jnp.zeros_like(s2)
            for jj in range(PEER_TOPK):
                n1 = jnp.where(s1 >= tau - tb[jj], float(jj + 1), n1)
                r2 = jnp.where(s2 < tb[jj], float(jj + 1), r2)
            c1_ref[h, tc] = jnp.exp(s1 - ta[0])
            n1_ref[h, tc] = n1
            e2_ref[h, :, lanes] = (jnp.exp(s2 - tb[0]) * rz).astype(BF16)
            r2_ref[h, :, lanes] = r2.astype(BF16)


def _query(x1t, wq_t, keys_hi, keys_lo, peer_u, peer_v, tm=256):
    T = x1t.shape[1]
    n_exp = peer_u.shape[0]
    steps = T // tm
    slab = n_exp // steps
    per_tile = PEER_TE // slab
    assert slab * steps == n_exp and per_tile * slab == PEER_TE and slab % LANES == 0
    nc = tm // LANES
    row_spec = pl.BlockSpec((PEER_HEADS, nc, PEER_NKEYS, LANES), lambda i: (0, i, 0, 0))
    col_spec = pl.BlockSpec((PEER_HEADS, PEER_NKEYS, tm), lambda i: (0, 0, i))
    key_spec = pl.BlockSpec((2 * PEER_HEADS, PEER_NKEYS, PEER_HALF), lambda i: (0, 0, 0))
    tab_spec = pl.BlockSpec((slab, D_MODEL), lambda i: (i, 0))
    row_shape = jax.ShapeDtypeStruct((PEER_HEADS, T // LANES, PEER_NKEYS, LANES), F32)
    col_shape = jax.ShapeDtypeStruct((PEER_HEADS, PEER_NKEYS, T), BF16)
    return pl.pallas_call(
        functools.partial(_query_kernel, tm=tm),
        grid=(steps,),
        in_specs=[
            pl.BlockSpec((D_MODEL, tm), lambda i: (0, i)),
            pl.BlockSpec((D_MODEL, D_MODEL), lambda i: (0, 0), pipeline_mode=pl.Buffered(1)),
            key_spec, key_spec, tab_spec, tab_spec,
        ],
        out_specs=[row_spec, row_spec, col_spec, col_spec, tab_spec,
                   pl.BlockSpec((None, D_MODEL, slab), lambda i: (i // per_tile, 0, i % per_tile))],
        out_shape=[row_shape, row_shape, col_shape, col_shape,
                   jax.ShapeDtypeStruct((n_exp, D_MODEL), BF16),
                   jax.ShapeDtypeStruct((n_exp // PEER_TE, D_MODEL, PEER_TE), BF16)],
        scratch_shapes=[pltpu.VMEM((D_MODEL, tm), F32), pltpu.VMEM((PEER_NKEYS, tm), F32),
                        pltpu.VMEM((PEER_NKEYS, tm), F32)],
        compiler_params=_params(("parallel",), 56),
        name="query",
    )(x1t, wq_t, keys_hi, keys_lo, peer_u, peer_v)


PEER_SB = 64
PEER_RG = 2
SUBLANES = 8
PEER_TE = 1024


def _gelu(x):
    return 0.5 * x * (1.0 + lax.erf(x * (1.0 / math.sqrt(2.0))))


def _bcast_row_bf16(tile, ri, rows):
    packed = jnp.broadcast_to(tile[ri:ri + 1, :], (2 * SUBLANES, LANES)).astype(BF16)
    return jnp.concatenate([packed] * (rows // (2 * SUBLANES)), axis=0)


def _peer_kernel(x1t_ref, u_ref, vt_ref, c1_ref, n1_ref, e2_ref, r2_ref, yt_ref, st_scr, ht_scr, *, tm, te):
    j = pl.program_id(1)
    n1 = te // PEER_NKEYS
    grows = PEER_RG * PEER_NKEYS

    @pl.when(j == 0)
    def _():
        yt_ref[...] = jnp.zeros_like(yt_ref)

    st_scr[...] = jnp.dot(u_ref[...], x1t_ref[...], preferred_element_type=F32)

    nsb = PEER_NKEYS // PEER_SB
    tile_rows = pl.ds(pl.multiple_of(j * n1, SUBLANES), n1)
    for gi in range(n1 // PEER_RG):
        crows = slice(gi * grows, (gi + 1) * grows)
        for tc in range(tm // LANES):
            lanes = slice(tc * LANES, (tc + 1) * LANES)
            g = [[jnp.zeros((PEER_SB, LANES), BF16) for _ in range(nsb)] for _ in range(PEER_RG)]
            for h in range(PEER_HEADS):
                c1_t = c1_ref[h, tc, tile_rows, :]
                n1_t = n1_ref[h, tc, tile_rows, :]
                c1b = [_bcast_row_bf16(c1_t, gi * PEER_RG + r, PEER_SB) for r in range(PEER_RG)]
                n1b = [_bcast_row_bf16(n1_t, gi * PEER_RG + r, PEER_SB) for r in range(PEER_RG)]
                for sb in range(nsb):
                    rows = slice(sb * PEER_SB, (sb + 1) * PEER_SB)
                    r2c = r2_ref[h, rows, lanes]
                    e2c = e2_ref[h, rows, lanes]
                    for r in range(PEER_RG):
                        g[r][sb] = g[r][sb] + c1b[r] * jnp.where(r2c < n1b[r], e2c, jnp.zeros_like(e2c))
            for r in range(PEER_RG):
                for sb in range(nsb):
                    base = gi * grows + r * PEER_NKEYS + sb * PEER_SB
                    srows = slice(base, base + PEER_SB)
                    ht_scr[srows, lanes] = g[r][sb] * _gelu(st_scr[srows, lanes].astype(BF16))
    yt_ref[...] += jnp.dot(vt_ref[...], ht_scr[...], preferred_element_type=F32)


def _peer(x1t, u_tab, vt_tiles, c1, n1, e2, r2, tm=512):
    T = x1t.shape[1]
    nj, _, te = vt_tiles.shape
    row_spec = pl.BlockSpec((PEER_HEADS, tm // LANES, PEER_NKEYS, LANES), lambda i, j: (0, i, 0, 0))
    col_spec = pl.BlockSpec((PEER_HEADS, PEER_NKEYS, tm), lambda i, j: (0, 0, i))
    return pl.pallas_call(
        functools.partial(_peer_kernel, tm=tm, te=te),
        grid=(T // tm, nj),
        in_specs=[
            pl.BlockSpec((D_MODEL, tm), lambda i, j: (0, i)),
            pl.BlockSpec((te, D_MODEL), lambda i, j: (j, 0)),
            pl.BlockSpec((None, D_MODEL, te), lambda i, j: (j, 0, 0)),
            row_spec, row_spec, col_spec, col_spec,
        ],
        out_specs=pl.BlockSpec((D_MODEL, tm), lambda i, j: (0, i)),
        out_shape=jax.ShapeDtypeStruct((D_MODEL, T), F32),
        scratch_shapes=[
            pltpu.VMEM((te, tm), F32),
            pltpu.VMEM((te, tm), BF16),
        ],
        compiler_params=_params(("parallel", "arbitrary"), 56),
        name="peer",
    )(x1t, u_tab, vt_tiles, c1, n1, e2, r2)


def _final_kernel(x1_ref, yt_ref, p_ref, wg_ref, wp_ref, lw_ref, lb_ref, o_ref):
    for r0 in range(0, x1_ref.shape[0], ROW_SUB):
        rows = slice(r0, r0 + ROW_SUB)
        x1 = x1_ref[rows, :]
        gate = jax.nn.sigmoid(jnp.dot(x1.astype(BF16), wg_ref[...], preferred_element_type=F32))
        emb = jnp.dot(p_ref[rows, :].astype(BF16), wp_ref[...], preferred_element_type=F32)
        y = jnp.transpose(yt_ref[:, rows])
        o_ref[rows, :] = _layer_norm(ALPHA * x1 + y + gate * emb, lw_ref[...], lb_ref[...])


def _final(x1, y_ffn_t, p2, w_gate, w_proj, ln_w, ln_b, tm=512):
    T = x1.shape[0]
    return pl.pallas_call(
        _final_kernel,
        grid=(T // tm,),
        in_specs=[
            pl.BlockSpec((tm, D_MODEL), lambda i: (i, 0)),
            pl.BlockSpec((D_MODEL, tm), lambda i: (0, i)),
            pl.BlockSpec((tm, PLE_DIM), lambda i: (i, 0)),
            pl.BlockSpec((D_MODEL, D_MODEL), lambda i: (0, 0), pipeline_mode=pl.Buffered(1)),
            pl.BlockSpec((PLE_DIM, D_MODEL), lambda i: (0, 0), pipeline_mode=pl.Buffered(1)),
            pl.BlockSpec((1, D_MODEL), lambda i: (0, 0)),
            pl.BlockSpec((1, D_MODEL), lambda i: (0, 0)),
        ],
        out_specs=pl.BlockSpec((tm, D_MODEL), lambda i: (i, 0)),
        out_shape=jax.ShapeDtypeStruct((T, D_MODEL), F32),
        compiler_params=_params(("parallel",), 48),
        name="final",
    )(x1, y_ffn_t, p2, w_gate, w_proj, ln_w, ln_b)


def _layer(x2, p2, B, S, w_in, gla_w_gate_up, gla_b_gate, gla_norm_w, pool_w, pool_scale, w_out,
           ln1_w, ln1_b, peer_w_query, peer_sub_keys, peer_u, peer_v, ple_w_gate, ple_w_proj, ln2_w, ln2_b):
    glr0 = COL_R
    w_main = jnp.concatenate([w_in[:, :glr0], w_in[:, glr0 + GLA_GATE_RANK:]], axis=1).astype(BF16)
    w_glr = jnp.pad(w_in[:, glr0:glr0 + GLA_GATE_RANK], ((0, 0), (0, LANES - GLA_GATE_RANK))).astype(BF16)
    proj, glr = _proj(x2, w_main, w_glr)

    wg = jnp.pad(gla_w_gate_up, ((0, LANES - GLA_GATE_RANK), (0, 0))).astype(BF16)
    y_gla, y_pool = _gla(proj, glr, wg, gla_b_gate.reshape(1, GLA_KEY_WIDTH), gla_norm_w.reshape(1, GLA_WIDTH),
                         pool_w.astype(BF16), pool_scale.reshape(1, POOL_WIDTH), B, S)

    x1, x1t = _outproj(y_pool, y_gla, x2, w_out.astype(BF16),
                       ln1_w.reshape(1, D_MODEL), ln1_b.reshape(1, D_MODEL))

    keys = peer_sub_keys.reshape(2 * PEER_HEADS, PEER_NKEYS, PEER_HALF)
    keys_hi, keys_lo = _split_bf16(keys)
    c1, n1, e2, r2, u_bf16, vt_tiles = _query(x1t, peer_w_query.T.astype(BF16), keys_hi, keys_lo, peer_u, peer_v)
    y_ffn_t = _peer(x1t, u_bf16, vt_tiles, c1, n1, e2, r2)

    return _final(x1, y_ffn_t, p2, ple_w_gate.astype(BF16), ple_w_proj.astype(BF16),
                  ln2_w.reshape(1, D_MODEL), ln2_b.reshape(1, D_MODEL))


def kernel(x, p, w_in, gla_w_gate_up, gla_b_gate, gla_norm_w, pool_w, pool_scale, w_out, ln1_w, ln1_b,
           peer_w_query, peer_sub_keys, peer_u, peer_v, ple_w_gate, ple_w_proj, ln2_w, ln2_b):
    B, S, D = x.shape
    x2 = x.reshape(B * S, D)
    for i in range(w_in.shape[0]):
        x2 = _layer(x2, p[i].reshape(B * S, PLE_DIM), B, S, w_in[i], gla_w_gate_up[i], gla_b_gate[i],
                    gla_norm_w[i], pool_w[i], pool_scale[i], w_out[i], ln1_w[i], ln1_b[i],
                    peer_w_query[i], peer_sub_keys[i], peer_u[i], peer_v[i], ple_w_gate[i],
                    ple_w_proj[i], ln2_w[i], ln2_b[i])
    return x2.reshape(B, S, D)
```

```python
import functools
import math

import jax
import jax.numpy as jnp
from jax import lax
from jax.experimental import pallas as pl
from jax.experimental.pallas import tpu as pltpu

F32 = jnp.float32
BF16 = jnp.bfloat16

D_MODEL = 2048
PLE_DIM = 256
POOL_WIDTH = 1024
POOL_WINDOWS = (2, 4, 8, 16)
POOL_GC = 256
POOL_HALO = 16
GLA_WIDTH = 1024
GLA_HEADS = 4
GLA_DV = 256
GLA_DK = 128
GLA_KEY_WIDTH = 512
GLA_GATE_RANK = 16
GLA_GATE_TEMP = 16.0
GLA_CHUNK = 64
PEER_HEADS = 8
PEER_NKEYS = 128
PEER_HALF = 128
PEER_TOPK = 16
DEPTH = 1
ALPHA = float((2 * DEPTH) ** 0.25)
LN_EPS = 1e-5
RMS_EPS = 1e-6
LANES = 128
NEG_INF = float("-inf")
ROW_SUB = 256

COL_Q = POOL_WIDTH
COL_K = COL_Q + GLA_KEY_WIDTH
COL_V = COL_K + GLA_KEY_WIDTH
COL_R = COL_V + GLA_WIDTH
PROJ_COLS = COL_R + GLA_WIDTH


def _sigmoid(x):
    return 0.5 * jnp.tanh(0.5 * x) + 0.5


def _params(sem, vmem_mib):
    return pltpu.CompilerParams(dimension_semantics=sem, vmem_limit_bytes=vmem_mib * 1024 * 1024)


def _proj_kernel(x_ref, w_ref, wg_ref, o_ref, glr_ref, xb_ref):
    @pl.when(pl.program_id(1) == 0)
    def _():
        xb = x_ref[...].astype(BF16)
        xb_ref[...] = xb
        glr_ref[...] = jnp.dot(xb, wg_ref[...], preferred_element_type=F32)

    o_ref[...] = jnp.dot(xb_ref[...], w_ref[...], preferred_element_type=F32).astype(o_ref.dtype)


def _proj(x2, w_main, w_glr, tm=1024, tn=2048):
    T = x2.shape[0]
    return pl.pallas_call(
        _proj_kernel,
        grid=(T // tm, PROJ_COLS // tn),
        in_specs=[
            pl.BlockSpec((tm, D_MODEL), lambda i, n: (i, 0)),
            pl.BlockSpec((D_MODEL, tn), lambda i, n: (0, n)),
            pl.BlockSpec((D_MODEL, LANES), lambda i, n: (0, 0)),
        ],
        out_specs=[
            pl.BlockSpec((tm, tn), lambda i, n: (i, n)),
            pl.BlockSpec((tm, LANES), lambda i, n: (i, 0)),
        ],
        out_shape=[
            jax.ShapeDtypeStruct((T, PROJ_COLS), BF16),
            jax.ShapeDtypeStruct((T, LANES), F32),
        ],
        scratch_shapes=[pltpu.VMEM((tm, D_MODEL), BF16)],
        compiler_params=_params(("parallel", "arbitrary"), 52),
        name="proj",
    )(x2, w_main, w_glr)


def _pool_block(u_ref, halo_ref, w_ref, sc_ref, o_ref, ext_ref, t):
    tm = u_ref.shape[0]
    halo = jnp.where(t == 0, 0.0, halo_ref[...].astype(F32))
    ext_ref[0:POOL_HALO, :] = halo
    ext_ref[POOL_HALO:, :] = u_ref[...].astype(F32)
    pos = t * tm + lax.broadcasted_iota(jnp.int32, (tm, 1), 0)
    for g, w in enumerate(POOL_WINDOWS):
        cols = slice(g * POOL_GC, (g + 1) * POOL_GC)
        u = ext_ref[POOL_HALO:, cols]
        acc = u
        for j in range(1, w):
            acc = acc + ext_ref[POOL_HALO - j:POOL_HALO - j + tm, cols]
        cnt = jnp.minimum(pos + 1, w).astype(F32)
        d = acc / cnt - u
        y = jnp.dot(d.astype(BF16), w_ref[g], preferred_element_type=F32)
        o_ref[:, cols] = (y * sc_ref[:, cols]).astype(o_ref.dtype)


def _gla_kernel(q_ref, k_ref, v_ref, r_ref, glr_ref, wg_ref, bg_ref, nw_ref, u_ref, halo_ref, pw_ref, psc_ref,
                o_ref, yp_ref, s_ref, ext_ref, *, n_chunks):
    @pl.when(pl.program_id(1) == 0)
    def _():
        s_ref[...] = jnp.zeros_like(s_ref)

    _pool_block(u_ref, halo_ref, pw_ref, psc_ref, yp_ref, ext_ref, pl.program_id(1))
    C = GLA_CHUNK
    row = lax.broadcasted_iota(jnp.int32, (C, C), 0)
    col = lax.broadcasted_iota(jnp.int32, (C, C), 1)
    causal = col <= row
    tril = causal.astype(BF16)
    wg = wg_ref[...]
    bg = bg_ref[...]
    nt = (((1,), (1,)), ((), ()))
    for c in range(n_chunks):
        rows = slice(c * C, (c + 1) * C)
        z = jnp.dot(glr_ref[rows, :].astype(BF16), wg, preferred_element_type=F32) + bg
        g = jax.nn.log_sigmoid(z) / GLA_GATE_TEMP
        g_hi, g_lo = _split_bf16(g)
        b_all = (jnp.dot(tril, g_hi, preferred_element_type=F32)
                 + jnp.dot(tril, g_lo, preferred_element_type=F32))
        for h in range(GLA_HEADS):
            kc = slice(h * GLA_DK, (h + 1) * GLA_DK)
            vc = slice(h * GLA_DV, (h + 1) * GLA_DV)
            b = b_all[:, kc]
            b_last = b[C - 1:C, :]
            b_mid = b[C // 2 - 1:C // 2, :]
            q = q_ref[rows, kc].astype(F32) * (GLA_DK ** -0.5)
            k = k_ref[rows, kc].astype(F32)
            v = v_ref[rows, vc]
            q_state = (q * jnp.exp(b)).astype(BF16)
            q_in = (q * jnp.exp(b - b_mid)).astype(BF16)
            k_in = (k * jnp.exp(b_mid - b)).astype(BF16)
            k_out = k * jnp.exp(b_last - b)
            attn = lax.dot_general(q_in, k_in, nt, preferred_element_type=F32)
            attn = jnp.where(causal, attn, 0.0).astype(BF16)
            s = s_ref[h]
            o = (jnp.dot(attn, v, preferred_element_type=F32)
                 + jnp.dot(q_state, s.astype(BF16), preferred_element_type=F32))
            decay = jnp.transpose(jnp.broadcast_to(jnp.exp(b_last), (C, GLA_DK)))[:, 0:1]
            s_ref[h] = decay * s + jnp.dot(jnp.transpose(k_out).astype(BF16), v, preferred_element_type=F32)
            o = o * lax.rsqrt(jnp.mean(jnp.square(o), axis=-1, keepdims=True) + RMS_EPS)
            o = o * nw_ref[:, vc]
            r = r_ref[rows, vc].astype(F32)
            o_ref[rows, vc] = (o * (r * _sigmoid(r))).astype(o_ref.dtype)


def _gla(proj, glr, wg, bg, nw, pool_w, pool_scale, B, S, L=512):
    T = proj.shape[0]
    nl = S // L
    hb = L // POOL_HALO
    rb = lambda b, l: b * nl + l
    out_spec = pl.BlockSpec((L, GLA_WIDTH), lambda b, l: (rb(b, l), 0))
    return pl.pallas_call(
        functools.partial(_gla_kernel, n_chunks=L // GLA_CHUNK),
        grid=(B, nl),
        in_specs=[
            pl.BlockSpec((L, GLA_KEY_WIDTH), lambda b, l: (rb(b, l), COL_Q // GLA_KEY_WIDTH)),
            pl.BlockSpec((L, GLA_KEY_WIDTH), lambda b, l: (rb(b, l), COL_K // GLA_KEY_WIDTH)),
            pl.BlockSpec((L, GLA_WIDTH), lambda b, l: (rb(b, l), COL_V // GLA_WIDTH)),
            pl.BlockSpec((L, GLA_WIDTH), lambda b, l: (rb(b, l), COL_R // GLA_WIDTH)),
            pl.BlockSpec((L, LANES), lambda b, l: (rb(b, l), 0)),
            pl.BlockSpec((LANES, GLA_KEY_WIDTH), lambda b, l: (0, 0)),
            pl.BlockSpec((1, GLA_KEY_WIDTH), lambda b, l: (0, 0)),
            pl.BlockSpec((1, GLA_WIDTH), lambda b, l: (0, 0)),
            pl.BlockSpec((L, POOL_WIDTH), lambda b, l: (rb(b, l), 0)),
            pl.BlockSpec((POOL_HALO, POOL_WIDTH), lambda b, l: (jnp.maximum(rb(b, l) * hb - 1, 0), 0)),
            pl.BlockSpec((len(POOL_WINDOWS), POOL_GC, POOL_GC), lambda b, l: (0, 0, 0)),
            pl.BlockSpec((1, POOL_WIDTH), lambda b, l: (0, 0)),
        ],
        out_specs=[out_spec, out_spec],
        out_shape=[jax.ShapeDtypeStruct((T, GLA_WIDTH), BF16), jax.ShapeDtypeStruct((T, POOL_WIDTH), BF16)],
        scratch_shapes=[pltpu.VMEM((GLA_HEADS, GLA_DK, GLA_DV), F32),
                        pltpu.VMEM((POOL_HALO + L, POOL_WIDTH), F32)],
        compiler_params=_params(("parallel", "arbitrary"), 40),
        name="gla",
    )(proj, proj, proj, proj, glr, wg, bg, nw, proj, proj, pool_w, pool_scale)


def _layer_norm(h, w, b):
    mu = jnp.mean(h, axis=-1, keepdims=True)
    hc = h - mu
    var = jnp.mean(jnp.square(hc), axis=-1, keepdims=True)
    return hc * lax.rsqrt(var + LN_EPS) * w + b


def _outproj_kernel(yp_ref, yg_ref, x_ref, w_ref, lw_ref, lb_ref, x1_ref, x1t_ref):
    for r0 in range(0, x_ref.shape[0], ROW_SUB):
        rows = slice(r0, r0 + ROW_SUB)
        mix = (jnp.dot(yp_ref[rows, :], w_ref[0:POOL_WIDTH, :], preferred_element_type=F32)
               + jnp.dot(yg_ref[rows, :], w_ref[POOL_WIDTH:, :], preferred_element_type=F32))
        x1 = _layer_norm(ALPHA * x_ref[rows, :] + mix, lw_ref[...], lb_ref[...])
        x1_ref[rows, :] = x1
        x1t_ref[:, rows] = jnp.transpose(x1).astype(BF16)


def _outproj(y_pool, y_gla, x2, w_out, ln_w, ln_b, tm=512):
    T = x2.shape[0]
    return pl.pallas_call(
        _outproj_kernel,
        grid=(T // tm,),
        in_specs=[
            pl.BlockSpec((tm, POOL_WIDTH), lambda i: (i, 0)),
            pl.BlockSpec((tm, GLA_WIDTH), lambda i: (i, 0)),
            pl.BlockSpec((tm, D_MODEL), lambda i: (i, 0)),
            pl.BlockSpec((D_MODEL, D_MODEL), lambda i: (0, 0), pipeline_mode=pl.Buffered(1)),
            pl.BlockSpec((1, D_MODEL), lambda i: (0, 0)),
            pl.BlockSpec((1, D_MODEL), lambda i: (0, 0)),
        ],
        out_specs=[
            pl.BlockSpec((tm, D_MODEL), lambda i: (i, 0)),
            pl.BlockSpec((D_MODEL, tm), lambda i: (0, i)),
        ],
        out_shape=[
            jax.ShapeDtypeStruct((T, D_MODEL), F32),
            jax.ShapeDtypeStruct((D_MODEL, T), BF16),
        ],
        compiler_params=_params(("parallel",), 48),
        name="outproj",
    )(y_pool, y_gla, x2, w_out, ln_w, ln_b)


N_SORT = PEER_TOPK + 1


def _sort_network(n):
    pairs = []

    def merge(lo, m, r):
        step = 2 * r
        if step < m:
            merge(lo, m, step)
            merge(lo + r, m, step)
            pairs.extend((i, i + r) for i in range(lo + r, lo + m - r, step))
        else:
            pairs.append((lo, lo + r))

    def sort(lo, m):
        if m > 1:
            sort(lo, m // 2)
            sort(lo + m // 2, m // 2)
            merge(lo, m, 1)

    sort(0, n)
    return tuple(pairs)


def _pop_sorted(v, n_out):
    nv = len(v)
    width = 1 << (nv - 1).bit_length()
    for i, j in _sort_network(width):
        if j < nv:
            v[i], v[j] = jnp.maximum(v[i], v[j]), jnp.minimum(v[i], v[j])
    tops = []
    rid = lax.broadcasted_iota(jnp.int32, (SUBLANES, LANES), 0).astype(F32)
    for kk in range(n_out):
        m = jnp.max(v[0], axis=0, keepdims=True)
        tops.append(m)
        first = jnp.min(jnp.where(v[0] == m, rid, float(SUBLANES)), axis=0, keepdims=True)
        hit = rid == first
        for k in range(min(n_out - 1 - kk, nv)):
            v[k] = jnp.where(hit, v[k + 1] if k + 1 < nv else NEG_INF, v[k])
    return tops


def _sorted_top(arr):
    return _pop_sorted([arr[k:k + SUBLANES] for k in range(0, arr.shape[0], SUBLANES)], N_SORT)


def _rows_to_tile(rows):
    rid = lax.broadcasted_iota(jnp.int32, (SUBLANES, LANES), 0)
    tile = jnp.full((SUBLANES, LANES), NEG_INF, F32)
    for k, r in enumerate(rows):
        tile = jnp.where(rid == k, r, tile)
    return tile


def _pair_stats(a, b):
    r8 = lax.broadcasted_iota(jnp.int32, (SUBLANES, LANES), 0)
    b_lo, b_hi, a_hi = _rows_to_tile(b[0:8]), _rows_to_tile(b[8:16]), _rows_to_tile(a[8:16])
    p2 = jnp.where(r8 < 5, a[2] + b_lo, jnp.where(r8 == 5, a[16] + b[0], jnp.where(r8 == 6, a[0] + b[16], NEG_INF)))
    pieces = [
        a[0] + b_lo, a[0] + b_hi, a[1] + b_lo, p2,
        jnp.where(r8 < 4, a[3] + b_lo, NEG_INF),
        jnp.where(r8 < 3, a[4] + b_lo, NEG_INF),
        jnp.where(r8 < 2, a[5] + b_lo, NEG_INF),
        jnp.where(r8 < 2, a[6] + b_lo, NEG_INF),
        jnp.where(r8 < 2, a[7] + b_lo, NEG_INF),
        a_hi + b[0],
    ]
    sums = _pop_sorted(pieces, N_SORT)
    top16 = jnp.concatenate([_rows_to_tile(sums[0:8]), _rows_to_tile(sums[8:16])], axis=0)
    z = jnp.sum(jnp.exp(top16 - sums[0]), axis=0, keepdims=True)
    return 0.5 * (sums[PEER_TOPK - 1] + sums[PEER_TOPK]), 1.0 / z


def _split_bf16(x):
    hi = x.astype(BF16)
    return hi, (x - hi.astype(F32)).astype(BF16)


def _query_kernel(x1t_ref, wq_ref, khi_ref, klo_ref, u_ref, v_ref,
                  c1_ref, n1_ref, e2_ref, r2_ref, ub_ref, vt_ref, q_ref, s1_scr, s2_scr, *, tm):
    ub_ref[...] = u_ref[...].astype(BF16)
    vt_ref[...] = jnp.transpose(v_ref[...]).astype(BF16)
    q_ref[...] = jnp.dot(wq_ref[...], x1t_ref[...], preferred_element_type=F32)
    for h in range(PEER_HEADS):
        for p in range(2):
            hp = 2 * h + p
            q_hi, q_lo = _split_bf16(q_ref[hp * PEER_HALF:(hp + 1) * PEER_HALF, :])
            k_hi, k_lo = khi_ref[hp], klo_ref[hp]
            sc = (jnp.dot(k_hi, q_hi, preferred_element_type=F32)
                  + jnp.dot(k_hi, q_lo, preferred_element_type=F32)
                  + jnp.dot(k_lo, q_hi, preferred_element_type=F32))
            if p == 0:
                s1_scr[...] = sc
            else:
                s2_scr[...] = sc
        for tc in range(tm // LANES):
            lanes = slice(tc * LANES, (tc + 1) * LANES)
            s1 = s1_scr[:, lanes]
            s2 = s2_scr[:, lanes]
            ta, tb = _sorted_top(s1), _sorted_top(s2)
            tau, rz = _pair_stats(ta, tb)
            n1 = jnp.zeros_like(s1)
            r2 = jnp.zeros_like(s2)
            for jj in range(PEER_TOPK):
                n1 = jnp.where(s1 >= tau - tb[jj], float(jj + 1), n1)
                r2 = jnp.where(s2 < tb[jj], float(jj + 1), r2)
            c1_ref[h, tc] = jnp.exp(s1 - ta[0])
            n1_ref[h, tc] = n1
            e2_ref[h, :, lanes] = (jnp.exp(s2 - tb[0]) * rz).astype(BF16)
            r2_ref[h, :, lanes] = r2.astype(BF16)


def _query(x1t, wq_t, keys_hi, keys_lo, peer_u, peer_v, tm=256):
    T = x1t.shape[1]
    n_exp = peer_u.shape[0]
    steps = T // tm
    slab = n_exp // steps
    per_tile = PEER_TE // slab
    assert slab * steps == n_exp and per_tile * slab == PEER_TE and slab % LANES == 0
    nc = tm // LANES
    row_spec = pl.BlockSpec((PEER_HEADS, nc, PEER_NKEYS, LANES), lambda i: (0, i, 0, 0))
    col_spec = pl.BlockSpec((PEER_HEADS, PEER_NKEYS, tm), lambda i: (0, 0, i))
    key_spec = pl.BlockSpec((2 * PEER_HEADS, PEER_NKEYS, PEER_HALF), lambda i: (0, 0, 0))
    tab_spec = pl.BlockSpec((slab, D_MODEL), lambda i: (i, 0))
    row_shape = jax.ShapeDtypeStruct((PEER_HEADS, T // LANES, PEER_NKEYS, LANES), F32)
    col_shape = jax.ShapeDtypeStruct((PEER_HEADS, PEER_NKEYS, T), BF16)
    return pl.pallas_call(
        functools.partial(_query_kernel, tm=tm),
        grid=(steps,),
        in_specs=[
            pl.BlockSpec((D_MODEL, tm), lambda i: (0, i)),
            pl.BlockSpec((D_MODEL, D_MODEL), lambda i: (0, 0), pipeline_mode=pl.Buffered(1)),
            key_spec, key_spec, tab_spec, tab_spec,
        ],
        out_specs=[row_spec, row_spec, col_spec, col_spec, tab_spec,
                   pl.BlockSpec((None, D_MODEL, slab), lambda i: (i // per_tile, 0, i % per_tile))],
        out_shape=[row_shape, row_shape, col_shape, col_shape,
                   jax.ShapeDtypeStruct((n_exp, D_MODEL), BF16),
                   jax.ShapeDtypeStruct((n_exp // PEER_TE, D_MODEL, PEER_TE), BF16)],
        scratch_shapes=[pltpu.VMEM((D_MODEL, tm), F32), pltpu.VMEM((PEER_NKEYS, tm), F32),
                        pltpu.VMEM((PEER_NKEYS, tm), F32)],
        compiler_params=_params(("parallel",), 56),
        name="query",
    )(x1t, wq_t, keys_hi, keys_lo, peer_u, peer_v)


PEER_SB = 64
PEER_RG = 2
SUBLANES = 8
PEER_TE = 1024


def _gelu(x):
    return 0.5 * x * (1.0 + lax.erf(x * (1.0 / math.sqrt(2.0))))


def _bcast_row_bf16(tile, ri, rows):
    packed = jnp.broadcast_to(tile[ri:ri + 1, :], (2 * SUBLANES, LANES)).astype(BF16)
    return jnp.concatenate([packed] * (rows // (2 * SUBLANES)), axis=0)


def _peer_kernel(x1t_ref, u_ref, vt_ref, c1_ref, n1_ref, e2_ref, r2_ref, yt_ref, st_scr, ht_scr, *, tm, te):
    j = pl.program_id(1)
    n1 = te // PEER_NKEYS
    grows = PEER_RG * PEER_NKEYS

    @pl.when(j == 0)
    def _():
        yt_ref[...] = jnp.zeros_like(yt_ref)

    st_scr[...] = jnp.dot(u_ref[...], x1t_ref[...], preferred_element_type=F32)

    nsb = PEER_NKEYS // PEER_SB
    tile_rows = pl.ds(pl.multiple_of(j * n1, SUBLANES), n1)
    for gi in range(n1 // PEER_RG):
        crows = slice(gi * grows, (gi + 1) * grows)
        for tc in range(tm // LANES):
            lanes = slice(tc * LANES, (tc + 1) * LANES)
            g = [[jnp.zeros((PEER_SB, LANES), BF16) for _ in range(nsb)] for _ in range(PEER_RG)]
            for h in range(PEER_HEADS):
                c1_t = c1_ref[h, tc, tile_rows, :]
                n1_t = n1_ref[h, tc, tile_rows, :]
                c1b = [_bcast_row_bf16(c1_t, gi * PEER_RG + r, PEER_SB) for r in range(PEER_RG)]
                n1b = [_bcast_row_bf16(n1_t, gi * PEER_RG + r, PEER_SB) for r in range(PEER_RG)]
                for sb in range(nsb):
                    rows = slice(sb * PEER_SB, (sb + 1) * PEER_SB)
                    r2c = r2_ref[h, rows, lanes]
                    e2c = e2_ref[h, rows, lanes]
                    for r in range(PEER_RG):
                        g[r][sb] = g[r][sb] + c1b[r] * jnp.where(r2c < n1b[r], e2c, jnp.zeros_like(e2c))
            for r in range(PEER_RG):
                for sb in range(nsb):
                    base = gi * grows + r * PEER_NKEYS + sb * PEER_SB
                    srows = slice(base, base + PEER_SB)
                    ht_scr[srows, lanes] = g[r][sb] * _gelu(st_scr[srows, lanes].astype(BF16))
    yt_ref[...] += jnp.dot(vt_ref[...], ht_scr[...], preferred_element_type=F32)


def _peer(x1t, u_tab, vt_tiles, c1, n1, e2, r2, tm=512):
    T = x1t.shape[1]
    nj, _, te = vt_tiles.shape
    row_spec = pl.BlockSpec((PEER_HEADS, tm // LANES, PEER_NKEYS, LANES), lambda i, j: (0, i, 0, 0))
    col_spec = pl.BlockSpec((PEER_HEADS, PEER_NKEYS, tm), lambda i, j: (0, 0, i))
    return pl.pallas_call(
        functools.partial(_peer_kernel, tm=tm, te=te),
        grid=(T // tm, nj),
        in_specs=[
            pl.BlockSpec((D_MODEL, tm), lambda i, j: (0, i)),
            pl.BlockSpec((te, D_MODEL), lambda i, j: (j, 0)),
            pl.BlockSpec((None, D_MODEL, te), lambda i, j: (j, 0, 0)),
            row_spec, row_spec, col_spec, col_spec,
        ],
        out_specs=pl.BlockSpec((D_MODEL, tm), lambda i, j: (0, i)),
        out_shape=jax.ShapeDtypeStruct((D_MODEL, T), F32),
        scratch_shapes=[
            pltpu.VMEM((te, tm), F32),
            pltpu.VMEM((te, tm), BF16),
        ],
        compiler_params=_params(("parallel", "arbitrary"), 56),
        name="peer",
    )(x1t, u_tab, vt_tiles, c1, n1, e2, r2)


def _final_kernel(x1_ref, yt_ref, p_ref, wg_ref, wp_ref, lw_ref, lb_ref, o_ref):
    for r0 in range(0, x1_ref.shape[0], ROW_SUB):
        rows = slice(r0, r0 + ROW_SUB)
        x1 = x1_ref[rows, :]
        gate = _sigmoid(jnp.dot(x1.astype(BF16), wg_ref[...], preferred_element_type=F32))
        emb = jnp.dot(p_ref[rows, :].astype(BF16), wp_ref[...], preferred_element_type=F32)
        y = jnp.transpose(yt_ref[:, rows])
        o_ref[rows, :] = _layer_norm(ALPHA * x1 + y + gate * emb, lw_ref[...], lb_ref[...])


def _final(x1, y_ffn_t, p2, w_gate, w_proj, ln_w, ln_b, tm=512):
    T = x1.shape[0]
    return pl.pallas_call(
        _final_kernel,
        grid=(T // tm,),
        in_specs=[
            pl.BlockSpec((tm, D_MODEL), lambda i: (i, 0)),
            pl.BlockSpec((D_MODEL, tm), lambda i: (0, i)),
            pl.BlockSpec((tm, PLE_DIM), lambda i: (i, 0)),
            pl.BlockSpec((D_MODEL, D_MODEL), lambda i: (0, 0), pipeline_mode=pl.Buffered(1)),
            pl.BlockSpec((PLE_DIM, D_MODEL), lambda i: (0, 0), pipeline_mode=pl.Buffered(1)),
            pl.BlockSpec((1, D_MODEL), lambda i: (0, 0)),
            pl.BlockSpec((1, D_MODEL), lambda i: (0, 0)),
        ],
        out_specs=pl.BlockSpec((tm, D_MODEL), lambda i: (i, 0)),
        out_shape=jax.ShapeDtypeStruct((T, D_MODEL), F32),
        compiler_params=_params(("parallel",), 48),
        name="final",
    )(x1, y_ffn_t, p2, w_gate, w_proj, ln_w, ln_b)


def _layer(x2, p2, B, S, w_in, gla_w_gate_up, gla_b_gate, gla_norm_w, pool_w, pool_scale, w_out,
           ln1_w, ln1_b, peer_w_query, peer_sub_keys, peer_u, peer_v, ple_w_gate, ple_w_proj, ln2_w, ln2_b):
    glr0 = COL_R
    w_main = jnp.concatenate([w_in[:, :glr0], w_in[:, glr0 + GLA_GATE_RANK:]], axis=1).astype(BF16)
    w_glr = jnp.pad(w_in[:, glr0:glr0 + GLA_GATE_RANK], ((0, 0), (0, LANES - GLA_GATE_RANK))).astype(BF16)
    proj, glr = _proj(x2, w_main, w_glr)

    wg = jnp.pad(gla_w_gate_up, ((0, LANES - GLA_GATE_RANK), (0, 0))).astype(BF16)
    y_gla, y_pool = _gla(proj, glr, wg, gla_b_gate.reshape(1, GLA_KEY_WIDTH), gla_norm_w.reshape(1, GLA_WIDTH),
                         pool_w.astype(BF16), pool_scale.reshape(1, POOL_WIDTH), B, S)

    x1, x1t = _outproj(y_pool, y_gla, x2, w_out.astype(BF16),
                       ln1_w.reshape(1, D_MODEL), ln1_b.reshape(1, D_MODEL))

    keys = peer_sub_keys.reshape(2 * PEER_HEADS, PEER_NKEYS, PEER_HALF)
    keys_hi, keys_lo = _split_bf16(keys)
    c1, n1, e2, r2, u_bf16, vt_tiles = _query(x1t, peer_w_query.T.astype(BF16), keys_hi, keys_lo, peer_u, peer_v)
    y_ffn_t = _peer(x1t, u_bf16, vt_tiles, c1, n1, e2, r2)

    return _final(x1, y_ffn_t, p2, ple_w_gate.astype(BF16), ple_w_proj.astype(BF16),
                  ln2_w.reshape(1, D_MODEL), ln2_b.reshape(1, D_MODEL))


def kernel(x, p, w_in, gla_w_gate_up, gla_b_gate, gla_norm_w, pool_w, pool_scale, w_out, ln1_w, ln1_b,
           peer_w_query, peer_sub_keys, peer_u, peer_v, ple_w_gate, ple_w_proj, ln2_w, ln2_b):
    B, S, D = x.shape
    x2 = x.reshape(B * S, D)
    for i in range(w_in.shape[0]):
        x2 = _layer(x2, p[i].reshape(B * S, PLE_DIM), B, S, w_in[i], gla_w_gate_up[i], gla_b_gate[i],
                    gla_norm_w[i], pool_w[i], pool_scale[i], w_out[i], ln1_w[i], ln1_b[i],
                    peer_w_query[i], peer_sub_keys[i], peer_u[i], peer_v[i], ple_w_gate[i],
                    ple_w_proj[i], ln2_w[i], ln2_b[i])
    return x2.reshape(B, S, D)
```
